```python
import math
import jax, jax.numpy as jnp
from jax import lax
import numpy as np

D_MODEL = 1024
BATCH = 8
SEQ = 2048
DEPTH = 1
DEC_BATCH = 16
DEC_SEQ = 2048
PAST_LEN = 128

RET_HEADS = 4
RET_QK_DIM = 128
RET_V_DIM = 256
RET_CHUNK = 128
ROPE_BASE = 10000.0
RET_DECAY_OFFSET_FWD = 5.0
RET_DECAY_OFFSET_BWD = 5.5
ATT_HEADS = 8
ATT_KV_HEADS = 2
ATT_HEAD_DIM = 128
WINDOW = 128
ATT_BLOCK = 128
REL_BUCKETS = 32
REL_MAX_DIST = 128
MEM_LEN = 256
MEM_HEADS = 4
MEM_HEAD_DIM = D_MODEL // MEM_HEADS
N_BRANCH = 3
N_EXPERTS = 32
TOP_K = 4
D_FF = D_MODEL
SWIGLU_LIMIT = 7.0
SWIGLU_ALPHA = 1.702
EXPERT_BLOCK = 128
EPS = 1e-6

RET_Q_W = RET_HEADS * RET_QK_DIM
RET_V_W = RET_HEADS * RET_V_DIM
ATT_Q_W = ATT_HEADS * ATT_HEAD_DIM
ATT_KV_W = ATT_KV_HEADS * ATT_HEAD_DIM
MEM_Q_W = MEM_HEADS * MEM_HEAD_DIM
SPLITS = (RET_Q_W, RET_Q_W, RET_V_W, RET_V_W, ATT_Q_W, ATT_KV_W, ATT_KV_W, MEM_Q_W, D_MODEL, D_MODEL, D_MODEL)
IN_WIDTH = RET_Q_W * 2 + RET_V_W * 2 + ATT_Q_W + ATT_KV_W * 2 + MEM_Q_W + N_BRANCH * D_MODEL

kernel_name = "hybrid_bidir_retention_window_gqa_memory_moe"


def rmsnorm(x, g):
    xf = x.astype(jnp.float32)
    y = xf * lax.rsqrt(jnp.mean(xf * xf, axis=-1, keepdims=True) + EPS)
    return (y * g.astype(jnp.float32)).astype(x.dtype)


def rope(x, pos):
    half = x.shape[-1] // 2
    inv = ROPE_BASE ** (-jnp.arange(half, dtype=jnp.float32) / half)
    ang = pos[:, None] * inv[None, :]
    cos = jnp.cos(ang)[:, None, :]
    sin = jnp.sin(ang)[:, None, :]
    xf = x.astype(jnp.float32)
    x1, x2 = xf[..., :half], xf[..., half:]
    return jnp.concatenate([x1 * cos - x2 * sin, x1 * sin + x2 * cos], axis=-1).astype(x.dtype)


def retention_causal(q, k, v, log_gamma, strict):
    q = q.astype(jnp.float32)
    k = k.astype(jnp.float32)
    v = v.astype(jnp.float32)
    B, H, N, C, dk = q.shape
    dv = v.shape[-1]
    idx = jnp.arange(C, dtype=jnp.float32)
    diff = idx[:, None] - idx[None, :]
    mask = (diff > 0) if strict else (diff >= 0)
    dmat = jnp.where(mask[None], jnp.exp(jnp.where(mask, diff, 0.0)[None] * log_gamma[:, None, None]), 0.0)
    s = jnp.einsum('bhnid,bhnjd->bhnij', q, k) * dmat[None, :, None]
    intra = jnp.einsum('bhnij,bhnje->bhnie', s, v)
    zeta = jnp.exp((C - 1 - idx)[None, :] * log_gamma[:, None])
    kv = jnp.einsum('bhnjd,bhnje->nbhde', k * zeta[None, :, None, :, None], v)
    chunk_decay = jnp.exp(C * log_gamma)[None, :, None, None]

    def step(state, kv_n):
        return state * chunk_decay + kv_n, state

    _, s_prev = lax.scan(step, jnp.zeros((B, H, dk, dv), jnp.float32), kv)
    xi = jnp.exp((idx + 1.0)[None, :] * log_gamma[:, None])
    cross = jnp.einsum('bhnid,nbhde->bhnie', q * xi[None, :, None, :, None], s_prev)
    return intra + cross


def bidir_retention(q, k, v):
    B, L, H, _ = q.shape
    dv = v.shape[-1]
    N = L // RET_CHUNK
    heads = jnp.arange(H, dtype=jnp.float32)
    lg_fwd = jnp.log1p(-jnp.exp2(-RET_DECAY_OFFSET_FWD - heads))
    lg_bwd = jnp.log1p(-jnp.exp2(-RET_DECAY_OFFSET_BWD - heads))

    def chunk(t):
        return t.transpose(0, 2, 1, 3).reshape(B, H, N, RET_CHUNK, t.shape[-1])

    fwd = retention_causal(chunk(q), chunk(k), chunk(v), lg_fwd, False).reshape(B, H, L, dv)
    qr, kr, vr = jnp.flip(q, 1), jnp.flip(k, 1), jnp.flip(v, 1)
    bwd = retention_causal(chunk(qr), chunk(kr), chunk(vr), lg_bwd, True).reshape(B, H, L, dv)
    out = fwd + jnp.flip(bwd, 2)
    return out.transpose(0, 2, 1, 3)


def t5_bucket(rel):
    nb = REL_BUCKETS // 2
    max_exact = nb // 2
    ret = jnp.where(rel > 0, nb, 0)
    n = jnp.abs(rel)
    large = max_exact + (jnp.log(jnp.maximum(n, 1).astype(jnp.float32) / max_exact)
                         / math.log(REL_MAX_DIST / max_exact) * (nb - max_exact)).astype(jnp.int32)
    large = jnp.minimum(large, nb - 1)
    return ret + jnp.where(n < max_exact, n, large)


def window_attention(q, k, v, sink, rel_table):
    B, L, _, d = q.shape
    W = ATT_BLOCK
    N = L // W
    G = ATT_HEADS // ATT_KV_HEADS

    def band(t):
        tp = jnp.pad(t, ((0, 0), (W, W), (0, 0), (0, 0))).reshape(B, N + 2, W, ATT_KV_HEADS, d)
        return jnp.concatenate([tp[:, :-2], tp[:, 1:-1], tp[:, 2:]], axis=2)

    kb, vb = band(k), band(v)
    qb = q.reshape(B, N, W, ATT_KV_HEADS, G, d)
    s = jnp.einsum('bnqgrd,bnkgd->bngrqk', qb, kb).astype(jnp.float32) * (d ** -0.5)
    qi = jnp.arange(W)[:, None]
    ki = jnp.arange(3 * W)[None, :]
    rel = ki - W - qi
    bias = rel_table[t5_bucket(rel)].astype(jnp.float32)
    bias = bias.transpose(2, 0, 1).reshape(ATT_KV_HEADS, G, W, 3 * W)
    key_pos = jnp.arange(N)[:, None, None] * W - W + ki[None]
    valid = (jnp.abs(rel) <= WINDOW)[None] & (key_pos >= 0) & (key_pos < L)
    s = jnp.where(valid[None, :, None, None], s + bias[None, None], -jnp.inf)
    sink_f = sink.astype(jnp.float32).reshape(1, 1, ATT_KV_HEADS, G, 1, 1)
    m = jnp.maximum(jnp.max(s, axis=-1, keepdims=True), sink_f)
    p = jnp.exp(s - m)
    denom = jnp.sum(p, axis=-1, keepdims=True) + jnp.exp(sink_f - m)
    o = jnp.einsum('bngrqk,bnkgd->bnqgrd', (p / denom).astype(v.dtype), vb)
    return o.reshape(B, L, ATT_HEADS * d)


def memory_attention(q, mk, mv):
    B, L, _, dm = q.shape
    s = jnp.einsum('blhd,bmhd->bhlm', q, mk).astype(jnp.float32) * (dm ** -0.5)
    p = jax.nn.softmax(s, axis=-1)
    o = jnp.einsum('bhlm,bmhd->blhd', p.astype(mv.dtype), mv)
    return o.reshape(B, L, MEM_HEADS * dm)


def clamped_swiglu(h):
    gate, lin = jnp.split(h, 2, axis=-1)
    gate = jnp.minimum(gate, SWIGLU_LIMIT)
    lin = jnp.clip(lin, -SWIGLU_LIMIT, SWIGLU_LIMIT)
    return gate * jax.nn.sigmoid(SWIGLU_ALPHA * gate) * (lin + 1.0)


def moe(x, w_router, b_router, w_e_in, b_e_in, w_e_out, b_e_out):
    B, L, D = x.shape
    T = B * L
    xf = x.reshape(T, D)
    logits = (xf @ w_router + b_router).astype(jnp.float32)
    top_val, top_idx = lax.top_k(logits, TOP_K)
    gates = jax.nn.softmax(top_val, axis=-1)
    A = T * TOP_K
    flat_e = top_idx.reshape(A)
    flat_tok = jnp.arange(A, dtype=jnp.int32) // TOP_K
    order = jnp.argsort(flat_e)
    sorted_e = flat_e[order]
    counts = jnp.bincount(flat_e, length=N_EXPERTS)
    starts = jnp.cumsum(counts) - counts
    padded = (counts + EXPERT_BLOCK - 1) // EXPERT_BLOCK * EXPERT_BLOCK
    pad_ends = jnp.cumsum(padded)
    pad_starts = pad_ends - padded
    slot_sorted = pad_starts[sorted_e] + jnp.arange(A, dtype=jnp.int32) - starts[sorted_e]
    slot = jnp.zeros((A,), jnp.int32).at[order].set(slot_sorted.astype(jnp.int32))
    n_blocks = -(-A // EXPERT_BLOCK) + N_EXPERTS
    R = n_blocks * EXPERT_BLOCK
    buf = jnp.zeros((R, D), x.dtype).at[slot].set(xf[flat_tok])
    block_start = jnp.arange(n_blocks, dtype=pad_ends.dtype) * EXPERT_BLOCK
    block_expert = jnp.minimum(jnp.searchsorted(pad_ends, block_start, side='right'), N_EXPERTS - 1)

    def expert_block(args):
        xb, e = args
        h = xb @ w_e_in[e] + b_e_in[e]
        return clamped_swiglu(h) @ w_e_out[e] + b_e_out[e]

    out = lax.map(expert_block, (buf.reshape(n_blocks, EXPERT_BLOCK, D), block_expert)).reshape(R, D)
    y = out[slot].reshape(T, TOP_K, D)
    return jnp.einsum('tk,tkd->td', gates.astype(y.dtype), y).reshape(B, L, D)


def layer(x, mem, rel_table, norm_mix, w_in, ret_out_norm, att_q_norm, att_k_norm, att_sink,
          mem_norm, w_mem_kv, mem_q_norm, mem_k_norm, w_branch, w_out, norm_ffn,
          w_router, b_router, w_e_in, b_e_in, w_e_out, b_e_out):
    B, L, _ = x.shape
    M = mem.shape[1]
    h = rmsnorm(x, norm_mix)
    proj = h @ w_in
    offs = np.cumsum(SPLITS)[:-1].tolist()
    rq, rk, rv, rg, aq, ak, av, mq, g_ret, g_att, g_mem = jnp.split(proj, offs, axis=-1)
    pos = jnp.arange(L, dtype=jnp.float32)

    def heads(t, n):
        return t.reshape(B, L, n, -1)

    rq = rope(heads(rq, RET_HEADS), pos)
    rk = rope(heads(rk, RET_HEADS), pos) * (RET_QK_DIM ** -0.5)
    ret = bidir_retention(rq, rk, heads(rv, RET_HEADS))
    ret = rmsnorm(ret, ret_out_norm.reshape(RET_HEADS, RET_V_DIM)).reshape(B, L, RET_V_W).astype(x.dtype)
    ret = ret * jax.nn.silu(rg)
    aq = rmsnorm(heads(aq, ATT_HEADS), att_q_norm)
    ak = rmsnorm(heads(ak, ATT_KV_HEADS), att_k_norm)
    att = window_attention(aq, ak, heads(av, ATT_KV_HEADS), att_sink, rel_table)
    mkv = rmsnorm(mem, mem_norm) @ w_mem_kv
    mk, mv = jnp.split(mkv, 2, axis=-1)
    mk = rmsnorm(mk.reshape(B, M, MEM_HEADS, MEM_HEAD_DIM), mem_k_norm)
    mv = mv.reshape(B, M, MEM_HEADS, MEM_HEAD_DIM)
    mq = rmsnorm(heads(mq, MEM_HEADS), mem_q_norm)
    mo = memory_attention(mq, mk, mv)
    merged = (jax.nn.sigmoid(g_ret) * (ret @ w_branch[0])
              + jax.nn.sigmoid(g_att) * (att @ w_branch[1])
              + jax.nn.sigmoid(g_mem) * (mo @ w_branch[2]))
    x = x + merged @ w_out
    x = x + moe(rmsnorm(x, norm_ffn), w_router, b_router, w_e_in, b_e_in, w_e_out, b_e_out)
    return x


def trunk(x, mem, rel_table, layer_params):
    for l in range(DEPTH):
        x = layer(x, mem, rel_table, *[p[l] for p in layer_params])
    return x


def setup_inputs(seed: int = 0) -> dict:
    key = jax.random.key(seed)
    ks = jax.random.split(key, 26)
    f32 = jnp.float32

    def nrm(k, shape, scale):
        return jax.random.normal(k, shape, f32) * scale

    def gain(k, shape):
        return 1.0 + 0.05 * jax.random.normal(k, shape, f32)

    return {
        "x_prompt": nrm(ks[0], (BATCH, SEQ, D_MODEL), 1.0),
        "x_sample": nrm(ks[1], (DEC_BATCH, DEC_SEQ, D_MODEL), 1.0),
        "mem_prompt": nrm(ks[2], (BATCH, MEM_LEN, D_MODEL), 1.0),
        "mem_sample": nrm(ks[3], (DEC_BATCH, MEM_LEN, D_MODEL), 1.0),
        "rel_table": nrm(ks[4], (REL_BUCKETS, ATT_HEADS), 0.5),
        "norm_mix": gain(ks[5], (DEPTH, D_MODEL)),
        "w_in": nrm(ks[6], (DEPTH, D_MODEL, IN_WIDTH), D_MODEL ** -0.5),
        "ret_out_norm": gain(ks[7], (DEPTH, RET_V_W)),
        "att_q_norm": gain(ks[8], (DEPTH, ATT_HEAD_DIM)),
        "att_k_norm": gain(ks[9], (DEPTH, ATT_HEAD_DIM)),
        "att_sink": nrm(ks[10], (DEPTH, ATT_HEADS), 0.5),
        "mem_norm": gain(ks[11], (DEPTH, D_MODEL)),
        "w_mem_kv": nrm(ks[12], (DEPTH, D_MODEL, 2 * MEM_Q_W), D_MODEL ** -0.5),
        "mem_q_norm": gain(ks[13], (DEPTH, MEM_HEAD_DIM)),
        "mem_k_norm": gain(ks[14], (DEPTH, MEM_HEAD_DIM)),
        "w_branch": nrm(ks[15], (DEPTH, N_BRANCH, D_MODEL, D_MODEL), D_MODEL ** -0.5),
        "w_out": nrm(ks[16], (DEPTH, D_MODEL, D_MODEL), D_MODEL ** -0.5),
        "norm_ffn": gain(ks[17], (DEPTH, D_MODEL)),
        "w_router": nrm(ks[18], (DEPTH, D_MODEL, N_EXPERTS), D_MODEL ** -0.5),
        "b_router": nrm(ks[19], (DEPTH, N_EXPERTS), 0.01),
        "w_e_in": nrm(ks[20], (DEPTH, N_EXPERTS, D_MODEL, 2 * D_FF), D_MODEL ** -0.5),
        "b_e_in": nrm(ks[21], (DEPTH, N_EXPERTS, 2 * D_FF), 0.02),
        "w_e_out": nrm(ks[22], (DEPTH, N_EXPERTS, D_FF, D_MODEL), D_FF ** -0.5),
        "b_e_out": nrm(ks[23], (DEPTH, N_EXPERTS, D_MODEL), 0.02),
    }


def reference(x_prompt, x_sample, mem_prompt, mem_sample, rel_table, norm_mix, w_in, ret_out_norm,
              att_q_norm, att_k_norm, att_sink, mem_norm, w_mem_kv, mem_q_norm, mem_k_norm,
              w_branch, w_out, norm_ffn, w_router, b_router, w_e_in, b_e_in, w_e_out, b_e_out):
    layer_params = (norm_mix, w_in, ret_out_norm, att_q_norm, att_k_norm, att_sink, mem_norm,
                    w_mem_kv, mem_q_norm, mem_k_norm, w_branch, w_out, norm_ffn,
                    w_router, b_router, w_e_in, b_e_in, w_e_out, b_e_out)
    y_prompt = trunk(x_prompt, mem_prompt, rel_table, layer_params)
    y_sample = trunk(x_sample, mem_sample, rel_table, layer_params)
    return (y_prompt, y_sample)
```

```python
import functools
import math

import jax
import jax.numpy as jnp
import numpy as np
from jax import lax
from jax.experimental import pallas as pl
from jax.experimental.pallas import tpu as pltpu
from jax.experimental.pallas import tpu_sc as plsc

F32 = jnp.float32
BF16 = jnp.bfloat16
I32 = jnp.int32

D_MODEL = 1024
EPS = 1e-6
RET_HEADS = 4
RET_QK = 128
RET_V = 256
CHUNK = 128
ROPE_BASE = 10000.0
DECAY_FWD = 5.0
DECAY_BWD = 5.5
ATT_HEADS = 8
ATT_KV = 2
ATT_GROUP = ATT_HEADS // ATT_KV
ATT_D = 128
WINDOW = 128
REL_BUCKETS = 32
REL_MAX_DIST = 128
MEM_HEADS = 4
MEM_D = 256
N_EXPERTS = 32
TOP_K = 4
D_FF = 1024
SWIGLU_LIMIT = 7.0
SWIGLU_ALPHA = 1.702
EXPERT_BLOCK = 512
HALF = D_MODEL // 2

IN_WIDTH = 8704
OFF_GATES = 0
OFF_RQ = 3072
OFF_RK = 3584
OFF_RV = 4096
OFF_RG = 5120
OFF_AQ = 6144
OFF_AK = 7168
OFF_AV = 7424
OFF_MQ = 7680

VMEM_LIMIT = 56 * 1024 * 1024


def _cp(sem, vmem=VMEM_LIMIT):
    return pltpu.CompilerParams(dimension_semantics=sem, vmem_limit_bytes=vmem)


def _rms(x, g):
    return x * lax.rsqrt(jnp.mean(x * x, axis=-1, keepdims=True) + EPS) * g


def _pack_bf16_pairs(x):
    w = x.shape[1] // 2
    bits = pltpu.bitcast(x.astype(BF16).astype(F32), I32)
    hi = bits[:, :w] & jnp.int32(-65536)
    lo = lax.shift_right_logical(bits[:, w:], jnp.int32(16))
    return hi | lo


def _unpack_bf16_pairs(u):
    hi = pltpu.bitcast(u & jnp.int32(-65536), F32)
    lo = pltpu.bitcast(lax.shift_left(u, jnp.int32(16)), F32)
    return hi, lo


def _in_proj_kernel(x_ref, g_ref, w_ref, o_ref, h_scr):
    @pl.when(pl.program_id(1) == 0)
    def _():
        h_scr[...] = _rms(x_ref[...], g_ref[...]).astype(BF16)

    o_ref[...] = jnp.dot(h_scr[...], w_ref[...], preferred_element_type=F32).astype(BF16)


def _in_proj(x2, gain, w_bf16, bm=1024, bn=512):
    t = x2.shape[0]
    n = w_bf16.shape[1]
    return pl.pallas_call(
        _in_proj_kernel,
        out_shape=jax.ShapeDtypeStruct((t, n), BF16),
        grid=(t // bm, n // bn),
        in_specs=[
            pl.BlockSpec((bm, D_MODEL), lambda i, j: (i, 0)),
            pl.BlockSpec((1, D_MODEL), lambda i, j: (0, 0)),
            pl.BlockSpec((D_MODEL, bn), lambda i, j: (0, j)),
        ],
        out_specs=pl.BlockSpec((bm, bn), lambda i, j: (i, j)),
        scratch_shapes=[pltpu.VMEM((bm, D_MODEL), BF16)],
        compiler_params=_cp(("parallel", "arbitrary")),
        name="in_proj",
    )(x2, gain, w_bf16)


def _retention_kernel(dec_ref, q_ref, k_ref, v_ref, g_ref, cos_ref, sin_ref, tab_ref, gain_ref,
                      o_ref, qs, ks, acc, st):
    h = pl.program_id(1)
    seq = q_ref.shape[0]
    n_chunks = seq // CHUNK
    dec_f = dec_ref[2 * h]
    dec_b = dec_ref[2 * h + 1]
    nt = (((1,), (1,)), ((), ()))
    tn = (((0,), (0,)), ((), ()))

    def rope(x, c, s):
        return x * c + pltpu.roll(x, RET_QK // 2, 1) * s

    st[...] = jnp.zeros_like(st)

    def fwd(n, carry):
        r0 = pl.multiple_of(n * CHUNK, CHUNK)
        rows = pl.ds(r0, CHUNK)
        c = cos_ref[rows, :]
        s = sin_ref[rows, :]
        q = rope(q_ref[rows, :].astype(F32), c, s)
        k = rope(k_ref[rows, :].astype(F32), c, s) * (RET_QK ** -0.5)
        qs[rows, :] = q
        ks[rows, :] = k
        v = v_ref[rows, :]
        sc = lax.dot_general(q.astype(BF16), k.astype(BF16), nt, preferred_element_type=F32)
        intra = jnp.dot((sc * tab_ref[0]).astype(BF16), v, preferred_element_type=F32)
        state = st[...]
        cross = jnp.dot((q * tab_ref[1]).astype(BF16), state.astype(BF16), preferred_element_type=F32)
        acc[rows, :] = intra + cross
        st[...] = state * dec_f + lax.dot_general((k * tab_ref[2]).astype(BF16), v, tn,
                                                  preferred_element_type=F32)
        return carry

    lax.fori_loop(0, n_chunks, fwd, 0)
    st[...] = jnp.zeros_like(st)

    def bwd(i, carry):
        n = n_chunks - 1 - i
        r0 = pl.multiple_of(n * CHUNK, CHUNK)
        rows = pl.ds(r0, CHUNK)
        q = qs[rows, :]
        k = ks[rows, :]
        v = v_ref[rows, :]
        state = st[...]
        cross = jnp.dot((q * tab_ref[3]).astype(BF16), state.astype(BF16), preferred_element_type=F32)
        o = acc[rows, :] + cross
        st[...] = state * dec_b + lax.dot_general((k * tab_ref[4]).astype(BF16), v, tn,
                                                  preferred_element_type=F32)
        y = _rms(o, gain_ref[...])
        g = g_ref[rows, :].astype(F32)
        o_ref[rows, :] = (y * (g * jax.nn.sigmoid(g))).astype(BF16)
        return carry

    lax.fori_loop(0, n_chunks, bwd, 0)


def _retention(proj3, gain, cos, sin, tabs, decays):
    b, seq, _ = proj3.shape
    return pl.pallas_call(
        _retention_kernel,
        out_shape=jax.ShapeDtypeStruct((b, seq, RET_HEADS * RET_V), BF16),
        grid=(b, RET_HEADS),
        in_specs=[
            pl.BlockSpec(memory_space=pltpu.SMEM),
            pl.BlockSpec((None, seq, RET_QK), lambda i, h: (i, 0, OFF_RQ // RET_QK + h)),
            pl.BlockSpec((None, seq, RET_QK), lambda i, h: (i, 0, OFF_RK // RET_QK + h)),
            pl.BlockSpec((None, seq, RET_V), lambda i, h: (i, 0, OFF_RV // RET_V + h)),
            pl.BlockSpec((None, seq, RET_V), lambda i, h: (i, 0, OFF_RG // RET_V + h)),
            pl.BlockSpec((seq, RET_QK), lambda i, h: (0, 0)),
            pl.BlockSpec((seq, RET_QK), lambda i, h: (0, 0)),
            pl.BlockSpec((None, 5, CHUNK, CHUNK), lambda i, h: (h, 0, 0, 0)),
            pl.BlockSpec((1, RET_V), lambda i, h: (0, h)),
        ],
        out_specs=pl.BlockSpec((None, seq, RET_V), lambda i, h: (i, 0, h)),
        scratch_shapes=[
            pltpu.VMEM((seq, RET_QK), F32),
            pltpu.VMEM((seq, RET_QK), F32),
            pltpu.VMEM((seq, RET_V), F32),
            pltpu.VMEM((RET_QK, RET_V), F32),
        ],
        compiler_params=_cp(("parallel", "parallel")),
        name="retention",
    )(decays, proj3, proj3, proj3, proj3, cos, sin, tabs, gain)


def _retention_tables(seq):
    half = RET_QK // 2
    inv = ROPE_BASE ** (-jnp.arange(half, dtype=F32) / half)
    ang = jnp.arange(seq, dtype=F32)[:, None] * inv[None, :]
    cos = jnp.concatenate([jnp.cos(ang), jnp.cos(ang)], axis=1)
    sin = jnp.concatenate([-jnp.sin(ang), jnp.sin(ang)], axis=1)
    heads = jnp.arange(RET_HEADS, dtype=F32)
    lg_f = jnp.log1p(-jnp.exp2(-DECAY_FWD - heads))[:, None, None]
    lg_b = jnp.log1p(-jnp.exp2(-DECAY_BWD - heads))[:, None, None]
    idx = jnp.arange(CHUNK, dtype=F32)
    diff = (idx[:, None] - idx[None, :])[None]
    dmat = jnp.where(diff >= 0, jnp.exp(jnp.where(diff >= 0, diff, 0.0) * lg_f),
                     jnp.exp(jnp.where(diff < 0, -diff, 0.0) * lg_b))
    col = jnp.broadcast_to(idx[None, :, None], (RET_HEADS, CHUNK, CHUNK))
    xi_f = jnp.exp((col + 1.0) * lg_f)
    zeta_f = jnp.exp((CHUNK - 1.0 - col) * lg_f)
    xi_b = jnp.exp((CHUNK - col) * lg_b)
    zeta_b = jnp.exp(col * lg_b)
    tabs = jnp.stack([dmat, xi_f, zeta_f, xi_b, zeta_b], axis=1)
    decays = jnp.stack([jnp.exp(CHUNK * lg_f[:, 0, 0]), jnp.exp(CHUNK * lg_b[:, 0, 0])], axis=1).reshape(-1)
    return cos, sin, tabs, decays


def _window_kernel(sink_ref, q_ref, k_ref, v_ref, qg_ref, kg_ref, bias_ref, o_ref, kp, vp):
    g = pl.program_id(1)
    seq = q_ref.shape[0]
    n_blocks = seq // WINDOW
    nt = (((1,), (1,)), ((), ()))
    zeros = jnp.zeros((WINDOW, ATT_D), BF16)
    kp[0:WINDOW, :] = zeros
    vp[0:WINDOW, :] = zeros
    kp[seq + WINDOW:seq + 2 * WINDOW, :] = zeros
    vp[seq + WINDOW:seq + 2 * WINDOW, :] = zeros

    def prep(n, carry):
        r0 = pl.multiple_of(n * WINDOW, WINDOW)
        dst = pl.ds(r0 + WINDOW, WINDOW)
        kp[dst, :] = _rms(k_ref[pl.ds(r0, WINDOW), :].astype(F32), kg_ref[...]).astype(BF16)
        vp[dst, :] = v_ref[pl.ds(r0, WINDOW), :]
        return carry

    lax.fori_loop(0, n_blocks, prep, 0)
    col = lax.broadcasted_iota(I32, (WINDOW, 3 * WINDOW), 1)

    def block(n, carry):
        r0 = pl.multiple_of(n * WINDOW, WINDOW)
        kb = kp[pl.ds(r0, 3 * WINDOW), :]
        vb = vp[pl.ds(r0, 3 * WINDOW), :]
        lo = jnp.where(n == 0, WINDOW, 0)
        hi = jnp.where(n == n_blocks - 1, 2 * WINDOW, 3 * WINDOW)
        outside = (col < lo) | (col >= hi)
        for r in range(ATT_GROUP):
            q = _rms(q_ref[pl.ds(r0, WINDOW), r * ATT_D:(r + 1) * ATT_D].astype(F32), qg_ref[...])
            s = lax.dot_general(q.astype(BF16), kb, nt, preferred_element_type=F32)
            s = s * (ATT_D ** -0.5) + bias_ref[r]
            s = jnp.where(outside, -jnp.inf, s)
            sink = sink_ref[g * ATT_GROUP + r]
            m = jnp.maximum(jnp.max(s, axis=-1, keepdims=True), sink)
            p = jnp.exp(s - m)
            denom = jnp.sum(p, axis=-1, keepdims=True) + jnp.exp(sink - m)
            o = jnp.dot(p.astype(BF16), vb, preferred_element_type=F32) / denom
            o_ref[pl.ds(r0, WINDOW), r * ATT_D:(r + 1) * ATT_D] = o.astype(BF16)
        return carry

    lax.fori_loop(0, n_blocks, block, 0)


def _window_attention(proj3, q_gain, k_gain, sink, bias):
    b, seq, _ = proj3.shape
    gw = ATT_GROUP * ATT_D
    return pl.pallas_call(
        _window_kernel,
        out_shape=jax.ShapeDtypeStruct((b, seq, ATT_HEADS * ATT_D), BF16),
        grid=(b, ATT_KV),
        in_specs=[
            pl.BlockSpec(memory_space=pltpu.SMEM),
            pl.BlockSpec((None, seq, gw), lambda i, g: (i, 0, OFF_AQ // gw + g)),
            pl.BlockSpec((None, seq, ATT_D), lambda i, g: (i, 0, OFF_AK // ATT_D + g)),
            pl.BlockSpec((None, seq, ATT_D), lambda i, g: (i, 0, OFF_AV // ATT_D + g)),
            pl.BlockSpec((1, ATT_D), lambda i, g: (0, 0)),
            pl.BlockSpec((1, ATT_D), lambda i, g: (0, 0)),
            pl.BlockSpec((ATT_GROUP, WINDOW, 3 * WINDOW), lambda i, g: (g, 0, 0)),
        ],
        out_specs=pl.BlockSpec((None, seq, gw), lambda i, g: (i, 0, g)),
        scratch_shapes=[
            pltpu.VMEM((seq + 2 * WINDOW, ATT_D), BF16),
            pltpu.VMEM((seq + 2 * WINDOW, ATT_D), BF16),
        ],
        compiler_params=_cp(("parallel", "parallel")),
        name="window_attention",
    )(sink, proj3, proj3, proj3, q_gain, k_gain, bias)


def _window_bias(rel_table):
    nb = REL_BUCKETS // 2
    max_exact = nb // 2
    qi = jnp.arange(WINDOW)[:, None]
    ki = jnp.arange(3 * WINDOW)[None, :]
    rel = ki - WINDOW - qi
    ret = jnp.where(rel > 0, nb, 0)
    n = jnp.abs(rel)
    large = max_exact + (jnp.log(jnp.maximum(n, 1).astype(F32) / max_exact)
                         / math.log(REL_MAX_DIST / max_exact) * (nb - max_exact)).astype(I32)
    large = jnp.minimum(large, nb - 1)
    bucket = ret + jnp.where(n < max_exact, n, large)
    bias = rel_table[bucket].astype(F32).transpose(2, 0, 1)
    return jnp.where((jnp.abs(rel) <= WINDOW)[None], bias, -jnp.inf)


def _mem_kv_kernel(m_ref, g_ref, w_ref, kg_ref, o_ref, h_scr):
    j = pl.program_id(1)

    @pl.when(j == 0)
    def _():
        h_scr[...] = _rms(m_ref[...], g_ref[...]).astype(BF16)

    o = jnp.dot(h_scr[...], w_ref[...], preferred_element_type=F32)
    o_ref[...] = jnp.where(j < MEM_HEADS, _rms(o, kg_ref[...]), o).astype(BF16)


def _mem_kv(mem2, gain, w_bf16, k_gain, m_len):
    rows = mem2.shape[0]
    n = w_bf16.shape[1]
    return pl.pallas_call(
        _mem_kv_kernel,
        out_shape=jax.ShapeDtypeStruct((rows, n), BF16),
        grid=(rows // m_len, n // MEM_D),
        in_specs=[
            pl.BlockSpec((m_len, D_MODEL), lambda i, j: (i, 0)),
            pl.BlockSpec((1, D_MODEL), lambda i, j: (0, 0)),
            pl.BlockSpec((D_MODEL, MEM_D), lambda i, j: (0, j)),
            pl.BlockSpec((1, MEM_D), lambda i, j: (0, 0)),
        ],
        out_specs=pl.BlockSpec((m_len, MEM_D), lambda i, j: (i, j)),
        scratch_shapes=[pltpu.VMEM((m_len, D_MODEL), BF16)],
        compiler_params=_cp(("parallel", "arbitrary")),
        name="mem_kv",
    )(mem2, gain, w_bf16, k_gain)


def _mem_attn_kernel(q_ref, k_ref, v_ref, qg_ref, o_ref, *, bq):
    seq = q_ref.shape[0]
    nt = (((1,), (1,)), ((), ()))

    def block(n, carry):
        rows = pl.ds(pl.multiple_of(n * bq, bq), bq)
        q = _rms(q_ref[rows, :].astype(F32), qg_ref[...])
        s = lax.dot_general(q.astype(BF16), k_ref[...], nt, preferred_element_type=F32) * (MEM_D ** -0.5)
        p = jnp.exp(s - jnp.max(s, axis=-1, keepdims=True))
        denom = jnp.sum(p, axis=-1, keepdims=True)
        o = jnp.dot(p.astype(BF16), v_ref[...], preferred_element_type=F32) / denom
        o_ref[rows, :] = o.astype(BF16)
        return carry

    lax.fori_loop(0, seq // bq, block, 0)


def _mem_attention(proj3, mkv3, q_gain, bq=256):
    b, seq, _ = proj3.shape
    m_len = mkv3.shape[1]
    return pl.pallas_call(
        functools.partial(_mem_attn_kernel, bq=bq),
        out_shape=jax.ShapeDtypeStruct((b, seq, MEM_HEADS * MEM_D), BF16),
        grid=(b, MEM_HEADS),
        in_specs=[
            pl.BlockSpec((None, seq, MEM_D), lambda i, h: (i, 0, OFF_MQ // MEM_D + h)),
            pl.BlockSpec((None, m_len, MEM_D), lambda i, h: (i, 0, h)),
            pl.BlockSpec((None, m_len, MEM_D), lambda i, h: (i, 0, MEM_HEADS + h)),
            pl.BlockSpec((1, MEM_D), lambda i, h: (0, 0)),
        ],
        out_specs=pl.BlockSpec((None, seq, MEM_D), lambda i, h: (i, 0, h)),
        compiler_params=_cp(("parallel", "parallel")),
        name="mem_attention",
    )(proj3, mkv3, mkv3, q_gain)


def _merge_kernel(x_ref, g0_ref, g1_ref, g2_ref, b0_ref, b1_ref, b2_ref, wb_ref, wo_ref, gn_ref,
                  wr_ref, br_ref, xo_ref, hp_ref, lg_ref):
    merged = None
    for gate_ref, br, i in ((g0_ref, b0_ref, 0), (g1_ref, b1_ref, 1), (g2_ref, b2_ref, 2)):
        t = jax.nn.sigmoid(gate_ref[...].astype(F32)) * jnp.dot(br[...], wb_ref[i],
                                                                 preferred_element_type=F32)
        merged = t if merged is None else merged + t
    x = x_ref[...] + jnp.dot(merged.astype(BF16), wo_ref[...], preferred_element_type=F32)
    xo_ref[...] = x
    h = _rms(x, gn_ref[...])
    lg_ref[...] = jnp.dot(h, wr_ref[...], preferred_element_type=F32,
                          precision=lax.Precision.HIGHEST) + br_ref[...]
    hp_ref[...] = _pack_bf16_pairs(h)


def _merge(x2, proj, ret, att, mo, wb, wo, gain, w_router, b_router, bm=512):
    t = x2.shape[0]
    const2 = lambda i: (0, 0)
    row = lambda i: (i, 0)
    return pl.pallas_call(
        _merge_kernel,
        out_shape=(
            jax.ShapeDtypeStruct((t, D_MODEL), F32),
            jax.ShapeDtypeStruct((t, HALF), I32),
            jax.ShapeDtypeStruct((t, N_EXPERTS), F32),
        ),
        grid=(t // bm,),
        in_specs=[
            pl.BlockSpec((bm, D_MODEL), row),
            pl.BlockSpec((bm, D_MODEL), lambda i: (i, 0)),
            pl.BlockSpec((bm, D_MODEL), lambda i: (i, 1)),
            pl.BlockSpec((bm, D_MODEL), lambda i: (i, 2)),
            pl.BlockSpec((bm, D_MODEL), row),
            pl.BlockSpec((bm, D_MODEL), row),
            pl.BlockSpec((bm, D_MODEL), row),
            pl.BlockSpec((3, D_MODEL, D_MODEL), lambda i: (0, 0, 0)),
            pl.BlockSpec((D_MODEL, D_MODEL), const2),
            pl.BlockSpec((1, D_MODEL), const2),
            pl.BlockSpec((D_MODEL, N_EXPERTS), const2),
            pl.BlockSpec((1, N_EXPERTS), const2),
        ],
        out_specs=(
            pl.BlockSpec((bm, D_MODEL), row),
            pl.BlockSpec((bm, HALF), row),
            pl.BlockSpec((bm, N_EXPERTS), row),
        ),
        compiler_params=_cp(("parallel",)),
        name="merge_router",
    )(x2, proj, proj, proj, ret, att, mo, wb, wo, gain, w_router, b_router)


def _route_kernel(l_ref, tri_ref, eid_ref, rank_ref, gate_ref, cnt_ref):
    @pl.when(pl.program_id(0) == 0)
    def _():
        cnt_ref[...] = jnp.zeros_like(cnt_ref)

    l = l_ref[...]
    ne, bt = l.shape
    iota_e = lax.broadcasted_iota(I32, (ne, bt), 0)
    picked = jnp.zeros((ne, bt), jnp.bool_)
    vals, idxs = [], []
    for _ in range(TOP_K):
        m = jnp.max(l, axis=0, keepdims=True)
        idx = jnp.min(jnp.where(l == m, iota_e, ne), axis=0, keepdims=True)
        sel = iota_e == idx
        picked = picked | sel
        l = jnp.where(sel, -jnp.inf, l)
        vals.append(m)
        idxs.append(idx)
    ex = [jnp.exp(v - vals[0]) for v in vals]
    tot = ex[0] + ex[1] + ex[2] + ex[3]
    onehot = jnp.where(picked, 1.0, 0.0)
    before = jnp.dot(onehot.astype(BF16), tri_ref[...], preferred_element_type=F32) + cnt_ref[:, 0:1]
    for k in range(TOP_K):
        eid_ref[k:k + 1, :] = idxs[k]
        gate_ref[k:k + 1, :] = ex[k] / tot
        rank_ref[k:k + 1, :] = jnp.sum(jnp.where(iota_e == idxs[k], before, 0.0), axis=0,
                                       keepdims=True).astype(I32)
    cnt_ref[...] = cnt_ref[...] + jnp.sum(onehot, axis=1, keepdims=True)


def _route(logits_t, tri, bt=512):
    ne, t = logits_t.shape
    blk = lambda i: (0, i)
    return pl.pallas_call(
        _route_kernel,
        out_shape=(
            jax.ShapeDtypeStruct((TOP_K, t), I32),
            jax.ShapeDtypeStruct((TOP_K, t), I32),
            jax.ShapeDtypeStruct((TOP_K, t), F32),
            jax.ShapeDtypeStruct((ne, 128), F32),
        ),
        grid=(t // bt,),
        in_specs=[pl.BlockSpec((ne, bt), blk), pl.BlockSpec((bt, bt), lambda i: (0, 0))],
        out_specs=(
            pl.BlockSpec((TOP_K, bt), blk),
            pl.BlockSpec((TOP_K, bt), blk),
            pl.BlockSpec((TOP_K, bt), blk),
            pl.BlockSpec((ne, 128), lambda i: (0, 0)),
        ),
        compiler_params=_cp(("arbitrary",)),
        name="route_topk",
    )(logits_t, tri)


def _slot_kernel(start_ref, eid_ref, rank_ref, slot_ref):
    eid = eid_ref[...]
    base = jnp.zeros(eid.shape, I32)
    for e in range(N_EXPERTS):
        base = jnp.where(eid == e, start_ref[e], base)
    slot_ref[...] = base + rank_ref[...]


def _slots(pad_starts, eid, rank, bt=2048):
    t = eid.shape[1]
    blk = lambda i, s: (0, i)
    return pl.pallas_call(
        _slot_kernel,
        out_shape=jax.ShapeDtypeStruct((TOP_K, t), I32),
        grid_spec=pltpu.PrefetchScalarGridSpec(
            num_scalar_prefetch=1,
            grid=(t // bt,),
            in_specs=[pl.BlockSpec((TOP_K, bt), blk), pl.BlockSpec((TOP_K, bt), blk)],
            out_specs=pl.BlockSpec((TOP_K, bt), blk),
        ),
        compiler_params=_cp(("parallel",)),
        name="route_slots",
    )(pad_starts, eid, rank)


def _sc_workers():
    info = plsc.get_sparse_core_info()
    return info.num_cores, info.num_subcores


def _dispatch_rows(hp, slot, n_rows, win=64):
    t, w = hp.shape
    nc, ns = _sc_workers()
    per_worker = t // (nc * ns)
    mesh = plsc.VectorSubcoreMesh(core_axis_name="c", subcore_axis_name="s")

    @functools.partial(
        pl.kernel,
        out_type=jax.ShapeDtypeStruct((n_rows, w), hp.dtype),
        mesh=mesh,
        scratch_types=[pltpu.VMEM((win,), I32)] * TOP_K + [pltpu.VMEM((win, w), hp.dtype)],
        name="moe_dispatch",
    )
    def k(hp_hbm, slot_hbm, buf_hbm, *scratch):
        idx_v, rows_v = scratch[:TOP_K], scratch[TOP_K]
        wid = lax.axis_index("s") * nc + lax.axis_index("c")
        base = wid * per_worker

        @pl.loop(0, per_worker // win)
        def _(i):
            off = pl.multiple_of(base + i * win, win)
            pltpu.sync_copy(hp_hbm.at[pl.ds(off, win)], rows_v)
            for kk in range(TOP_K):
                pltpu.sync_copy(slot_hbm.at[pl.ds(kk * t + off, win)], idx_v[kk])
                pltpu.sync_copy(rows_v, buf_hbm.at[idx_v[kk]])

    return k(hp, slot.reshape(-1))


def _gather_rows(out_rows, slot, win=64):
    t = slot.shape[1]
    w = out_rows.shape[1]
    nc, ns = _sc_workers()
    per_worker = t // (nc * ns)
    mesh = plsc.VectorSubcoreMesh(core_axis_name="c", subcore_axis_name="s")

    @functools.partial(
        pl.kernel,
        out_type=jax.ShapeDtypeStruct((TOP_K * t, w), out_rows.dtype),
        mesh=mesh,
        scratch_types=[pltpu.VMEM((win,), I32), pltpu.VMEM((win, w), out_rows.dtype)],
        name="moe_gather",
    )
    def k(rows_hbm, slot_hbm, og_hbm, idx_v, rows_v):
        wid = lax.axis_index("s") * nc + lax.axis_index("c")
        base = wid * per_worker

        @pl.loop(0, per_worker // win)
        def _(i):
            off = pl.multiple_of(base + i * win, win)
            for kk in range(TOP_K):
                pltpu.sync_copy(slot_hbm.at[pl.ds(kk * t + off, win)], idx_v)
                pltpu.sync_copy(rows_hbm.at[idx_v], rows_v)
                pltpu.sync_copy(rows_v, og_hbm.at[pl.ds(kk * t + off, win)])

    return k(out_rows, slot.reshape(-1)).reshape(TOP_K, t, w)


def _expert_kernel(be_ref, nu_ref, x_ref, wi_ref, bi_ref, wo_ref, bo_ref, o_ref):
    @pl.when(pl.program_id(0) < nu_ref[0])
    def _():
        hi, lo = _unpack_bf16_pairs(x_ref[...])
        h = (jnp.dot(hi.astype(BF16), wi_ref[0:HALF, :], preferred_element_type=F32)
             + jnp.dot(lo.astype(BF16), wi_ref[HALF:, :], preferred_element_type=F32) + bi_ref[...])
        gate = jnp.minimum(h[:, :D_FF], SWIGLU_LIMIT)
        lin = jnp.clip(h[:, D_FF:], -SWIGLU_LIMIT, SWIGLU_LIMIT)
        act = gate * jax.nn.sigmoid(SWIGLU_ALPHA * gate) * (lin + 1.0)
        o = jnp.dot(act.astype(BF16), wo_ref[...], preferred_element_type=F32) + bo_ref[...]
        o_ref[...] = _pack_bf16_pairs(o)


def _experts(block_expert, n_used, buf, w_in, b_in, w_out, b_out):
    n_rows, w = buf.shape
    n_blocks = n_rows // EXPERT_BLOCK
    return pl.pallas_call(
        _expert_kernel,
        out_shape=jax.ShapeDtypeStruct((n_rows, w), buf.dtype),
        grid_spec=pltpu.PrefetchScalarGridSpec(
            num_scalar_prefetch=2,
            grid=(n_blocks,),
            in_specs=[
                pl.BlockSpec((EXPERT_BLOCK, w), lambda i, be, nu: (i, 0)),
                pl.BlockSpec((None, D_MODEL, 2 * D_FF), lambda i, be, nu: (be[i], 0, 0)),
                pl.BlockSpec((None, 1, 2 * D_FF), lambda i, be, nu: (be[i], 0, 0)),
                pl.BlockSpec((None, D_FF, D_MODEL), lambda i, be, nu: (be[i], 0, 0)),
                pl.BlockSpec((None, 1, D_MODEL), lambda i, be, nu: (be[i], 0, 0)),
            ],
            out_specs=pl.BlockSpec((EXPERT_BLOCK, w), lambda i, be, nu: (i, 0)),
        ),
        compiler_params=_cp(("arbitrary",)),
        name="expert_ffn",
    )(block_expert, n_used, buf, w_in, b_in, w_out, b_out)


def _combine_kernel(x_ref, og_ref, g_ref, y_ref):
    g = g_ref[...]
    acc_hi = x_ref[:, :HALF]
    acc_lo = x_ref[:, HALF:]
    for k in range(TOP_K):
        hi, lo = _unpack_bf16_pairs(og_ref[k])
        gk = g[:, k:k + 1]
        acc_hi = acc_hi + gk * hi
        acc_lo = acc_lo + gk * lo
    y_ref[:, :HALF] = acc_hi
    y_ref[:, HALF:] = acc_lo


def _combine(x_mid, og, gates_tk, bm=512):
    t = x_mid.shape[0]
    return pl.pallas_call(
        _combine_kernel,
        out_shape=jax.ShapeDtypeStruct((t, D_MODEL), F32),
        grid=(t // bm,),
        in_specs=[
            pl.BlockSpec((bm, D_MODEL), lambda i: (i, 0)),
            pl.BlockSpec((TOP_K, bm, HALF), lambda i: (0, i, 0)),
            pl.BlockSpec((bm, TOP_K), lambda i: (i, 0)),
        ],
        out_specs=pl.BlockSpec((bm, D_MODEL), lambda i: (i, 0)),
        compiler_params=_cp(("parallel",)),
        name="moe_combine",
    )(x_mid, og, gates_tk)


def _block_tables(counts, n_blocks):
    padded = (counts + EXPERT_BLOCK - 1) // EXPERT_BLOCK * EXPERT_BLOCK
    pad_ends = jnp.cumsum(padded)
    pad_starts = pad_ends - padded
    block_start = jnp.arange(n_blocks, dtype=I32) * EXPERT_BLOCK
    block_expert = jnp.minimum(jnp.searchsorted(pad_ends, block_start, side="right"), N_EXPERTS - 1)
    n_used = (pad_ends[-1] // EXPERT_BLOCK).reshape(1)
    return pad_starts.astype(I32), block_expert.astype(I32), n_used.astype(I32)


def _layer(x, mem, consts, params):
    b, seq, d = x.shape
    m_len = mem.shape[1]
    t = b * seq
    x2 = x.reshape(t, d)
    proj = _in_proj(x2, params["norm_mix"], params["w_in"])
    proj3 = proj.reshape(b, seq, IN_WIDTH)
    ret = _retention(proj3, params["ret_out_norm"], consts["cos"], consts["sin"], consts["tabs"],
                     consts["decays"])
    att = _window_attention(proj3, params["att_q_norm"], params["att_k_norm"], params["att_sink"],
                            consts["bias"])
    mkv = _mem_kv(mem.reshape(b * m_len, d), params["mem_norm"], params["w_mem_kv"],
                  params["mem_k_norm"], m_len)
    mo = _mem_attention(proj3, mkv.reshape(b, m_len, 2 * MEM_HEADS * MEM_D), params["mem_q_norm"])
    x_mid, hp, logits = _merge(x2, proj, ret.reshape(t, d), att.reshape(t, d), mo.reshape(t, d),
                               params["w_branch"], params["w_out"], params["norm_ffn"],
                               params["w_router"], params["b_router"])
    eid, rank, gates, counts = _route(logits.T, consts["tri"])
    n_blocks = t * TOP_K // EXPERT_BLOCK + N_EXPERTS
    pad_starts, block_expert, n_used = _block_tables(counts[:, 0].astype(I32), n_blocks)
    slot = _slots(pad_starts, eid, rank)
    buf = _dispatch_rows(hp, slot, n_blocks * EXPERT_BLOCK)
    out_rows = _experts(block_expert, n_used, buf, params["w_e_in"], params["b_e_in"],
                        params["w_e_out"], params["b_e_out"])
    og = _gather_rows(out_rows, slot)
    y = _combine(x_mid, og, gates.T)
    return y.reshape(b, seq, d)


def _prepare(seq, rel_table, norm_mix, w_in, ret_out_norm, att_q_norm, att_k_norm, att_sink, mem_norm,
             w_mem_kv, mem_q_norm, mem_k_norm, w_branch, w_out, norm_ffn, w_router, b_router,
             w_e_in, b_e_in, w_e_out, b_e_out):
    gates_at = IN_WIDTH - 3 * D_MODEL
    w_in0 = w_in[0]
    params = {
        "norm_mix": norm_mix[0].reshape(1, -1),
        "w_in": jnp.concatenate([w_in0[:, gates_at:], w_in0[:, :gates_at]], axis=1).astype(BF16),
        "ret_out_norm": ret_out_norm[0].reshape(1, -1),
        "att_q_norm": att_q_norm[0].reshape(1, -1),
        "att_k_norm": att_k_norm[0].reshape(1, -1),
        "att_sink": att_sink[0].astype(F32),
        "mem_norm": mem_norm[0].reshape(1, -1),
        "w_mem_kv": w_mem_kv[0].astype(BF16),
        "mem_q_norm": mem_q_norm[0].reshape(1, -1),
        "mem_k_norm": mem_k_norm[0].reshape(1, -1),
        "w_branch": w_branch[0].astype(BF16),
        "w_out": w_out[0].astype(BF16),
        "norm_ffn": norm_ffn[0].reshape(1, -1),
        "w_router": w_router[0],
        "b_router": b_router[0].reshape(1, -1),
        "w_e_in": w_e_in[0].astype(BF16),
        "b_e_in": b_e_in[0].reshape(N_EXPERTS, 1, -1),
        "w_e_out": w_e_out[0].astype(BF16),
        "b_e_out": b_e_out[0].reshape(N_EXPERTS, 1, -1),
    }
    cos, sin, tabs, decays = _retention_tables(seq)
    route_bt = 512
    tri = (jnp.arange(route_bt)[:, None] < jnp.arange(route_bt)[None, :]).astype(BF16)
    consts = {"cos": cos, "sin": sin, "tabs": tabs, "decays": decays,
              "bias": _window_bias(rel_table), "tri": tri}
    return consts, params


def kernel(x_prompt, x_sample, mem_prompt, mem_sample, rel_table, norm_mix, w_in, ret_out_norm, att_q_norm, att_k_norm, att_sink, mem_norm, w_mem_kv, mem_q_norm, mem_k_norm, w_branch, w_out, norm_ffn, w_router, b_router, w_e_in, b_e_in, w_e_out, b_e_out):
    consts, params = _prepare(x_prompt.shape[1], rel_table, norm_mix, w_in, ret_out_norm, att_q_norm,
                              att_k_norm, att_sink, mem_norm, w_mem_kv, mem_q_norm, mem_k_norm,
                              w_branch, w_out, norm_ffn, w_router, b_router, w_e_in, b_e_in,
                              w_e_out, b_e_out)
    y_prompt = _layer(x_prompt, mem_prompt, consts, params)
    y_sample = _layer(x_sample, mem_sample, consts, params)
    return (y_prompt, y_sample)
```

```python
import functools
import math

import jax
import jax.numpy as jnp
import numpy as np
from jax import lax
from jax.experimental import pallas as pl
from jax.experimental.pallas import tpu as pltpu
from jax.experimental.pallas import tpu_sc as plsc

F32 = jnp.float32
BF16 = jnp.bfloat16
I32 = jnp.int32

D_MODEL = 1024
EPS = 1e-6
RET_HEADS = 4
RET_QK = 128
RET_V = 256
CHUNK = 128
ROPE_BASE = 10000.0
DECAY_FWD = 5.0
DECAY_BWD = 5.5
ATT_HEADS = 8
ATT_KV = 2
ATT_GROUP = ATT_HEADS // ATT_KV
ATT_D = 128
WINDOW = 128
REL_BUCKETS = 32
REL_MAX_DIST = 128
MEM_HEADS = 4
MEM_D = 256
N_EXPERTS = 32
TOP_K = 4
D_FF = 1024
SWIGLU_LIMIT = 7.0
SWIGLU_ALPHA = 1.702
EXPERT_BLOCK = 512
HALF = D_MODEL // 2

IN_WIDTH = 8704
OFF_GATES = 0
OFF_RQ = 3072
OFF_RK = 3584
OFF_RV = 4096
OFF_RG = 5120
OFF_AQ = 6144
OFF_AK = 7168
OFF_AV = 7424
OFF_MQ = 7680

VMEM_LIMIT = 56 * 1024 * 1024


def _cp(sem, vmem=VMEM_LIMIT):
    return pltpu.CompilerParams(dimension_semantics=sem, vmem_limit_bytes=vmem)


def _rms(x, g):
    return x * lax.rsqrt(jnp.mean(x * x, axis=-1, keepdims=True) + EPS) * g


def _pack_bf16_pairs(x):
    w = x.shape[1] // 2
    bits = pltpu.bitcast(x.astype(BF16).astype(F32), I32)
    hi = bits[:, :w] & jnp.int32(-65536)
    lo = lax.shift_right_logical(bits[:, w:], jnp.int32(16))
    return hi | lo


def _unpack_bf16_pairs(u):
    hi = pltpu.bitcast(u & jnp.int32(-65536), F32)
    lo = pltpu.bitcast(lax.shift_left(u, jnp.int32(16)), F32)
    return hi, lo


def _in_proj_kernel(x_ref, g_ref, w_ref, o_ref, h_scr):
    @pl.when(pl.program_id(1) == 0)
    def _():
        h_scr[...] = _rms(x_ref[...], g_ref[...]).astype(BF16)

    o_ref[...] = jnp.dot(h_scr[...], w_ref[...], preferred_element_type=F32).astype(BF16)


def _in_proj(x2, gain, w_bf16, bm=1024, bn=512):
    t = x2.shape[0]
    n = w_bf16.shape[1]
    return pl.pallas_call(
        _in_proj_kernel,
        out_shape=jax.ShapeDtypeStruct((t, n), BF16),
        grid=(t // bm, n // bn),
        in_specs=[
            pl.BlockSpec((bm, D_MODEL), lambda i, j: (i, 0)),
            pl.BlockSpec((1, D_MODEL), lambda i, j: (0, 0)),
            pl.BlockSpec((D_MODEL, bn), lambda i, j: (0, j)),
        ],
        out_specs=pl.BlockSpec((bm, bn), lambda i, j: (i, j)),
        scratch_shapes=[pltpu.VMEM((bm, D_MODEL), BF16)],
        compiler_params=_cp(("parallel", "arbitrary")),
        name="in_proj",
    )(x2, gain, w_bf16)


def _retention_kernel(dec_ref, q_ref, k_ref, v_ref, g_ref, cos_ref, sin_ref, tab_ref, gain_ref,
                      o_ref, qs, ks, acc, st):
    h = pl.program_id(1)
    seq = q_ref.shape[0]
    n_chunks = seq // CHUNK
    dec_f = dec_ref[2 * h]
    dec_b = dec_ref[2 * h + 1]
    nt = (((1,), (1,)), ((), ()))
    tn = (((0,), (0,)), ((), ()))

    def rope(x, c, s):
        return x * c + pltpu.roll(x, RET_QK // 2, 1) * s

    st[...] = jnp.zeros_like(st)

    def fwd(n, carry):
        r0 = pl.multiple_of(n * CHUNK, CHUNK)
        rows = pl.ds(r0, CHUNK)
        c = cos_ref[rows, :]
        s = sin_ref[rows, :]
        q = rope(q_ref[rows, :].astype(F32), c, s)
        k = rope(k_ref[rows, :].astype(F32), c, s) * (RET_QK ** -0.5)
        qs[rows, :] = q
        ks[rows, :] = k
        v = v_ref[rows, :]
        sc = lax.dot_general(q.astype(BF16), k.astype(BF16), nt, preferred_element_type=F32)
        intra = jnp.dot((sc * tab_ref[0]).astype(BF16), v, preferred_element_type=F32)
        state = st[...]
        cross = jnp.dot((q * tab_ref[1]).astype(BF16), state.astype(BF16), preferred_element_type=F32)
        acc[rows, :] = intra + cross
        st[...] = state * dec_f + lax.dot_general((k * tab_ref[2]).astype(BF16), v, tn,
                                                  preferred_element_type=F32)
        return carry

    lax.fori_loop(0, n_chunks, fwd, 0)
    st[...] = jnp.zeros_like(st)

    def bwd(i, carry):
        n = n_chunks - 1 - i
        r0 = pl.multiple_of(n * CHUNK, CHUNK)
        rows = pl.ds(r0, CHUNK)
        q = qs[rows, :]
        k = ks[rows, :]
        v = v_ref[rows, :]
        state = st[...]
        cross = jnp.dot((q * tab_ref[3]).astype(BF16), state.astype(BF16), preferred_element_type=F32)
        o = acc[rows, :] + cross
        st[...] = state * dec_b + lax.dot_general((k * tab_ref[4]).astype(BF16), v, tn,
                                                  preferred_element_type=F32)
        y = _rms(o, gain_ref[...])
        g = g_ref[rows, :].astype(F32)
        o_ref[rows, :] = (y * (g * jax.nn.sigmoid(g))).astype(BF16)
        return carry

    lax.fori_loop(0, n_chunks, bwd, 0)


def _retention(proj3, gain, cos, sin, tabs, decays):
    b, seq, _ = proj3.shape
    return pl.pallas_call(
        _retention_kernel,
        out_shape=jax.ShapeDtypeStruct((b, seq, RET_HEADS * RET_V), BF16),
        grid=(b, RET_HEADS),
        in_specs=[
            pl.BlockSpec(memory_space=pltpu.SMEM),
            pl.BlockSpec((None, seq, RET_QK), lambda i, h: (i, 0, OFF_RQ // RET_QK + h)),
            pl.BlockSpec((None, seq, RET_QK), lambda i, h: (i, 0, OFF_RK // RET_QK + h)),
            pl.BlockSpec((None, seq, RET_V), lambda i, h: (i, 0, OFF_RV // RET_V + h)),
            pl.BlockSpec((None, seq, RET_V), lambda i, h: (i, 0, OFF_RG // RET_V + h)),
            pl.BlockSpec((seq, RET_QK), lambda i, h: (0, 0)),
            pl.BlockSpec((seq, RET_QK), lambda i, h: (0, 0)),
            pl.BlockSpec((None, 5, CHUNK, CHUNK), lambda i, h: (h, 0, 0, 0)),
            pl.BlockSpec((1, RET_V), lambda i, h: (0, h)),
        ],
        out_specs=pl.BlockSpec((None, seq, RET_V), lambda i, h: (i, 0, h)),
        scratch_shapes=[
            pltpu.VMEM((seq, RET_QK), F32),
            pltpu.VMEM((seq, RET_QK), F32),
            pltpu.VMEM((seq, RET_V), F32),
            pltpu.VMEM((RET_QK, RET_V), F32),
        ],
        compiler_params=_cp(("parallel", "parallel")),
        name="retention",
    )(decays, proj3, proj3, proj3, proj3, cos, sin, tabs, gain)


def _retention_tables(seq):
    half = RET_QK // 2
    inv = ROPE_BASE ** (-jnp.arange(half, dtype=F32) / half)
    ang = jnp.arange(seq, dtype=F32)[:, None] * inv[None, :]
    cos = jnp.concatenate([jnp.cos(ang), jnp.cos(ang)], axis=1)
    sin = jnp.concatenate([-jnp.sin(ang), jnp.sin(ang)], axis=1)
    heads = jnp.arange(RET_HEADS, dtype=F32)
    lg_f = jnp.log1p(-jnp.exp2(-DECAY_FWD - heads))[:, None, None]
    lg_b = jnp.log1p(-jnp.exp2(-DECAY_BWD - heads))[:, None, None]
    idx = jnp.arange(CHUNK, dtype=F32)
    diff = (idx[:, None] - idx[None, :])[None]
    dmat = jnp.where(diff >= 0, jnp.exp(jnp.where(diff >= 0, diff, 0.0) * lg_f),
                     jnp.exp(jnp.where(diff < 0, -diff, 0.0) * lg_b))
    col = jnp.broadcast_to(idx[None, :, None], (RET_HEADS, CHUNK, CHUNK))
    xi_f = jnp.exp((col + 1.0) * lg_f)
    zeta_f = jnp.exp((CHUNK - 1.0 - col) * lg_f)
    xi_b = jnp.exp((CHUNK - col) * lg_b)
    zeta_b = jnp.exp(col * lg_b)
    tabs = jnp.stack([dmat, xi_f, zeta_f, xi_b, zeta_b], axis=1)
    decays = jnp.stack([jnp.exp(CHUNK * lg_f[:, 0, 0]), jnp.exp(CHUNK * lg_b[:, 0, 0])], axis=1).reshape(-1)
    return cos, sin, tabs, decays


def _window_kernel(sink_ref, q_ref, k_ref, v_ref, qg_ref, kg_ref, bias_ref, o_ref, kp, vp):
    g = pl.program_id(1)
    seq = q_ref.shape[0]
    n_blocks = seq // WINDOW
    nt = (((1,), (1,)), ((), ()))
    zeros = jnp.zeros((WINDOW, ATT_D), BF16)
    kp[0:WINDOW, :] = zeros
    vp[0:WINDOW, :] = zeros
    kp[seq + WINDOW:seq + 2 * WINDOW, :] = zeros
    vp[seq + WINDOW:seq + 2 * WINDOW, :] = zeros

    def prep(n, carry):
        r0 = pl.multiple_of(n * WINDOW, WINDOW)
        dst = pl.ds(r0 + WINDOW, WINDOW)
        kp[dst, :] = _rms(k_ref[pl.ds(r0, WINDOW), :].astype(F32), kg_ref[...]).astype(BF16)
        vp[dst, :] = v_ref[pl.ds(r0, WINDOW), :]
        return carry

    lax.fori_loop(0, n_blocks, prep, 0)
    col = lax.broadcasted_iota(I32, (WINDOW, 3 * WINDOW), 1)

    def block(n, carry):
        r0 = pl.multiple_of(n * WINDOW, WINDOW)
        kb = kp[pl.ds(r0, 3 * WINDOW), :]
        vb = vp[pl.ds(r0, 3 * WINDOW), :]
        lo = jnp.where(n == 0, WINDOW, 0)
        hi = jnp.where(n == n_blocks - 1, 2 * WINDOW, 3 * WINDOW)
        outside = (col < lo) | (col >= hi)
        for r in range(ATT_GROUP):
            q = _rms(q_ref[pl.ds(r0, WINDOW), r * ATT_D:(r + 1) * ATT_D].astype(F32), qg_ref[...])
            s = lax.dot_general(q.astype(BF16), kb, nt, preferred_element_type=F32)
            s = s * (ATT_D ** -0.5) + bias_ref[r]
            s = jnp.where(outside, -jnp.inf, s)
            sink = sink_ref[g * ATT_GROUP + r]
            m = jnp.maximum(jnp.max(s, axis=-1, keepdims=True), sink)
            p = jnp.exp(s - m)
            denom = jnp.sum(p, axis=-1, keepdims=True) + jnp.exp(sink - m)
            o = jnp.dot(p.astype(BF16), vb, preferred_element_type=F32) / denom
            o_ref[pl.ds(r0, WINDOW), r * ATT_D:(r + 1) * ATT_D] = o.astype(BF16)
        return carry

    lax.fori_loop(0, n_blocks, block, 0)


def _window_attention(proj3, q_gain, k_gain, sink, bias):
    b, seq, _ = proj3.shape
    gw = ATT_GROUP * ATT_D
    return pl.pallas_call(
        _window_kernel,
        out_shape=jax.ShapeDtypeStruct((b, seq, ATT_HEADS * ATT_D), BF16),
        grid=(b, ATT_KV),
        in_specs=[
            pl.BlockSpec(memory_space=pltpu.SMEM),
            pl.BlockSpec((None, seq, gw), lambda i, g: (i, 0, OFF_AQ // gw + g)),
            pl.BlockSpec((None, seq, ATT_D), lambda i, g: (i, 0, OFF_AK // ATT_D + g)),
            pl.BlockSpec((None, seq, ATT_D), lambda i, g: (i, 0, OFF_AV // ATT_D + g)),
            pl.BlockSpec((1, ATT_D), lambda i, g: (0, 0)),
            pl.BlockSpec((1, ATT_D), lambda i, g: (0, 0)),
            pl.BlockSpec((ATT_GROUP, WINDOW, 3 * WINDOW), lambda i, g: (g, 0, 0)),
        ],
        out_specs=pl.BlockSpec((None, seq, gw), lambda i, g: (i, 0, g)),
        scratch_shapes=[
            pltpu.VMEM((seq + 2 * WINDOW, ATT_D), BF16),
            pltpu.VMEM((seq + 2 * WINDOW, ATT_D), BF16),
        ],
        compiler_params=_cp(("parallel", "parallel")),
        name="window_attention",
    )(sink, proj3, proj3, proj3, q_gain, k_gain, bias)


def _window_bias(rel_table):
    nb = REL_BUCKETS // 2
    max_exact = nb // 2
    qi = jnp.arange(WINDOW)[:, None]
    ki = jnp.arange(3 * WINDOW)[None, :]
    rel = ki - WINDOW - qi
    ret = jnp.where(rel > 0, nb, 0)
    n = jnp.abs(rel)
    large = max_exact + (jnp.log(jnp.maximum(n, 1).astype(F32) / max_exact)
                         / math.log(REL_MAX_DIST / max_exact) * (nb - max_exact)).astype(I32)
    large = jnp.minimum(large, nb - 1)
    bucket = ret + jnp.where(n < max_exact, n, large)
    onehot = (bucket[None] == jnp.arange(REL_BUCKETS)[:, None, None]).astype(F32)
    bias = jnp.einsum("bh,bqk->hqk", rel_table.astype(F32), onehot, precision=lax.Precision.HIGHEST)
    return jnp.where((jnp.abs(rel) <= WINDOW)[None], bias, -jnp.inf)


def _mem_kv_kernel(m_ref, g_ref, w_ref, kg_ref, o_ref, h_scr):
    j = pl.program_id(1)

    @pl.when(j == 0)
    def _():
        h_scr[...] = _rms(m_ref[...], g_ref[...]).astype(BF16)

    o = jnp.dot(h_scr[...], w_ref[...], preferred_element_type=F32)
    o_ref[...] = jnp.where(j < MEM_HEADS, _rms(o, kg_ref[...]), o).astype(BF16)


def _mem_kv(mem2, gain, w_bf16, k_gain, m_len):
    rows = mem2.shape[0]
    n = w_bf16.shape[1]
    return pl.pallas_call(
        _mem_kv_kernel,
        out_shape=jax.ShapeDtypeStruct((rows, n), BF16),
        grid=(rows // m_len, n // MEM_D),
        in_specs=[
            pl.BlockSpec((m_len, D_MODEL), lambda i, j: (i, 0)),
            pl.BlockSpec((1, D_MODEL), lambda i, j: (0, 0)),
            pl.BlockSpec((D_MODEL, MEM_D), lambda i, j: (0, j)),
            pl.BlockSpec((1, MEM_D), lambda i, j: (0, 0)),
        ],
        out_specs=pl.BlockSpec((m_len, MEM_D), lambda i, j: (i, j)),
        scratch_shapes=[pltpu.VMEM((m_len, D_MODEL), BF16)],
        compiler_params=_cp(("parallel", "arbitrary")),
        name="mem_kv",
    )(mem2, gain, w_bf16, k_gain)


def _mem_attn_kernel(q_ref, k_ref, v_ref, qg_ref, o_ref, *, bq):
    seq = q_ref.shape[0]
    nt = (((1,), (1,)), ((), ()))

    def block(n, carry):
        rows = pl.ds(pl.multiple_of(n * bq, bq), bq)
        q = _rms(q_ref[rows, :].astype(F32), qg_ref[...])
        s = lax.dot_general(q.astype(BF16), k_ref[...], nt, preferred_element_type=F32) * (MEM_D ** -0.5)
        p = jnp.exp(s - jnp.max(s, axis=-1, keepdims=True))
        denom = jnp.sum(p, axis=-1, keepdims=True)
        o = jnp.dot(p.astype(BF16), v_ref[...], preferred_element_type=F32) / denom
        o_ref[rows, :] = o.astype(BF16)
        return carry

    lax.fori_loop(0, seq // bq, block, 0)


def _mem_attention(proj3, mkv3, q_gain, bq=256):
    b, seq, _ = proj3.shape
    m_len = mkv3.shape[1]
    return pl.pallas_call(
        functools.partial(_mem_attn_kernel, bq=bq),
        out_shape=jax.ShapeDtypeStruct((b, seq, MEM_HEADS * MEM_D), BF16),
        grid=(b, MEM_HEADS),
        in_specs=[
            pl.BlockSpec((None, seq, MEM_D), lambda i, h: (i, 0, OFF_MQ // MEM_D + h)),
            pl.BlockSpec((None, m_len, MEM_D), lambda i, h: (i, 0, h)),
            pl.BlockSpec((None, m_len, MEM_D), lambda i, h: (i, 0, MEM_HEADS + h)),
            pl.BlockSpec((1, MEM_D), lambda i, h: (0, 0)),
        ],
        out_specs=pl.BlockSpec((None, seq, MEM_D), lambda i, h: (i, 0, h)),
        compiler_params=_cp(("parallel", "parallel")),
        name="mem_attention",
    )(proj3, mkv3, mkv3, q_gain)


def _merge_kernel(x_ref, g0_ref, g1_ref, g2_ref, b0_ref, b1_ref, b2_ref, wb_ref, wo_ref, gn_ref,
                  wr_ref, br_ref, xo_ref, hp_ref, lg_ref):
    merged = None
    for gate_ref, br, i in ((g0_ref, b0_ref, 0), (g1_ref, b1_ref, 1), (g2_ref, b2_ref, 2)):
        t = jax.nn.sigmoid(gate_ref[...].astype(F32)) * jnp.dot(br[...], wb_ref[i],
                                                                 preferred_element_type=F32)
        merged = t if merged is None else merged + t
    x = x_ref[...] + jnp.dot(merged.astype(BF16), wo_ref[...], preferred_element_type=F32)
    xo_ref[...] = x
    h = _rms(x, gn_ref[...])
    lg_ref[...] = jnp.dot(h, wr_ref[...], preferred_element_type=F32,
                          precision=lax.Precision.HIGHEST) + br_ref[...]
    hp_ref[...] = _pack_bf16_pairs(h)


def _merge(x2, proj, ret, att, mo, wb, wo, gain, w_router, b_router, bm=512):
    t = x2.shape[0]
    const2 = lambda i: (0, 0)
    row = lambda i: (i, 0)
    return pl.pallas_call(
        _merge_kernel,
        out_shape=(
            jax.ShapeDtypeStruct((t, D_MODEL), F32),
            jax.ShapeDtypeStruct((t, HALF), I32),
            jax.ShapeDtypeStruct((t, N_EXPERTS), F32),
        ),
        grid=(t // bm,),
        in_specs=[
            pl.BlockSpec((bm, D_MODEL), row),
            pl.BlockSpec((bm, D_MODEL), lambda i: (i, 0)),
            pl.BlockSpec((bm, D_MODEL), lambda i: (i, 1)),
            pl.BlockSpec((bm, D_MODEL), lambda i: (i, 2)),
            pl.BlockSpec((bm, D_MODEL), row),
            pl.BlockSpec((bm, D_MODEL), row),
            pl.BlockSpec((bm, D_MODEL), row),
            pl.BlockSpec((3, D_MODEL, D_MODEL), lambda i: (0, 0, 0)),
            pl.BlockSpec((D_MODEL, D_MODEL), const2),
            pl.BlockSpec((1, D_MODEL), const2),
            pl.BlockSpec((D_MODEL, N_EXPERTS), const2),
            pl.BlockSpec((1, N_EXPERTS), const2),
        ],
        out_specs=(
            pl.BlockSpec((bm, D_MODEL), row),
            pl.BlockSpec((bm, HALF), row),
            pl.BlockSpec((bm, N_EXPERTS), row),
        ),
        compiler_params=_cp(("parallel",)),
        name="merge_router",
    )(x2, proj, proj, proj, ret, att, mo, wb, wo, gain, w_router, b_router)


def _route_kernel(l_ref, tri_ref, eid_ref, rank_ref, gate_ref, cnt_ref):
    @pl.when(pl.program_id(0) == 0)
    def _():
        cnt_ref[...] = jnp.zeros_like(cnt_ref)

    l = l_ref[...]
    ne, bt = l.shape
    iota_e = lax.broadcasted_iota(I32, (ne, bt), 0)
    picked = jnp.zeros((ne, bt), jnp.bool_)
    vals, idxs = [], []
    for _ in range(TOP_K):
        m = jnp.max(l, axis=0, keepdims=True)
        idx = jnp.min(jnp.where(l == m, iota_e, ne), axis=0, keepdims=True)
        sel = iota_e == idx
        picked = picked | sel
        l = jnp.where(sel, -jnp.inf, l)
        vals.append(m)
        idxs.append(idx)
    ex = [jnp.exp(v - vals[0]) for v in vals]
    tot = ex[0] + ex[1] + ex[2] + ex[3]
    onehot = jnp.where(picked, 1.0, 0.0)
    before = jnp.dot(onehot.astype(BF16), tri_ref[...], preferred_element_type=F32) + cnt_ref[:, 0:1]
    for k in range(TOP_K):
        eid_ref[k:k + 1, :] = idxs[k]
        gate_ref[k:k + 1, :] = ex[k] / tot
        rank_ref[k:k + 1, :] = jnp.sum(jnp.where(iota_e == idxs[k], before, 0.0), axis=0,
                                       keepdims=True).astype(I32)
    cnt_ref[...] = cnt_ref[...] + jnp.sum(onehot, axis=1, keepdims=True)


def _route(logits_t, tri, bt=512):
    ne, t = logits_t.shape
    blk = lambda i: (0, i)
    return pl.pallas_call(
        _route_kernel,
        out_shape=(
            jax.ShapeDtypeStruct((TOP_K, t), I32),
            jax.ShapeDtypeStruct((TOP_K, t), I32),
            jax.ShapeDtypeStruct((TOP_K, t), F32),
            jax.ShapeDtypeStruct((ne, 128), F32),
        ),
        grid=(t // bt,),
        in_specs=[pl.BlockSpec((ne, bt), blk), pl.BlockSpec((bt, bt), lambda i: (0, 0))],
        out_specs=(
            pl.BlockSpec((TOP_K, bt), blk),
            pl.BlockSpec((TOP_K, bt), blk),
            pl.BlockSpec((TOP_K, bt), blk),
            pl.BlockSpec((ne, 128), lambda i: (0, 0)),
        ),
        compiler_params=_cp(("arbitrary",)),
        name="route_topk",
    )(logits_t, tri)


def _slot_kernel(start_ref, eid_ref, rank_ref, slot_ref):
    eid = eid_ref[...]
    base = jnp.zeros(eid.shape, I32)
    for e in range(N_EXPERTS):
        base = jnp.where(eid == e, start_ref[e], base)
    slot_ref[...] = base + rank_ref[...]


def _slots(pad_starts, eid, rank, bt=2048):
    t = eid.shape[1]
    blk = lambda i, s: (0, i)
    return pl.pallas_call(
        _slot_kernel,
        out_shape=jax.ShapeDtypeStruct((TOP_K, t), I32),
        grid_spec=pltpu.PrefetchScalarGridSpec(
            num_scalar_prefetch=1,
            grid=(t // bt,),
            in_specs=[pl.BlockSpec((TOP_K, bt), blk), pl.BlockSpec((TOP_K, bt), blk)],
            out_specs=pl.BlockSpec((TOP_K, bt), blk),
        ),
        compiler_params=_cp(("parallel",)),
        name="route_slots",
    )(pad_starts, eid, rank)


def _sc_workers():
    info = plsc.get_sparse_core_info()
    return info.num_cores, info.num_subcores


def _dispatch_rows(hp, slot, n_rows, win=64):
    t, w = hp.shape
    nc, ns = _sc_workers()
    per_worker = t // (nc * ns)
    mesh = plsc.VectorSubcoreMesh(core_axis_name="c", subcore_axis_name="s")

    @functools.partial(
        pl.kernel,
        out_type=jax.ShapeDtypeStruct((n_rows, w), hp.dtype),
        mesh=mesh,
        scratch_types=[pltpu.VMEM((win,), I32)] * TOP_K + [pltpu.VMEM((win, w), hp.dtype)],
        name="moe_dispatch",
    )
    def k(hp_hbm, slot_hbm, buf_hbm, *scratch):
        idx_v, rows_v = scratch[:TOP_K], scratch[TOP_K]
        wid = lax.axis_index("s") * nc + lax.axis_index("c")
        base = wid * per_worker

        @pl.loop(0, per_worker // win)
        def _(i):
            off = pl.multiple_of(base + i * win, win)
            pltpu.sync_copy(hp_hbm.at[pl.ds(off, win)], rows_v)
            for kk in range(TOP_K):
                pltpu.sync_copy(slot_hbm.at[pl.ds(kk * t + off, win)], idx_v[kk])
                pltpu.sync_copy(rows_v, buf_hbm.at[idx_v[kk]])

    return k(hp, slot.reshape(-1))


def _gather_rows(out_rows, slot, win=64):
    t = slot.shape[1]
    w = out_rows.shape[1]
    nc, ns = _sc_workers()
    per_worker = t // (nc * ns)
    mesh = plsc.VectorSubcoreMesh(core_axis_name="c", subcore_axis_name="s")

    @functools.partial(
        pl.kernel,
        out_type=jax.ShapeDtypeStruct((TOP_K * t, w), out_rows.dtype),
        mesh=mesh,
        scratch_types=[pltpu.VMEM((win,), I32), pltpu.VMEM((win, w), out_rows.dtype)],
        name="moe_gather",
    )
    def k(rows_hbm, slot_hbm, og_hbm, idx_v, rows_v):
        wid = lax.axis_index("s") * nc + lax.axis_index("c")
        base = wid * per_worker

        @pl.loop(0, per_worker // win)
        def _(i):
            off = pl.multiple_of(base + i * win, win)
            for kk in range(TOP_K):
                pltpu.sync_copy(slot_hbm.at[pl.ds(kk * t + off, win)], idx_v)
                pltpu.sync_copy(rows_hbm.at[idx_v], rows_v)
                pltpu.sync_copy(rows_v, og_hbm.at[pl.ds(kk * t + off, win)])

    return k(out_rows, slot.reshape(-1)).reshape(TOP_K, t, w)


def _expert_kernel(be_ref, nu_ref, x_ref, wi_ref, bi_ref, wo_ref, bo_ref, o_ref):
    @pl.when(pl.program_id(0) < nu_ref[0])
    def _():
        hi, lo = _unpack_bf16_pairs(x_ref[...])
        h = (jnp.dot(hi.astype(BF16), wi_ref[0:HALF, :], preferred_element_type=F32)
             + jnp.dot(lo.astype(BF16), wi_ref[HALF:, :], preferred_element_type=F32) + bi_ref[...])
        gate = jnp.minimum(h[:, :D_FF], SWIGLU_LIMIT)
        lin = jnp.clip(h[:, D_FF:], -SWIGLU_LIMIT, SWIGLU_LIMIT)
        act = gate * jax.nn.sigmoid(SWIGLU_ALPHA * gate) * (lin + 1.0)
        o = jnp.dot(act.astype(BF16), wo_ref[...], preferred_element_type=F32) + bo_ref[...]
        o_ref[...] = _pack_bf16_pairs(o)


def _experts(block_expert, n_used, buf, w_in, b_in, w_out, b_out):
    n_rows, w = buf.shape
    n_blocks = n_rows // EXPERT_BLOCK
    return pl.pallas_call(
        _expert_kernel,
        out_shape=jax.ShapeDtypeStruct((n_rows, w), buf.dtype),
        grid_spec=pltpu.PrefetchScalarGridSpec(
            num_scalar_prefetch=2,
            grid=(n_blocks,),
            in_specs=[
                pl.BlockSpec((EXPERT_BLOCK, w), lambda i, be, nu: (i, 0)),
                pl.BlockSpec((None, D_MODEL, 2 * D_FF), lambda i, be, nu: (be[i], 0, 0)),
                pl.BlockSpec((None, 1, 2 * D_FF), lambda i, be, nu: (be[i], 0, 0)),
                pl.BlockSpec((None, D_FF, D_MODEL), lambda i, be, nu: (be[i], 0, 0)),
                pl.BlockSpec((None, 1, D_MODEL), lambda i, be, nu: (be[i], 0, 0)),
            ],
            out_specs=pl.BlockSpec((EXPERT_BLOCK, w), lambda i, be, nu: (i, 0)),
        ),
        compiler_params=_cp(("arbitrary",)),
        name="expert_ffn",
    )(block_expert, n_used, buf, w_in, b_in, w_out, b_out)


def _combine_kernel(x_ref, og_ref, g_ref, y_ref):
    g = g_ref[...]
    acc_hi = x_ref[:, :HALF]
    acc_lo = x_ref[:, HALF:]
    for k in range(TOP_K):
        hi, lo = _unpack_bf16_pairs(og_ref[k])
        gk = g[:, k:k + 1]
        acc_hi = acc_hi + gk * hi
        acc_lo = acc_lo + gk * lo
    y_ref[:, :HALF] = acc_hi
    y_ref[:, HALF:] = acc_lo


def _combine(x_mid, og, gates_tk, bm=512):
    t = x_mid.shape[0]
    return pl.pallas_call(
        _combine_kernel,
        out_shape=jax.ShapeDtypeStruct((t, D_MODEL), F32),
        grid=(t // bm,),
        in_specs=[
            pl.BlockSpec((bm, D_MODEL), lambda i: (i, 0)),
            pl.BlockSpec((TOP_K, bm, HALF), lambda i: (0, i, 0)),
            pl.BlockSpec((bm, TOP_K), lambda i: (i, 0)),
        ],
        out_specs=pl.BlockSpec((bm, D_MODEL), lambda i: (i, 0)),
        compiler_params=_cp(("parallel",)),
        name="moe_combine",
    )(x_mid, og, gates_tk)


def _block_tables(counts, n_blocks):
    padded = (counts + EXPERT_BLOCK - 1) // EXPERT_BLOCK * EXPERT_BLOCK
    pad_ends = jnp.cumsum(padded)
    pad_starts = pad_ends - padded
    block_start = jnp.arange(n_blocks, dtype=I32) * EXPERT_BLOCK
    block_expert = jnp.minimum(jnp.sum(pad_ends[None, :] <= block_start[:, None], axis=1), N_EXPERTS - 1)
    n_used = (pad_ends[-1] // EXPERT_BLOCK).reshape(1)
    return pad_starts.astype(I32), block_expert.astype(I32), n_used.astype(I32)


def _layer(x, mem, consts, params):
    b, seq, d = x.shape
    m_len = mem.shape[1]
    t = b * seq
    x2 = x.reshape(t, d)
    proj = _in_proj(x2, params["norm_mix"], params["w_in"])
    proj3 = proj.reshape(b, seq, IN_WIDTH)
    ret = _retention(proj3, params["ret_out_norm"], consts["cos"], consts["sin"], consts["tabs"],
                     consts["decays"])
    att = _window_attention(proj3, params["att_q_norm"], params["att_k_norm"], params["att_sink"],
                            consts["bias"])
    mkv = _mem_kv(mem.reshape(b * m_len, d), params["mem_norm"], params["w_mem_kv"],
                  params["mem_k_norm"], m_len)
    mo = _mem_attention(proj3, mkv.reshape(b, m_len, 2 * MEM_HEADS * MEM_D), params["mem_q_norm"])
    x_mid, hp, logits = _merge(x2, proj, ret.reshape(t, d), att.reshape(t, d), mo.reshape(t, d),
                               params["w_branch"], params["w_out"], params["norm_ffn"],
                               params["w_router"], params["b_router"])
    eid, rank, gates, counts = _route(logits.T, consts["tri"])
    n_blocks = t * TOP_K // EXPERT_BLOCK + N_EXPERTS
    pad_starts, block_expert, n_used = _block_tables(counts[:, 0].astype(I32), n_blocks)
    slot = _slots(pad_starts, eid, rank)
    buf = _dispatch_rows(hp, slot, n_blocks * EXPERT_BLOCK)
    out_rows = _experts(block_expert, n_used, buf, params["w_e_in"], params["b_e_in"],
                        params["w_e_out"], params["b_e_out"])
    og = _gather_rows(out_rows, slot)
    y = _combine(x_mid, og, gates.T)
    return y.reshape(b, seq, d)


def _prepare(seq, rel_table, norm_mix, w_in, ret_out_norm, att_q_norm, att_k_norm, att_sink, mem_norm,
             w_mem_kv, mem_q_norm, mem_k_norm, w_branch, w_out, norm_ffn, w_router, b_router,
             w_e_in, b_e_in, w_e_out, b_e_out):
    gates_at = IN_WIDTH - 3 * D_MODEL
    w_in0 = w_in[0]
    params = {
        "norm_mix": norm_mix[0].reshape(1, -1),
        "w_in": jnp.concatenate([w_in0[:, gates_at:], w_in0[:, :gates_at]], axis=1).astype(BF16),
        "ret_out_norm": ret_out_norm[0].reshape(1, -1),
        "att_q_norm": att_q_norm[0].reshape(1, -1),
        "att_k_norm": att_k_norm[0].reshape(1, -1),
        "att_sink": att_sink[0].astype(F32),
        "mem_norm": mem_norm[0].reshape(1, -1),
        "w_mem_kv": w_mem_kv[0].astype(BF16),
        "mem_q_norm": mem_q_norm[0].reshape(1, -1),
        "mem_k_norm": mem_k_norm[0].reshape(1, -1),
        "w_branch": w_branch[0].astype(BF16),
        "w_out": w_out[0].astype(BF16),
        "norm_ffn": norm_ffn[0].reshape(1, -1),
        "w_router": w_router[0],
        "b_router": b_router[0].reshape(1, -1),
        "w_e_in": w_e_in[0].astype(BF16),
        "b_e_in": b_e_in[0].reshape(N_EXPERTS, 1, -1),
        "w_e_out": w_e_out[0].astype(BF16),
        "b_e_out": b_e_out[0].reshape(N_EXPERTS, 1, -1),
    }
    cos, sin, tabs, decays = _retention_tables(seq)
    route_bt = 512
    tri = (jnp.arange(route_bt)[:, None] < jnp.arange(route_bt)[None, :]).astype(BF16)
    consts = {"cos": cos, "sin": sin, "tabs": tabs, "decays": decays,
              "bias": _window_bias(rel_table), "tri": tri}
    return consts, params


def kernel(x_prompt, x_sample, mem_prompt, mem_sample, rel_table, norm_mix, w_in, ret_out_norm, att_q_norm, att_k_norm, att_sink, mem_norm, w_mem_kv, mem_q_norm, mem_k_norm, w_branch, w_out, norm_ffn, w_router, b_router, w_e_in, b_e_in, w_e_out, b_e_out):
    consts, params = _prepare(x_prompt.shape[1], rel_table, norm_mix, w_in, ret_out_norm, att_q_norm,
                              att_k_norm, att_sink, mem_norm, w_mem_kv, mem_q_norm, mem_k_norm,
                              w_branch, w_out, norm_ffn, w_router, b_router, w_e_in, b_e_in,
                              w_e_out, b_e_out)
    y_prompt = _layer(x_prompt, mem_prompt, consts, params)
    y_sample = _layer(x_sample, mem_sample, consts, params)
    return (y_prompt, y_sample)
```

```python
import functools
import math

import jax
import jax.numpy as jnp
import numpy as np
from jax import lax
from jax.experimental import pallas as pl
from jax.experimental.pallas import tpu as pltpu
from jax.experimental.pallas import tpu_sc as plsc

F32 = jnp.float32
BF16 = jnp.bfloat16
I32 = jnp.int32

D_MODEL = 1024
EPS = 1e-6
RET_HEADS = 4
RET_QK = 128
RET_V = 256
CHUNK = 128
ROPE_BASE = 10000.0
DECAY_FWD = 5.0
DECAY_BWD = 5.5
ATT_HEADS = 8
ATT_KV = 2
ATT_GROUP = ATT_HEADS // ATT_KV
ATT_D = 128
WINDOW = 128
REL_BUCKETS = 32
REL_MAX_DIST = 128
MEM_HEADS = 4
MEM_D = 256
N_EXPERTS = 32
TOP_K = 4
D_FF = 1024
SWIGLU_LIMIT = 7.0
SWIGLU_ALPHA = 1.702
EXPERT_BLOCK = 512
HALF = D_MODEL // 2

IN_WIDTH = 8704
OFF_GATES = 0
OFF_RQ = 3072
OFF_RK = 3584
OFF_RV = 4096
OFF_RG = 5120
OFF_AQ = 6144
OFF_AK = 7168
OFF_AV = 7424
OFF_MQ = 7680

VMEM_LIMIT = 56 * 1024 * 1024


def _cp(sem, vmem=VMEM_LIMIT):
    return pltpu.CompilerParams(dimension_semantics=sem, vmem_limit_bytes=vmem)


def _rms(x, g):
    return x * lax.rsqrt(jnp.mean(x * x, axis=-1, keepdims=True) + EPS) * g


def _pack_bf16_pairs(x):
    w = x.shape[1] // 2
    bits = pltpu.bitcast(x.astype(BF16).astype(F32), I32)
    hi = bits[:, :w] & jnp.int32(-65536)
    lo = lax.shift_right_logical(bits[:, w:], jnp.int32(16))
    return hi | lo


def _unpack_bf16_pairs(u):
    hi = pltpu.bitcast(u & jnp.int32(-65536), F32)
    lo = pltpu.bitcast(lax.shift_left(u, jnp.int32(16)), F32)
    return hi, lo


def _in_proj_kernel(x_ref, g_ref, w_ref, o_ref, h_scr):
    @pl.when(pl.program_id(1) == 0)
    def _():
        h_scr[...] = _rms(x_ref[...], g_ref[...]).astype(BF16)

    o_ref[...] = jnp.dot(h_scr[...], w_ref[...], preferred_element_type=F32).astype(BF16)


def _in_proj(x2, gain, w_bf16, bm=1024, bn=512):
    t = x2.shape[0]
    n = w_bf16.shape[1]
    return pl.pallas_call(
        _in_proj_kernel,
        out_shape=jax.ShapeDtypeStruct((t, n), BF16),
        grid=(t // bm, n // bn),
        in_specs=[
            pl.BlockSpec((bm, D_MODEL), lambda i, j: (i, 0)),
            pl.BlockSpec((1, D_MODEL), lambda i, j: (0, 0)),
            pl.BlockSpec((D_MODEL, bn), lambda i, j: (0, j)),
        ],
        out_specs=pl.BlockSpec((bm, bn), lambda i, j: (i, j)),
        scratch_shapes=[pltpu.VMEM((bm, D_MODEL), BF16)],
        compiler_params=_cp(("parallel", "arbitrary")),
        name="in_proj",
    )(x2, gain, w_bf16)


def _retention_kernel(dec_ref, q_ref, k_ref, v_ref, g_ref, cos_ref, sin_ref, tab_ref, gain_ref,
                      o_ref, lhs, kb, rhs, kvf, kvb):
    h = pl.program_id(1)
    seq = q_ref.shape[0]
    n_chunks = seq // CHUNK
    dec_f = dec_ref[2 * h]
    dec_b = dec_ref[2 * h + 1]
    nt = (((1,), (1,)), ((), ()))
    tn = (((0,), (0,)), ((), ()))

    def rope(x, c, s):
        return x * c + pltpu.roll(x, RET_QK // 2, 1) * s

    def prep(n, carry):
        rows = pl.ds(pl.multiple_of(n * CHUNK, CHUNK), CHUNK)
        c = cos_ref[rows, :]
        s = sin_ref[rows, :]
        q = rope(q_ref[rows, :].astype(F32), c, s)
        k = rope(k_ref[rows, :].astype(F32), c, s) * (RET_QK ** -0.5)
        lhs[rows, 0:CHUNK] = q.astype(BF16)
        lhs[rows, CHUNK:2 * CHUNK] = (q * tab_ref[1]).astype(BF16)
        lhs[rows, 2 * CHUNK:3 * CHUNK] = (q * tab_ref[3]).astype(BF16)
        kb[rows, :] = k.astype(BF16)
        v = v_ref[rows, :]
        rhs[n, 0:CHUNK, :] = v
        kvf[n] = lax.dot_general((k * tab_ref[2]).astype(BF16), v, tn, preferred_element_type=F32)
        kvb[n] = lax.dot_general((k * tab_ref[4]).astype(BF16), v, tn, preferred_element_type=F32)
        return carry

    lax.fori_loop(0, n_chunks, prep, 0, unroll=4)

    def scan_f(n, state):
        rhs[n, CHUNK:2 * CHUNK, :] = state.astype(BF16)
        return state * dec_f + kvf[n]

    lax.fori_loop(0, n_chunks, scan_f, jnp.zeros((RET_QK, RET_V), F32))

    def scan_b(i, state):
        n = n_chunks - 1 - i
        rhs[n, 2 * CHUNK:3 * CHUNK, :] = state.astype(BF16)
        return state * dec_b + kvb[n]

    lax.fori_loop(0, n_chunks, scan_b, jnp.zeros((RET_QK, RET_V), F32))

    def chunk(n, carry):
        rows = pl.ds(pl.multiple_of(n * CHUNK, CHUNK), CHUNK)
        sc = lax.dot_general(lhs[rows, 0:CHUNK], kb[rows, :], nt, preferred_element_type=F32)
        a = jnp.concatenate([(sc * tab_ref[0]).astype(BF16), lhs[rows, CHUNK:3 * CHUNK]], axis=1)
        o = jnp.dot(a, rhs[n], preferred_element_type=F32)
        y = _rms(o, gain_ref[...])
        g = g_ref[rows, :].astype(F32)
        o_ref[rows, :] = (y * (g * jax.nn.sigmoid(g))).astype(BF16)
        return carry

    lax.fori_loop(0, n_chunks, chunk, 0, unroll=8)


def _retention(proj3, gain, cos, sin, tabs, decays):
    b, seq, _ = proj3.shape
    return pl.pallas_call(
        _retention_kernel,
        out_shape=jax.ShapeDtypeStruct((b, seq, RET_HEADS * RET_V), BF16),
        grid=(b, RET_HEADS),
        in_specs=[
            pl.BlockSpec(memory_space=pltpu.SMEM),
            pl.BlockSpec((None, seq, RET_QK), lambda i, h: (i, 0, OFF_RQ // RET_QK + h)),
            pl.BlockSpec((None, seq, RET_QK), lambda i, h: (i, 0, OFF_RK // RET_QK + h)),
            pl.BlockSpec((None, seq, RET_V), lambda i, h: (i, 0, OFF_RV // RET_V + h)),
            pl.BlockSpec((None, seq, RET_V), lambda i, h: (i, 0, OFF_RG // RET_V + h)),
            pl.BlockSpec((seq, RET_QK), lambda i, h: (0, 0)),
            pl.BlockSpec((seq, RET_QK), lambda i, h: (0, 0)),
            pl.BlockSpec((None, 5, CHUNK, CHUNK), lambda i, h: (h, 0, 0, 0)),
            pl.BlockSpec((1, RET_V), lambda i, h: (0, h)),
        ],
        out_specs=pl.BlockSpec((None, seq, RET_V), lambda i, h: (i, 0, h)),
        scratch_shapes=[
            pltpu.VMEM((seq, 3 * CHUNK), BF16),
            pltpu.VMEM((seq, RET_QK), BF16),
            pltpu.VMEM((seq // CHUNK, 3 * CHUNK, RET_V), BF16),
            pltpu.VMEM((seq // CHUNK, RET_QK, RET_V), F32),
            pltpu.VMEM((seq // CHUNK, RET_QK, RET_V), F32),
        ],
        compiler_params=_cp(("parallel", "parallel")),
        name="retention",
    )(decays, proj3, proj3, proj3, proj3, cos, sin, tabs, gain)


def _retention_tables(seq):
    half = RET_QK // 2
    inv = ROPE_BASE ** (-jnp.arange(half, dtype=F32) / half)
    ang = jnp.arange(seq, dtype=F32)[:, None] * inv[None, :]
    cos = jnp.concatenate([jnp.cos(ang), jnp.cos(ang)], axis=1)
    sin = jnp.concatenate([-jnp.sin(ang), jnp.sin(ang)], axis=1)
    heads = jnp.arange(RET_HEADS, dtype=F32)
    lg_f = jnp.log1p(-jnp.exp2(-DECAY_FWD - heads))[:, None, None]
    lg_b = jnp.log1p(-jnp.exp2(-DECAY_BWD - heads))[:, None, None]
    idx = jnp.arange(CHUNK, dtype=F32)
    diff = (idx[:, None] - idx[None, :])[None]
    dmat = jnp.where(diff >= 0, jnp.exp(jnp.where(diff >= 0, diff, 0.0) * lg_f),
                     jnp.exp(jnp.where(diff < 0, -diff, 0.0) * lg_b))
    col = jnp.broadcast_to(idx[None, :, None], (RET_HEADS, CHUNK, CHUNK))
    xi_f = jnp.exp((col + 1.0) * lg_f)
    zeta_f = jnp.exp((CHUNK - 1.0 - col) * lg_f)
    xi_b = jnp.exp((CHUNK - col) * lg_b)
    zeta_b = jnp.exp(col * lg_b)
    tabs = jnp.stack([dmat, xi_f, zeta_f, xi_b, zeta_b], axis=1)
    decays = jnp.stack([jnp.exp(CHUNK * lg_f[:, 0, 0]), jnp.exp(CHUNK * lg_b[:, 0, 0])], axis=1).reshape(-1)
    return cos, sin, tabs, decays


def _window_kernel(sink_ref, q_ref, k_ref, v_ref, qg_ref, kg_ref, bias_ref, o_ref,
                   qn, kp, vp, s_scr, p_scr, e_scr):
    g = pl.program_id(1)
    seq = q_ref.shape[0]
    n_blocks = seq // WINDOW
    stack = ATT_GROUP * WINDOW
    unit = 64
    nt = (((1,), (1,)), ((), ()))
    heads = [slice(r * ATT_D, (r + 1) * ATT_D) for r in range(ATT_GROUP)]
    kp[0:WINDOW, :] = jnp.zeros((WINDOW, ATT_D), BF16)
    kp[seq + WINDOW:seq + 2 * WINDOW, :] = jnp.zeros((WINDOW, ATT_D), BF16)
    vp[0:WINDOW, :] = jnp.zeros((WINDOW, 2 * ATT_D), BF16)
    vp[seq + WINDOW:seq + 2 * WINDOW, :] = jnp.zeros((WINDOW, 2 * ATT_D), BF16)

    def prep(n, carry):
        r0 = pl.multiple_of(n * WINDOW, WINDOW)
        src = pl.ds(r0, WINDOW)
        dst = pl.ds(r0 + WINDOW, WINDOW)
        kp[dst, :] = _rms(k_ref[src, :].astype(F32), kg_ref[...]).astype(BF16)
        vp[dst, 0:ATT_D] = v_ref[src, :]
        vp[dst, ATT_D:2 * ATT_D] = jnp.ones((WINDOW, ATT_D), BF16)
        for hs in heads:
            q = _rms(q_ref[src, hs].astype(F32), qg_ref[...]) * (ATT_D ** -0.5)
            qn[src, hs] = q.astype(BF16)
        return carry

    lax.fori_loop(0, n_blocks, prep, 0, unroll=2)

    def scores(n, carry):
        r0 = pl.multiple_of(n * WINDOW, WINDOW)
        kb = kp[pl.ds(r0, 3 * WINDOW), :]
        q4 = jnp.concatenate([qn[pl.ds(r0, WINDOW), hs] for hs in heads], axis=0)
        edge = jnp.where(n == 0, 0, jnp.where(n == n_blocks - 1, 2, 1))
        s_scr[pl.ds(pl.multiple_of(n * stack, stack), stack), :] = (
            lax.dot_general(q4, kb, nt, preferred_element_type=F32) + bias_ref[edge])
        return carry

    lax.fori_loop(0, n_blocks, scores, 0, unroll=2)

    def softmax(n, carry):
        base = pl.multiple_of(n * stack, stack)
        for u in range(stack // unit):
            rows = pl.ds(base + u * unit, unit)
            sink = sink_ref[g * ATT_GROUP + (u * unit) // WINDOW]
            s = s_scr[rows, :]
            m = jnp.maximum(jnp.max(s, axis=-1, keepdims=True), sink)
            mb = jnp.broadcast_to(m, (unit, WINDOW))
            p_scr[rows, :] = jnp.exp(s - jnp.concatenate([mb, mb, mb], axis=1)).astype(BF16)
            e_scr[rows, :] = jnp.exp(sink - mb)
        return carry

    lax.fori_loop(0, n_blocks, softmax, 0)

    def outputs(n, carry):
        r0 = pl.multiple_of(n * WINDOW, WINDOW)
        rows = pl.ds(pl.multiple_of(n * stack, stack), stack)
        oe = jnp.dot(p_scr[rows, :], vp[pl.ds(r0, 3 * WINDOW), :], preferred_element_type=F32)
        o = oe[:, 0:ATT_D] / (oe[:, ATT_D:2 * ATT_D] + e_scr[rows, :])
        for r, hs in enumerate(heads):
            o_ref[pl.ds(r0, WINDOW), hs] = o[r * WINDOW:(r + 1) * WINDOW, :].astype(BF16)
        return carry

    lax.fori_loop(0, n_blocks, outputs, 0, unroll=2)


def _window_attention(proj3, q_gain, k_gain, sink, bias):
    b, seq, _ = proj3.shape
    gw = ATT_GROUP * ATT_D
    return pl.pallas_call(
        _window_kernel,
        out_shape=jax.ShapeDtypeStruct((b, seq, ATT_HEADS * ATT_D), BF16),
        grid=(b, ATT_KV),
        in_specs=[
            pl.BlockSpec(memory_space=pltpu.SMEM),
            pl.BlockSpec((None, seq, gw), lambda i, g: (i, 0, OFF_AQ // gw + g)),
            pl.BlockSpec((None, seq, ATT_D), lambda i, g: (i, 0, OFF_AK // ATT_D + g)),
            pl.BlockSpec((None, seq, ATT_D), lambda i, g: (i, 0, OFF_AV // ATT_D + g)),
            pl.BlockSpec((1, ATT_D), lambda i, g: (0, 0)),
            pl.BlockSpec((1, ATT_D), lambda i, g: (0, 0)),
            pl.BlockSpec((None, 3, ATT_GROUP * WINDOW, 3 * WINDOW), lambda i, g: (g, 0, 0, 0)),
        ],
        out_specs=pl.BlockSpec((None, seq, gw), lambda i, g: (i, 0, g)),
        scratch_shapes=[
            pltpu.VMEM((seq, gw), BF16),
            pltpu.VMEM((seq + 2 * WINDOW, ATT_D), BF16),
            pltpu.VMEM((seq + 2 * WINDOW, 2 * ATT_D), BF16),
            pltpu.VMEM((seq * ATT_GROUP, 3 * WINDOW), F32),
            pltpu.VMEM((seq * ATT_GROUP, 3 * WINDOW), BF16),
            pltpu.VMEM((seq * ATT_GROUP, WINDOW), F32),
        ],
        compiler_params=_cp(("parallel", "parallel")),
        name="window_attention",
    )(sink, proj3, proj3, proj3, q_gain, k_gain, bias)


def _window_bias(rel_table):
    nb = REL_BUCKETS // 2
    max_exact = nb // 2
    qi = jnp.arange(WINDOW)[:, None]
    ki = jnp.arange(3 * WINDOW)[None, :]
    rel = ki - WINDOW - qi
    ret = jnp.where(rel > 0, nb, 0)
    n = jnp.abs(rel)
    large = max_exact + (jnp.log(jnp.maximum(n, 1).astype(F32) / max_exact)
                         / math.log(REL_MAX_DIST / max_exact) * (nb - max_exact)).astype(I32)
    large = jnp.minimum(large, nb - 1)
    bucket = ret + jnp.where(n < max_exact, n, large)
    onehot = (bucket[None] == jnp.arange(REL_BUCKETS)[:, None, None]).astype(F32)
    bias = jnp.einsum("bh,bqk->hqk", rel_table.astype(F32), onehot, precision=lax.Precision.HIGHEST)
    bias = jnp.where((jnp.abs(rel) <= WINDOW)[None], bias, -jnp.inf)
    bias = bias.reshape(ATT_KV, ATT_GROUP * WINDOW, 3 * WINDOW)
    first = jnp.where(ki < WINDOW, -jnp.inf, bias)
    last = jnp.where(ki >= 2 * WINDOW, -jnp.inf, bias)
    return jnp.stack([first, bias, last], axis=1)


def _mem_kv_kernel(m_ref, g_ref, w_ref, kg_ref, o_ref, h_scr):
    j = pl.program_id(1)

    @pl.when(j == 0)
    def _():
        h_scr[...] = _rms(m_ref[...], g_ref[...]).astype(BF16)

    o = jnp.dot(h_scr[...], w_ref[...], preferred_element_type=F32)
    o_ref[...] = jnp.where(j < MEM_HEADS, _rms(o, kg_ref[...]), o).astype(BF16)


def _mem_kv(mem2, gain, w_bf16, k_gain, m_len):
    rows = mem2.shape[0]
    n = w_bf16.shape[1]
    return pl.pallas_call(
        _mem_kv_kernel,
        out_shape=jax.ShapeDtypeStruct((rows, n), BF16),
        grid=(rows // m_len, n // MEM_D),
        in_specs=[
            pl.BlockSpec((m_len, D_MODEL), lambda i, j: (i, 0)),
            pl.BlockSpec((1, D_MODEL), lambda i, j: (0, 0)),
            pl.BlockSpec((D_MODEL, MEM_D), lambda i, j: (0, j)),
            pl.BlockSpec((1, MEM_D), lambda i, j: (0, 0)),
        ],
        out_specs=pl.BlockSpec((m_len, MEM_D), lambda i, j: (i, j)),
        scratch_shapes=[pltpu.VMEM((m_len, D_MODEL), BF16)],
        compiler_params=_cp(("parallel", "arbitrary")),
        name="mem_kv",
    )(mem2, gain, w_bf16, k_gain)


def _mem_attn_kernel(q_ref, k_ref, v_ref, qg_ref, o_ref, *, bq):
    seq = q_ref.shape[0]
    nt = (((1,), (1,)), ((), ()))

    def block(n, carry):
        rows = pl.ds(pl.multiple_of(n * bq, bq), bq)
        q = _rms(q_ref[rows, :].astype(F32), qg_ref[...]) * (MEM_D ** -0.5)
        s = lax.dot_general(q.astype(BF16), k_ref[...], nt, preferred_element_type=F32)
        p = jnp.exp(s - jnp.max(s, axis=-1, keepdims=True))
        denom = jnp.sum(p, axis=-1, keepdims=True)
        o = jnp.dot(p.astype(BF16), v_ref[...], preferred_element_type=F32) / denom
        o_ref[rows, :] = o.astype(BF16)
        return carry

    lax.fori_loop(0, seq // bq, block, 0, unroll=8)


def _mem_attention(proj3, mkv3, q_gain, bq=256):
    b, seq, _ = proj3.shape
    m_len = mkv3.shape[1]
    return pl.pallas_call(
        functools.partial(_mem_attn_kernel, bq=bq),
        out_shape=jax.ShapeDtypeStruct((b, seq, MEM_HEADS * MEM_D), BF16),
        grid=(b, MEM_HEADS),
        in_specs=[
            pl.BlockSpec((None, seq, MEM_D), lambda i, h: (i, 0, OFF_MQ // MEM_D + h)),
            pl.BlockSpec((None, m_len, MEM_D), lambda i, h: (i, 0, h)),
            pl.BlockSpec((None, m_len, MEM_D), lambda i, h: (i, 0, MEM_HEADS + h)),
            pl.BlockSpec((1, MEM_D), lambda i, h: (0, 0)),
        ],
        out_specs=pl.BlockSpec((None, seq, MEM_D), lambda i, h: (i, 0, h)),
        compiler_params=_cp(("parallel", "parallel")),
        name="mem_attention",
    )(proj3, mkv3, mkv3, q_gain)


def _merge_kernel(x_ref, g0_ref, g1_ref, g2_ref, b0_ref, b1_ref, b2_ref, wb_ref, wo_ref, gn_ref,
                  wr_ref, br_ref, xo_ref, hp_ref, lg_ref):
    merged = None
    for gate_ref, br, i in ((g0_ref, b0_ref, 0), (g1_ref, b1_ref, 1), (g2_ref, b2_ref, 2)):
        t = jax.nn.sigmoid(gate_ref[...].astype(F32)) * jnp.dot(br[...], wb_ref[i],
                                                                 preferred_element_type=F32)
        merged = t if merged is None else merged + t
    x = x_ref[...] + jnp.dot(merged.astype(BF16), wo_ref[...], preferred_element_type=F32)
    xo_ref[...] = x
    h = _rms(x, gn_ref[...])
    lg_ref[...] = jnp.dot(h, wr_ref[...], preferred_element_type=F32,
                          precision=lax.Precision.HIGHEST) + br_ref[...]
    hp_ref[...] = _pack_bf16_pairs(h)


def _merge(x2, proj, ret, att, mo, wb, wo, gain, w_router, b_router, bm=512):
    t = x2.shape[0]
    const2 = lambda i: (0, 0)
    row = lambda i: (i, 0)
    return pl.pallas_call(
        _merge_kernel,
        out_shape=(
            jax.ShapeDtypeStruct((t, D_MODEL), F32),
            jax.ShapeDtypeStruct((t, HALF), I32),
            jax.ShapeDtypeStruct((t, N_EXPERTS), F32),
        ),
        grid=(t // bm,),
        in_specs=[
            pl.BlockSpec((bm, D_MODEL), row),
            pl.BlockSpec((bm, D_MODEL), lambda i: (i, 0)),
            pl.BlockSpec((bm, D_MODEL), lambda i: (i, 1)),
            pl.BlockSpec((bm, D_MODEL), lambda i: (i, 2)),
            pl.BlockSpec((bm, D_MODEL), row),
            pl.BlockSpec((bm, D_MODEL), row),
            pl.BlockSpec((bm, D_MODEL), row),
            pl.BlockSpec((3, D_MODEL, D_MODEL), lambda i: (0, 0, 0)),
            pl.BlockSpec((D_MODEL, D_MODEL), const2),
            pl.BlockSpec((1, D_MODEL), const2),
            pl.BlockSpec((D_MODEL, N_EXPERTS), const2),
            pl.BlockSpec((1, N_EXPERTS), const2),
        ],
        out_specs=(
            pl.BlockSpec((bm, D_MODEL), row),
            pl.BlockSpec((bm, HALF), row),
            pl.BlockSpec((bm, N_EXPERTS), row),
        ),
        compiler_params=_cp(("parallel",)),
        name="merge_router",
    )(x2, proj, proj, proj, ret, att, mo, wb, wo, gain, w_router, b_router)


def _route_kernel(l_ref, tri_ref, eid_ref, rank_ref, gate_ref, cnt_ref):
    @pl.when(pl.program_id(0) == 0)
    def _():
        cnt_ref[...] = jnp.zeros_like(cnt_ref)

    l = l_ref[...]
    ne, bt = l.shape
    iota_e = lax.broadcasted_iota(I32, (ne, bt), 0)
    picked = jnp.zeros((ne, bt), jnp.bool_)
    vals, idxs = [], []
    for _ in range(TOP_K):
        m = jnp.max(l, axis=0, keepdims=True)
        idx = jnp.min(jnp.where(l == m, iota_e, ne), axis=0, keepdims=True)
        sel = iota_e == idx
        picked = picked | sel
        l = jnp.where(sel, -jnp.inf, l)
        vals.append(m)
        idxs.append(idx)
    ex = [jnp.exp(v - vals[0]) for v in vals]
    tot = ex[0] + ex[1] + ex[2] + ex[3]
    onehot = jnp.where(picked, 1.0, 0.0)
    before = jnp.dot(onehot.astype(BF16), tri_ref[...], preferred_element_type=F32) + cnt_ref[:, 0:1]
    for k in range(TOP_K):
        eid_ref[k:k + 1, :] = idxs[k]
        gate_ref[k:k + 1, :] = ex[k] / tot
        rank_ref[k:k + 1, :] = jnp.sum(jnp.where(iota_e == idxs[k], before, 0.0), axis=0,
                                       keepdims=True).astype(I32)
    cnt_ref[...] = cnt_ref[...] + jnp.sum(onehot, axis=1, keepdims=True)


def _route(logits_t, tri, bt=512):
    ne, t = logits_t.shape
    blk = lambda i: (0, i)
    return pl.pallas_call(
        _route_kernel,
        out_shape=(
            jax.ShapeDtypeStruct((TOP_K, t), I32),
            jax.ShapeDtypeStruct((TOP_K, t), I32),
            jax.ShapeDtypeStruct((TOP_K, t), F32),
            jax.ShapeDtypeStruct((ne, 128), F32),
        ),
        grid=(t // bt,),
        in_specs=[pl.BlockSpec((ne, bt), blk), pl.BlockSpec((bt, bt), lambda i: (0, 0))],
        out_specs=(
            pl.BlockSpec((TOP_K, bt), blk),
            pl.BlockSpec((TOP_K, bt), blk),
            pl.BlockSpec((TOP_K, bt), blk),
            pl.BlockSpec((ne, 128), lambda i: (0, 0)),
        ),
        compiler_params=_cp(("arbitrary",)),
        name="route_topk",
    )(logits_t, tri)


def _slot_kernel(start_ref, eid_ref, rank_ref, slot_ref):
    eid = eid_ref[...]
    base = jnp.zeros(eid.shape, I32)
    for e in range(N_EXPERTS):
        base = jnp.where(eid == e, start_ref[e], base)
    slot_ref[...] = base + rank_ref[...]


def _slots(pad_starts, eid, rank, bt=2048):
    t = eid.shape[1]
    blk = lambda i, s: (0, i)
    return pl.pallas_call(
        _slot_kernel,
        out_shape=jax.ShapeDtypeStruct((TOP_K, t), I32),
        grid_spec=pltpu.PrefetchScalarGridSpec(
            num_scalar_prefetch=1,
            grid=(t // bt,),
            in_specs=[pl.BlockSpec((TOP_K, bt), blk), pl.BlockSpec((TOP_K, bt), blk)],
            out_specs=pl.BlockSpec((TOP_K, bt), blk),
        ),
        compiler_params=_cp(("parallel",)),
        name="route_slots",
    )(pad_starts, eid, rank)


def _sc_workers():
    info = plsc.get_sparse_core_info()
    return info.num_cores, info.num_subcores


def _dispatch_rows(hp, slot, n_rows, win=64):
    t, w = hp.shape
    nc, ns = _sc_workers()
    per_worker = t // (nc * ns)
    mesh = plsc.VectorSubcoreMesh(core_axis_name="c", subcore_axis_name="s")

    @functools.partial(
        pl.kernel,
        out_type=jax.ShapeDtypeStruct((n_rows, w), hp.dtype),
        mesh=mesh,
        scratch_types=[pltpu.VMEM((win,), I32)] * TOP_K + [pltpu.VMEM((win, w), hp.dtype)],
        name="moe_dispatch",
    )
    def k(hp_hbm, slot_hbm, buf_hbm, *scratch):
        idx_v, rows_v = scratch[:TOP_K], scratch[TOP_K]
        wid = lax.axis_index("s") * nc + lax.axis_index("c")
        base = wid * per_worker

        @pl.loop(0, per_worker // win)
        def _(i):
            off = pl.multiple_of(base + i * win, win)
            pltpu.sync_copy(hp_hbm.at[pl.ds(off, win)], rows_v)
            for kk in range(TOP_K):
                pltpu.sync_copy(slot_hbm.at[pl.ds(kk * t + off, win)], idx_v[kk])
                pltpu.sync_copy(rows_v, buf_hbm.at[idx_v[kk]])

    return k(hp, slot.reshape(-1))


def _gather_rows(out_rows, slot, win=64):
    t = slot.shape[1]
    w = out_rows.shape[1]
    nc, ns = _sc_workers()
    per_worker = t // (nc * ns)
    mesh = plsc.VectorSubcoreMesh(core_axis_name="c", subcore_axis_name="s")

    @functools.partial(
        pl.kernel,
        out_type=jax.ShapeDtypeStruct((TOP_K * t, w), out_rows.dtype),
        mesh=mesh,
        scratch_types=[pltpu.VMEM((win,), I32), pltpu.VMEM((win, w), out_rows.dtype)],
        name="moe_gather",
    )
    def k(rows_hbm, slot_hbm, og_hbm, idx_v, rows_v):
        wid = lax.axis_index("s") * nc + lax.axis_index("c")
        base = wid * per_worker

        @pl.loop(0, per_worker // win)
        def _(i):
            off = pl.multiple_of(base + i * win, win)
            for kk in range(TOP_K):
                pltpu.sync_copy(slot_hbm.at[pl.ds(kk * t + off, win)], idx_v)
                pltpu.sync_copy(rows_hbm.at[idx_v], rows_v)
                pltpu.sync_copy(rows_v, og_hbm.at[pl.ds(kk * t + off, win)])

    return k(out_rows, slot.reshape(-1)).reshape(TOP_K, t, w)


def _expert_kernel(be_ref, nu_ref, x_ref, wi_ref, bi_ref, wo_ref, bo_ref, o_ref):
    @pl.when(pl.program_id(0) < nu_ref[0])
    def _():
        hi, lo = _unpack_bf16_pairs(x_ref[...])
        h = (jnp.dot(hi.astype(BF16), wi_ref[0:HALF, :], preferred_element_type=F32)
             + jnp.dot(lo.astype(BF16), wi_ref[HALF:, :], preferred_element_type=F32) + bi_ref[...])
        gate = jnp.minimum(h[:, :D_FF], SWIGLU_LIMIT)
        lin = jnp.clip(h[:, D_FF:], -SWIGLU_LIMIT, SWIGLU_LIMIT)
        act = gate * jax.nn.sigmoid(SWIGLU_ALPHA * gate) * (lin + 1.0)
        o = jnp.dot(act.astype(BF16), wo_ref[...], preferred_element_type=F32) + bo_ref[...]
        o_ref[...] = _pack_bf16_pairs(o)


def _experts(block_expert, n_used, buf, w_in, b_in, w_out, b_out):
    n_rows, w = buf.shape
    n_blocks = n_rows // EXPERT_BLOCK
    return pl.pallas_call(
        _expert_kernel,
        out_shape=jax.ShapeDtypeStruct((n_rows, w), buf.dtype),
        grid_spec=pltpu.PrefetchScalarGridSpec(
            num_scalar_prefetch=2,
            grid=(n_blocks,),
            in_specs=[
                pl.BlockSpec((EXPERT_BLOCK, w), lambda i, be, nu: (i, 0)),
                pl.BlockSpec((None, D_MODEL, 2 * D_FF), lambda i, be, nu: (be[i], 0, 0)),
                pl.BlockSpec((None, 1, 2 * D_FF), lambda i, be, nu: (be[i], 0, 0)),
                pl.BlockSpec((None, D_FF, D_MODEL), lambda i, be, nu: (be[i], 0, 0)),
                pl.BlockSpec((None, 1, D_MODEL), lambda i, be, nu: (be[i], 0, 0)),
            ],
            out_specs=pl.BlockSpec((EXPERT_BLOCK, w), lambda i, be, nu: (i, 0)),
        ),
        compiler_params=_cp(("arbitrary",)),
        name="expert_ffn",
    )(block_expert, n_used, buf, w_in, b_in, w_out, b_out)


def _combine_kernel(x_ref, og_ref, g_ref, y_ref):
    g = g_ref[...]
    acc_hi = x_ref[:, :HALF]
    acc_lo = x_ref[:, HALF:]
    for k in range(TOP_K):
        hi, lo = _unpack_bf16_pairs(og_ref[k])
        gk = g[:, k:k + 1]
        acc_hi = acc_hi + gk * hi
        acc_lo = acc_lo + gk * lo
    y_ref[:, :HALF] = acc_hi
    y_ref[:, HALF:] = acc_lo


def _combine(x_mid, og, gates_tk, bm=512):
    t = x_mid.shape[0]
    return pl.pallas_call(
        _combine_kernel,
        out_shape=jax.ShapeDtypeStruct((t, D_MODEL), F32),
        grid=(t // bm,),
        in_specs=[
            pl.BlockSpec((bm, D_MODEL), lambda i: (i, 0)),
            pl.BlockSpec((TOP_K, bm, HALF), lambda i: (0, i, 0)),
            pl.BlockSpec((bm, TOP_K), lambda i: (i, 0)),
        ],
        out_specs=pl.BlockSpec((bm, D_MODEL), lambda i: (i, 0)),
        compiler_params=_cp(("parallel",)),
        name="moe_combine",
    )(x_mid, og, gates_tk)


def _block_tables(counts, n_blocks):
    padded = (counts + EXPERT_BLOCK - 1) // EXPERT_BLOCK * EXPERT_BLOCK
    pad_ends = jnp.cumsum(padded)
    pad_starts = pad_ends - padded
    block_start = jnp.arange(n_blocks, dtype=I32) * EXPERT_BLOCK
    block_expert = jnp.minimum(jnp.sum(pad_ends[None, :] <= block_start[:, None], axis=1), N_EXPERTS - 1)
    n_used = (pad_ends[-1] // EXPERT_BLOCK).reshape(1)
    return pad_starts.astype(I32), block_expert.astype(I32), n_used.astype(I32)


def _layer(x, mem, consts, params):
    b, seq, d = x.shape
    m_len = mem.shape[1]
    t = b * seq
    x2 = x.reshape(t, d)
    proj = _in_proj(x2, params["norm_mix"], params["w_in"])
    proj3 = proj.reshape(b, seq, IN_WIDTH)
    ret = _retention(proj3, params["ret_out_norm"], consts["cos"], consts["sin"], consts["tabs"],
                     consts["decays"])
    att = _window_attention(proj3, params["att_q_norm"], params["att_k_norm"], params["att_sink"],
                            consts["bias"])
    mkv = _mem_kv(mem.reshape(b * m_len, d), params["mem_norm"], params["w_mem_kv"],
                  params["mem_k_norm"], m_len)
    mo = _mem_attention(proj3, mkv.reshape(b, m_len, 2 * MEM_HEADS * MEM_D), params["mem_q_norm"])
    x_mid, hp, logits = _merge(x2, proj, ret.reshape(t, d), att.reshape(t, d), mo.reshape(t, d),
                               params["w_branch"], params["w_out"], params["norm_ffn"],
                               params["w_router"], params["b_router"])
    eid, rank, gates, counts = _route(logits.T, consts["tri"])
    n_blocks = t * TOP_K // EXPERT_BLOCK + N_EXPERTS
    pad_starts, block_expert, n_used = _block_tables(counts[:, 0].astype(I32), n_blocks)
    slot = _slots(pad_starts, eid, rank)
    buf = _dispatch_rows(hp, slot, n_blocks * EXPERT_BLOCK)
    out_rows = _experts(block_expert, n_used, buf, params["w_e_in"], params["b_e_in"],
                        params["w_e_out"], params["b_e_out"])
    og = _gather_rows(out_rows, slot)
    y = _combine(x_mid, og, gates.T)
    return y.reshape(b, seq, d)


def _prepare(seq, rel_table, norm_mix, w_in, ret_out_norm, att_q_norm, att_k_norm, att_sink, mem_norm,
             w_mem_kv, mem_q_norm, mem_k_norm, w_branch, w_out, norm_ffn, w_router, b_router,
             w_e_in, b_e_in, w_e_out, b_e_out):
    gates_at = IN_WIDTH - 3 * D_MODEL
    w_in0 = w_in[0]
    params = {
        "norm_mix": norm_mix[0].reshape(1, -1),
        "w_in": jnp.concatenate([w_in0[:, gates_at:], w_in0[:, :gates_at]], axis=1).astype(BF16),
        "ret_out_norm": ret_out_norm[0].reshape(1, -1),
        "att_q_norm": att_q_norm[0].reshape(1, -1),
        "att_k_norm": att_k_norm[0].reshape(1, -1),
        "att_sink": att_sink[0].astype(F32),
        "mem_norm": mem_norm[0].reshape(1, -1),
        "w_mem_kv": w_mem_kv[0].astype(BF16),
        "mem_q_norm": mem_q_norm[0].reshape(1, -1),
        "mem_k_norm": mem_k_norm[0].reshape(1, -1),
        "w_branch": w_branch[0].astype(BF16),
        "w_out": w_out[0].astype(BF16),
        "norm_ffn": norm_ffn[0].reshape(1, -1),
        "w_router": w_router[0],
        "b_router": b_router[0].reshape(1, -1),
        "w_e_in": w_e_in[0].astype(BF16),
        "b_e_in": b_e_in[0].reshape(N_EXPERTS, 1, -1),
        "w_e_out": w_e_out[0].astype(BF16),
        "b_e_out": b_e_out[0].reshape(N_EXPERTS, 1, -1),
    }
    cos, sin, tabs, decays = _retention_tables(seq)
    route_bt = 512
    tri = (jnp.arange(route_bt)[:, None] < jnp.arange(route_bt)[None, :]).astype(BF16)
    consts = {"cos": cos, "sin": sin, "tabs": tabs, "decays": decays,
              "bias": _window_bias(rel_table), "tri": tri}
    return consts, params


def kernel(x_prompt, x_sample, mem_prompt, mem_sample, rel_table, norm_mix, w_in, ret_out_norm, att_q_norm, att_k_norm, att_sink, mem_norm, w_mem_kv, mem_q_norm, mem_k_norm, w_branch, w_out, norm_ffn, w_router, b_router, w_e_in, b_e_in, w_e_out, b_e_out):
    consts, params = _prepare(x_prompt.shape[1], rel_table, norm_mix, w_in, ret_out_norm, att_q_norm,
                              att_k_norm, att_sink, mem_norm, w_mem_kv, mem_q_norm, mem_k_norm,
                              w_branch, w_out, norm_ffn, w_router, b_router, w_e_in, b_e_in,
                              w_e_out, b_e_out)
    y_prompt = _layer(x_prompt, mem_prompt, consts, params)
    y_sample = _layer(x_sample, mem_sample, consts, params)
    return (y_prompt, y_sample)
```

```python
import functools
import math

import jax
import jax.numpy as jnp
import numpy as np
from jax import lax
from jax.experimental import pallas as pl
from jax.experimental.pallas import tpu as pltpu
from jax.experimental.pallas import tpu_sc as plsc

F32 = jnp.float32
BF16 = jnp.bfloat16
I32 = jnp.int32

D_MODEL = 1024
EPS = 1e-6
RET_HEADS = 4
RET_QK = 128
RET_V = 256
CHUNK = 128
ROPE_BASE = 10000.0
DECAY_FWD = 5.0
DECAY_BWD = 5.5
ATT_HEADS = 8
ATT_KV = 2
ATT_GROUP = ATT_HEADS // ATT_KV
ATT_D = 128
WINDOW = 128
REL_BUCKETS = 32
REL_MAX_DIST = 128
MEM_HEADS = 4
MEM_D = 256
N_EXPERTS = 32
TOP_K = 4
D_FF = 1024
SWIGLU_LIMIT = 7.0
SWIGLU_ALPHA = 1.702
EXPERT_BLOCK = 512
HALF = D_MODEL // 2
ROUTER_LANES = 128

IN_WIDTH = 8704
OFF_GATES = 0
OFF_RQ = 3072
OFF_RK = 3584
OFF_RV = 4096
OFF_RG = 5120
OFF_AQ = 6144
OFF_AK = 7168
OFF_AV = 7424
OFF_MQ = 7680

VMEM_LIMIT = 56 * 1024 * 1024


def _cp(sem, vmem=VMEM_LIMIT):
    return pltpu.CompilerParams(dimension_semantics=sem, vmem_limit_bytes=vmem)


def _rms(x, g):
    return x * lax.rsqrt(jnp.mean(x * x, axis=-1, keepdims=True) + EPS) * g


def _pack_bf16_pairs(x):
    w = x.shape[1] // 2
    bits = pltpu.bitcast(x.astype(BF16).astype(F32), I32)
    hi = bits[:, :w] & jnp.int32(-65536)
    lo = lax.shift_right_logical(bits[:, w:], jnp.int32(16))
    return hi | lo


def _unpack_bf16_pairs(u):
    hi = pltpu.bitcast(u & jnp.int32(-65536), F32)
    lo = pltpu.bitcast(lax.shift_left(u, jnp.int32(16)), F32)
    return hi, lo


def _in_proj_kernel(x_ref, g_ref, w_ref, o_ref, h_scr):
    @pl.when(pl.program_id(1) == 0)
    def _():
        h_scr[...] = _rms(x_ref[...], g_ref[...]).astype(BF16)

    sub = 256

    def cols(c, carry):
        sl = pl.ds(pl.multiple_of(c * sub, sub), sub)
        o_ref[:, sl] = jnp.dot(h_scr[...], w_ref[:, sl], preferred_element_type=F32).astype(BF16)
        return carry

    lax.fori_loop(0, o_ref.shape[1] // sub, cols, 0, unroll=True)


def _in_proj(x2, gain, w_bf16, bm=1024, bn=4352):
    t = x2.shape[0]
    n = w_bf16.shape[1]
    return pl.pallas_call(
        _in_proj_kernel,
        out_shape=jax.ShapeDtypeStruct((t, n), BF16),
        grid=(t // bm, n // bn),
        in_specs=[
            pl.BlockSpec((bm, D_MODEL), lambda i, j: (i, 0)),
            pl.BlockSpec((1, D_MODEL), lambda i, j: (0, 0)),
            pl.BlockSpec((D_MODEL, bn), lambda i, j: (0, j)),
        ],
        out_specs=pl.BlockSpec((bm, bn), lambda i, j: (i, j)),
        scratch_shapes=[pltpu.VMEM((bm, D_MODEL), BF16)],
        compiler_params=_cp(("parallel", "arbitrary")),
        name="in_proj",
    )(x2, gain, w_bf16)


def _retention_kernel(dec_ref, q_ref, k_ref, v_ref, g_ref, cos_ref, sin_ref, tab_ref, gain_ref,
                      o_ref, lhs, kb, rhs, kvf, kvb):
    h = pl.program_id(1)
    seq = q_ref.shape[0]
    n_chunks = seq // CHUNK
    dec_f = dec_ref[2 * h]
    dec_b = dec_ref[2 * h + 1]
    nt = (((1,), (1,)), ((), ()))
    tn = (((0,), (0,)), ((), ()))

    def rope(x, c, s):
        return x * c + pltpu.roll(x, RET_QK // 2, 1) * s

    def prep(n, carry):
        rows = pl.ds(pl.multiple_of(n * CHUNK, CHUNK), CHUNK)
        c = cos_ref[rows, :]
        s = sin_ref[rows, :]
        q = rope(q_ref[rows, :].astype(F32), c, s)
        k = rope(k_ref[rows, :].astype(F32), c, s) * (RET_QK ** -0.5)
        lhs[rows, 0:CHUNK] = q.astype(BF16)
        lhs[rows, CHUNK:2 * CHUNK] = (q * tab_ref[1]).astype(BF16)
        lhs[rows, 2 * CHUNK:3 * CHUNK] = (q * tab_ref[3]).astype(BF16)
        kb[rows, :] = k.astype(BF16)
        v = v_ref[rows, :]
        rhs[n, 0:CHUNK, :] = v
        kvf[n] = lax.dot_general((k * tab_ref[2]).astype(BF16), v, tn, preferred_element_type=F32)
        kvb[n] = lax.dot_general((k * tab_ref[4]).astype(BF16), v, tn, preferred_element_type=F32)
        return carry

    lax.fori_loop(0, n_chunks, prep, 0, unroll=4)

    def scan_f(n, state):
        rhs[n, CHUNK:2 * CHUNK, :] = state.astype(BF16)
        return state * dec_f + kvf[n]

    lax.fori_loop(0, n_chunks, scan_f, jnp.zeros((RET_QK, RET_V), F32))

    def scan_b(i, state):
        n = n_chunks - 1 - i
        rhs[n, 2 * CHUNK:3 * CHUNK, :] = state.astype(BF16)
        return state * dec_b + kvb[n]

    lax.fori_loop(0, n_chunks, scan_b, jnp.zeros((RET_QK, RET_V), F32))

    def chunk(n, carry):
        rows = pl.ds(pl.multiple_of(n * CHUNK, CHUNK), CHUNK)
        sc = lax.dot_general(lhs[rows, 0:CHUNK], kb[rows, :], nt, preferred_element_type=F32)
        a = jnp.concatenate([(sc * tab_ref[0]).astype(BF16), lhs[rows, CHUNK:3 * CHUNK]], axis=1)
        o = jnp.dot(a, rhs[n], preferred_element_type=F32)
        y = _rms(o, gain_ref[...])
        g = g_ref[rows, :].astype(F32)
        o_ref[rows, :] = (y * (g * jax.nn.sigmoid(g))).astype(BF16)
        return carry

    lax.fori_loop(0, n_chunks, chunk, 0, unroll=8)


def _retention(proj3, gain, cos, sin, tabs, decays):
    b, seq, _ = proj3.shape
    return pl.pallas_call(
        _retention_kernel,
        out_shape=jax.ShapeDtypeStruct((b, seq, RET_HEADS * RET_V), BF16),
        grid=(b, RET_HEADS),
        in_specs=[
            pl.BlockSpec(memory_space=pltpu.SMEM),
            pl.BlockSpec((None, seq, RET_QK), lambda i, h: (i, 0, OFF_RQ // RET_QK + h)),
            pl.BlockSpec((None, seq, RET_QK), lambda i, h: (i, 0, OFF_RK // RET_QK + h)),
            pl.BlockSpec((None, seq, RET_V), lambda i, h: (i, 0, OFF_RV // RET_V + h)),
            pl.BlockSpec((None, seq, RET_V), lambda i, h: (i, 0, OFF_RG // RET_V + h)),
            pl.BlockSpec((seq, RET_QK), lambda i, h: (0, 0)),
            pl.BlockSpec((seq, RET_QK), lambda i, h: (0, 0)),
            pl.BlockSpec((None, 5, CHUNK, CHUNK), lambda i, h: (h, 0, 0, 0)),
            pl.BlockSpec((1, RET_V), lambda i, h: (0, h)),
        ],
        out_specs=pl.BlockSpec((None, seq, RET_V), lambda i, h: (i, 0, h)),
        scratch_shapes=[
            pltpu.VMEM((seq, 3 * CHUNK), BF16),
            pltpu.VMEM((seq, RET_QK), BF16),
            pltpu.VMEM((seq // CHUNK, 3 * CHUNK, RET_V), BF16),
            pltpu.VMEM((seq // CHUNK, RET_QK, RET_V), F32),
            pltpu.VMEM((seq // CHUNK, RET_QK, RET_V), F32),
        ],
        compiler_params=_cp(("parallel", "parallel")),
        name="retention",
    )(decays, proj3, proj3, proj3, proj3, cos, sin, tabs, gain)


def _retention_tables(seq):
    half = RET_QK // 2
    inv = ROPE_BASE ** (-jnp.arange(half, dtype=F32) / half)
    ang = jnp.arange(seq, dtype=F32)[:, None] * inv[None, :]
    cos = jnp.concatenate([jnp.cos(ang), jnp.cos(ang)], axis=1)
    sin = jnp.concatenate([-jnp.sin(ang), jnp.sin(ang)], axis=1)
    heads = jnp.arange(RET_HEADS, dtype=F32)
    lg_f = jnp.log1p(-jnp.exp2(-DECAY_FWD - heads))[:, None, None]
    lg_b = jnp.log1p(-jnp.exp2(-DECAY_BWD - heads))[:, None, None]
    idx = jnp.arange(CHUNK, dtype=F32)
    diff = (idx[:, None] - idx[None, :])[None]
    dmat = jnp.where(diff >= 0, jnp.exp(jnp.where(diff >= 0, diff, 0.0) * lg_f),
                     jnp.exp(jnp.where(diff < 0, -diff, 0.0) * lg_b))
    col = jnp.broadcast_to(idx[None, :, None], (RET_HEADS, CHUNK, CHUNK))
    xi_f = jnp.exp((col + 1.0) * lg_f)
    zeta_f = jnp.exp((CHUNK - 1.0 - col) * lg_f)
    xi_b = jnp.exp((CHUNK - col) * lg_b)
    zeta_b = jnp.exp(col * lg_b)
    tabs = jnp.stack([dmat, xi_f, zeta_f, xi_b, zeta_b], axis=1)
    decays = jnp.stack([jnp.exp(CHUNK * lg_f[:, 0, 0]), jnp.exp(CHUNK * lg_b[:, 0, 0])], axis=1).reshape(-1)
    return cos, sin, tabs, decays


def _window_kernel(sink_ref, q_ref, k_ref, v_ref, qg_ref, kg_ref, bias_ref, o_ref,
                   qn, kp, vp, s_scr, p_scr, e_scr):
    g = pl.program_id(1)
    seq = q_ref.shape[0]
    n_blocks = seq // WINDOW
    stack = ATT_GROUP * WINDOW
    unit = 64
    nt = (((1,), (1,)), ((), ()))
    heads = [slice(r * ATT_D, (r + 1) * ATT_D) for r in range(ATT_GROUP)]
    kp[0:WINDOW, :] = jnp.zeros((WINDOW, ATT_D), BF16)
    kp[seq + WINDOW:seq + 2 * WINDOW, :] = jnp.zeros((WINDOW, ATT_D), BF16)
    vp[0:WINDOW, :] = jnp.zeros((WINDOW, 2 * ATT_D), BF16)
    vp[seq + WINDOW:seq + 2 * WINDOW, :] = jnp.zeros((WINDOW, 2 * ATT_D), BF16)

    def prep(n, carry):
        r0 = pl.multiple_of(n * WINDOW, WINDOW)
        src = pl.ds(r0, WINDOW)
        dst = pl.ds(r0 + WINDOW, WINDOW)
        kp[dst, :] = _rms(k_ref[src, :].astype(F32), kg_ref[...]).astype(BF16)
        vp[dst, 0:ATT_D] = v_ref[src, :]
        vp[dst, ATT_D:2 * ATT_D] = jnp.ones((WINDOW, ATT_D), BF16)
        for hs in heads:
            q = _rms(q_ref[src, hs].astype(F32), qg_ref[...]) * (ATT_D ** -0.5)
            qn[src, hs] = q.astype(BF16)
        return carry

    lax.fori_loop(0, n_blocks, prep, 0, unroll=2)

    def scores(n, carry):
        r0 = pl.multiple_of(n * WINDOW, WINDOW)
        kb = kp[pl.ds(r0, 3 * WINDOW), :]
        q4 = jnp.concatenate([qn[pl.ds(r0, WINDOW), hs] for hs in heads], axis=0)
        edge = jnp.where(n == 0, 0, jnp.where(n == n_blocks - 1, 2, 1))
        s_scr[pl.ds(pl.multiple_of(n * stack, stack), stack), :] = (
            lax.dot_general(q4, kb, nt, preferred_element_type=F32) + bias_ref[edge])
        return carry

    lax.fori_loop(0, n_blocks, scores, 0, unroll=2)

    def softmax(n, carry):
        base = pl.multiple_of(n * stack, stack)
        for u in range(stack // unit):
            rows = pl.ds(base + u * unit, unit)
            sink = sink_ref[g * ATT_GROUP + (u * unit) // WINDOW]
            s = s_scr[rows, :]
            m = jnp.maximum(jnp.max(s, axis=-1, keepdims=True), sink)
            mb = jnp.broadcast_to(m, (unit, WINDOW))
            p_scr[rows, :] = jnp.exp(s - jnp.concatenate([mb, mb, mb], axis=1)).astype(BF16)
            e_scr[rows, :] = jnp.exp(sink - mb)
        return carry

    lax.fori_loop(0, n_blocks, softmax, 0)

    def outputs(n, carry):
        r0 = pl.multiple_of(n * WINDOW, WINDOW)
        rows = pl.ds(pl.multiple_of(n * stack, stack), stack)
        oe = jnp.dot(p_scr[rows, :], vp[pl.ds(r0, 3 * WINDOW), :], preferred_element_type=F32)
        o = oe[:, 0:ATT_D] / (oe[:, ATT_D:2 * ATT_D] + e_scr[rows, :])
        for r, hs in enumerate(heads):
            o_ref[pl.ds(r0, WINDOW), hs] = o[r * WINDOW:(r + 1) * WINDOW, :].astype(BF16)
        return carry

    lax.fori_loop(0, n_blocks, outputs, 0, unroll=2)


def _window_attention(proj3, q_gain, k_gain, sink, bias):
    b, seq, _ = proj3.shape
    gw = ATT_GROUP * ATT_D
    return pl.pallas_call(
        _window_kernel,
        out_shape=jax.ShapeDtypeStruct((b, seq, ATT_HEADS * ATT_D), BF16),
        grid=(b, ATT_KV),
        in_specs=[
            pl.BlockSpec(memory_space=pltpu.SMEM),
            pl.BlockSpec((None, seq, gw), lambda i, g: (i, 0, OFF_AQ // gw + g)),
            pl.BlockSpec((None, seq, ATT_D), lambda i, g: (i, 0, OFF_AK // ATT_D + g)),
            pl.BlockSpec((None, seq, ATT_D), lambda i, g: (i, 0, OFF_AV // ATT_D + g)),
            pl.BlockSpec((1, ATT_D), lambda i, g: (0, 0)),
            pl.BlockSpec((1, ATT_D), lambda i, g: (0, 0)),
            pl.BlockSpec((None, 3, ATT_GROUP * WINDOW, 3 * WINDOW), lambda i, g: (g, 0, 0, 0)),
        ],
        out_specs=pl.BlockSpec((None, seq, gw), lambda i, g: (i, 0, g)),
        scratch_shapes=[
            pltpu.VMEM((seq, gw), BF16),
            pltpu.VMEM((seq + 2 * WINDOW, ATT_D), BF16),
            pltpu.VMEM((seq + 2 * WINDOW, 2 * ATT_D), BF16),
            pltpu.VMEM((seq * ATT_GROUP, 3 * WINDOW), F32),
            pltpu.VMEM((seq * ATT_GROUP, 3 * WINDOW), BF16),
            pltpu.VMEM((seq * ATT_GROUP, WINDOW), F32),
        ],
        compiler_params=_cp(("parallel", "parallel")),
        name="window_attention",
    )(sink, proj3, proj3, proj3, q_gain, k_gain, bias)


def _window_bias(rel_table):
    nb = REL_BUCKETS // 2
    max_exact = nb // 2
    qi = jnp.arange(WINDOW)[:, None]
    ki = jnp.arange(3 * WINDOW)[None, :]
    rel = ki - WINDOW - qi
    ret = jnp.where(rel > 0, nb, 0)
    n = jnp.abs(rel)
    large = max_exact + (jnp.log(jnp.maximum(n, 1).astype(F32) / max_exact)
                         / math.log(REL_MAX_DIST / max_exact) * (nb - max_exact)).astype(I32)
    large = jnp.minimum(large, nb - 1)
    bucket = ret + jnp.where(n < max_exact, n, large)
    onehot = (bucket[None] == jnp.arange(REL_BUCKETS)[:, None, None]).astype(F32)
    bias = jnp.einsum("bh,bqk->hqk", rel_table.astype(F32), onehot, precision=lax.Precision.HIGHEST)
    bias = jnp.where((jnp.abs(rel) <= WINDOW)[None], bias, -jnp.inf)
    bias = bias.reshape(ATT_KV, ATT_GROUP * WINDOW, 3 * WINDOW)
    first = jnp.where(ki < WINDOW, -jnp.inf, bias)
    last = jnp.where(ki >= 2 * WINDOW, -jnp.inf, bias)
    return jnp.stack([first, bias, last], axis=1)


def _mem_kv_kernel(m_ref, g_ref, w_ref, kg_ref, o_ref, h_scr):
    j = pl.program_id(1)

    @pl.when(j == 0)
    def _():
        h_scr[...] = _rms(m_ref[...], g_ref[...]).astype(BF16)

    o = jnp.dot(h_scr[...], w_ref[...], preferred_element_type=F32)
    o_ref[...] = jnp.where(j < MEM_HEADS, _rms(o, kg_ref[...]), o).astype(BF16)


def _mem_kv(mem2, gain, w_bf16, k_gain, m_len):
    rows = mem2.shape[0]
    n = w_bf16.shape[1]
    return pl.pallas_call(
        _mem_kv_kernel,
        out_shape=jax.ShapeDtypeStruct((rows, n), BF16),
        grid=(rows // m_len, n // MEM_D),
        in_specs=[
            pl.BlockSpec((m_len, D_MODEL), lambda i, j: (i, 0)),
            pl.BlockSpec((1, D_MODEL), lambda i, j: (0, 0)),
            pl.BlockSpec((D_MODEL, MEM_D), lambda i, j: (0, j)),
            pl.BlockSpec((1, MEM_D), lambda i, j: (0, 0)),
        ],
        out_specs=pl.BlockSpec((m_len, MEM_D), lambda i, j: (i, j)),
        scratch_shapes=[pltpu.VMEM((m_len, D_MODEL), BF16)],
        compiler_params=_cp(("parallel", "arbitrary")),
        name="mem_kv",
    )(mem2, gain, w_bf16, k_gain)


def _mem_attn_kernel(q_ref, k_ref, v_ref, qg_ref, o_ref, *, bq):
    seq = q_ref.shape[0]
    nt = (((1,), (1,)), ((), ()))

    def block(n, carry):
        rows = pl.ds(pl.multiple_of(n * bq, bq), bq)
        q = _rms(q_ref[rows, :].astype(F32), qg_ref[...]) * (MEM_D ** -0.5)
        s = lax.dot_general(q.astype(BF16), k_ref[...], nt, preferred_element_type=F32)
        p = jnp.exp(s - jnp.max(s, axis=-1, keepdims=True))
        denom = jnp.sum(p, axis=-1, keepdims=True)
        o = jnp.dot(p.astype(BF16), v_ref[...], preferred_element_type=F32) / denom
        o_ref[rows, :] = o.astype(BF16)
        return carry

    lax.fori_loop(0, seq // bq, block, 0, unroll=8)


def _mem_attention(proj3, mkv3, q_gain, bq=256):
    b, seq, _ = proj3.shape
    m_len = mkv3.shape[1]
    return pl.pallas_call(
        functools.partial(_mem_attn_kernel, bq=bq),
        out_shape=jax.ShapeDtypeStruct((b, seq, MEM_HEADS * MEM_D), BF16),
        grid=(b, MEM_HEADS),
        in_specs=[
            pl.BlockSpec((None, seq, MEM_D), lambda i, h: (i, 0, OFF_MQ // MEM_D + h)),
            pl.BlockSpec((None, m_len, MEM_D), lambda i, h: (i, 0, h)),
            pl.BlockSpec((None, m_len, MEM_D), lambda i, h: (i, 0, MEM_HEADS + h)),
            pl.BlockSpec((1, MEM_D), lambda i, h: (0, 0)),
        ],
        out_specs=pl.BlockSpec((None, seq, MEM_D), lambda i, h: (i, 0, h)),
        compiler_params=_cp(("parallel", "parallel")),
        name="mem_attention",
    )(proj3, mkv3, mkv3, q_gain)


def _merge_kernel(x_ref, g0_ref, g1_ref, g2_ref, b0_ref, b1_ref, b2_ref, wb_ref, wo_ref, gn_ref,
                  wr_ref, xo_ref, hp_ref, lg_ref):
    merged = None
    for gate_ref, br, i in ((g0_ref, b0_ref, 0), (g1_ref, b1_ref, 1), (g2_ref, b2_ref, 2)):
        gate = 0.5 * jnp.tanh(0.5 * gate_ref[...].astype(F32)) + 0.5
        t = gate * jnp.dot(br[...], wb_ref[i], preferred_element_type=F32)
        merged = t if merged is None else merged + t
    x = x_ref[...] + jnp.dot(merged.astype(BF16), wo_ref[...], preferred_element_type=F32)
    xo_ref[...] = x
    h = _rms(x, gn_ref[...])
    h_hi = h.astype(BF16)
    h_lo = (h - h_hi.astype(F32)).astype(BF16)
    lg_ref[...] = jnp.dot(jnp.concatenate([h_hi, h_lo], axis=1), wr_ref[...], preferred_element_type=F32)
    hp_ref[...] = _pack_bf16_pairs(h)


def _merge(x2, proj, ret, att, mo, wb, wo, gain, w_router_cat, bm=512):
    t = x2.shape[0]
    const2 = lambda i: (0, 0)
    row = lambda i: (i, 0)
    return pl.pallas_call(
        _merge_kernel,
        out_shape=(
            jax.ShapeDtypeStruct((t, D_MODEL), F32),
            jax.ShapeDtypeStruct((t, HALF), I32),
            jax.ShapeDtypeStruct((t, ROUTER_LANES), F32),
        ),
        grid=(t // bm,),
        in_specs=[
            pl.BlockSpec((bm, D_MODEL), row),
            pl.BlockSpec((bm, D_MODEL), lambda i: (i, 0)),
            pl.BlockSpec((bm, D_MODEL), lambda i: (i, 1)),
            pl.BlockSpec((bm, D_MODEL), lambda i: (i, 2)),
            pl.BlockSpec((bm, D_MODEL), row),
            pl.BlockSpec((bm, D_MODEL), row),
            pl.BlockSpec((bm, D_MODEL), row),
            pl.BlockSpec((3, D_MODEL, D_MODEL), lambda i: (0, 0, 0)),
            pl.BlockSpec((D_MODEL, D_MODEL), const2),
            pl.BlockSpec((1, D_MODEL), const2),
            pl.BlockSpec((2 * D_MODEL, ROUTER_LANES), const2),
        ],
        out_specs=(
            pl.BlockSpec((bm, D_MODEL), row),
            pl.BlockSpec((bm, HALF), row),
            pl.BlockSpec((bm, ROUTER_LANES), row),
        ),
        compiler_params=_cp(("parallel",)),
        name="merge_router",
    )(x2, proj, proj, proj, ret, att, mo, wb, wo, gain, w_router_cat)


def _route_kernel(l_ref, b_ref, tri_ref, eid_ref, rank_ref, gate_ref, cnt_ref):
    @pl.when(pl.program_id(0) == 0)
    def _():
        cnt_ref[...] = jnp.zeros_like(cnt_ref)

    l = l_ref[0:N_EXPERTS, :] + l_ref[N_EXPERTS:2 * N_EXPERTS, :] + b_ref[...]
    ne, bt = l.shape
    iota_e = lax.broadcasted_iota(I32, (ne, bt), 0)
    picked = jnp.zeros((ne, bt), jnp.bool_)
    vals, idxs = [], []
    for _ in range(TOP_K):
        m = jnp.max(l, axis=0, keepdims=True)
        idx = jnp.min(jnp.where(l == m, iota_e, ne), axis=0, keepdims=True)
        sel = iota_e == idx
        picked = picked | sel
        l = jnp.where(sel, -jnp.inf, l)
        vals.append(m)
        idxs.append(idx)
    ex = [jnp.exp(v - vals[0]) for v in vals]
    tot = ex[0] + ex[1] + ex[2] + ex[3]
    onehot = jnp.where(picked, 1.0, 0.0)
    before = jnp.dot(onehot.astype(BF16), tri_ref[...], preferred_element_type=F32) + cnt_ref[:, 0:1]
    for k in range(TOP_K):
        eid_ref[k:k + 1, :] = idxs[k]
        gate_ref[k:k + 1, :] = ex[k] / tot
        rank_ref[k:k + 1, :] = jnp.sum(jnp.where(iota_e == idxs[k], before, 0.0), axis=0,
                                       keepdims=True).astype(I32)
    cnt_ref[...] = cnt_ref[...] + jnp.sum(onehot, axis=1, keepdims=True)


def _route(logits_t, bias_col, tri, bt=512):
    ne, t = N_EXPERTS, logits_t.shape[1]
    blk = lambda i: (0, i)
    return pl.pallas_call(
        _route_kernel,
        out_shape=(
            jax.ShapeDtypeStruct((TOP_K, t), I32),
            jax.ShapeDtypeStruct((TOP_K, t), I32),
            jax.ShapeDtypeStruct((TOP_K, t), F32),
            jax.ShapeDtypeStruct((ne, 128), F32),
        ),
        grid=(t // bt,),
        in_specs=[pl.BlockSpec((ROUTER_LANES, bt), blk), pl.BlockSpec((ne, 1), lambda i: (0, 0)),
                  pl.BlockSpec((bt, bt), lambda i: (0, 0))],
        out_specs=(
            pl.BlockSpec((TOP_K, bt), blk),
            pl.BlockSpec((TOP_K, bt), blk),
            pl.BlockSpec((TOP_K, bt), blk),
            pl.BlockSpec((ne, 128), lambda i: (0, 0)),
        ),
        compiler_params=_cp(("arbitrary",)),
        name="route_topk",
    )(logits_t, bias_col, tri)


def _slot_kernel(start_ref, eid_ref, rank_ref, slot_ref):
    eid = eid_ref[...]
    base = jnp.zeros(eid.shape, I32)
    for e in range(N_EXPERTS):
        base = jnp.where(eid == e, start_ref[e], base)
    slot_ref[...] = base + rank_ref[...]


def _slots(pad_starts, eid, rank, bt=2048):
    t = eid.shape[1]
    blk = lambda i, s: (0, i)
    return pl.pallas_call(
        _slot_kernel,
        out_shape=jax.ShapeDtypeStruct((TOP_K, t), I32),
        grid_spec=pltpu.PrefetchScalarGridSpec(
            num_scalar_prefetch=1,
            grid=(t // bt,),
            in_specs=[pl.BlockSpec((TOP_K, bt), blk), pl.BlockSpec((TOP_K, bt), blk)],
            out_specs=pl.BlockSpec((TOP_K, bt), blk),
        ),
        compiler_params=_cp(("parallel",)),
        name="route_slots",
    )(pad_starts, eid, rank)


def _sc_workers():
    info = plsc.get_sparse_core_info()
    return info.num_cores, info.num_subcores


def _dispatch_rows(hp, slot, n_rows, win=64):
    t, w = hp.shape
    nc, ns = _sc_workers()
    per_worker = t // (nc * ns)
    mesh = plsc.VectorSubcoreMesh(core_axis_name="c", subcore_axis_name="s")

    @functools.partial(
        pl.kernel,
        out_type=jax.ShapeDtypeStruct((n_rows, w), hp.dtype),
        mesh=mesh,
        scratch_types=[pltpu.VMEM((win,), I32)] * TOP_K + [pltpu.VMEM((win, w), hp.dtype)],
        name="moe_dispatch",
    )
    def k(hp_hbm, slot_hbm, buf_hbm, *scratch):
        idx_v, rows_v = scratch[:TOP_K], scratch[TOP_K]
        wid = lax.axis_index("s") * nc + lax.axis_index("c")
        base = wid * per_worker

        @pl.loop(0, per_worker // win)
        def _(i):
            off = pl.multiple_of(base + i * win, win)
            pltpu.sync_copy(hp_hbm.at[pl.ds(off, win)], rows_v)
            for kk in range(TOP_K):
                pltpu.sync_copy(slot_hbm.at[pl.ds(kk * t + off, win)], idx_v[kk])
                pltpu.sync_copy(rows_v, buf_hbm.at[idx_v[kk]])

    return k(hp, slot.reshape(-1))


def _gather_rows(out_rows, slot, win=64):
    t = slot.shape[1]
    w = out_rows.shape[1]
    nc, ns = _sc_workers()
    per_worker = t // (nc * ns)
    mesh = plsc.VectorSubcoreMesh(core_axis_name="c", subcore_axis_name="s")

    @functools.partial(
        pl.kernel,
        out_type=jax.ShapeDtypeStruct((TOP_K * t, w), out_rows.dtype),
        mesh=mesh,
        scratch_types=[pltpu.VMEM((win,), I32), pltpu.VMEM((win, w), out_rows.dtype)],
        name="moe_gather",
    )
    def k(rows_hbm, slot_hbm, og_hbm, idx_v, rows_v):
        wid = lax.axis_index("s") * nc + lax.axis_index("c")
        base = wid * per_worker

        @pl.loop(0, per_worker // win)
        def _(i):
            off = pl.multiple_of(base + i * win, win)
            for kk in range(TOP_K):
                pltpu.sync_copy(slot_hbm.at[pl.ds(kk * t + off, win)], idx_v)
                pltpu.sync_copy(rows_hbm.at[idx_v], rows_v)
                pltpu.sync_copy(rows_v, og_hbm.at[pl.ds(kk * t + off, win)])

    return k(out_rows, slot.reshape(-1)).reshape(TOP_K, t, w)


def _expert_kernel(be_ref, nu_ref, x_ref, wi_ref, bi_ref, wo_ref, bo_ref, o_ref):
    @pl.when(pl.program_id(0) < nu_ref[0])
    def _():
        hi, lo = _unpack_bf16_pairs(x_ref[...])
        h = (jnp.dot(hi.astype(BF16), wi_ref[0:HALF, :], preferred_element_type=F32)
             + jnp.dot(lo.astype(BF16), wi_ref[HALF:, :], preferred_element_type=F32) + bi_ref[...])
        gate = jnp.minimum(h[:, :D_FF], SWIGLU_LIMIT)
        lin = jnp.clip(h[:, D_FF:], -SWIGLU_LIMIT, SWIGLU_LIMIT)
        act = gate * jax.nn.sigmoid(SWIGLU_ALPHA * gate) * (lin + 1.0)
        o = jnp.dot(act.astype(BF16), wo_ref[...], preferred_element_type=F32) + bo_ref[...]
        o_ref[...] = _pack_bf16_pairs(o)


def _experts(block_expert, n_used, buf, w_in, b_in, w_out, b_out):
    n_rows, w = buf.shape
    n_blocks = n_rows // EXPERT_BLOCK
    return pl.pallas_call(
        _expert_kernel,
        out_shape=jax.ShapeDtypeStruct((n_rows, w), buf.dtype),
        grid_spec=pltpu.PrefetchScalarGridSpec(
            num_scalar_prefetch=2,
            grid=(n_blocks,),
            in_specs=[
                pl.BlockSpec((EXPERT_BLOCK, w), lambda i, be, nu: (i, 0)),
                pl.BlockSpec((None, D_MODEL, 2 * D_FF), lambda i, be, nu: (be[i], 0, 0)),
                pl.BlockSpec((None, 1, 2 * D_FF), lambda i, be, nu: (be[i], 0, 0)),
                pl.BlockSpec((None, D_FF, D_MODEL), lambda i, be, nu: (be[i], 0, 0)),
                pl.BlockSpec((None, 1, D_MODEL), lambda i, be, nu: (be[i], 0, 0)),
            ],
            out_specs=pl.BlockSpec((EXPERT_BLOCK, w), lambda i, be, nu: (i, 0)),
        ),
        compiler_params=_cp(("arbitrary",)),
        name="expert_ffn",
    )(block_expert, n_used, buf, w_in, b_in, w_out, b_out)


def _combine_kernel(x_ref, og_ref, g_ref, y_ref):
    g = g_ref[...]
    acc_hi = x_ref[:, :HALF]
    acc_lo = x_ref[:, HALF:]
    for k in range(TOP_K):
        hi, lo = _unpack_bf16_pairs(og_ref[k])
        gk = g[:, k:k + 1]
        acc_hi = acc_hi + gk * hi
        acc_lo = acc_lo + gk * lo
    y_ref[:, :HALF] = acc_hi
    y_ref[:, HALF:] = acc_lo


def _combine(x_mid, og, gates_tk, bm=512):
    t = x_mid.shape[0]
    return pl.pallas_call(
        _combine_kernel,
        out_shape=jax.ShapeDtypeStruct((t, D_MODEL), F32),
        grid=(t // bm,),
        in_specs=[
            pl.BlockSpec((bm, D_MODEL), lambda i: (i, 0)),
            pl.BlockSpec((TOP_K, bm, HALF), lambda i: (0, i, 0)),
            pl.BlockSpec((bm, TOP_K), lambda i: (i, 0)),
        ],
        out_specs=pl.BlockSpec((bm, D_MODEL), lambda i: (i, 0)),
        compiler_params=_cp(("parallel",)),
        name="moe_combine",
    )(x_mid, og, gates_tk)


def _block_tables(counts, n_blocks):
    padded = (counts + EXPERT_BLOCK - 1) // EXPERT_BLOCK * EXPERT_BLOCK
    pad_ends = jnp.cumsum(padded)
    pad_starts = pad_ends - padded
    block_start = jnp.arange(n_blocks, dtype=I32) * EXPERT_BLOCK
    block_expert = jnp.minimum(jnp.sum(pad_ends[None, :] <= block_start[:, None], axis=1), N_EXPERTS - 1)
    n_used = (pad_ends[-1] // EXPERT_BLOCK).reshape(1)
    return pad_starts.astype(I32), block_expert.astype(I32), n_used.astype(I32)


def _layer(x, mem, consts, params):
    b, seq, d = x.shape
    m_len = mem.shape[1]
    t = b * seq
    x2 = x.reshape(t, d)
    proj = _in_proj(x2, params["norm_mix"], params["w_in"])
    proj3 = proj.reshape(b, seq, IN_WIDTH)
    ret = _retention(proj3, params["ret_out_norm"], consts["cos"], consts["sin"], consts["tabs"],
                     consts["decays"])
    att = _window_attention(proj3, params["att_q_norm"], params["att_k_norm"], params["att_sink"],
                            consts["bias"])
    mkv = _mem_kv(mem.reshape(b * m_len, d), params["mem_norm"], params["w_mem_kv"],
                  params["mem_k_norm"], m_len)
    mo = _mem_attention(proj3, mkv.reshape(b, m_len, 2 * MEM_HEADS * MEM_D), params["mem_q_norm"])
    x_mid, hp, logits = _merge(x2, proj, ret.reshape(t, d), att.reshape(t, d), mo.reshape(t, d),
                               params["w_branch"], params["w_out"], params["norm_ffn"],
                               params["w_router"])
    eid, rank, gates, counts = _route(logits.T, params["b_router"], consts["tri"])
    n_blocks = t * TOP_K // EXPERT_BLOCK + N_EXPERTS
    pad_starts, block_expert, n_used = _block_tables(counts[:, 0].astype(I32), n_blocks)
    slot = _slots(pad_starts, eid, rank)
    buf = _dispatch_rows(hp, slot, n_blocks * EXPERT_BLOCK)
    out_rows = _experts(block_expert, n_used, buf, params["w_e_in"], params["b_e_in"],
                        params["w_e_out"], params["b_e_out"])
    og = _gather_rows(out_rows, slot)
    y = _combine(x_mid, og, gates.T)
    return y.reshape(b, seq, d)


def _router_weights(w):
    w_hi = w.astype(BF16)
    w_lo = (w - w_hi.astype(F32)).astype(BF16)
    pad = jnp.zeros((D_MODEL, ROUTER_LANES - 2 * N_EXPERTS), BF16)
    top = jnp.concatenate([w_hi, w_lo, pad], axis=1)
    bottom = jnp.concatenate([w_hi, jnp.zeros_like(w_lo), pad], axis=1)
    return jnp.concatenate([top, bottom], axis=0)


def _prepare(seq, rel_table, norm_mix, w_in, ret_out_norm, att_q_norm, att_k_norm, att_sink, mem_norm,
             w_mem_kv, mem_q_norm, mem_k_norm, w_branch, w_out, norm_ffn, w_router, b_router,
             w_e_in, b_e_in, w_e_out, b_e_out):
    gates_at = IN_WIDTH - 3 * D_MODEL
    w_in0 = w_in[0]
    params = {
        "norm_mix": norm_mix[0].reshape(1, -1),
        "w_in": jnp.concatenate([w_in0[:, gates_at:], w_in0[:, :gates_at]], axis=1).astype(BF16),
        "ret_out_norm": ret_out_norm[0].reshape(1, -1),
        "att_q_norm": att_q_norm[0].reshape(1, -1),
        "att_k_norm": att_k_norm[0].reshape(1, -1),
        "att_sink": att_sink[0].astype(F32),
        "mem_norm": mem_norm[0].reshape(1, -1),
        "w_mem_kv": w_mem_kv[0].astype(BF16),
        "mem_q_norm": mem_q_norm[0].reshape(1, -1),
        "mem_k_norm": mem_k_norm[0].reshape(1, -1),
        "w_branch": w_branch[0].astype(BF16),
        "w_out": w_out[0].astype(BF16),
        "norm_ffn": norm_ffn[0].reshape(1, -1),
        "w_router": _router_weights(w_router[0]),
        "b_router": b_router[0].reshape(-1, 1),
        "w_e_in": w_e_in[0].astype(BF16),
        "b_e_in": b_e_in[0].reshape(N_EXPERTS, 1, -1),
        "w_e_out": w_e_out[0].astype(BF16),
        "b_e_out": b_e_out[0].reshape(N_EXPERTS, 1, -1),
    }
    cos, sin, tabs, decays = _retention_tables(seq)
    route_bt = 512
    tri = (jnp.arange(route_bt)[:, None] < jnp.arange(route_bt)[None, :]).astype(BF16)
    consts = {"cos": cos, "sin": sin, "tabs": tabs, "decays": decays,
              "bias": _window_bias(rel_table), "tri": tri}
    return consts, params


def kernel(x_prompt, x_sample, mem_prompt, mem_sample, rel_table, norm_mix, w_in, ret_out_norm, att_q_norm, att_k_norm, att_sink, mem_norm, w_mem_kv, mem_q_norm, mem_k_norm, w_branch, w_out, norm_ffn, w_router, b_router, w_e_in, b_e_in, w_e_out, b_e_out):
    consts, params = _prepare(x_prompt.shape[1], rel_table, norm_mix, w_in, ret_out_norm, att_q_norm,
                              att_k_norm, att_sink, mem_norm, w_mem_kv, mem_q_norm, mem_k_norm,
                              w_branch, w_out, norm_ffn, w_router, b_router, w_e_in, b_e_in,
                              w_e_out, b_e_out)
    y_prompt = _layer(x_prompt, mem_prompt, consts, params)
    y_sample = _layer(x_sample, mem_sample, consts, params)
    return (y_prompt, y_sample)
```

```python
import functools
import math

import jax
import jax.numpy as jnp
import numpy as np
from jax import lax
from jax.experimental import pallas as pl
from jax.experimental.pallas import tpu as pltpu
from jax.experimental.pallas import tpu_sc as plsc

F32 = jnp.float32
BF16 = jnp.bfloat16
I32 = jnp.int32

D_MODEL = 1024
EPS = 1e-6
RET_HEADS = 4
RET_QK = 128
RET_V = 256
CHUNK = 128
ROPE_BASE = 10000.0
DECAY_FWD = 5.0
DECAY_BWD = 5.5
ATT_HEADS = 8
ATT_KV = 2
ATT_GROUP = ATT_HEADS // ATT_KV
ATT_D = 128
WINDOW = 128
REL_BUCKETS = 32
REL_MAX_DIST = 128
MEM_HEADS = 4
MEM_D = 256
N_EXPERTS = 32
TOP_K = 4
D_FF = 1024
SWIGLU_LIMIT = 7.0
SWIGLU_ALPHA = 1.702
EXPERT_BLOCK = 512
HALF = D_MODEL // 2
ROUTER_LANES = 128

IN_WIDTH = 8704
OFF_GATES = 0
OFF_RQ = 3072
OFF_RK = 3584
OFF_RV = 4096
OFF_RG = 5120
OFF_AQ = 6144
OFF_AK = 7168
OFF_AV = 7424
OFF_MQ = 7680

VMEM_LIMIT = 56 * 1024 * 1024


def _cp(sem, vmem=VMEM_LIMIT):
    return pltpu.CompilerParams(dimension_semantics=sem, vmem_limit_bytes=vmem)


def _rms(x, g):
    return x * lax.rsqrt(jnp.mean(x * x, axis=-1, keepdims=True) + EPS) * g


def _pack_bf16_pairs(x):
    w = x.shape[1] // 2
    bits = pltpu.bitcast(x.astype(BF16).astype(F32), I32)
    hi = bits[:, :w] & jnp.int32(-65536)
    lo = lax.shift_right_logical(bits[:, w:], jnp.int32(16))
    return hi | lo


def _unpack_bf16_pairs(u):
    hi = pltpu.bitcast(u & jnp.int32(-65536), F32)
    lo = pltpu.bitcast(lax.shift_left(u, jnp.int32(16)), F32)
    return hi, lo


def _in_proj_kernel(x_ref, g_ref, w_ref, o_ref, h_scr):
    @pl.when(pl.program_id(1) == 0)
    def _():
        h_scr[...] = _rms(x_ref[...], g_ref[...]).astype(BF16)

    sub = 256

    def cols(c, carry):
        sl = pl.ds(pl.multiple_of(c * sub, sub), sub)
        o_ref[:, sl] = jnp.dot(h_scr[...], w_ref[:, sl], preferred_element_type=F32).astype(BF16)
        return carry

    lax.fori_loop(0, o_ref.shape[1] // sub, cols, 0, unroll=True)


def _in_proj(x2, gain, w_bf16, bm=1024, bn=4352):
    t = x2.shape[0]
    n = w_bf16.shape[1]
    return pl.pallas_call(
        _in_proj_kernel,
        out_shape=jax.ShapeDtypeStruct((t, n), BF16),
        grid=(t // bm, n // bn),
        in_specs=[
            pl.BlockSpec((bm, D_MODEL), lambda i, j: (i, 0)),
            pl.BlockSpec((1, D_MODEL), lambda i, j: (0, 0)),
            pl.BlockSpec((D_MODEL, bn), lambda i, j: (0, j)),
        ],
        out_specs=pl.BlockSpec((bm, bn), lambda i, j: (i, j)),
        scratch_shapes=[pltpu.VMEM((bm, D_MODEL), BF16)],
        compiler_params=_cp(("parallel", "arbitrary")),
        name="in_proj",
    )(x2, gain, w_bf16)


def _retention_kernel(dec_ref, q_ref, k_ref, v_ref, g_ref, cos_ref, sin_ref, tab_ref, gain_ref,
                      o_ref, lhs, kb, rhs, kvf, kvb):
    h = pl.program_id(1)
    seq = q_ref.shape[0]
    n_chunks = seq // CHUNK
    dec_f = dec_ref[2 * h]
    dec_b = dec_ref[2 * h + 1]
    nt = (((1,), (1,)), ((), ()))
    tn = (((0,), (0,)), ((), ()))

    def rope(x, c, s):
        return x * c + pltpu.roll(x, RET_QK // 2, 1) * s

    def prep(n, carry):
        rows = pl.ds(pl.multiple_of(n * CHUNK, CHUNK), CHUNK)
        c = cos_ref[rows, :]
        s = sin_ref[rows, :]
        q = rope(q_ref[rows, :].astype(F32), c, s)
        k = rope(k_ref[rows, :].astype(F32), c, s) * (RET_QK ** -0.5)
        lhs[rows, 0:CHUNK] = q.astype(BF16)
        lhs[rows, CHUNK:2 * CHUNK] = (q * tab_ref[1]).astype(BF16)
        lhs[rows, 2 * CHUNK:3 * CHUNK] = (q * tab_ref[3]).astype(BF16)
        kb[rows, :] = k.astype(BF16)
        v = v_ref[rows, :]
        rhs[n, 0:CHUNK, :] = v
        kvf[n] = lax.dot_general((k * tab_ref[2]).astype(BF16), v, tn, preferred_element_type=F32)
        kvb[n] = lax.dot_general((k * tab_ref[4]).astype(BF16), v, tn, preferred_element_type=F32)
        return carry

    lax.fori_loop(0, n_chunks, prep, 0, unroll=4)

    def scan_f(n, state):
        rhs[n, CHUNK:2 * CHUNK, :] = state.astype(BF16)
        return state * dec_f + kvf[n]

    lax.fori_loop(0, n_chunks, scan_f, jnp.zeros((RET_QK, RET_V), F32))

    def scan_b(i, state):
        n = n_chunks - 1 - i
        rhs[n, 2 * CHUNK:3 * CHUNK, :] = state.astype(BF16)
        return state * dec_b + kvb[n]

    lax.fori_loop(0, n_chunks, scan_b, jnp.zeros((RET_QK, RET_V), F32))

    def chunk(n, carry):
        rows = pl.ds(pl.multiple_of(n * CHUNK, CHUNK), CHUNK)
        sc = lax.dot_general(lhs[rows, 0:CHUNK], kb[rows, :], nt, preferred_element_type=F32)
        a = jnp.concatenate([(sc * tab_ref[0]).astype(BF16), lhs[rows, CHUNK:3 * CHUNK]], axis=1)
        o = jnp.dot(a, rhs[n], preferred_element_type=F32)
        y = _rms(o, gain_ref[...])
        g = g_ref[rows, :].astype(F32)
        o_ref[rows, :] = (y * (g * jax.nn.sigmoid(g))).astype(BF16)
        return carry

    lax.fori_loop(0, n_chunks, chunk, 0, unroll=8)


def _retention(proj3, gain, cos, sin, tabs, decays):
    b, seq, _ = proj3.shape
    return pl.pallas_call(
        _retention_kernel,
        out_shape=jax.ShapeDtypeStruct((b, seq, RET_HEADS * RET_V), BF16),
        grid=(b, RET_HEADS),
        in_specs=[
            pl.BlockSpec(memory_space=pltpu.SMEM),
            pl.BlockSpec((None, seq, RET_QK), lambda i, h: (i, 0, OFF_RQ // RET_QK + h)),
            pl.BlockSpec((None, seq, RET_QK), lambda i, h: (i, 0, OFF_RK // RET_QK + h)),
            pl.BlockSpec((None, seq, RET_V), lambda i, h: (i, 0, OFF_RV // RET_V + h)),
            pl.BlockSpec((None, seq, RET_V), lambda i, h: (i, 0, OFF_RG // RET_V + h)),
            pl.BlockSpec((seq, RET_QK), lambda i, h: (0, 0)),
            pl.BlockSpec((seq, RET_QK), lambda i, h: (0, 0)),
            pl.BlockSpec((None, 5, CHUNK, CHUNK), lambda i, h: (h, 0, 0, 0)),
            pl.BlockSpec((1, RET_V), lambda i, h: (0, h)),
        ],
        out_specs=pl.BlockSpec((None, seq, RET_V), lambda i, h: (i, 0, h)),
        scratch_shapes=[
            pltpu.VMEM((seq, 3 * CHUNK), BF16),
            pltpu.VMEM((seq, RET_QK), BF16),
            pltpu.VMEM((seq // CHUNK, 3 * CHUNK, RET_V), BF16),
            pltpu.VMEM((seq // CHUNK, RET_QK, RET_V), F32),
            pltpu.VMEM((seq // CHUNK, RET_QK, RET_V), F32),
        ],
        compiler_params=_cp(("parallel", "parallel")),
        name="retention",
    )(decays, proj3, proj3, proj3, proj3, cos, sin, tabs, gain)


def _retention_tables(seq):
    half = RET_QK // 2
    inv = ROPE_BASE ** (-jnp.arange(half, dtype=F32) / half)
    ang = jnp.arange(seq, dtype=F32)[:, None] * inv[None, :]
    cos = jnp.concatenate([jnp.cos(ang), jnp.cos(ang)], axis=1)
    sin = jnp.concatenate([-jnp.sin(ang), jnp.sin(ang)], axis=1)
    heads = jnp.arange(RET_HEADS, dtype=F32)
    lg_f = jnp.log1p(-jnp.exp2(-DECAY_FWD - heads))[:, None, None]
    lg_b = jnp.log1p(-jnp.exp2(-DECAY_BWD - heads))[:, None, None]
    idx = jnp.arange(CHUNK, dtype=F32)
    diff = (idx[:, None] - idx[None, :])[None]
    dmat = jnp.where(diff >= 0, jnp.exp(jnp.where(diff >= 0, diff, 0.0) * lg_f),
                     jnp.exp(jnp.where(diff < 0, -diff, 0.0) * lg_b))
    col = jnp.broadcast_to(idx[None, :, None], (RET_HEADS, CHUNK, CHUNK))
    xi_f = jnp.exp((col + 1.0) * lg_f)
    zeta_f = jnp.exp((CHUNK - 1.0 - col) * lg_f)
    xi_b = jnp.exp((CHUNK - col) * lg_b)
    zeta_b = jnp.exp(col * lg_b)
    tabs = jnp.stack([dmat, xi_f, zeta_f, xi_b, zeta_b], axis=1)
    decays = jnp.stack([jnp.exp(CHUNK * lg_f[:, 0, 0]), jnp.exp(CHUNK * lg_b[:, 0, 0])], axis=1).reshape(-1)
    return cos, sin, tabs, decays


def _window_kernel(sink_ref, q_ref, k_ref, v_ref, qg_ref, kg_ref, bias_ref, o_ref,
                   qn, kp, vp, s_scr, p_scr, e_scr):
    g = pl.program_id(1)
    seq = q_ref.shape[0]
    n_blocks = seq // WINDOW
    stack = ATT_GROUP * WINDOW
    unit = 64
    nt = (((1,), (1,)), ((), ()))
    heads = [slice(r * ATT_D, (r + 1) * ATT_D) for r in range(ATT_GROUP)]
    kp[0:WINDOW, :] = jnp.zeros((WINDOW, ATT_D), BF16)
    kp[seq + WINDOW:seq + 2 * WINDOW, :] = jnp.zeros((WINDOW, ATT_D), BF16)
    vp[0:WINDOW, :] = jnp.zeros((WINDOW, 2 * ATT_D), BF16)
    vp[seq + WINDOW:seq + 2 * WINDOW, :] = jnp.zeros((WINDOW, 2 * ATT_D), BF16)

    def prep(n, carry):
        r0 = pl.multiple_of(n * WINDOW, WINDOW)
        src = pl.ds(r0, WINDOW)
        dst = pl.ds(r0 + WINDOW, WINDOW)
        kp[dst, :] = _rms(k_ref[src, :].astype(F32), kg_ref[...]).astype(BF16)
        vp[dst, 0:ATT_D] = v_ref[src, :]
        vp[dst, ATT_D:2 * ATT_D] = jnp.ones((WINDOW, ATT_D), BF16)
        for hs in heads:
            q = _rms(q_ref[src, hs].astype(F32), qg_ref[...]) * (ATT_D ** -0.5)
            qn[src, hs] = q.astype(BF16)
        return carry

    lax.fori_loop(0, n_blocks, prep, 0, unroll=2)

    def scores(n, carry):
        r0 = pl.multiple_of(n * WINDOW, WINDOW)
        kb = kp[pl.ds(r0, 3 * WINDOW), :]
        q4 = jnp.concatenate([qn[pl.ds(r0, WINDOW), hs] for hs in heads], axis=0)
        edge = jnp.where(n == 0, 0, jnp.where(n == n_blocks - 1, 2, 1))
        s_scr[pl.ds(pl.multiple_of(n * stack, stack), stack), :] = (
            lax.dot_general(q4, kb, nt, preferred_element_type=F32) + bias_ref[edge])
        return carry

    lax.fori_loop(0, n_blocks, scores, 0, unroll=2)

    def softmax(n, carry):
        base = pl.multiple_of(n * stack, stack)
        for u in range(stack // unit):
            rows = pl.ds(base + u * unit, unit)
            sink = sink_ref[g * ATT_GROUP + (u * unit) // WINDOW]
            s = s_scr[rows, :]
            m = jnp.maximum(jnp.max(s, axis=-1, keepdims=True), sink)
            mb = jnp.broadcast_to(m, (unit, WINDOW))
            p_scr[rows, :] = jnp.exp(s - jnp.concatenate([mb, mb, mb], axis=1)).astype(BF16)
            e_scr[rows, :] = jnp.exp(sink - mb)
        return carry

    lax.fori_loop(0, n_blocks, softmax, 0)

    def outputs(n, carry):
        r0 = pl.multiple_of(n * WINDOW, WINDOW)
        rows = pl.ds(pl.multiple_of(n * stack, stack), stack)
        oe = jnp.dot(p_scr[rows, :], vp[pl.ds(r0, 3 * WINDOW), :], preferred_element_type=F32)
        o = oe[:, 0:ATT_D] / (oe[:, ATT_D:2 * ATT_D] + e_scr[rows, :])
        for r, hs in enumerate(heads):
            o_ref[pl.ds(r0, WINDOW), hs] = o[r * WINDOW:(r + 1) * WINDOW, :].astype(BF16)
        return carry

    lax.fori_loop(0, n_blocks, outputs, 0, unroll=2)


def _window_attention(proj3, q_gain, k_gain, sink, bias):
    b, seq, _ = proj3.shape
    gw = ATT_GROUP * ATT_D
    return pl.pallas_call(
        _window_kernel,
        out_shape=jax.ShapeDtypeStruct((b, seq, ATT_HEADS * ATT_D), BF16),
        grid=(b, ATT_KV),
        in_specs=[
            pl.BlockSpec(memory_space=pltpu.SMEM),
            pl.BlockSpec((None, seq, gw), lambda i, g: (i, 0, OFF_AQ // gw + g)),
            pl.BlockSpec((None, seq, ATT_D), lambda i, g: (i, 0, OFF_AK // ATT_D + g)),
            pl.BlockSpec((None, seq, ATT_D), lambda i, g: (i, 0, OFF_AV // ATT_D + g)),
            pl.BlockSpec((1, ATT_D), lambda i, g: (0, 0)),
            pl.BlockSpec((1, ATT_D), lambda i, g: (0, 0)),
            pl.BlockSpec((None, 3, ATT_GROUP * WINDOW, 3 * WINDOW), lambda i, g: (g, 0, 0, 0)),
        ],
        out_specs=pl.BlockSpec((None, seq, gw), lambda i, g: (i, 0, g)),
        scratch_shapes=[
            pltpu.VMEM((seq, gw), BF16),
            pltpu.VMEM((seq + 2 * WINDOW, ATT_D), BF16),
            pltpu.VMEM((seq + 2 * WINDOW, 2 * ATT_D), BF16),
            pltpu.VMEM((seq * ATT_GROUP, 3 * WINDOW), F32),
            pltpu.VMEM((seq * ATT_GROUP, 3 * WINDOW), BF16),
            pltpu.VMEM((seq * ATT_GROUP, WINDOW), F32),
        ],
        compiler_params=_cp(("parallel", "parallel")),
        name="window_attention",
    )(sink, proj3, proj3, proj3, q_gain, k_gain, bias)


def _window_bias(rel_table):
    nb = REL_BUCKETS // 2
    max_exact = nb // 2
    qi = jnp.arange(WINDOW)[:, None]
    ki = jnp.arange(3 * WINDOW)[None, :]
    rel = ki - WINDOW - qi
    ret = jnp.where(rel > 0, nb, 0)
    n = jnp.abs(rel)
    large = max_exact + (jnp.log(jnp.maximum(n, 1).astype(F32) / max_exact)
                         / math.log(REL_MAX_DIST / max_exact) * (nb - max_exact)).astype(I32)
    large = jnp.minimum(large, nb - 1)
    bucket = ret + jnp.where(n < max_exact, n, large)
    onehot = (bucket[None] == jnp.arange(REL_BUCKETS)[:, None, None]).astype(F32)
    bias = jnp.einsum("bh,bqk->hqk", rel_table.astype(F32), onehot, precision=lax.Precision.HIGHEST)
    bias = jnp.where((jnp.abs(rel) <= WINDOW)[None], bias, -jnp.inf)
    bias = bias.reshape(ATT_KV, ATT_GROUP * WINDOW, 3 * WINDOW)
    first = jnp.where(ki < WINDOW, -jnp.inf, bias)
    last = jnp.where(ki >= 2 * WINDOW, -jnp.inf, bias)
    return jnp.stack([first, bias, last], axis=1)


def _mem_kv_kernel(m_ref, g_ref, w_ref, kg_ref, o_ref):
    h = _rms(m_ref[...], g_ref[...]).astype(BF16)
    kw = MEM_HEADS * MEM_D
    for j in range(MEM_HEADS):
        cols = slice(j * MEM_D, (j + 1) * MEM_D)
        k = jnp.dot(h, w_ref[:, cols], preferred_element_type=F32)
        o_ref[:, cols] = _rms(k, kg_ref[...]).astype(BF16)
    o_ref[:, kw:] = jnp.dot(h, w_ref[:, kw:], preferred_element_type=F32).astype(BF16)


def _mem_kv(mem2, gain, w_bf16, k_gain, bm=512):
    rows = mem2.shape[0]
    bm = min(bm, rows)
    n = w_bf16.shape[1]
    return pl.pallas_call(
        _mem_kv_kernel,
        out_shape=jax.ShapeDtypeStruct((rows, n), BF16),
        grid=(rows // bm,),
        in_specs=[
            pl.BlockSpec((bm, D_MODEL), lambda i: (i, 0)),
            pl.BlockSpec((1, D_MODEL), lambda i: (0, 0)),
            pl.BlockSpec((D_MODEL, n), lambda i: (0, 0)),
            pl.BlockSpec((1, MEM_D), lambda i: (0, 0)),
        ],
        out_specs=pl.BlockSpec((bm, n), lambda i: (i, 0)),
        compiler_params=_cp(("parallel",)),
        name="mem_kv",
    )(mem2, gain, w_bf16, k_gain)


def _mem_attn_kernel(q_ref, k_ref, v_ref, qg_ref, o_ref, *, bq):
    seq = q_ref.shape[0]
    nt = (((1,), (1,)), ((), ()))

    def block(n, carry):
        rows = pl.ds(pl.multiple_of(n * bq, bq), bq)
        q = _rms(q_ref[rows, :].astype(F32), qg_ref[...]) * (MEM_D ** -0.5)
        s = lax.dot_general(q.astype(BF16), k_ref[...], nt, preferred_element_type=F32)
        p = jnp.exp(s - jnp.max(s, axis=-1, keepdims=True))
        denom = jnp.sum(p, axis=-1, keepdims=True)
        o = jnp.dot(p.astype(BF16), v_ref[...], preferred_element_type=F32) / denom
        o_ref[rows, :] = o.astype(BF16)
        return carry

    lax.fori_loop(0, seq // bq, block, 0, unroll=8)


def _mem_attention(proj3, mkv3, q_gain, bq=256):
    b, seq, _ = proj3.shape
    m_len = mkv3.shape[1]
    return pl.pallas_call(
        functools.partial(_mem_attn_kernel, bq=bq),
        out_shape=jax.ShapeDtypeStruct((b, seq, MEM_HEADS * MEM_D), BF16),
        grid=(b, MEM_HEADS),
        in_specs=[
            pl.BlockSpec((None, seq, MEM_D), lambda i, h: (i, 0, OFF_MQ // MEM_D + h)),
            pl.BlockSpec((None, m_len, MEM_D), lambda i, h: (i, 0, h)),
            pl.BlockSpec((None, m_len, MEM_D), lambda i, h: (i, 0, MEM_HEADS + h)),
            pl.BlockSpec((1, MEM_D), lambda i, h: (0, 0)),
        ],
        out_specs=pl.BlockSpec((None, seq, MEM_D), lambda i, h: (i, 0, h)),
        compiler_params=_cp(("parallel", "parallel")),
        name="mem_attention",
    )(proj3, mkv3, mkv3, q_gain)


def _merge_kernel(x_ref, g0_ref, g1_ref, g2_ref, b0_ref, b1_ref, b2_ref, wb_ref, wo_ref, gn_ref,
                  wr_ref, xo_ref, hp_ref, lg_ref):
    merged = None
    for gate_ref, br, i in ((g0_ref, b0_ref, 0), (g1_ref, b1_ref, 1), (g2_ref, b2_ref, 2)):
        gate = 0.5 * jnp.tanh(0.5 * gate_ref[...].astype(F32)) + 0.5
        t = gate * jnp.dot(br[...], wb_ref[i], preferred_element_type=F32)
        merged = t if merged is None else merged + t
    x = x_ref[...] + jnp.dot(merged.astype(BF16), wo_ref[...], preferred_element_type=F32)
    xo_ref[...] = x
    h = _rms(x, gn_ref[...])
    h_hi = h.astype(BF16)
    h_lo = (h - h_hi.astype(F32)).astype(BF16)
    lg_ref[...] = jnp.dot(jnp.concatenate([h_hi, h_lo], axis=1), wr_ref[...], preferred_element_type=F32)
    hp_ref[...] = _pack_bf16_pairs(h)


def _merge(x2, proj, ret, att, mo, wb, wo, gain, w_router_cat, bm=512):
    t = x2.shape[0]
    const2 = lambda i: (0, 0)
    row = lambda i: (i, 0)
    return pl.pallas_call(
        _merge_kernel,
        out_shape=(
            jax.ShapeDtypeStruct((t, D_MODEL), F32),
            jax.ShapeDtypeStruct((t, HALF), I32),
            jax.ShapeDtypeStruct((t, ROUTER_LANES), F32),
        ),
        grid=(t // bm,),
        in_specs=[
            pl.BlockSpec((bm, D_MODEL), row),
            pl.BlockSpec((bm, D_MODEL), lambda i: (i, 0)),
            pl.BlockSpec((bm, D_MODEL), lambda i: (i, 1)),
            pl.BlockSpec((bm, D_MODEL), lambda i: (i, 2)),
            pl.BlockSpec((bm, D_MODEL), row),
            pl.BlockSpec((bm, D_MODEL), row),
            pl.BlockSpec((bm, D_MODEL), row),
            pl.BlockSpec((3, D_MODEL, D_MODEL), lambda i: (0, 0, 0)),
            pl.BlockSpec((D_MODEL, D_MODEL), const2),
            pl.BlockSpec((1, D_MODEL), const2),
            pl.BlockSpec((2 * D_MODEL, ROUTER_LANES), const2),
        ],
        out_specs=(
            pl.BlockSpec((bm, D_MODEL), row),
            pl.BlockSpec((bm, HALF), row),
            pl.BlockSpec((bm, ROUTER_LANES), row),
        ),
        compiler_params=_cp(("parallel",)),
        name="merge_router",
    )(x2, proj, proj, proj, ret, att, mo, wb, wo, gain, w_router_cat)


def _route_kernel(l_ref, b_ref, tri_ref, eid_ref, rank_ref, gate_ref, cnt_ref):
    @pl.when(pl.program_id(0) == 0)
    def _():
        cnt_ref[...] = jnp.zeros_like(cnt_ref)

    l = l_ref[0:N_EXPERTS, :] + l_ref[N_EXPERTS:2 * N_EXPERTS, :] + b_ref[...]
    ne, bt = l.shape
    iota_e = lax.broadcasted_iota(I32, (ne, bt), 0)
    picked = jnp.zeros((ne, bt), jnp.bool_)
    vals, idxs = [], []
    for _ in range(TOP_K):
        m = jnp.max(l, axis=0, keepdims=True)
        idx = jnp.min(jnp.where(l == m, iota_e, ne), axis=0, keepdims=True)
        sel = iota_e == idx
        picked = picked | sel
        l = jnp.where(sel, -jnp.inf, l)
        vals.append(m)
        idxs.append(idx)
    ex = [jnp.exp(v - vals[0]) for v in vals]
    tot = ex[0] + ex[1] + ex[2] + ex[3]
    onehot = jnp.where(picked, 1.0, 0.0)
    before = jnp.dot(onehot.astype(BF16), tri_ref[...], preferred_element_type=F32) + cnt_ref[:, 0:1]
    for k in range(TOP_K):
        eid_ref[k:k + 1, :] = idxs[k]
        gate_ref[k:k + 1, :] = ex[k] / tot
        rank_ref[k:k + 1, :] = jnp.sum(jnp.where(iota_e == idxs[k], before, 0.0), axis=0,
                                       keepdims=True).astype(I32)
    cnt_ref[...] = cnt_ref[...] + jnp.sum(onehot, axis=1, keepdims=True)


def _route(logits_t, bias_col, tri, bt=512):
    ne, t = N_EXPERTS, logits_t.shape[1]
    blk = lambda i: (0, i)
    return pl.pallas_call(
        _route_kernel,
        out_shape=(
            jax.ShapeDtypeStruct((TOP_K, t), I32),
            jax.ShapeDtypeStruct((TOP_K, t), I32),
            jax.ShapeDtypeStruct((TOP_K, t), F32),
            jax.ShapeDtypeStruct((ne, 128), F32),
        ),
        grid=(t // bt,),
        in_specs=[pl.BlockSpec((ROUTER_LANES, bt), blk), pl.BlockSpec((ne, 1), lambda i: (0, 0)),
                  pl.BlockSpec((bt, bt), lambda i: (0, 0))],
        out_specs=(
            pl.BlockSpec((TOP_K, bt), blk),
            pl.BlockSpec((TOP_K, bt), blk),
            pl.BlockSpec((TOP_K, bt), blk),
            pl.BlockSpec((ne, 128), lambda i: (0, 0)),
        ),
        compiler_params=_cp(("arbitrary",)),
        name="route_topk",
    )(logits_t, bias_col, tri)


def _slot_kernel(start_ref, eid_ref, rank_ref, slot_ref):
    eid = eid_ref[...]
    base = jnp.zeros(eid.shape, I32)
    for e in range(N_EXPERTS):
        base = jnp.where(eid == e, start_ref[e], base)
    slot_ref[...] = base + rank_ref[...]


def _slots(pad_starts, eid, rank, bt=2048):
    t = eid.shape[1]
    blk = lambda i, s: (0, i)
    return pl.pallas_call(
        _slot_kernel,
        out_shape=jax.ShapeDtypeStruct((TOP_K, t), I32),
        grid_spec=pltpu.PrefetchScalarGridSpec(
            num_scalar_prefetch=1,
            grid=(t // bt,),
            in_specs=[pl.BlockSpec((TOP_K, bt), blk), pl.BlockSpec((TOP_K, bt), blk)],
            out_specs=pl.BlockSpec((TOP_K, bt), blk),
        ),
        compiler_params=_cp(("parallel",)),
        name="route_slots",
    )(pad_starts, eid, rank)


def _sc_workers():
    info = plsc.get_sparse_core_info()
    return info.num_cores, info.num_subcores


def _dispatch_rows(hp, slot, n_rows, win=64):
    t, w = hp.shape
    nc, ns = _sc_workers()
    per_worker = t // (nc * ns)
    mesh = plsc.VectorSubcoreMesh(core_axis_name="c", subcore_axis_name="s")

    @functools.partial(
        pl.kernel,
        out_type=jax.ShapeDtypeStruct((n_rows, w), hp.dtype),
        mesh=mesh,
        scratch_types=[pltpu.VMEM((win,), I32)] * TOP_K + [pltpu.VMEM((win, w), hp.dtype)],
        name="moe_dispatch",
    )
    def k(hp_hbm, slot_hbm, buf_hbm, *scratch):
        idx_v, rows_v = scratch[:TOP_K], scratch[TOP_K]
        wid = lax.axis_index("s") * nc + lax.axis_index("c")
        base = wid * per_worker

        @pl.loop(0, per_worker // win)
        def _(i):
            off = pl.multiple_of(base + i * win, win)
            pltpu.sync_copy(hp_hbm.at[pl.ds(off, win)], rows_v)
            for kk in range(TOP_K):
                pltpu.sync_copy(slot_hbm.at[pl.ds(kk * t + off, win)], idx_v[kk])
                pltpu.sync_copy(rows_v, buf_hbm.at[idx_v[kk]])

    return k(hp, slot.reshape(-1))


def _gather_rows(out_rows, slot, win=64):
    t = slot.shape[1]
    w = out_rows.shape[1]
    nc, ns = _sc_workers()
    per_worker = t // (nc * ns)
    mesh = plsc.VectorSubcoreMesh(core_axis_name="c", subcore_axis_name="s")

    @functools.partial(
        pl.kernel,
        out_type=jax.ShapeDtypeStruct((TOP_K * t, w), out_rows.dtype),
        mesh=mesh,
        scratch_types=[pltpu.VMEM((win,), I32), pltpu.VMEM((win, w), out_rows.dtype)],
        name="moe_gather",
    )
    def k(rows_hbm, slot_hbm, og_hbm, idx_v, rows_v):
        wid = lax.axis_index("s") * nc + lax.axis_index("c")
        base = wid * per_worker

        @pl.loop(0, per_worker // win)
        def _(i):
            off = pl.multiple_of(base + i * win, win)
            for kk in range(TOP_K):
                pltpu.sync_copy(slot_hbm.at[pl.ds(kk * t + off, win)], idx_v)
                pltpu.sync_copy(rows_hbm.at[idx_v], rows_v)
                pltpu.sync_copy(rows_v, og_hbm.at[pl.ds(kk * t + off, win)])

    return k(out_rows, slot.reshape(-1)).reshape(TOP_K, t, w)


def _expert_kernel(be_ref, nu_ref, x_ref, wi_ref, bi_ref, wo_ref, bo_ref, o_ref, wi_bf, wo_bf):
    i = pl.program_id(0)
    active = i < nu_ref[0]
    new_expert = jnp.logical_or(i == 0, be_ref[i] != be_ref[jnp.maximum(i - 1, 0)])

    @pl.when(jnp.logical_and(active, new_expert))
    def _():
        rows = 128

        def cast_in(c, carry):
            sl = pl.ds(pl.multiple_of(c * rows, rows), rows)
            wi_bf[sl, :] = wi_ref[sl, :].astype(BF16)
            return carry

        def cast_out(c, carry):
            sl = pl.ds(pl.multiple_of(c * rows, rows), rows)
            wo_bf[sl, :] = wo_ref[sl, :].astype(BF16)
            return carry

        lax.fori_loop(0, D_MODEL // rows, cast_in, 0)
        lax.fori_loop(0, D_FF // rows, cast_out, 0)

    @pl.when(active)
    def _():
        hi, lo = _unpack_bf16_pairs(x_ref[...])
        h = (jnp.dot(hi.astype(BF16), wi_bf[0:HALF, :], preferred_element_type=F32)
             + jnp.dot(lo.astype(BF16), wi_bf[HALF:, :], preferred_element_type=F32) + bi_ref[...])
        gate = jnp.minimum(h[:, :D_FF], SWIGLU_LIMIT)
        lin = jnp.clip(h[:, D_FF:], -SWIGLU_LIMIT, SWIGLU_LIMIT)
        act = gate * (0.5 * jnp.tanh((0.5 * SWIGLU_ALPHA) * gate) + 0.5) * (lin + 1.0)
        o = jnp.dot(act.astype(BF16), wo_bf[...], preferred_element_type=F32) + bo_ref[...]
        o_ref[...] = _pack_bf16_pairs(o)


def _experts(block_expert, n_used, buf, w_in, b_in, w_out, b_out):
    n_rows, w = buf.shape
    n_blocks = n_rows // EXPERT_BLOCK
    return pl.pallas_call(
        _expert_kernel,
        out_shape=jax.ShapeDtypeStruct((n_rows, w), buf.dtype),
        grid_spec=pltpu.PrefetchScalarGridSpec(
            num_scalar_prefetch=2,
            grid=(n_blocks,),
            in_specs=[
                pl.BlockSpec((EXPERT_BLOCK, w), lambda i, be, nu: (i, 0)),
                pl.BlockSpec((None, D_MODEL, 2 * D_FF), lambda i, be, nu: (be[i], 0, 0)),
                pl.BlockSpec((None, 1, 2 * D_FF), lambda i, be, nu: (be[i], 0, 0)),
                pl.BlockSpec((None, D_FF, D_MODEL), lambda i, be, nu: (be[i], 0, 0)),
                pl.BlockSpec((None, 1, D_MODEL), lambda i, be, nu: (be[i], 0, 0)),
            ],
            out_specs=pl.BlockSpec((EXPERT_BLOCK, w), lambda i, be, nu: (i, 0)),
            scratch_shapes=[pltpu.VMEM((D_MODEL, 2 * D_FF), BF16), pltpu.VMEM((D_FF, D_MODEL), BF16)],
        ),
        compiler_params=_cp(("arbitrary",)),
        name="expert_ffn",
    )(block_expert, n_used, buf, w_in, b_in, w_out, b_out)


def _combine_kernel(x_ref, og_ref, g_ref, y_ref):
    g = g_ref[...]
    acc_hi = x_ref[:, :HALF]
    acc_lo = x_ref[:, HALF:]
    for k in range(TOP_K):
        hi, lo = _unpack_bf16_pairs(og_ref[k])
        gk = g[:, k:k + 1]
        acc_hi = acc_hi + gk * hi
        acc_lo = acc_lo + gk * lo
    y_ref[:, :HALF] = acc_hi
    y_ref[:, HALF:] = acc_lo


def _combine(x_mid, og, gates_tk, bm=1024):
    t = x_mid.shape[0]
    return pl.pallas_call(
        _combine_kernel,
        out_shape=jax.ShapeDtypeStruct((t, D_MODEL), F32),
        grid=(t // bm,),
        in_specs=[
            pl.BlockSpec((bm, D_MODEL), lambda i: (i, 0)),
            pl.BlockSpec((TOP_K, bm, HALF), lambda i: (0, i, 0)),
            pl.BlockSpec((bm, TOP_K), lambda i: (i, 0)),
        ],
        out_specs=pl.BlockSpec((bm, D_MODEL), lambda i: (i, 0)),
        compiler_params=_cp(("parallel",)),
        name="moe_combine",
    )(x_mid, og, gates_tk)


def _block_tables(counts, n_blocks):
    padded = (counts + EXPERT_BLOCK - 1) // EXPERT_BLOCK * EXPERT_BLOCK
    pad_ends = jnp.cumsum(padded)
    pad_starts = pad_ends - padded
    block_start = jnp.arange(n_blocks, dtype=I32) * EXPERT_BLOCK
    block_expert = jnp.minimum(jnp.sum(pad_ends[None, :] <= block_start[:, None], axis=1), N_EXPERTS - 1)
    n_used = (pad_ends[-1] // EXPERT_BLOCK).reshape(1)
    return pad_starts.astype(I32), block_expert.astype(I32), n_used.astype(I32)


def _layer(x, mem, consts, params):
    b, seq, d = x.shape
    m_len = mem.shape[1]
    t = b * seq
    x2 = x.reshape(t, d)
    proj = _in_proj(x2, params["norm_mix"], params["w_in"])
    proj3 = proj.reshape(b, seq, IN_WIDTH)
    ret = _retention(proj3, params["ret_out_norm"], consts["cos"], consts["sin"], consts["tabs"],
                     consts["decays"])
    att = _window_attention(proj3, params["att_q_norm"], params["att_k_norm"], params["att_sink"],
                            consts["bias"])
    mkv = _mem_kv(mem.reshape(b * m_len, d), params["mem_norm"], params["w_mem_kv"],
                  params["mem_k_norm"])
    mo = _mem_attention(proj3, mkv.reshape(b, m_len, 2 * MEM_HEADS * MEM_D), params["mem_q_norm"])
    x_mid, hp, logits = _merge(x2, proj, ret.reshape(t, d), att.reshape(t, d), mo.reshape(t, d),
                               params["w_branch"], params["w_out"], params["norm_ffn"],
                               params["w_router"])
    eid, rank, gates, counts = _route(logits.T, params["b_router"], consts["tri"])
    n_blocks = t * TOP_K // EXPERT_BLOCK + N_EXPERTS
    pad_starts, block_expert, n_used = _block_tables(counts[:, 0].astype(I32), n_blocks)
    slot = _slots(pad_starts, eid, rank)
    buf = _dispatch_rows(hp, slot, n_blocks * EXPERT_BLOCK)
    out_rows = _experts(block_expert, n_used, buf, params["w_e_in"], params["b_e_in"],
                        params["w_e_out"], params["b_e_out"])
    og = _gather_rows(out_rows, slot)
    y = _combine(x_mid, og, gates.T)
    return y.reshape(b, seq, d)


def _router_weights(w):
    w_hi = w.astype(BF16)
    w_lo = (w - w_hi.astype(F32)).astype(BF16)
    pad = jnp.zeros((D_MODEL, ROUTER_LANES - 2 * N_EXPERTS), BF16)
    top = jnp.concatenate([w_hi, w_lo, pad], axis=1)
    bottom = jnp.concatenate([w_hi, jnp.zeros_like(w_lo), pad], axis=1)
    return jnp.concatenate([top, bottom], axis=0)


def _prepare(seq, rel_table, norm_mix, w_in, ret_out_norm, att_q_norm, att_k_norm, att_sink, mem_norm,
             w_mem_kv, mem_q_norm, mem_k_norm, w_branch, w_out, norm_ffn, w_router, b_router,
             w_e_in, b_e_in, w_e_out, b_e_out):
    gates_at = IN_WIDTH - 3 * D_MODEL
    w_in0 = w_in[0]
    params = {
        "norm_mix": norm_mix[0].reshape(1, -1),
        "w_in": jnp.concatenate([w_in0[:, gates_at:], w_in0[:, :gates_at]], axis=1).astype(BF16),
        "ret_out_norm": ret_out_norm[0].reshape(1, -1),
        "att_q_norm": att_q_norm[0].reshape(1, -1),
        "att_k_norm": att_k_norm[0].reshape(1, -1),
        "att_sink": att_sink[0].astype(F32),
        "mem_norm": mem_norm[0].reshape(1, -1),
        "w_mem_kv": w_mem_kv[0].astype(BF16),
        "mem_q_norm": mem_q_norm[0].reshape(1, -1),
        "mem_k_norm": mem_k_norm[0].reshape(1, -1),
        "w_branch": w_branch[0].astype(BF16),
        "w_out": w_out[0].astype(BF16),
        "norm_ffn": norm_ffn[0].reshape(1, -1),
        "w_router": _router_weights(w_router[0]),
        "b_router": b_router[0].reshape(-1, 1),
        "w_e_in": w_e_in[0],
        "b_e_in": b_e_in[0].reshape(N_EXPERTS, 1, -1),
        "w_e_out": w_e_out[0],
        "b_e_out": b_e_out[0].reshape(N_EXPERTS, 1, -1),
    }
    cos, sin, tabs, decays = _retention_tables(seq)
    route_bt = 512
    tri = (jnp.arange(route_bt)[:, None] < jnp.arange(route_bt)[None, :]).astype(BF16)
    consts = {"cos": cos, "sin": sin, "tabs": tabs, "decays": decays,
              "bias": _window_bias(rel_table), "tri": tri}
    return consts, params


def kernel(x_prompt, x_sample, mem_prompt, mem_sample, rel_table, norm_mix, w_in, ret_out_norm, att_q_norm, att_k_norm, att_sink, mem_norm, w_mem_kv, mem_q_norm, mem_k_norm, w_branch, w_out, norm_ffn, w_router, b_router, w_e_in, b_e_in, w_e_out, b_e_out):
    consts, params = _prepare(x_prompt.shape[1], rel_table, norm_mix, w_in, ret_out_norm, att_q_norm,
                              att_k_norm, att_sink, mem_norm, w_mem_kv, mem_q_norm, mem_k_norm,
                              w_branch, w_out, norm_ffn, w_router, b_router, w_e_in, b_e_in,
                              w_e_out, b_e_out)
    y_prompt = _layer(x_prompt, mem_prompt, consts, params)
    y_sample = _layer(x_sample, mem_sample, consts, params)
    return (y_prompt, y_sample)
```

```python
import functools
import math

import jax
import jax.numpy as jnp
import numpy as np
from jax import lax
from jax.experimental import pallas as pl
from jax.experimental.pallas import tpu as pltpu
from jax.experimental.pallas import tpu_sc as plsc

F32 = jnp.float32
BF16 = jnp.bfloat16
I32 = jnp.int32

D_MODEL = 1024
EPS = 1e-6
RET_HEADS = 4
RET_QK = 128
RET_V = 256
CHUNK = 128
ROPE_BASE = 10000.0
DECAY_FWD = 5.0
DECAY_BWD = 5.5
ATT_HEADS = 8
ATT_KV = 2
ATT_GROUP = ATT_HEADS // ATT_KV
ATT_D = 128
WINDOW = 128
REL_BUCKETS = 32
REL_MAX_DIST = 128
MEM_HEADS = 4
MEM_D = 256
N_EXPERTS = 32
TOP_K = 4
D_FF = 1024
SWIGLU_LIMIT = 7.0
SWIGLU_ALPHA = 1.702
EXPERT_BLOCK = 512
HALF = D_MODEL // 2
ROUTER_LANES = 128

IN_WIDTH = 8704
OFF_GATES = 0
OFF_RQ = 3072
OFF_RK = 3584
OFF_RV = 4096
OFF_RG = 5120
OFF_AQ = 6144
OFF_AK = 7168
OFF_AV = 7424
OFF_MQ = 7680

VMEM_LIMIT = 56 * 1024 * 1024
LOG2E = 1.4426950408889634


def _cp(sem, vmem=VMEM_LIMIT):
    return pltpu.CompilerParams(dimension_semantics=sem, vmem_limit_bytes=vmem)


def _rms(x, g):
    return x * lax.rsqrt(jnp.mean(x * x, axis=-1, keepdims=True) + EPS) * g


def _pack_bf16_pairs(x):
    w = x.shape[1] // 2
    bits = pltpu.bitcast(x.astype(BF16).astype(F32), I32)
    hi = bits[:, :w] & jnp.int32(-65536)
    lo = lax.shift_right_logical(bits[:, w:], jnp.int32(16))
    return hi | lo


def _unpack_bf16_pairs(u):
    hi = pltpu.bitcast(u & jnp.int32(-65536), F32)
    lo = pltpu.bitcast(lax.shift_left(u, jnp.int32(16)), F32)
    return hi, lo


def _in_proj_kernel(x_ref, g_ref, w_ref, o_ref, h_scr):
    @pl.when(pl.program_id(1) == 0)
    def _():
        h_scr[...] = _rms(x_ref[...], g_ref[...]).astype(BF16)

    sub = 256

    def cols(c, carry):
        sl = pl.ds(pl.multiple_of(c * sub, sub), sub)
        o_ref[:, sl] = jnp.dot(h_scr[...], w_ref[:, sl], preferred_element_type=F32).astype(BF16)
        return carry

    lax.fori_loop(0, o_ref.shape[1] // sub, cols, 0, unroll=True)


def _in_proj(x2, gain, w_bf16, bm=1024, bn=4352):
    t = x2.shape[0]
    n = w_bf16.shape[1]
    return pl.pallas_call(
        _in_proj_kernel,
        out_shape=jax.ShapeDtypeStruct((t, n), BF16),
        grid=(t // bm, n // bn),
        in_specs=[
            pl.BlockSpec((bm, D_MODEL), lambda i, j: (i, 0)),
            pl.BlockSpec((1, D_MODEL), lambda i, j: (0, 0)),
            pl.BlockSpec((D_MODEL, bn), lambda i, j: (0, j)),
        ],
        out_specs=pl.BlockSpec((bm, bn), lambda i, j: (i, j)),
        scratch_shapes=[pltpu.VMEM((bm, D_MODEL), BF16)],
        compiler_params=_cp(("parallel", "arbitrary")),
        name="in_proj",
    )(x2, gain, w_bf16)


def _retention_kernel(dec_ref, q_ref, k_ref, v_ref, g_ref, cos_ref, sin_ref, tab_ref, gain_ref,
                      o_ref, lhs, kb, rhs, kvf, kvb):
    h = pl.program_id(1)
    seq = q_ref.shape[0]
    n_chunks = seq // CHUNK
    dec_f = dec_ref[2 * h]
    dec_b = dec_ref[2 * h + 1]
    nt = (((1,), (1,)), ((), ()))
    tn = (((0,), (0,)), ((), ()))

    def rope(x, c, s):
        return x * c + pltpu.roll(x, RET_QK // 2, 1) * s

    def prep(n, carry):
        rows = pl.ds(pl.multiple_of(n * CHUNK, CHUNK), CHUNK)
        c = cos_ref[rows, :]
        s = sin_ref[rows, :]
        q = rope(q_ref[rows, :].astype(F32), c, s)
        k = rope(k_ref[rows, :].astype(F32), c, s) * (RET_QK ** -0.5)
        lhs[rows, 0:CHUNK] = q.astype(BF16)
        lhs[rows, CHUNK:2 * CHUNK] = (q * tab_ref[1]).astype(BF16)
        lhs[rows, 2 * CHUNK:3 * CHUNK] = (q * tab_ref[3]).astype(BF16)
        kb[rows, :] = k.astype(BF16)
        v = v_ref[rows, :]
        rhs[n, 0:CHUNK, :] = v
        kvf[n] = lax.dot_general((k * tab_ref[2]).astype(BF16), v, tn, preferred_element_type=F32)
        kvb[n] = lax.dot_general((k * tab_ref[4]).astype(BF16), v, tn, preferred_element_type=F32)
        return carry

    lax.fori_loop(0, n_chunks, prep, 0, unroll=8)

    def scan_f(n, state):
        rhs[n, CHUNK:2 * CHUNK, :] = state.astype(BF16)
        return state * dec_f + kvf[n]

    lax.fori_loop(0, n_chunks, scan_f, jnp.zeros((RET_QK, RET_V), F32))

    def scan_b(i, state):
        n = n_chunks - 1 - i
        rhs[n, 2 * CHUNK:3 * CHUNK, :] = state.astype(BF16)
        return state * dec_b + kvb[n]

    lax.fori_loop(0, n_chunks, scan_b, jnp.zeros((RET_QK, RET_V), F32))

    def chunk(n, carry):
        rows = pl.ds(pl.multiple_of(n * CHUNK, CHUNK), CHUNK)
        sc = lax.dot_general(lhs[rows, 0:CHUNK], kb[rows, :], nt, preferred_element_type=F32)
        a = jnp.concatenate([(sc * tab_ref[0]).astype(BF16), lhs[rows, CHUNK:3 * CHUNK]], axis=1)
        o = jnp.dot(a, rhs[n], preferred_element_type=F32)
        y = _rms(o, gain_ref[...])
        g = g_ref[rows, :].astype(F32)
        o_ref[rows, :] = (y * (g * (0.5 * jnp.tanh(0.5 * g) + 0.5))).astype(BF16)
        return carry

    lax.fori_loop(0, n_chunks, chunk, 0, unroll=True)


def _retention(proj3, gain, cos, sin, tabs, decays):
    b, seq, _ = proj3.shape
    return pl.pallas_call(
        _retention_kernel,
        out_shape=jax.ShapeDtypeStruct((b, seq, RET_HEADS * RET_V), BF16),
        grid=(b, RET_HEADS),
        in_specs=[
            pl.BlockSpec(memory_space=pltpu.SMEM),
            pl.BlockSpec((None, seq, RET_QK), lambda i, h: (i, 0, OFF_RQ // RET_QK + h)),
            pl.BlockSpec((None, seq, RET_QK), lambda i, h: (i, 0, OFF_RK // RET_QK + h)),
            pl.BlockSpec((None, seq, RET_V), lambda i, h: (i, 0, OFF_RV // RET_V + h)),
            pl.BlockSpec((None, seq, RET_V), lambda i, h: (i, 0, OFF_RG // RET_V + h)),
            pl.BlockSpec((seq, RET_QK), lambda i, h: (0, 0)),
            pl.BlockSpec((seq, RET_QK), lambda i, h: (0, 0)),
            pl.BlockSpec((None, 5, CHUNK, CHUNK), lambda i, h: (h, 0, 0, 0)),
            pl.BlockSpec((1, RET_V), lambda i, h: (0, h)),
        ],
        out_specs=pl.BlockSpec((None, seq, RET_V), lambda i, h: (i, 0, h)),
        scratch_shapes=[
            pltpu.VMEM((seq, 3 * CHUNK), BF16),
            pltpu.VMEM((seq, RET_QK), BF16),
            pltpu.VMEM((seq // CHUNK, 3 * CHUNK, RET_V), BF16),
            pltpu.VMEM((seq // CHUNK, RET_QK, RET_V), F32),
            pltpu.VMEM((seq // CHUNK, RET_QK, RET_V), F32),
        ],
        compiler_params=_cp(("parallel", "parallel")),
        name="retention",
    )(decays, proj3, proj3, proj3, proj3, cos, sin, tabs, gain)


def _retention_tables(seq):
    half = RET_QK // 2
    inv = ROPE_BASE ** (-jnp.arange(half, dtype=F32) / half)
    ang = jnp.arange(seq, dtype=F32)[:, None] * inv[None, :]
    cos = jnp.concatenate([jnp.cos(ang), jnp.cos(ang)], axis=1)
    sin = jnp.concatenate([-jnp.sin(ang), jnp.sin(ang)], axis=1)
    heads = jnp.arange(RET_HEADS, dtype=F32)
    lg_f = jnp.log1p(-jnp.exp2(-DECAY_FWD - heads))[:, None, None]
    lg_b = jnp.log1p(-jnp.exp2(-DECAY_BWD - heads))[:, None, None]
    idx = jnp.arange(CHUNK, dtype=F32)
    diff = (idx[:, None] - idx[None, :])[None]
    dmat = jnp.where(diff >= 0, jnp.exp(jnp.where(diff >= 0, diff, 0.0) * lg_f),
                     jnp.exp(jnp.where(diff < 0, -diff, 0.0) * lg_b))
    col = jnp.broadcast_to(idx[None, :, None], (RET_HEADS, CHUNK, CHUNK))
    xi_f = jnp.exp((col + 1.0) * lg_f)
    zeta_f = jnp.exp((CHUNK - 1.0 - col) * lg_f)
    xi_b = jnp.exp((CHUNK - col) * lg_b)
    zeta_b = jnp.exp(col * lg_b)
    tabs = jnp.stack([dmat, xi_f, zeta_f, xi_b, zeta_b], axis=1)
    decays = jnp.stack([jnp.exp(CHUNK * lg_f[:, 0, 0]), jnp.exp(CHUNK * lg_b[:, 0, 0])], axis=1).reshape(-1)
    return cos, sin, tabs, decays


def _window_kernel(sink_ref, q_ref, k_ref, v_ref, qg_ref, kg_ref, bias_ref, o_ref,
                   qn, kp, vp, s_scr, p_scr, e_scr):
    g = pl.program_id(1)
    seq = q_ref.shape[0]
    n_blocks = seq // WINDOW
    stack = ATT_GROUP * WINDOW
    unit = 64
    nt = (((1,), (1,)), ((), ()))
    heads = [slice(r * ATT_D, (r + 1) * ATT_D) for r in range(ATT_GROUP)]
    kp[0:WINDOW, :] = jnp.zeros((WINDOW, ATT_D), BF16)
    kp[seq + WINDOW:seq + 2 * WINDOW, :] = jnp.zeros((WINDOW, ATT_D), BF16)
    vp[0:WINDOW, :] = jnp.zeros((WINDOW, 2 * ATT_D), BF16)
    vp[seq + WINDOW:seq + 2 * WINDOW, :] = jnp.zeros((WINDOW, 2 * ATT_D), BF16)

    def prep(n, carry):
        r0 = pl.multiple_of(n * WINDOW, WINDOW)
        src = pl.ds(r0, WINDOW)
        dst = pl.ds(r0 + WINDOW, WINDOW)
        kp[dst, :] = _rms(k_ref[src, :].astype(F32), kg_ref[...]).astype(BF16)
        vp[dst, 0:ATT_D] = v_ref[src, :]
        vp[dst, ATT_D:2 * ATT_D] = jnp.ones((WINDOW, ATT_D), BF16)
        for hs in heads:
            q = _rms(q_ref[src, hs].astype(F32), qg_ref[...]) * (ATT_D ** -0.5 * LOG2E)
            qn[src, hs] = q.astype(BF16)
        return carry

    lax.fori_loop(0, n_blocks, prep, 0, unroll=2)

    def scores(n, carry):
        r0 = pl.multiple_of(n * WINDOW, WINDOW)
        kb = kp[pl.ds(r0, 3 * WINDOW), :]
        q4 = jnp.concatenate([qn[pl.ds(r0, WINDOW), hs] for hs in heads], axis=0)
        edge = jnp.where(n == 0, 0, jnp.where(n == n_blocks - 1, 2, 1))
        s_scr[pl.ds(pl.multiple_of(n * stack, stack), stack), :] = (
            lax.dot_general(q4, kb, nt, preferred_element_type=F32) + bias_ref[edge])
        return carry

    lax.fori_loop(0, n_blocks, scores, 0, unroll=4)

    def softmax(n, carry):
        base = pl.multiple_of(n * stack, stack)
        for u in range(stack // unit):
            rows = pl.ds(base + u * unit, unit)
            sink = sink_ref[g * ATT_GROUP + (u * unit) // WINDOW] * LOG2E
            s = s_scr[rows, :]
            m = jnp.maximum(jnp.max(s, axis=-1, keepdims=True), sink)
            mb = jnp.broadcast_to(m, (unit, WINDOW))
            p_scr[rows, :] = jnp.exp2(s - jnp.concatenate([mb, mb, mb], axis=1)).astype(BF16)
            e_scr[rows, :] = jnp.exp2(sink - mb)
        return carry

    lax.fori_loop(0, n_blocks, softmax, 0, unroll=2)

    def outputs(n, carry):
        r0 = pl.multiple_of(n * WINDOW, WINDOW)
        rows = pl.ds(pl.multiple_of(n * stack, stack), stack)
        oe = jnp.dot(p_scr[rows, :], vp[pl.ds(r0, 3 * WINDOW), :], preferred_element_type=F32)
        o = oe[:, 0:ATT_D] / (oe[:, ATT_D:2 * ATT_D] + e_scr[rows, :])
        for r, hs in enumerate(heads):
            o_ref[pl.ds(r0, WINDOW), hs] = o[r * WINDOW:(r + 1) * WINDOW, :].astype(BF16)
        return carry

    lax.fori_loop(0, n_blocks, outputs, 0, unroll=4)


def _window_attention(proj3, q_gain, k_gain, sink, bias):
    b, seq, _ = proj3.shape
    gw = ATT_GROUP * ATT_D
    return pl.pallas_call(
        _window_kernel,
        out_shape=jax.ShapeDtypeStruct((b, seq, ATT_HEADS * ATT_D), BF16),
        grid=(b, ATT_KV),
        in_specs=[
            pl.BlockSpec(memory_space=pltpu.SMEM),
            pl.BlockSpec((None, seq, gw), lambda i, g: (i, 0, OFF_AQ // gw + g)),
            pl.BlockSpec((None, seq, ATT_D), lambda i, g: (i, 0, OFF_AK // ATT_D + g)),
            pl.BlockSpec((None, seq, ATT_D), lambda i, g: (i, 0, OFF_AV // ATT_D + g)),
            pl.BlockSpec((1, ATT_D), lambda i, g: (0, 0)),
            pl.BlockSpec((1, ATT_D), lambda i, g: (0, 0)),
            pl.BlockSpec((None, 3, ATT_GROUP * WINDOW, 3 * WINDOW), lambda i, g: (g, 0, 0, 0)),
        ],
        out_specs=pl.BlockSpec((None, seq, gw), lambda i, g: (i, 0, g)),
        scratch_shapes=[
            pltpu.VMEM((seq, gw), BF16),
            pltpu.VMEM((seq + 2 * WINDOW, ATT_D), BF16),
            pltpu.VMEM((seq + 2 * WINDOW, 2 * ATT_D), BF16),
            pltpu.VMEM((seq * ATT_GROUP, 3 * WINDOW), F32),
            pltpu.VMEM((seq * ATT_GROUP, 3 * WINDOW), BF16),
            pltpu.VMEM((seq * ATT_GROUP, WINDOW), F32),
        ],
        compiler_params=_cp(("parallel", "parallel")),
        name="window_attention",
    )(sink, proj3, proj3, proj3, q_gain, k_gain, bias)


def _window_bias(rel_table):
    nb = REL_BUCKETS // 2
    max_exact = nb // 2
    qi = jnp.arange(WINDOW)[:, None]
    ki = jnp.arange(3 * WINDOW)[None, :]
    rel = ki - WINDOW - qi
    ret = jnp.where(rel > 0, nb, 0)
    n = jnp.abs(rel)
    large = max_exact + (jnp.log(jnp.maximum(n, 1).astype(F32) / max_exact)
                         / math.log(REL_MAX_DIST / max_exact) * (nb - max_exact)).astype(I32)
    large = jnp.minimum(large, nb - 1)
    bucket = ret + jnp.where(n < max_exact, n, large)
    onehot = (bucket[None] == jnp.arange(REL_BUCKETS)[:, None, None]).astype(F32)
    bias = jnp.einsum("bh,bqk->hqk", rel_table.astype(F32), onehot, precision=lax.Precision.HIGHEST)
    bias = jnp.where((jnp.abs(rel) <= WINDOW)[None], bias * LOG2E, -jnp.inf)
    bias = bias.reshape(ATT_KV, ATT_GROUP * WINDOW, 3 * WINDOW)
    first = jnp.where(ki < WINDOW, -jnp.inf, bias)
    last = jnp.where(ki >= 2 * WINDOW, -jnp.inf, bias)
    return jnp.stack([first, bias, last], axis=1)


def _mem_kv_kernel(m_ref, g_ref, w_ref, kg_ref, o_ref):
    h = _rms(m_ref[...], g_ref[...]).astype(BF16)
    kw = MEM_HEADS * MEM_D
    for j in range(MEM_HEADS):
        cols = slice(j * MEM_D, (j + 1) * MEM_D)
        k = jnp.dot(h, w_ref[:, cols], preferred_element_type=F32)
        o_ref[:, cols] = _rms(k, kg_ref[...]).astype(BF16)
    o_ref[:, kw:] = jnp.dot(h, w_ref[:, kw:], preferred_element_type=F32).astype(BF16)


def _mem_kv(mem2, gain, w_bf16, k_gain, bm=512):
    rows = mem2.shape[0]
    bm = min(bm, rows)
    n = w_bf16.shape[1]
    return pl.pallas_call(
        _mem_kv_kernel,
        out_shape=jax.ShapeDtypeStruct((rows, n), BF16),
        grid=(rows // bm,),
        in_specs=[
            pl.BlockSpec((bm, D_MODEL), lambda i: (i, 0)),
            pl.BlockSpec((1, D_MODEL), lambda i: (0, 0)),
            pl.BlockSpec((D_MODEL, n), lambda i: (0, 0)),
            pl.BlockSpec((1, MEM_D), lambda i: (0, 0)),
        ],
        out_specs=pl.BlockSpec((bm, n), lambda i: (i, 0)),
        compiler_params=_cp(("parallel",)),
        name="mem_kv",
    )(mem2, gain, w_bf16, k_gain)


def _mem_attn_kernel(q_ref, k_ref, v_ref, qg_ref, o_ref, *, bq):
    seq = q_ref.shape[0]
    nt = (((1,), (1,)), ((), ()))

    def block(n, carry):
        rows = pl.ds(pl.multiple_of(n * bq, bq), bq)
        q = _rms(q_ref[rows, :].astype(F32), qg_ref[...]) * (MEM_D ** -0.5)
        s = lax.dot_general(q.astype(BF16), k_ref[...], nt, preferred_element_type=F32)
        p = jnp.exp(s - jnp.max(s, axis=-1, keepdims=True))
        denom = jnp.sum(p, axis=-1, keepdims=True)
        o = jnp.dot(p.astype(BF16), v_ref[...], preferred_element_type=F32) / denom
        o_ref[rows, :] = o.astype(BF16)
        return carry

    lax.fori_loop(0, seq // bq, block, 0, unroll=8)


def _mem_attention(proj3, mkv3, q_gain, bq=256):
    b, seq, _ = proj3.shape
    m_len = mkv3.shape[1]
    return pl.pallas_call(
        functools.partial(_mem_attn_kernel, bq=bq),
        out_shape=jax.ShapeDtypeStruct((b, seq, MEM_HEADS * MEM_D), BF16),
        grid=(b, MEM_HEADS),
        in_specs=[
            pl.BlockSpec((None, seq, MEM_D), lambda i, h: (i, 0, OFF_MQ // MEM_D + h)),
            pl.BlockSpec((None, m_len, MEM_D), lambda i, h: (i, 0, h)),
            pl.BlockSpec((None, m_len, MEM_D), lambda i, h: (i, 0, MEM_HEADS + h)),
            pl.BlockSpec((1, MEM_D), lambda i, h: (0, 0)),
        ],
        out_specs=pl.BlockSpec((None, seq, MEM_D), lambda i, h: (i, 0, h)),
        compiler_params=_cp(("parallel", "parallel")),
        name="mem_attention",
    )(proj3, mkv3, mkv3, q_gain)


def _merge_kernel(x_ref, g0_ref, g1_ref, g2_ref, b0_ref, b1_ref, b2_ref, wb_ref, wo_ref, gn_ref,
                  wr_ref, xo_ref, hp_ref, lg_ref):
    merged = None
    for gate_ref, br, i in ((g0_ref, b0_ref, 0), (g1_ref, b1_ref, 1), (g2_ref, b2_ref, 2)):
        gate = 0.5 * jnp.tanh(0.5 * gate_ref[...].astype(F32)) + 0.5
        t = gate * jnp.dot(br[...], wb_ref[i], preferred_element_type=F32)
        merged = t if merged is None else merged + t
    x = x_ref[...] + jnp.dot(merged.astype(BF16), wo_ref[...], preferred_element_type=F32)
    xo_ref[...] = x
    h = _rms(x, gn_ref[...])
    h_hi = h.astype(BF16)
    h_lo = (h - h_hi.astype(F32)).astype(BF16)
    lg_ref[...] = jnp.dot(jnp.concatenate([h_hi, h_lo], axis=1), wr_ref[...], preferred_element_type=F32)
    hp_ref[...] = _pack_bf16_pairs(h)


def _merge(x2, proj, ret, att, mo, wb, wo, gain, w_router_cat, bm=512):
    t = x2.shape[0]
    const2 = lambda i: (0, 0)
    row = lambda i: (i, 0)
    return pl.pallas_call(
        _merge_kernel,
        out_shape=(
            jax.ShapeDtypeStruct((t, D_MODEL), F32),
            jax.ShapeDtypeStruct((t, HALF), I32),
            jax.ShapeDtypeStruct((t, ROUTER_LANES), F32),
        ),
        grid=(t // bm,),
        in_specs=[
            pl.BlockSpec((bm, D_MODEL), row),
            pl.BlockSpec((bm, D_MODEL), lambda i: (i, 0)),
            pl.BlockSpec((bm, D_MODEL), lambda i: (i, 1)),
            pl.BlockSpec((bm, D_MODEL), lambda i: (i, 2)),
            pl.BlockSpec((bm, D_MODEL), row),
            pl.BlockSpec((bm, D_MODEL), row),
            pl.BlockSpec((bm, D_MODEL), row),
            pl.BlockSpec((3, D_MODEL, D_MODEL), lambda i: (0, 0, 0)),
            pl.BlockSpec((D_MODEL, D_MODEL), const2),
            pl.BlockSpec((1, D_MODEL), const2),
            pl.BlockSpec((2 * D_MODEL, ROUTER_LANES), const2),
        ],
        out_specs=(
            pl.BlockSpec((bm, D_MODEL), row),
            pl.BlockSpec((bm, HALF), row),
            pl.BlockSpec((bm, ROUTER_LANES), row),
        ),
        compiler_params=_cp(("parallel",)),
        name="merge_router",
    )(x2, proj, proj, proj, ret, att, mo, wb, wo, gain, w_router_cat)


def _route_kernel(l_ref, b_ref, tri_ref, eid_ref, rank_ref, gate_ref, cnt_ref):
    @pl.when(pl.program_id(0) == 0)
    def _():
        cnt_ref[...] = jnp.zeros_like(cnt_ref)

    l = l_ref[0:N_EXPERTS, :] + l_ref[N_EXPERTS:2 * N_EXPERTS, :] + b_ref[...]
    ne, bt = l.shape
    iota_e = lax.broadcasted_iota(I32, (ne, bt), 0)
    picked = jnp.zeros((ne, bt), jnp.bool_)
    vals, idxs = [], []
    for _ in range(TOP_K):
        m = jnp.max(l, axis=0, keepdims=True)
        idx = jnp.min(jnp.where(l == m, iota_e, ne), axis=0, keepdims=True)
        sel = iota_e == idx
        picked = picked | sel
        l = jnp.where(sel, -jnp.inf, l)
        vals.append(m)
        idxs.append(idx)
    ex = [jnp.exp(v - vals[0]) for v in vals]
    tot = ex[0] + ex[1] + ex[2] + ex[3]
    onehot = jnp.where(picked, 1.0, 0.0)
    before = jnp.dot(onehot.astype(BF16), tri_ref[...], preferred_element_type=F32) + cnt_ref[:, 0:1]
    for k in range(TOP_K):
        eid_ref[k:k + 1, :] = idxs[k]
        gate_ref[k:k + 1, :] = ex[k] / tot
        rank_ref[k:k + 1, :] = jnp.sum(jnp.where(iota_e == idxs[k], before, 0.0), axis=0,
                                       keepdims=True).astype(I32)
    cnt_ref[...] = cnt_ref[...] + jnp.sum(onehot, axis=1, keepdims=True)


def _route(logits_t, bias_col, tri, bt=512):
    ne, t = N_EXPERTS, logits_t.shape[1]
    blk = lambda i: (0, i)
    return pl.pallas_call(
        _route_kernel,
        out_shape=(
            jax.ShapeDtypeStruct((TOP_K, t), I32),
            jax.ShapeDtypeStruct((TOP_K, t), I32),
            jax.ShapeDtypeStruct((TOP_K, t), F32),
            jax.ShapeDtypeStruct((ne, 128), F32),
        ),
        grid=(t // bt,),
        in_specs=[pl.BlockSpec((ROUTER_LANES, bt), blk), pl.BlockSpec((ne, 1), lambda i: (0, 0)),
                  pl.BlockSpec((bt, bt), lambda i: (0, 0))],
        out_specs=(
            pl.BlockSpec((TOP_K, bt), blk),
            pl.BlockSpec((TOP_K, bt), blk),
            pl.BlockSpec((TOP_K, bt), blk),
            pl.BlockSpec((ne, 128), lambda i: (0, 0)),
        ),
        compiler_params=_cp(("arbitrary",)),
        name="route_topk",
    )(logits_t, bias_col, tri)


def _slot_kernel(start_ref, eid_ref, rank_ref, slot_ref):
    eid = eid_ref[...]
    base = jnp.zeros(eid.shape, I32)
    for e in range(N_EXPERTS):
        base = jnp.where(eid == e, start_ref[e], base)
    slot_ref[...] = base + rank_ref[...]


def _slots(pad_starts, eid, rank, bt=2048):
    t = eid.shape[1]
    blk = lambda i, s: (0, i)
    return pl.pallas_call(
        _slot_kernel,
        out_shape=jax.ShapeDtypeStruct((TOP_K, t), I32),
        grid_spec=pltpu.PrefetchScalarGridSpec(
            num_scalar_prefetch=1,
            grid=(t // bt,),
            in_specs=[pl.BlockSpec((TOP_K, bt), blk), pl.BlockSpec((TOP_K, bt), blk)],
            out_specs=pl.BlockSpec((TOP_K, bt), blk),
        ),
        compiler_params=_cp(("parallel",)),
        name="route_slots",
    )(pad_starts, eid, rank)


def _sc_workers():
    info = plsc.get_sparse_core_info()
    return info.num_cores, info.num_subcores


def _dispatch_rows(hp, slot, n_rows, win=64):
    t, w = hp.shape
    nc, ns = _sc_workers()
    per_worker = t // (nc * ns)
    mesh = plsc.VectorSubcoreMesh(core_axis_name="c", subcore_axis_name="s")

    @functools.partial(
        pl.kernel,
        out_type=jax.ShapeDtypeStruct((n_rows, w), hp.dtype),
        mesh=mesh,
        scratch_types=[pltpu.VMEM((win,), I32)] * TOP_K + [pltpu.VMEM((win, w), hp.dtype)],
        name="moe_dispatch",
    )
    def k(hp_hbm, slot_hbm, buf_hbm, *scratch):
        idx_v, rows_v = scratch[:TOP_K], scratch[TOP_K]
        wid = lax.axis_index("s") * nc + lax.axis_index("c")
        base = wid * per_worker

        @pl.loop(0, per_worker // win)
        def _(i):
            off = pl.multiple_of(base + i * win, win)
            pltpu.sync_copy(hp_hbm.at[pl.ds(off, win)], rows_v)
            for kk in range(TOP_K):
                pltpu.sync_copy(slot_hbm.at[pl.ds(kk * t + off, win)], idx_v[kk])
                pltpu.sync_copy(rows_v, buf_hbm.at[idx_v[kk]])

    return k(hp, slot.reshape(-1))


def _gather_rows(out_rows, slot, win=64):
    t = slot.shape[1]
    w = out_rows.shape[1]
    nc, ns = _sc_workers()
    per_worker = t // (nc * ns)
    mesh = plsc.VectorSubcoreMesh(core_axis_name="c", subcore_axis_name="s")

    @functools.partial(
        pl.kernel,
        out_type=jax.ShapeDtypeStruct((TOP_K * t, w), out_rows.dtype),
        mesh=mesh,
        scratch_types=[pltpu.VMEM((win,), I32), pltpu.VMEM((win, w), out_rows.dtype)],
        name="moe_gather",
    )
    def k(rows_hbm, slot_hbm, og_hbm, idx_v, rows_v):
        wid = lax.axis_index("s") * nc + lax.axis_index("c")
        base = wid * per_worker

        @pl.loop(0, per_worker // win)
        def _(i):
            off = pl.multiple_of(base + i * win, win)
            for kk in range(TOP_K):
                pltpu.sync_copy(slot_hbm.at[pl.ds(kk * t + off, win)], idx_v)
                pltpu.sync_copy(rows_hbm.at[idx_v], rows_v)
                pltpu.sync_copy(rows_v, og_hbm.at[pl.ds(kk * t + off, win)])

    return k(out_rows, slot.reshape(-1)).reshape(TOP_K, t, w)


def _expert_kernel(be_ref, nu_ref, x_ref, wi_ref, bi_ref, wo_ref, bo_ref, o_ref, wi_bf, wo_bf):
    i = pl.program_id(0)
    active = i < nu_ref[0]
    new_expert = jnp.logical_or(i == 0, be_ref[i] != be_ref[jnp.maximum(i - 1, 0)])

    @pl.when(jnp.logical_and(active, new_expert))
    def _():
        rows = 128

        def cast_in(c, carry):
            sl = pl.ds(pl.multiple_of(c * rows, rows), rows)
            wi_bf[sl, :] = wi_ref[sl, :].astype(BF16)
            return carry

        def cast_out(c, carry):
            sl = pl.ds(pl.multiple_of(c * rows, rows), rows)
            wo_bf[sl, :] = wo_ref[sl, :].astype(BF16)
            return carry

        lax.fori_loop(0, D_MODEL // rows, cast_in, 0)
        lax.fori_loop(0, D_FF // rows, cast_out, 0)

    @pl.when(active)
    def _():
        hi, lo = _unpack_bf16_pairs(x_ref[...])
        x = jnp.concatenate([hi.astype(BF16), lo.astype(BF16)], axis=1)
        h = jnp.dot(x, wi_bf[...], preferred_element_type=F32) + bi_ref[...]
        gate = jnp.minimum(h[:, :D_FF], SWIGLU_LIMIT)
        lin = jnp.clip(h[:, D_FF:], -SWIGLU_LIMIT, SWIGLU_LIMIT)
        act = gate * (0.5 * jnp.tanh((0.5 * SWIGLU_ALPHA) * gate) + 0.5) * (lin + 1.0)
        o = jnp.dot(act.astype(BF16), wo_bf[...], preferred_element_type=F32) + bo_ref[...]
        o_ref[...] = _pack_bf16_pairs(o)


def _experts(block_expert, n_used, buf, w_in, b_in, w_out, b_out):
    n_rows, w = buf.shape
    n_blocks = n_rows // EXPERT_BLOCK
    return pl.pallas_call(
        _expert_kernel,
        out_shape=jax.ShapeDtypeStruct((n_rows, w), buf.dtype),
        grid_spec=pltpu.PrefetchScalarGridSpec(
            num_scalar_prefetch=2,
            grid=(n_blocks,),
            in_specs=[
                pl.BlockSpec((EXPERT_BLOCK, w), lambda i, be, nu: (i, 0)),
                pl.BlockSpec((None, D_MODEL, 2 * D_FF), lambda i, be, nu: (be[i], 0, 0)),
                pl.BlockSpec((None, 1, 2 * D_FF), lambda i, be, nu: (be[i], 0, 0)),
                pl.BlockSpec((None, D_FF, D_MODEL), lambda i, be, nu: (be[i], 0, 0)),
                pl.BlockSpec((None, 1, D_MODEL), lambda i, be, nu: (be[i], 0, 0)),
            ],
            out_specs=pl.BlockSpec((EXPERT_BLOCK, w), lambda i, be, nu: (i, 0)),
            scratch_shapes=[pltpu.VMEM((D_MODEL, 2 * D_FF), BF16), pltpu.VMEM((D_FF, D_MODEL), BF16)],
        ),
        compiler_params=_cp(("arbitrary",)),
        name="expert_ffn",
    )(block_expert, n_used, buf, w_in, b_in, w_out, b_out)


def _combine_kernel(x_ref, og_ref, g_ref, y_ref):
    g = g_ref[...]
    acc_hi = x_ref[:, :HALF]
    acc_lo = x_ref[:, HALF:]
    for k in range(TOP_K):
        hi, lo = _unpack_bf16_pairs(og_ref[k])
        gk = g[:, k:k + 1]
        acc_hi = acc_hi + gk * hi
        acc_lo = acc_lo + gk * lo
    y_ref[:, :HALF] = acc_hi
    y_ref[:, HALF:] = acc_lo


def _combine(x_mid, og, gates_tk, bm=1024):
    t = x_mid.shape[0]
    return pl.pallas_call(
        _combine_kernel,
        out_shape=jax.ShapeDtypeStruct((t, D_MODEL), F32),
        grid=(t // bm,),
        in_specs=[
            pl.BlockSpec((bm, D_MODEL), lambda i: (i, 0)),
            pl.BlockSpec((TOP_K, bm, HALF), lambda i: (0, i, 0)),
            pl.BlockSpec((bm, TOP_K), lambda i: (i, 0)),
        ],
        out_specs=pl.BlockSpec((bm, D_MODEL), lambda i: (i, 0)),
        compiler_params=_cp(("parallel",)),
        name="moe_combine",
    )(x_mid, og, gates_tk)


def _block_tables(counts, n_blocks):
    padded = (counts + EXPERT_BLOCK - 1) // EXPERT_BLOCK * EXPERT_BLOCK
    pad_ends = jnp.cumsum(padded)
    pad_starts = pad_ends - padded
    block_start = jnp.arange(n_blocks, dtype=I32) * EXPERT_BLOCK
    block_expert = jnp.minimum(jnp.sum(pad_ends[None, :] <= block_start[:, None], axis=1), N_EXPERTS - 1)
    n_used = (pad_ends[-1] // EXPERT_BLOCK).reshape(1)
    return pad_starts.astype(I32), block_expert.astype(I32), n_used.astype(I32)


def _layer(x, mem, consts, params, after=None):
    b, seq, d = x.shape
    m_len = mem.shape[1]
    t = b * seq
    x2 = x.reshape(t, d)
    gain = params["norm_mix"]
    if after is not None:
        gain = gain + 0.0 * after.astype(F32)
    proj = _in_proj(x2, gain, params["w_in"])
    proj3 = proj.reshape(b, seq, IN_WIDTH)
    ret = _retention(proj3, params["ret_out_norm"], consts["cos"], consts["sin"], consts["tabs"],
                     consts["decays"])
    att = _window_attention(proj3, params["att_q_norm"], params["att_k_norm"], params["att_sink"],
                            consts["bias"])
    mkv = _mem_kv(mem.reshape(b * m_len, d), params["mem_norm"], params["w_mem_kv"],
                  params["mem_k_norm"])
    mo = _mem_attention(proj3, mkv.reshape(b, m_len, 2 * MEM_HEADS * MEM_D), params["mem_q_norm"])
    x_mid, hp, logits = _merge(x2, proj, ret.reshape(t, d), att.reshape(t, d), mo.reshape(t, d),
                               params["w_branch"], params["w_out"], params["norm_ffn"],
                               params["w_router"])
    eid, rank, gates, counts = _route(logits.T, params["b_router"], consts["tri"])
    n_blocks = t * TOP_K // EXPERT_BLOCK + N_EXPERTS
    pad_starts, block_expert, n_used = _block_tables(counts[:, 0].astype(I32), n_blocks)
    slot = _slots(pad_starts, eid, rank)
    buf = _dispatch_rows(hp, slot, n_blocks * EXPERT_BLOCK)
    out_rows = _experts(block_expert, n_used, buf, params["w_e_in"], params["b_e_in"],
                        params["w_e_out"], params["b_e_out"])
    og = _gather_rows(out_rows, slot)
    y = _combine(x_mid, og, gates.T)
    return y.reshape(b, seq, d), slot[0, 0]


def _router_weights(w):
    w_hi = w.astype(BF16)
    w_lo = (w - w_hi.astype(F32)).astype(BF16)
    pad = jnp.zeros((D_MODEL, ROUTER_LANES - 2 * N_EXPERTS), BF16)
    top = jnp.concatenate([w_hi, w_lo, pad], axis=1)
    bottom = jnp.concatenate([w_hi, jnp.zeros_like(w_lo), pad], axis=1)
    return jnp.concatenate([top, bottom], axis=0)


def _prepare(seq, rel_table, norm_mix, w_in, ret_out_norm, att_q_norm, att_k_norm, att_sink, mem_norm,
             w_mem_kv, mem_q_norm, mem_k_norm, w_branch, w_out, norm_ffn, w_router, b_router,
             w_e_in, b_e_in, w_e_out, b_e_out):
    gates_at = IN_WIDTH - 3 * D_MODEL
    w_in0 = w_in[0]
    params = {
        "norm_mix": norm_mix[0].reshape(1, -1),
        "w_in": jnp.concatenate([w_in0[:, gates_at:], w_in0[:, :gates_at]], axis=1).astype(BF16),
        "ret_out_norm": ret_out_norm[0].reshape(1, -1),
        "att_q_norm": att_q_norm[0].reshape(1, -1),
        "att_k_norm": att_k_norm[0].reshape(1, -1),
        "att_sink": att_sink[0].astype(F32),
        "mem_norm": mem_norm[0].reshape(1, -1),
        "w_mem_kv": w_mem_kv[0].astype(BF16),
        "mem_q_norm": mem_q_norm[0].reshape(1, -1),
        "mem_k_norm": mem_k_norm[0].reshape(1, -1),
        "w_branch": w_branch[0].astype(BF16),
        "w_out": w_out[0].astype(BF16),
        "norm_ffn": norm_ffn[0].reshape(1, -1),
        "w_router": _router_weights(w_router[0]),
        "b_router": b_router[0].reshape(-1, 1),
        "w_e_in": w_e_in[0],
        "b_e_in": b_e_in[0].reshape(N_EXPERTS, 1, -1),
        "w_e_out": w_e_out[0],
        "b_e_out": b_e_out[0].reshape(N_EXPERTS, 1, -1),
    }
    cos, sin, tabs, decays = _retention_tables(seq)
    route_bt = 512
    tri = (jnp.arange(route_bt)[:, None] < jnp.arange(route_bt)[None, :]).astype(BF16)
    consts = {"cos": cos, "sin": sin, "tabs": tabs, "decays": decays,
              "bias": _window_bias(rel_table), "tri": tri}
    return consts, params


def kernel(x_prompt, x_sample, mem_prompt, mem_sample, rel_table, norm_mix, w_in, ret_out_norm, att_q_norm, att_k_norm, att_sink, mem_norm, w_mem_kv, mem_q_norm, mem_k_norm, w_branch, w_out, norm_ffn, w_router, b_router, w_e_in, b_e_in, w_e_out, b_e_out):
    consts, params = _prepare(x_prompt.shape[1], rel_table, norm_mix, w_in, ret_out_norm, att_q_norm,
                              att_k_norm, att_sink, mem_norm, w_mem_kv, mem_q_norm, mem_k_norm,
                              w_branch, w_out, norm_ffn, w_router, b_router, w_e_in, b_e_in,
                              w_e_out, b_e_out)
    y_sample, first_slot = _layer(x_sample, mem_sample, consts, params)
    y_prompt, _ = _layer(x_prompt, mem_prompt, consts, params, after=first_slot)
    return (y_prompt, y_sample)
```

```python
import functools
import math

import jax
import jax.numpy as jnp
import numpy as np
from jax import lax
from jax.experimental import pallas as pl
from jax.experimental.pallas import tpu as pltpu
from jax.experimental.pallas import tpu_sc as plsc

F32 = jnp.float32
BF16 = jnp.bfloat16
I32 = jnp.int32

D_MODEL = 1024
EPS = 1e-6
RET_HEADS = 4
RET_QK = 128
RET_V = 256
CHUNK = 128
ROPE_BASE = 10000.0
DECAY_FWD = 5.0
DECAY_BWD = 5.5
ATT_HEADS = 8
ATT_KV = 2
ATT_GROUP = ATT_HEADS // ATT_KV
ATT_D = 128
WINDOW = 128
REL_BUCKETS = 32
REL_MAX_DIST = 128
MEM_HEADS = 4
MEM_D = 256
N_EXPERTS = 32
TOP_K = 4
D_FF = 1024
SWIGLU_LIMIT = 7.0
SWIGLU_ALPHA = 1.702
EXPERT_BLOCK = 512
HALF = D_MODEL // 2
ROUTER_LANES = 128

IN_WIDTH = 8704
OFF_GATES = 0
OFF_RQ = 3072
OFF_RK = 3584
OFF_RV = 4096
OFF_RG = 5120
OFF_AQ = 6144
OFF_AK = 7168
OFF_AV = 7424
OFF_MQ = 7680

VMEM_LIMIT = 56 * 1024 * 1024
LOG2E = 1.4426950408889634


def _cp(sem, vmem=VMEM_LIMIT):
    return pltpu.CompilerParams(dimension_semantics=sem, vmem_limit_bytes=vmem)


def _rms(x, g):
    return x * lax.rsqrt(jnp.mean(x * x, axis=-1, keepdims=True) + EPS) * g


def _pack_bf16_pairs(x):
    w = x.shape[1] // 2
    bits = pltpu.bitcast(x.astype(BF16).astype(F32), I32)
    hi = bits[:, :w] & jnp.int32(-65536)
    lo = lax.shift_right_logical(bits[:, w:], jnp.int32(16))
    return hi | lo


def _unpack_bf16_pairs(u):
    hi = pltpu.bitcast(u & jnp.int32(-65536), F32)
    lo = pltpu.bitcast(lax.shift_left(u, jnp.int32(16)), F32)
    return hi, lo


def _in_proj_kernel(x_ref, g_ref, w_ref, o_ref, h_scr):
    @pl.when(pl.program_id(1) == 0)
    def _():
        h_scr[...] = _rms(x_ref[...], g_ref[...]).astype(BF16)

    sub = 256

    def cols(c, carry):
        sl = pl.ds(pl.multiple_of(c * sub, sub), sub)
        o_ref[:, sl] = jnp.dot(h_scr[...], w_ref[:, sl], preferred_element_type=F32).astype(BF16)
        return carry

    lax.fori_loop(0, o_ref.shape[1] // sub, cols, 0, unroll=True)


def _in_proj(x2, gain, w_bf16, bm=1024, bn=4352):
    t = x2.shape[0]
    n = w_bf16.shape[1]
    return pl.pallas_call(
        _in_proj_kernel,
        out_shape=jax.ShapeDtypeStruct((t, n), BF16),
        grid=(t // bm, n // bn),
        in_specs=[
            pl.BlockSpec((bm, D_MODEL), lambda i, j: (i, 0)),
            pl.BlockSpec((1, D_MODEL), lambda i, j: (0, 0)),
            pl.BlockSpec((D_MODEL, bn), lambda i, j: (0, j)),
        ],
        out_specs=pl.BlockSpec((bm, bn), lambda i, j: (i, j)),
        scratch_shapes=[pltpu.VMEM((bm, D_MODEL), BF16)],
        compiler_params=_cp(("parallel", "arbitrary")),
        name="in_proj",
    )(x2, gain, w_bf16)


def _retention_kernel(dec_ref, q_ref, k_ref, v_ref, g_ref, cos_ref, sin_ref, tab_ref, gain_ref,
                      o_ref, lhs, kb, rhs, kvf, kvb):
    h = pl.program_id(1)
    seq = q_ref.shape[0]
    n_chunks = seq // CHUNK
    dec_f = dec_ref[2 * h]
    dec_b = dec_ref[2 * h + 1]
    nt = (((1,), (1,)), ((), ()))
    tn = (((0,), (0,)), ((), ()))

    def rope(x, c, s):
        return x * c + pltpu.roll(x, RET_QK // 2, 1) * s

    def prep(n, carry):
        rows = pl.ds(pl.multiple_of(n * CHUNK, CHUNK), CHUNK)
        c = cos_ref[rows, :]
        s = sin_ref[rows, :]
        q = rope(q_ref[rows, :].astype(F32), c, s)
        k = rope(k_ref[rows, :].astype(F32), c, s) * (RET_QK ** -0.5)
        lhs[rows, 0:CHUNK] = q.astype(BF16)
        lhs[rows, CHUNK:2 * CHUNK] = (q * tab_ref[1]).astype(BF16)
        lhs[rows, 2 * CHUNK:3 * CHUNK] = (q * tab_ref[3]).astype(BF16)
        kb[rows, :] = k.astype(BF16)
        v = v_ref[rows, :]
        rhs[n, 0:CHUNK, :] = v
        kvf[n] = lax.dot_general((k * tab_ref[2]).astype(BF16), v, tn, preferred_element_type=F32)
        kvb[n] = lax.dot_general((k * tab_ref[4]).astype(BF16), v, tn, preferred_element_type=F32)
        return carry

    lax.fori_loop(0, n_chunks, prep, 0, unroll=8)

    def scan_f(n, state):
        rhs[n, CHUNK:2 * CHUNK, :] = state.astype(BF16)
        return state * dec_f + kvf[n]

    lax.fori_loop(0, n_chunks, scan_f, jnp.zeros((RET_QK, RET_V), F32))

    def scan_b(i, state):
        n = n_chunks - 1 - i
        rhs[n, 2 * CHUNK:3 * CHUNK, :] = state.astype(BF16)
        return state * dec_b + kvb[n]

    lax.fori_loop(0, n_chunks, scan_b, jnp.zeros((RET_QK, RET_V), F32))

    def chunk(n, carry):
        rows = pl.ds(pl.multiple_of(n * CHUNK, CHUNK), CHUNK)
        sc = lax.dot_general(lhs[rows, 0:CHUNK], kb[rows, :], nt, preferred_element_type=F32)
        a = jnp.concatenate([(sc * tab_ref[0]).astype(BF16), lhs[rows, CHUNK:3 * CHUNK]], axis=1)
        o = jnp.dot(a, rhs[n], preferred_element_type=F32)
        y = _rms(o, gain_ref[...])
        g = g_ref[rows, :].astype(F32)
        o_ref[rows, :] = (y * (g * (0.5 * jnp.tanh(0.5 * g) + 0.5))).astype(BF16)
        return carry

    lax.fori_loop(0, n_chunks, chunk, 0, unroll=True)


def _retention(proj3, gain, cos, sin, tabs, decays):
    b, seq, _ = proj3.shape
    return pl.pallas_call(
        _retention_kernel,
        out_shape=jax.ShapeDtypeStruct((b, seq, RET_HEADS * RET_V), BF16),
        grid=(b, RET_HEADS),
        in_specs=[
            pl.BlockSpec(memory_space=pltpu.SMEM),
            pl.BlockSpec((None, seq, RET_QK), lambda i, h: (i, 0, OFF_RQ // RET_QK + h)),
            pl.BlockSpec((None, seq, RET_QK), lambda i, h: (i, 0, OFF_RK // RET_QK + h)),
            pl.BlockSpec((None, seq, RET_V), lambda i, h: (i, 0, OFF_RV // RET_V + h)),
            pl.BlockSpec((None, seq, RET_V), lambda i, h: (i, 0, OFF_RG // RET_V + h)),
            pl.BlockSpec((seq, RET_QK), lambda i, h: (0, 0)),
            pl.BlockSpec((seq, RET_QK), lambda i, h: (0, 0)),
            pl.BlockSpec((None, 5, CHUNK, CHUNK), lambda i, h: (h, 0, 0, 0)),
            pl.BlockSpec((1, RET_V), lambda i, h: (0, h)),
        ],
        out_specs=pl.BlockSpec((None, seq, RET_V), lambda i, h: (i, 0, h)),
        scratch_shapes=[
            pltpu.VMEM((seq, 3 * CHUNK), BF16),
            pltpu.VMEM((seq, RET_QK), BF16),
            pltpu.VMEM((seq // CHUNK, 3 * CHUNK, RET_V), BF16),
            pltpu.VMEM((seq // CHUNK, RET_QK, RET_V), F32),
            pltpu.VMEM((seq // CHUNK, RET_QK, RET_V), F32),
        ],
        compiler_params=_cp(("parallel", "parallel")),
        name="retention",
    )(decays, proj3, proj3, proj3, proj3, cos, sin, tabs, gain)


def _retention_tables(seq):
    half = RET_QK // 2
    inv = ROPE_BASE ** (-jnp.arange(half, dtype=F32) / half)
    ang = jnp.arange(seq, dtype=F32)[:, None] * inv[None, :]
    cos = jnp.concatenate([jnp.cos(ang), jnp.cos(ang)], axis=1)
    sin = jnp.concatenate([-jnp.sin(ang), jnp.sin(ang)], axis=1)
    heads = jnp.arange(RET_HEADS, dtype=F32)
    lg_f = jnp.log1p(-jnp.exp2(-DECAY_FWD - heads))[:, None, None]
    lg_b = jnp.log1p(-jnp.exp2(-DECAY_BWD - heads))[:, None, None]
    idx = jnp.arange(CHUNK, dtype=F32)
    diff = (idx[:, None] - idx[None, :])[None]
    dmat = jnp.where(diff >= 0, jnp.exp(jnp.where(diff >= 0, diff, 0.0) * lg_f),
                     jnp.exp(jnp.where(diff < 0, -diff, 0.0) * lg_b))
    col = jnp.broadcast_to(idx[None, :, None], (RET_HEADS, CHUNK, CHUNK))
    xi_f = jnp.exp((col + 1.0) * lg_f)
    zeta_f = jnp.exp((CHUNK - 1.0 - col) * lg_f)
    xi_b = jnp.exp((CHUNK - col) * lg_b)
    zeta_b = jnp.exp(col * lg_b)
    tabs = jnp.stack([dmat, xi_f, zeta_f, xi_b, zeta_b], axis=1)
    decays = jnp.stack([jnp.exp(CHUNK * lg_f[:, 0, 0]), jnp.exp(CHUNK * lg_b[:, 0, 0])], axis=1).reshape(-1)
    return cos, sin, tabs, decays


def _window_kernel(sink_ref, q_ref, k_ref, v_ref, qg_ref, kg_ref, bias_ref, o_ref,
                   qn, kp, vp, s_scr, p_scr, e_scr):
    g = pl.program_id(1)
    seq = q_ref.shape[0]
    n_blocks = seq // WINDOW
    stack = ATT_GROUP * WINDOW
    unit = 64
    nt = (((1,), (1,)), ((), ()))
    heads = [slice(r * ATT_D, (r + 1) * ATT_D) for r in range(ATT_GROUP)]
    kp[0:WINDOW, :] = jnp.zeros((WINDOW, ATT_D), BF16)
    kp[seq + WINDOW:seq + 2 * WINDOW, :] = jnp.zeros((WINDOW, ATT_D), BF16)
    vp[0:WINDOW, :] = jnp.zeros((WINDOW, 2 * ATT_D), BF16)
    vp[seq + WINDOW:seq + 2 * WINDOW, :] = jnp.zeros((WINDOW, 2 * ATT_D), BF16)

    def prep(n, carry):
        r0 = pl.multiple_of(n * WINDOW, WINDOW)
        src = pl.ds(r0, WINDOW)
        dst = pl.ds(r0 + WINDOW, WINDOW)
        kp[dst, :] = _rms(k_ref[src, :].astype(F32), kg_ref[...]).astype(BF16)
        vp[dst, 0:ATT_D] = v_ref[src, :]
        vp[dst, ATT_D:2 * ATT_D] = jnp.ones((WINDOW, ATT_D), BF16)
        for hs in heads:
            q = _rms(q_ref[src, hs].astype(F32), qg_ref[...]) * (ATT_D ** -0.5 * LOG2E)
            qn[src, hs] = q.astype(BF16)
        return carry

    lax.fori_loop(0, n_blocks, prep, 0, unroll=2)

    def scores(n, carry):
        r0 = pl.multiple_of(n * WINDOW, WINDOW)
        kb = kp[pl.ds(r0, 3 * WINDOW), :]
        q4 = jnp.concatenate([qn[pl.ds(r0, WINDOW), hs] for hs in heads], axis=0)
        edge = jnp.where(n == 0, 0, jnp.where(n == n_blocks - 1, 2, 1))
        s_scr[pl.ds(pl.multiple_of(n * stack, stack), stack), :] = (
            lax.dot_general(q4, kb, nt, preferred_element_type=F32) + bias_ref[edge])
        return carry

    lax.fori_loop(0, n_blocks, scores, 0, unroll=4)

    def softmax(n, carry):
        base = pl.multiple_of(n * stack, stack)
        for u in range(stack // unit):
            rows = pl.ds(base + u * unit, unit)
            sink = sink_ref[g * ATT_GROUP + (u * unit) // WINDOW] * LOG2E
            s = s_scr[rows, :]
            m = jnp.maximum(jnp.max(s, axis=-1, keepdims=True), sink)
            mb = jnp.broadcast_to(m, (unit, WINDOW))
            p_scr[rows, :] = jnp.exp2(s - jnp.concatenate([mb, mb, mb], axis=1)).astype(BF16)
            e_scr[rows, :] = jnp.exp2(sink - mb)
        return carry

    lax.fori_loop(0, n_blocks, softmax, 0, unroll=2)

    def outputs(n, carry):
        r0 = pl.multiple_of(n * WINDOW, WINDOW)
        rows = pl.ds(pl.multiple_of(n * stack, stack), stack)
        oe = jnp.dot(p_scr[rows, :], vp[pl.ds(r0, 3 * WINDOW), :], preferred_element_type=F32)
        o = oe[:, 0:ATT_D] / (oe[:, ATT_D:2 * ATT_D] + e_scr[rows, :])
        for r, hs in enumerate(heads):
            o_ref[pl.ds(r0, WINDOW), hs] = o[r * WINDOW:(r + 1) * WINDOW, :].astype(BF16)
        return carry

    lax.fori_loop(0, n_blocks, outputs, 0, unroll=4)


def _window_attention(proj3, q_gain, k_gain, sink, bias):
    b, seq, _ = proj3.shape
    gw = ATT_GROUP * ATT_D
    return pl.pallas_call(
        _window_kernel,
        out_shape=jax.ShapeDtypeStruct((b, seq, ATT_HEADS * ATT_D), BF16),
        grid=(b, ATT_KV),
        in_specs=[
            pl.BlockSpec(memory_space=pltpu.SMEM),
            pl.BlockSpec((None, seq, gw), lambda i, g: (i, 0, OFF_AQ // gw + g)),
            pl.BlockSpec((None, seq, ATT_D), lambda i, g: (i, 0, OFF_AK // ATT_D + g)),
            pl.BlockSpec((None, seq, ATT_D), lambda i, g: (i, 0, OFF_AV // ATT_D + g)),
            pl.BlockSpec((1, ATT_D), lambda i, g: (0, 0)),
            pl.BlockSpec((1, ATT_D), lambda i, g: (0, 0)),
            pl.BlockSpec((None, 3, ATT_GROUP * WINDOW, 3 * WINDOW), lambda i, g: (g, 0, 0, 0)),
        ],
        out_specs=pl.BlockSpec((None, seq, gw), lambda i, g: (i, 0, g)),
        scratch_shapes=[
            pltpu.VMEM((seq, gw), BF16),
            pltpu.VMEM((seq + 2 * WINDOW, ATT_D), BF16),
            pltpu.VMEM((seq + 2 * WINDOW, 2 * ATT_D), BF16),
            pltpu.VMEM((seq * ATT_GROUP, 3 * WINDOW), F32),
            pltpu.VMEM((seq * ATT_GROUP, 3 * WINDOW), BF16),
            pltpu.VMEM((seq * ATT_GROUP, WINDOW), F32),
        ],
        compiler_params=_cp(("parallel", "parallel")),
        name="window_attention",
    )(sink, proj3, proj3, proj3, q_gain, k_gain, bias)


def _window_bias(rel_table):
    nb = REL_BUCKETS // 2
    max_exact = nb // 2
    qi = jnp.arange(WINDOW)[:, None]
    ki = jnp.arange(3 * WINDOW)[None, :]
    rel = ki - WINDOW - qi
    ret = jnp.where(rel > 0, nb, 0)
    n = jnp.abs(rel)
    large = max_exact + (jnp.log(jnp.maximum(n, 1).astype(F32) / max_exact)
                         / math.log(REL_MAX_DIST / max_exact) * (nb - max_exact)).astype(I32)
    large = jnp.minimum(large, nb - 1)
    bucket = ret + jnp.where(n < max_exact, n, large)
    onehot = (bucket[None] == jnp.arange(REL_BUCKETS)[:, None, None]).astype(F32)
    bias = jnp.einsum("bh,bqk->hqk", rel_table.astype(F32), onehot, precision=lax.Precision.HIGHEST)
    bias = jnp.where((jnp.abs(rel) <= WINDOW)[None], bias * LOG2E, -jnp.inf)
    bias = bias.reshape(ATT_KV, ATT_GROUP * WINDOW, 3 * WINDOW)
    first = jnp.where(ki < WINDOW, -jnp.inf, bias)
    last = jnp.where(ki >= 2 * WINDOW, -jnp.inf, bias)
    return jnp.stack([first, bias, last], axis=1)


def _mem_kv_kernel(m_ref, g_ref, w_ref, kg_ref, o_ref):
    h = _rms(m_ref[...], g_ref[...]).astype(BF16)
    kw = MEM_HEADS * MEM_D
    for j in range(MEM_HEADS):
        cols = slice(j * MEM_D, (j + 1) * MEM_D)
        k = jnp.dot(h, w_ref[:, cols], preferred_element_type=F32)
        o_ref[:, cols] = _rms(k, kg_ref[...]).astype(BF16)
    o_ref[:, kw:] = jnp.dot(h, w_ref[:, kw:], preferred_element_type=F32).astype(BF16)


def _mem_kv(mem2, gain, w_bf16, k_gain, bm=512):
    rows = mem2.shape[0]
    bm = min(bm, rows)
    n = w_bf16.shape[1]
    return pl.pallas_call(
        _mem_kv_kernel,
        out_shape=jax.ShapeDtypeStruct((rows, n), BF16),
        grid=(rows // bm,),
        in_specs=[
            pl.BlockSpec((bm, D_MODEL), lambda i: (i, 0)),
            pl.BlockSpec((1, D_MODEL), lambda i: (0, 0)),
            pl.BlockSpec((D_MODEL, n), lambda i: (0, 0)),
            pl.BlockSpec((1, MEM_D), lambda i: (0, 0)),
        ],
        out_specs=pl.BlockSpec((bm, n), lambda i: (i, 0)),
        compiler_params=_cp(("parallel",)),
        name="mem_kv",
    )(mem2, gain, w_bf16, k_gain)


def _mem_attn_kernel(q_ref, k_ref, v_ref, qg_ref, o_ref, *, bq):
    seq = q_ref.shape[0]
    nt = (((1,), (1,)), ((), ()))

    def block(n, carry):
        rows = pl.ds(pl.multiple_of(n * bq, bq), bq)
        q = _rms(q_ref[rows, :].astype(F32), qg_ref[...]) * (MEM_D ** -0.5)
        s = lax.dot_general(q.astype(BF16), k_ref[...], nt, preferred_element_type=F32)
        p = jnp.exp(s - jnp.max(s, axis=-1, keepdims=True))
        denom = jnp.sum(p, axis=-1, keepdims=True)
        o = jnp.dot(p.astype(BF16), v_ref[...], preferred_element_type=F32) / denom
        o_ref[rows, :] = o.astype(BF16)
        return carry

    lax.fori_loop(0, seq // bq, block, 0, unroll=8)


def _mem_attention(proj3, mkv3, q_gain, bq=256):
    b, seq, _ = proj3.shape
    m_len = mkv3.shape[1]
    return pl.pallas_call(
        functools.partial(_mem_attn_kernel, bq=bq),
        out_shape=jax.ShapeDtypeStruct((b, seq, MEM_HEADS * MEM_D), BF16),
        grid=(b, MEM_HEADS),
        in_specs=[
            pl.BlockSpec((None, seq, MEM_D), lambda i, h: (i, 0, OFF_MQ // MEM_D + h)),
            pl.BlockSpec((None, m_len, MEM_D), lambda i, h: (i, 0, h)),
            pl.BlockSpec((None, m_len, MEM_D), lambda i, h: (i, 0, MEM_HEADS + h)),
            pl.BlockSpec((1, MEM_D), lambda i, h: (0, 0)),
        ],
        out_specs=pl.BlockSpec((None, seq, MEM_D), lambda i, h: (i, 0, h)),
        compiler_params=_cp(("parallel", "parallel")),
        name="mem_attention",
    )(proj3, mkv3, mkv3, q_gain)


def _merge_kernel(x_ref, g0_ref, g1_ref, g2_ref, b0_ref, b1_ref, b2_ref, wb_ref, wo_ref, gn_ref,
                  wr_ref, xo_ref, hp_ref, lg_ref, lg_scr):
    merged = None
    for gate_ref, br, i in ((g0_ref, b0_ref, 0), (g1_ref, b1_ref, 1), (g2_ref, b2_ref, 2)):
        gate = 0.5 * jnp.tanh(0.5 * gate_ref[...].astype(F32)) + 0.5
        t = gate * jnp.dot(br[...], wb_ref[i], preferred_element_type=F32)
        merged = t if merged is None else merged + t
    x = x_ref[...] + jnp.dot(merged.astype(BF16), wo_ref[...], preferred_element_type=F32)
    xo_ref[...] = x
    h = _rms(x, gn_ref[...])
    h_hi = h.astype(BF16)
    h_lo = (h - h_hi.astype(F32)).astype(BF16)
    lg_scr[...] = jnp.dot(jnp.concatenate([h_hi, h_lo], axis=1), wr_ref[...], preferred_element_type=F32)
    lg_ref[...] = lg_scr[...].T
    hp_ref[...] = _pack_bf16_pairs(h)


def _merge(x2, proj, ret, att, mo, wb, wo, gain, w_router_cat, bm=512):
    t = x2.shape[0]
    const2 = lambda i: (0, 0)
    row = lambda i: (i, 0)
    return pl.pallas_call(
        _merge_kernel,
        out_shape=(
            jax.ShapeDtypeStruct((t, D_MODEL), F32),
            jax.ShapeDtypeStruct((t, HALF), I32),
            jax.ShapeDtypeStruct((ROUTER_LANES, t), F32),
        ),
        grid=(t // bm,),
        in_specs=[
            pl.BlockSpec((bm, D_MODEL), row),
            pl.BlockSpec((bm, D_MODEL), lambda i: (i, 0)),
            pl.BlockSpec((bm, D_MODEL), lambda i: (i, 1)),
            pl.BlockSpec((bm, D_MODEL), lambda i: (i, 2)),
            pl.BlockSpec((bm, D_MODEL), row),
            pl.BlockSpec((bm, D_MODEL), row),
            pl.BlockSpec((bm, D_MODEL), row),
            pl.BlockSpec((3, D_MODEL, D_MODEL), lambda i: (0, 0, 0)),
            pl.BlockSpec((D_MODEL, D_MODEL), const2),
            pl.BlockSpec((1, D_MODEL), const2),
            pl.BlockSpec((2 * D_MODEL, ROUTER_LANES), const2),
        ],
        out_specs=(
            pl.BlockSpec((bm, D_MODEL), row),
            pl.BlockSpec((bm, HALF), row),
            pl.BlockSpec((ROUTER_LANES, bm), lambda i: (0, i)),
        ),
        scratch_shapes=[pltpu.VMEM((bm, ROUTER_LANES), F32)],
        compiler_params=_cp(("parallel",)),
        name="merge_router",
    )(x2, proj, proj, proj, ret, att, mo, wb, wo, gain, w_router_cat)


def _route_kernel(l_ref, b_ref, tri_ref, eid_ref, rank_ref, gate_ref, cnt_ref):
    @pl.when(pl.program_id(0) == 0)
    def _():
        cnt_ref[...] = jnp.zeros_like(cnt_ref)

    l = l_ref[0:N_EXPERTS, :] + l_ref[N_EXPERTS:2 * N_EXPERTS, :] + b_ref[...]
    ne, bt = l.shape
    iota_e = lax.broadcasted_iota(I32, (ne, bt), 0)
    picked = jnp.zeros((ne, bt), jnp.bool_)
    vals, idxs = [], []
    for _ in range(TOP_K):
        m = jnp.max(l, axis=0, keepdims=True)
        idx = jnp.min(jnp.where(l == m, iota_e, ne), axis=0, keepdims=True)
        sel = iota_e == idx
        picked = picked | sel
        l = jnp.where(sel, -jnp.inf, l)
        vals.append(m)
        idxs.append(idx)
    ex = [jnp.exp(v - vals[0]) for v in vals]
    tot = ex[0] + ex[1] + ex[2] + ex[3]
    onehot = jnp.where(picked, 1.0, 0.0)
    before = jnp.dot(onehot.astype(BF16), tri_ref[...], preferred_element_type=F32) + cnt_ref[:, 0:1]
    for k in range(TOP_K):
        eid_ref[k:k + 1, :] = idxs[k]
        gate_ref[k:k + 1, :] = ex[k] / tot
        rank_ref[k:k + 1, :] = jnp.sum(jnp.where(iota_e == idxs[k], before, 0.0), axis=0,
                                       keepdims=True).astype(I32)
    cnt_ref[...] = cnt_ref[...] + jnp.sum(onehot, axis=1, keepdims=True)


def _route(logits_t, bias_col, tri):
    bt = tri.shape[0]
    ne, t = N_EXPERTS, logits_t.shape[1]
    blk = lambda i: (0, i)
    return pl.pallas_call(
        _route_kernel,
        out_shape=(
            jax.ShapeDtypeStruct((TOP_K, t), I32),
            jax.ShapeDtypeStruct((TOP_K, t), I32),
            jax.ShapeDtypeStruct((TOP_K, t), F32),
            jax.ShapeDtypeStruct((ne, 128), F32),
        ),
        grid=(t // bt,),
        in_specs=[pl.BlockSpec((ROUTER_LANES, bt), blk), pl.BlockSpec((ne, 1), lambda i: (0, 0)),
                  pl.BlockSpec((bt, bt), lambda i: (0, 0))],
        out_specs=(
            pl.BlockSpec((TOP_K, bt), blk),
            pl.BlockSpec((TOP_K, bt), blk),
            pl.BlockSpec((TOP_K, bt), blk),
            pl.BlockSpec((ne, 128), lambda i: (0, 0)),
        ),
        compiler_params=_cp(("arbitrary",)),
        name="route_topk",
    )(logits_t, bias_col, tri)


def _slot_kernel(start_ref, eid_ref, rank_ref, slot_ref):
    eid = eid_ref[...]
    base = jnp.zeros(eid.shape, I32)
    for e in range(N_EXPERTS):
        base = jnp.where(eid == e, start_ref[e], base)
    slot_ref[...] = base + rank_ref[...]


def _slots(pad_starts, eid, rank, bt=2048):
    t = eid.shape[1]
    blk = lambda i, s: (0, i)
    return pl.pallas_call(
        _slot_kernel,
        out_shape=jax.ShapeDtypeStruct((TOP_K, t), I32),
        grid_spec=pltpu.PrefetchScalarGridSpec(
            num_scalar_prefetch=1,
            grid=(t // bt,),
            in_specs=[pl.BlockSpec((TOP_K, bt), blk), pl.BlockSpec((TOP_K, bt), blk)],
            out_specs=pl.BlockSpec((TOP_K, bt), blk),
        ),
        compiler_params=_cp(("parallel",)),
        name="route_slots",
    )(pad_starts, eid, rank)


def _sc_workers():
    info = plsc.get_sparse_core_info()
    return info.num_cores, info.num_subcores


def _dispatch_rows(hp, slot, n_rows, win=64):
    t, w = hp.shape
    nc, ns = _sc_workers()
    per_worker = t // (nc * ns)
    mesh = plsc.VectorSubcoreMesh(core_axis_name="c", subcore_axis_name="s")

    @functools.partial(
        pl.kernel,
        out_type=jax.ShapeDtypeStruct((n_rows, w), hp.dtype),
        mesh=mesh,
        scratch_types=[pltpu.VMEM((win,), I32)] * TOP_K + [pltpu.VMEM((win, w), hp.dtype)],
        name="moe_dispatch",
    )
    def k(hp_hbm, slot_hbm, buf_hbm, *scratch):
        idx_v, rows_v = scratch[:TOP_K], scratch[TOP_K]
        wid = lax.axis_index("s") * nc + lax.axis_index("c")
        base = wid * per_worker

        @pl.loop(0, per_worker // win)
        def _(i):
            off = pl.multiple_of(base + i * win, win)
            pltpu.sync_copy(hp_hbm.at[pl.ds(off, win)], rows_v)
            for kk in range(TOP_K):
                pltpu.sync_copy(slot_hbm.at[pl.ds(kk * t + off, win)], idx_v[kk])
                pltpu.sync_copy(rows_v, buf_hbm.at[idx_v[kk]])

    return k(hp, slot.reshape(-1))


def _gather_rows(out_rows, slot, win=64):
    t = slot.shape[1]
    w = out_rows.shape[1]
    nc, ns = _sc_workers()
    per_worker = t // (nc * ns)
    mesh = plsc.VectorSubcoreMesh(core_axis_name="c", subcore_axis_name="s")

    @functools.partial(
        pl.kernel,
        out_type=jax.ShapeDtypeStruct((TOP_K * t, w), out_rows.dtype),
        mesh=mesh,
        scratch_types=[pltpu.VMEM((win,), I32), pltpu.VMEM((win, w), out_rows.dtype)],
        name="moe_gather",
    )
    def k(rows_hbm, slot_hbm, og_hbm, idx_v, rows_v):
        wid = lax.axis_index("s") * nc + lax.axis_index("c")
        base = wid * per_worker

        @pl.loop(0, per_worker // win)
        def _(i):
            off = pl.multiple_of(base + i * win, win)
            for kk in range(TOP_K):
                pltpu.sync_copy(slot_hbm.at[pl.ds(kk * t + off, win)], idx_v)
                pltpu.sync_copy(rows_hbm.at[idx_v], rows_v)
                pltpu.sync_copy(rows_v, og_hbm.at[pl.ds(kk * t + off, win)])

    return k(out_rows, slot.reshape(-1)).reshape(TOP_K, t, w)


def _expert_kernel(be_ref, nu_ref, x_ref, wi_ref, bi_ref, wo_ref, bo_ref, o_ref, wi_bf, wo_bf, act_scr):
    i = pl.program_id(0)
    n_used = nu_ref[0]
    last = be_ref.shape[0] - 1
    e_up = be_ref[jnp.minimum(i, last)]
    e_up_prev = be_ref[jnp.clip(i - 1, 0, last)]
    e_down_prev = be_ref[jnp.clip(i - 2, 0, last)]
    up = i < n_used
    down = jnp.logical_and(i >= 1, i <= n_used)
    rows = 128

    @pl.when(jnp.logical_and(up, jnp.logical_or(i == 0, e_up != e_up_prev)))
    def _():
        def cast(c, carry):
            sl = pl.ds(pl.multiple_of(c * rows, rows), rows)
            wi_bf[sl, :] = wi_ref[sl, :].astype(BF16)
            return carry

        lax.fori_loop(0, D_MODEL // rows, cast, 0)

    @pl.when(jnp.logical_and(down, jnp.logical_or(i == 1, e_up_prev != e_down_prev)))
    def _():
        def cast(c, carry):
            sl = pl.ds(pl.multiple_of(c * rows, rows), rows)
            wo_bf[sl, :] = wo_ref[sl, :].astype(BF16)
            return carry

        lax.fori_loop(0, D_FF // rows, cast, 0)

    def up_half():
        hi, lo = _unpack_bf16_pairs(x_ref[...])
        x = jnp.concatenate([hi.astype(BF16), lo.astype(BF16)], axis=1)
        h = jnp.dot(x, wi_bf[...], preferred_element_type=F32) + bi_ref[...]
        gate = jnp.minimum(h[:, :D_FF], SWIGLU_LIMIT)
        lin = jnp.clip(h[:, D_FF:], -SWIGLU_LIMIT, SWIGLU_LIMIT)
        act = gate * (0.5 * jnp.tanh((0.5 * SWIGLU_ALPHA) * gate) + 0.5) * (lin + 1.0)
        act_scr[i % 2] = act.astype(BF16)

    def down_half():
        o = jnp.dot(act_scr[(i + 1) % 2], wo_bf[...], preferred_element_type=F32) + bo_ref[...]
        o_ref[...] = _pack_bf16_pairs(o)

    @pl.when(jnp.logical_and(up, down))
    def _():
        down_half()
        up_half()

    @pl.when(jnp.logical_and(up, jnp.logical_not(down)))
    def _():
        up_half()

    @pl.when(jnp.logical_and(down, jnp.logical_not(up)))
    def _():
        down_half()


def _experts(block_expert, n_used, buf, w_in, b_in, w_out, b_out):
    n_rows, w = buf.shape
    n_blocks = n_rows // EXPERT_BLOCK
    last = n_blocks - 1
    up_blk = lambda i, be, nu: (jnp.minimum(i, last), 0)
    up_exp = lambda i, be, nu: (be[jnp.minimum(i, last)], 0, 0)
    down_blk = lambda i, be, nu: (jnp.maximum(i - 1, 0), 0)
    down_exp = lambda i, be, nu: (be[jnp.maximum(i - 1, 0)], 0, 0)
    return pl.pallas_call(
        _expert_kernel,
        out_shape=jax.ShapeDtypeStruct((n_rows, w), buf.dtype),
        grid_spec=pltpu.PrefetchScalarGridSpec(
            num_scalar_prefetch=2,
            grid=(n_blocks + 1,),
            in_specs=[
                pl.BlockSpec((EXPERT_BLOCK, w), up_blk),
                pl.BlockSpec((None, D_MODEL, 2 * D_FF), up_exp),
                pl.BlockSpec((None, 1, 2 * D_FF), up_exp),
                pl.BlockSpec((None, D_FF, D_MODEL), down_exp),
                pl.BlockSpec((None, 1, D_MODEL), down_exp),
            ],
            out_specs=pl.BlockSpec((EXPERT_BLOCK, w), down_blk),
            scratch_shapes=[pltpu.VMEM((D_MODEL, 2 * D_FF), BF16), pltpu.VMEM((D_FF, D_MODEL), BF16),
                            pltpu.VMEM((2, EXPERT_BLOCK, D_FF), BF16)],
        ),
        compiler_params=_cp(("arbitrary",)),
        name="expert_ffn",
    )(block_expert, n_used, buf, w_in, b_in, w_out, b_out)


def _combine_kernel(x_ref, og_ref, g_ref, y_ref):
    g = g_ref[...]
    acc_hi = x_ref[:, :HALF]
    acc_lo = x_ref[:, HALF:]
    for k in range(TOP_K):
        hi, lo = _unpack_bf16_pairs(og_ref[k])
        gk = g[:, k:k + 1]
        acc_hi = acc_hi + gk * hi
        acc_lo = acc_lo + gk * lo
    y_ref[:, :HALF] = acc_hi
    y_ref[:, HALF:] = acc_lo


def _combine(x_mid, og, gates_tk, bm=1024):
    t = x_mid.shape[0]
    return pl.pallas_call(
        _combine_kernel,
        out_shape=jax.ShapeDtypeStruct((t, D_MODEL), F32),
        grid=(t // bm,),
        in_specs=[
            pl.BlockSpec((bm, D_MODEL), lambda i: (i, 0)),
            pl.BlockSpec((TOP_K, bm, HALF), lambda i: (0, i, 0)),
            pl.BlockSpec((bm, TOP_K), lambda i: (i, 0)),
        ],
        out_specs=pl.BlockSpec((bm, D_MODEL), lambda i: (i, 0)),
        compiler_params=_cp(("parallel",)),
        name="moe_combine",
    )(x_mid, og, gates_tk)


def _block_tables(counts, n_blocks):
    padded = (counts + EXPERT_BLOCK - 1) // EXPERT_BLOCK * EXPERT_BLOCK
    pad_ends = jnp.cumsum(padded)
    pad_starts = pad_ends - padded
    block_start = jnp.arange(n_blocks, dtype=I32) * EXPERT_BLOCK
    block_expert = jnp.minimum(jnp.sum(pad_ends[None, :] <= block_start[:, None], axis=1), N_EXPERTS - 1)
    n_used = (pad_ends[-1] // EXPERT_BLOCK).reshape(1)
    return pad_starts.astype(I32), block_expert.astype(I32), n_used.astype(I32)


def _layer(x, mem, consts, params, after=None):
    b, seq, d = x.shape
    m_len = mem.shape[1]
    t = b * seq
    x2 = x.reshape(t, d)
    gain = params["norm_mix"]
    if after is not None:
        gain = gain + 0.0 * after.astype(F32)
    proj = _in_proj(x2, gain, params["w_in"])
    proj3 = proj.reshape(b, seq, IN_WIDTH)
    ret = _retention(proj3, params["ret_out_norm"], consts["cos"], consts["sin"], consts["tabs"],
                     consts["decays"])
    att = _window_attention(proj3, params["att_q_norm"], params["att_k_norm"], params["att_sink"],
                            consts["bias"])
    mkv = _mem_kv(mem.reshape(b * m_len, d), params["mem_norm"], params["w_mem_kv"],
                  params["mem_k_norm"])
    mo = _mem_attention(proj3, mkv.reshape(b, m_len, 2 * MEM_HEADS * MEM_D), params["mem_q_norm"])
    x_mid, hp, logits_t = _merge(x2, proj, ret.reshape(t, d), att.reshape(t, d), mo.reshape(t, d),
                               params["w_branch"], params["w_out"], params["norm_ffn"],
                               params["w_router"])
    eid, rank, gates, counts = _route(logits_t, params["b_router"], consts["tri"])
    n_blocks = t * TOP_K // EXPERT_BLOCK + N_EXPERTS
    pad_starts, block_expert, n_used = _block_tables(counts[:, 0].astype(I32), n_blocks)
    slot = _slots(pad_starts, eid, rank)
    buf = _dispatch_rows(hp, slot, n_blocks * EXPERT_BLOCK)
    out_rows = _experts(block_expert, n_used, buf, params["w_e_in"], params["b_e_in"],
                        params["w_e_out"], params["b_e_out"])
    og = _gather_rows(out_rows, slot)
    y = _combine(x_mid, og, gates.T)
    return y.reshape(b, seq, d), slot[0, 0]


def _router_weights(w):
    w_hi = w.astype(BF16)
    w_lo = (w - w_hi.astype(F32)).astype(BF16)
    pad = jnp.zeros((D_MODEL, ROUTER_LANES - 2 * N_EXPERTS), BF16)
    top = jnp.concatenate([w_hi, w_lo, pad], axis=1)
    bottom = jnp.concatenate([w_hi, jnp.zeros_like(w_lo), pad], axis=1)
    return jnp.concatenate([top, bottom], axis=0)


def _prepare(seq, rel_table, norm_mix, w_in, ret_out_norm, att_q_norm, att_k_norm, att_sink, mem_norm,
             w_mem_kv, mem_q_norm, mem_k_norm, w_branch, w_out, norm_ffn, w_router, b_router,
             w_e_in, b_e_in, w_e_out, b_e_out):
    gates_at = IN_WIDTH - 3 * D_MODEL
    w_in0 = w_in[0]
    params = {
        "norm_mix": norm_mix[0].reshape(1, -1),
        "w_in": jnp.concatenate([w_in0[:, gates_at:], w_in0[:, :gates_at]], axis=1).astype(BF16),
        "ret_out_norm": ret_out_norm[0].reshape(1, -1),
        "att_q_norm": att_q_norm[0].reshape(1, -1),
        "att_k_norm": att_k_norm[0].reshape(1, -1),
        "att_sink": att_sink[0].astype(F32),
        "mem_norm": mem_norm[0].reshape(1, -1),
        "w_mem_kv": w_mem_kv[0].astype(BF16),
        "mem_q_norm": mem_q_norm[0].reshape(1, -1),
        "mem_k_norm": mem_k_norm[0].reshape(1, -1),
        "w_branch": w_branch[0].astype(BF16),
        "w_out": w_out[0].astype(BF16),
        "norm_ffn": norm_ffn[0].reshape(1, -1),
        "w_router": _router_weights(w_router[0]),
        "b_router": b_router[0].reshape(-1, 1),
        "w_e_in": w_e_in[0],
        "b_e_in": b_e_in[0].reshape(N_EXPERTS, 1, -1),
        "w_e_out": w_e_out[0],
        "b_e_out": b_e_out[0].reshape(N_EXPERTS, 1, -1),
    }
    cos, sin, tabs, decays = _retention_tables(seq)
    route_bt = 1024
    tri = (jnp.arange(route_bt)[:, None] < jnp.arange(route_bt)[None, :]).astype(BF16)
    consts = {"cos": cos, "sin": sin, "tabs": tabs, "decays": decays,
              "bias": _window_bias(rel_table), "tri": tri}
    return consts, params


def kernel(x_prompt, x_sample, mem_prompt, mem_sample, rel_table, norm_mix, w_in, ret_out_norm, att_q_norm, att_k_norm, att_sink, mem_norm, w_mem_kv, mem_q_norm, mem_k_norm, w_branch, w_out, norm_ffn, w_router, b_router, w_e_in, b_e_in, w_e_out, b_e_out):
    consts, params = _prepare(x_prompt.shape[1], rel_table, norm_mix, w_in, ret_out_norm, att_q_norm,
                              att_k_norm, att_sink, mem_norm, w_mem_kv, mem_q_norm, mem_k_norm,
                              w_branch, w_out, norm_ffn, w_router, b_router, w_e_in, b_e_in,
                              w_e_out, b_e_out)
    y_sample, first_slot = _layer(x_sample, mem_sample, consts, params)
    y_prompt, _ = _layer(x_prompt, mem_prompt, consts, params, after=first_slot)
    return (y_prompt, y_sample)
```

```python
import functools
import math

import jax
import jax.numpy as jnp
import numpy as np
from jax import lax
from jax.experimental import pallas as pl
from jax.experimental.pallas import tpu as pltpu
from jax.experimental.pallas import tpu_sc as plsc

F32 = jnp.float32
BF16 = jnp.bfloat16
I32 = jnp.int32

D_MODEL = 1024
EPS = 1e-6
RET_HEADS = 4
RET_QK = 128
RET_V = 256
CHUNK = 128
ROPE_BASE = 10000.0
DECAY_FWD = 5.0
DECAY_BWD = 5.5
ATT_HEADS = 8
ATT_KV = 2
ATT_GROUP = ATT_HEADS // ATT_KV
ATT_D = 128
WINDOW = 128
REL_BUCKETS = 32
REL_MAX_DIST = 128
MEM_HEADS = 4
MEM_D = 256
N_EXPERTS = 32
TOP_K = 4
D_FF = 1024
SWIGLU_LIMIT = 7.0
SWIGLU_ALPHA = 1.702
EXPERT_BLOCK = 512
EXPERT_BLOCK_LARGE = 1024
LARGE_BLOCK_MIN_ROWS = 4096
HALF = D_MODEL // 2
ROUTER_LANES = 128

IN_WIDTH = 8704
OFF_GATES = 0
OFF_RQ = 3072
OFF_RK = 3584
OFF_RV = 4096
OFF_RG = 5120
OFF_AQ = 6144
OFF_AK = 7168
OFF_AV = 7424
OFF_MQ = 7680

VMEM_LIMIT = 56 * 1024 * 1024
LOG2E = 1.4426950408889634


def _cp(sem, vmem=VMEM_LIMIT):
    return pltpu.CompilerParams(dimension_semantics=sem, vmem_limit_bytes=vmem)


def _rms(x, g):
    return x * lax.rsqrt(jnp.mean(x * x, axis=-1, keepdims=True) + EPS) * g


def _pack_bf16_pairs(x):
    w = x.shape[1] // 2
    bits = pltpu.bitcast(x.astype(BF16).astype(F32), I32)
    hi = bits[:, :w] & jnp.int32(-65536)
    lo = lax.shift_right_logical(bits[:, w:], jnp.int32(16))
    return hi | lo


def _unpack_bf16_pairs(u):
    hi = pltpu.bitcast(u & jnp.int32(-65536), F32)
    lo = pltpu.bitcast(lax.shift_left(u, jnp.int32(16)), F32)
    return hi, lo


def _in_proj_kernel(x_ref, g_ref, w_ref, o_ref, h_scr):
    @pl.when(pl.program_id(1) == 0)
    def _():
        h_scr[...] = _rms(x_ref[...], g_ref[...]).astype(BF16)

    sub = 256

    def cols(c, carry):
        sl = pl.ds(pl.multiple_of(c * sub, sub), sub)
        o_ref[:, sl] = jnp.dot(h_scr[...], w_ref[:, sl], preferred_element_type=F32).astype(BF16)
        return carry

    lax.fori_loop(0, o_ref.shape[1] // sub, cols, 0, unroll=True)


def _in_proj(x2, gain, w_bf16, bm=1024, bn=4352):
    t = x2.shape[0]
    n = w_bf16.shape[1]
    return pl.pallas_call(
        _in_proj_kernel,
        out_shape=jax.ShapeDtypeStruct((t, n), BF16),
        grid=(t // bm, n // bn),
        in_specs=[
            pl.BlockSpec((bm, D_MODEL), lambda i, j: (i, 0)),
            pl.BlockSpec((1, D_MODEL), lambda i, j: (0, 0)),
            pl.BlockSpec((D_MODEL, bn), lambda i, j: (0, j)),
        ],
        out_specs=pl.BlockSpec((bm, bn), lambda i, j: (i, j)),
        scratch_shapes=[pltpu.VMEM((bm, D_MODEL), BF16)],
        compiler_params=_cp(("parallel", "arbitrary")),
        name="in_proj",
    )(x2, gain, w_bf16)


def _retention_kernel(dec_ref, q_ref, k_ref, v_ref, g_ref, cos_ref, sin_ref, tab_ref, gain_ref,
                      o_ref, lhs, rhs, acc, kvf, kvb):
    h = pl.program_id(1)
    seq = q_ref.shape[0]
    n_chunks = seq // CHUNK
    dec_f = dec_ref[2 * h]
    dec_b = dec_ref[2 * h + 1]
    nt = (((1,), (1,)), ((), ()))

    def rope(x, c, s):
        return x * c + pltpu.roll(x, RET_QK // 2, 1) * s

    def intra(n, carry):
        rows = pl.ds(pl.multiple_of(n * CHUNK, CHUNK), CHUNK)
        c = cos_ref[rows, :]
        s = sin_ref[rows, :]
        q = rope(q_ref[rows, :].astype(F32), c, s)
        k = rope(k_ref[rows, :].astype(F32), c, s) * (RET_QK ** -0.5)
        lhs[rows, 0:CHUNK] = (q * tab_ref[1]).astype(BF16)
        lhs[rows, CHUNK:2 * CHUNK] = (q * tab_ref[3]).astype(BF16)
        sc = lax.dot_general(q.astype(BF16), k.astype(BF16), nt, preferred_element_type=F32)
        a = jnp.concatenate([(sc * tab_ref[0]).astype(BF16),
                             (k * tab_ref[2]).T.astype(BF16),
                             (k * tab_ref[4]).T.astype(BF16)], axis=0)
        r = jnp.dot(a, v_ref[rows, :], preferred_element_type=F32)
        acc[rows, :] = r[0:CHUNK]
        kvf[n] = r[CHUNK:2 * CHUNK]
        kvb[n] = r[2 * CHUNK:3 * CHUNK]
        return carry

    lax.fori_loop(0, n_chunks, intra, 0, unroll=8)

    def scan_f(n, state):
        rhs[n, 0:CHUNK, :] = state.astype(BF16)
        return state * dec_f + kvf[n]

    lax.fori_loop(0, n_chunks, scan_f, jnp.zeros((RET_QK, RET_V), F32))

    def scan_b(i, state):
        n = n_chunks - 1 - i
        rhs[n, CHUNK:2 * CHUNK, :] = state.astype(BF16)
        return state * dec_b + kvb[n]

    lax.fori_loop(0, n_chunks, scan_b, jnp.zeros((RET_QK, RET_V), F32))

    def cross(n, carry):
        rows = pl.ds(pl.multiple_of(n * CHUNK, CHUNK), CHUNK)
        o = acc[rows, :] + jnp.dot(lhs[rows, :], rhs[n], preferred_element_type=F32)
        y = _rms(o, gain_ref[...])
        g = g_ref[rows, :].astype(F32)
        o_ref[rows, :] = (y * (g * (0.5 * jnp.tanh(0.5 * g) + 0.5))).astype(BF16)
        return carry

    lax.fori_loop(0, n_chunks, cross, 0, unroll=True)


def _retention(proj3, gain, cos, sin, tabs, decays):
    b, seq, _ = proj3.shape
    return pl.pallas_call(
        _retention_kernel,
        out_shape=jax.ShapeDtypeStruct((b, seq, RET_HEADS * RET_V), BF16),
        grid=(b, RET_HEADS),
        in_specs=[
            pl.BlockSpec(memory_space=pltpu.SMEM),
            pl.BlockSpec((None, seq, RET_QK), lambda i, h: (i, 0, OFF_RQ // RET_QK + h)),
            pl.BlockSpec((None, seq, RET_QK), lambda i, h: (i, 0, OFF_RK // RET_QK + h)),
            pl.BlockSpec((None, seq, RET_V), lambda i, h: (i, 0, OFF_RV // RET_V + h)),
            pl.BlockSpec((None, seq, RET_V), lambda i, h: (i, 0, OFF_RG // RET_V + h)),
            pl.BlockSpec((seq, RET_QK), lambda i, h: (0, 0)),
            pl.BlockSpec((seq, RET_QK), lambda i, h: (0, 0)),
            pl.BlockSpec((None, 5, CHUNK, CHUNK), lambda i, h: (h, 0, 0, 0)),
            pl.BlockSpec((1, RET_V), lambda i, h: (0, h)),
        ],
        out_specs=pl.BlockSpec((None, seq, RET_V), lambda i, h: (i, 0, h)),
        scratch_shapes=[
            pltpu.VMEM((seq, 2 * CHUNK), BF16),
            pltpu.VMEM((seq // CHUNK, 2 * CHUNK, RET_V), BF16),
            pltpu.VMEM((seq, RET_V), F32),
            pltpu.VMEM((seq // CHUNK, RET_QK, RET_V), F32),
            pltpu.VMEM((seq // CHUNK, RET_QK, RET_V), F32),
        ],
        compiler_params=_cp(("parallel", "parallel")),
        name="retention",
    )(decays, proj3, proj3, proj3, proj3, cos, sin, tabs, gain)


def _retention_tables(seq):
    half = RET_QK // 2
    inv = ROPE_BASE ** (-jnp.arange(half, dtype=F32) / half)
    ang = jnp.arange(seq, dtype=F32)[:, None] * inv[None, :]
    cos = jnp.concatenate([jnp.cos(ang), jnp.cos(ang)], axis=1)
    sin = jnp.concatenate([-jnp.sin(ang), jnp.sin(ang)], axis=1)
    heads = jnp.arange(RET_HEADS, dtype=F32)
    lg_f = jnp.log1p(-jnp.exp2(-DECAY_FWD - heads))[:, None, None]
    lg_b = jnp.log1p(-jnp.exp2(-DECAY_BWD - heads))[:, None, None]
    idx = jnp.arange(CHUNK, dtype=F32)
    diff = (idx[:, None] - idx[None, :])[None]
    dmat = jnp.where(diff >= 0, jnp.exp(jnp.where(diff >= 0, diff, 0.0) * lg_f),
                     jnp.exp(jnp.where(diff < 0, -diff, 0.0) * lg_b))
    col = jnp.broadcast_to(idx[None, :, None], (RET_HEADS, CHUNK, CHUNK))
    xi_f = jnp.exp((col + 1.0) * lg_f)
    zeta_f = jnp.exp((CHUNK - 1.0 - col) * lg_f)
    xi_b = jnp.exp((CHUNK - col) * lg_b)
    zeta_b = jnp.exp(col * lg_b)
    tabs = jnp.stack([dmat, xi_f, zeta_f, xi_b, zeta_b], axis=1)
    decays = jnp.stack([jnp.exp(CHUNK * lg_f[:, 0, 0]), jnp.exp(CHUNK * lg_b[:, 0, 0])], axis=1).reshape(-1)
    return cos, sin, tabs, decays


def _window_kernel(sink_ref, q_ref, k_ref, v_ref, qg_ref, kg_ref, bias_ref, o_ref,
                   qn, kp, vp, s_scr, p_scr, e_scr):
    g = pl.program_id(1)
    seq = q_ref.shape[0]
    n_blocks = seq // WINDOW
    stack = ATT_GROUP * WINDOW
    unit = 64
    nt = (((1,), (1,)), ((), ()))
    heads = [slice(r * ATT_D, (r + 1) * ATT_D) for r in range(ATT_GROUP)]
    kp[0:WINDOW, :] = jnp.zeros((WINDOW, ATT_D), BF16)
    kp[seq + WINDOW:seq + 2 * WINDOW, :] = jnp.zeros((WINDOW, ATT_D), BF16)
    vp[0:WINDOW, :] = jnp.zeros((WINDOW, 2 * ATT_D), BF16)
    vp[seq + WINDOW:seq + 2 * WINDOW, :] = jnp.zeros((WINDOW, 2 * ATT_D), BF16)

    def prep(n, carry):
        r0 = pl.multiple_of(n * WINDOW, WINDOW)
        src = pl.ds(r0, WINDOW)
        dst = pl.ds(r0 + WINDOW, WINDOW)
        kp[dst, :] = _rms(k_ref[src, :].astype(F32), kg_ref[...]).astype(BF16)
        vp[dst, 0:ATT_D] = v_ref[src, :]
        vp[dst, ATT_D:2 * ATT_D] = jnp.ones((WINDOW, ATT_D), BF16)
        for hs in heads:
            q = _rms(q_ref[src, hs].astype(F32), qg_ref[...]) * (ATT_D ** -0.5 * LOG2E)
            qn[src, hs] = q.astype(BF16)
        return carry

    lax.fori_loop(0, n_blocks, prep, 0, unroll=2)

    def scores(n, carry):
        r0 = pl.multiple_of(n * WINDOW, WINDOW)
        kb = kp[pl.ds(r0, 3 * WINDOW), :]
        q4 = jnp.concatenate([qn[pl.ds(r0, WINDOW), hs] for hs in heads], axis=0)
        edge = jnp.where(n == 0, 0, jnp.where(n == n_blocks - 1, 2, 1))
        s_scr[pl.ds(pl.multiple_of(n * stack, stack), stack), :] = (
            lax.dot_general(q4, kb, nt, preferred_element_type=F32) + bias_ref[edge])
        return carry

    lax.fori_loop(0, n_blocks, scores, 0, unroll=4)

    def softmax(n, carry):
        base = pl.multiple_of(n * stack, stack)
        for u in range(stack // unit):
            rows = pl.ds(base + u * unit, unit)
            sink = sink_ref[g * ATT_GROUP + (u * unit) // WINDOW] * LOG2E
            s = s_scr[rows, :]
            m = jnp.maximum(jnp.max(s, axis=-1, keepdims=True), sink)
            mb = jnp.broadcast_to(m, (unit, WINDOW))
            p_scr[rows, :] = jnp.exp2(s - jnp.concatenate([mb, mb, mb], axis=1)).astype(BF16)
            e_scr[rows, :] = jnp.exp2(sink - mb)
        return carry

    lax.fori_loop(0, n_blocks, softmax, 0, unroll=2)

    def outputs(n, carry):
        r0 = pl.multiple_of(n * WINDOW, WINDOW)
        rows = pl.ds(pl.multiple_of(n * stack, stack), stack)
        oe = jnp.dot(p_scr[rows, :], vp[pl.ds(r0, 3 * WINDOW), :], preferred_element_type=F32)
        o = oe[:, 0:ATT_D] / (oe[:, ATT_D:2 * ATT_D] + e_scr[rows, :])
        for r, hs in enumerate(heads):
            o_ref[pl.ds(r0, WINDOW), hs] = o[r * WINDOW:(r + 1) * WINDOW, :].astype(BF16)
        return carry

    lax.fori_loop(0, n_blocks, outputs, 0, unroll=4)


def _window_attention(proj3, q_gain, k_gain, sink, bias):
    b, seq, _ = proj3.shape
    gw = ATT_GROUP * ATT_D
    return pl.pallas_call(
        _window_kernel,
        out_shape=jax.ShapeDtypeStruct((b, seq, ATT_HEADS * ATT_D), BF16),
        grid=(b, ATT_KV),
        in_specs=[
            pl.BlockSpec(memory_space=pltpu.SMEM),
            pl.BlockSpec((None, seq, gw), lambda i, g: (i, 0, OFF_AQ // gw + g)),
            pl.BlockSpec((None, seq, ATT_D), lambda i, g: (i, 0, OFF_AK // ATT_D + g)),
            pl.BlockSpec((None, seq, ATT_D), lambda i, g: (i, 0, OFF_AV // ATT_D + g)),
            pl.BlockSpec((1, ATT_D), lambda i, g: (0, 0)),
            pl.BlockSpec((1, ATT_D), lambda i, g: (0, 0)),
            pl.BlockSpec((None, 3, ATT_GROUP * WINDOW, 3 * WINDOW), lambda i, g: (g, 0, 0, 0)),
        ],
        out_specs=pl.BlockSpec((None, seq, gw), lambda i, g: (i, 0, g)),
        scratch_shapes=[
            pltpu.VMEM((seq, gw), BF16),
            pltpu.VMEM((seq + 2 * WINDOW, ATT_D), BF16),
            pltpu.VMEM((seq + 2 * WINDOW, 2 * ATT_D), BF16),
            pltpu.VMEM((seq * ATT_GROUP, 3 * WINDOW), F32),
            pltpu.VMEM((seq * ATT_GROUP, 3 * WINDOW), BF16),
            pltpu.VMEM((seq * ATT_GROUP, WINDOW), F32),
        ],
        compiler_params=_cp(("parallel", "parallel")),
        name="window_attention",
    )(sink, proj3, proj3, proj3, q_gain, k_gain, bias)


def _window_bias(rel_table):
    nb = REL_BUCKETS // 2
    max_exact = nb // 2
    qi = jnp.arange(WINDOW)[:, None]
    ki = jnp.arange(3 * WINDOW)[None, :]
    rel = ki - WINDOW - qi
    ret = jnp.where(rel > 0, nb, 0)
    n = jnp.abs(rel)
    large = max_exact + (jnp.log(jnp.maximum(n, 1).astype(F32) / max_exact)
                         / math.log(REL_MAX_DIST / max_exact) * (nb - max_exact)).astype(I32)
    large = jnp.minimum(large, nb - 1)
    bucket = ret + jnp.where(n < max_exact, n, large)
    onehot = (bucket[None] == jnp.arange(REL_BUCKETS)[:, None, None]).astype(F32)
    bias = jnp.einsum("bh,bqk->hqk", rel_table.astype(F32), onehot, precision=lax.Precision.HIGHEST)
    bias = jnp.where((jnp.abs(rel) <= WINDOW)[None], bias * LOG2E, -jnp.inf)
    bias = bias.reshape(ATT_KV, ATT_GROUP * WINDOW, 3 * WINDOW)
    first = jnp.where(ki < WINDOW, -jnp.inf, bias)
    last = jnp.where(ki >= 2 * WINDOW, -jnp.inf, bias)
    return jnp.stack([first, bias, last], axis=1)


def _mem_kv_kernel(m_ref, g_ref, w_ref, kg_ref, o_ref):
    h = _rms(m_ref[...], g_ref[...]).astype(BF16)
    kw = MEM_HEADS * MEM_D
    for j in range(MEM_HEADS):
        cols = slice(j * MEM_D, (j + 1) * MEM_D)
        k = jnp.dot(h, w_ref[:, cols], preferred_element_type=F32)
        o_ref[:, cols] = _rms(k, kg_ref[...]).astype(BF16)
    o_ref[:, kw:] = jnp.dot(h, w_ref[:, kw:], preferred_element_type=F32).astype(BF16)


def _mem_kv(mem2, gain, w_bf16, k_gain, bm=512):
    rows = mem2.shape[0]
    bm = min(bm, rows)
    n = w_bf16.shape[1]
    return pl.pallas_call(
        _mem_kv_kernel,
        out_shape=jax.ShapeDtypeStruct((rows, n), BF16),
        grid=(rows // bm,),
        in_specs=[
            pl.BlockSpec((bm, D_MODEL), lambda i: (i, 0)),
            pl.BlockSpec((1, D_MODEL), lambda i: (0, 0)),
            pl.BlockSpec((D_MODEL, n), lambda i: (0, 0)),
            pl.BlockSpec((1, MEM_D), lambda i: (0, 0)),
        ],
        out_specs=pl.BlockSpec((bm, n), lambda i: (i, 0)),
        compiler_params=_cp(("parallel",)),
        name="mem_kv",
    )(mem2, gain, w_bf16, k_gain)


def _mem_attn_kernel(q_ref, k_ref, v_ref, qg_ref, o_ref, *, bq):
    seq = q_ref.shape[0]
    nt = (((1,), (1,)), ((), ()))

    def block(n, carry):
        rows = pl.ds(pl.multiple_of(n * bq, bq), bq)
        q = _rms(q_ref[rows, :].astype(F32), qg_ref[...]) * (MEM_D ** -0.5)
        s = lax.dot_general(q.astype(BF16), k_ref[...], nt, preferred_element_type=F32)
        p = jnp.exp(s - jnp.max(s, axis=-1, keepdims=True))
        denom = jnp.sum(p, axis=-1, keepdims=True)
        o = jnp.dot(p.astype(BF16), v_ref[...], preferred_element_type=F32) / denom
        o_ref[rows, :] = o.astype(BF16)
        return carry

    lax.fori_loop(0, seq // bq, block, 0, unroll=8)


def _mem_attention(proj3, mkv3, q_gain, bq=256):
    b, seq, _ = proj3.shape
    m_len = mkv3.shape[1]
    return pl.pallas_call(
        functools.partial(_mem_attn_kernel, bq=bq),
        out_shape=jax.ShapeDtypeStruct((b, seq, MEM_HEADS * MEM_D), BF16),
        grid=(b, MEM_HEADS),
        in_specs=[
            pl.BlockSpec((None, seq, MEM_D), lambda i, h: (i, 0, OFF_MQ // MEM_D + h)),
            pl.BlockSpec((None, m_len, MEM_D), lambda i, h: (i, 0, h)),
            pl.BlockSpec((None, m_len, MEM_D), lambda i, h: (i, 0, MEM_HEADS + h)),
            pl.BlockSpec((1, MEM_D), lambda i, h: (0, 0)),
        ],
        out_specs=pl.BlockSpec((None, seq, MEM_D), lambda i, h: (i, 0, h)),
        compiler_params=_cp(("parallel", "parallel")),
        name="mem_attention",
    )(proj3, mkv3, mkv3, q_gain)


def _merge_kernel(x_ref, g0_ref, g1_ref, g2_ref, b0_ref, b1_ref, b2_ref, wb_ref, wo_ref, gn_ref,
                  wr_ref, xo_ref, hp_ref, lg_ref, lg_scr):
    merged = None
    for gate_ref, br, i in ((g0_ref, b0_ref, 0), (g1_ref, b1_ref, 1), (g2_ref, b2_ref, 2)):
        gate = 0.5 * jnp.tanh(0.5 * gate_ref[...].astype(F32)) + 0.5
        t = gate * jnp.dot(br[...], wb_ref[i], preferred_element_type=F32)
        merged = t if merged is None else merged + t
    x = x_ref[...] + jnp.dot(merged.astype(BF16), wo_ref[...], preferred_element_type=F32)
    xo_ref[...] = x
    h = _rms(x, gn_ref[...])
    h_hi = h.astype(BF16)
    h_lo = (h - h_hi.astype(F32)).astype(BF16)
    lg_scr[...] = jnp.dot(jnp.concatenate([h_hi, h_lo], axis=1), wr_ref[...], preferred_element_type=F32)
    lg_ref[...] = lg_scr[...].T
    hp_ref[...] = _pack_bf16_pairs(h)


def _merge(x2, proj, ret, att, mo, wb, wo, gain, w_router_cat, bm=512):
    t = x2.shape[0]
    const2 = lambda i: (0, 0)
    row = lambda i: (i, 0)
    return pl.pallas_call(
        _merge_kernel,
        out_shape=(
            jax.ShapeDtypeStruct((t, D_MODEL), F32),
            jax.ShapeDtypeStruct((t, HALF), I32),
            jax.ShapeDtypeStruct((ROUTER_LANES, t), F32),
        ),
        grid=(t // bm,),
        in_specs=[
            pl.BlockSpec((bm, D_MODEL), row),
            pl.BlockSpec((bm, D_MODEL), lambda i: (i, 0)),
            pl.BlockSpec((bm, D_MODEL), lambda i: (i, 1)),
            pl.BlockSpec((bm, D_MODEL), lambda i: (i, 2)),
            pl.BlockSpec((bm, D_MODEL), row),
            pl.BlockSpec((bm, D_MODEL), row),
            pl.BlockSpec((bm, D_MODEL), row),
            pl.BlockSpec((3, D_MODEL, D_MODEL), lambda i: (0, 0, 0)),
            pl.BlockSpec((D_MODEL, D_MODEL), const2),
            pl.BlockSpec((1, D_MODEL), const2),
            pl.BlockSpec((2 * D_MODEL, ROUTER_LANES), const2),
        ],
        out_specs=(
            pl.BlockSpec((bm, D_MODEL), row),
            pl.BlockSpec((bm, HALF), row),
            pl.BlockSpec((ROUTER_LANES, bm), lambda i: (0, i)),
        ),
        scratch_shapes=[pltpu.VMEM((bm, ROUTER_LANES), F32)],
        compiler_params=_cp(("parallel",)),
        name="merge_router",
    )(x2, proj, proj, proj, ret, att, mo, wb, wo, gain, w_router_cat)


def _route_kernel(l_ref, b_ref, tri_ref, eid_ref, rank_ref, gate_ref, cnt_ref):
    @pl.when(pl.program_id(0) == 0)
    def _():
        cnt_ref[...] = jnp.zeros_like(cnt_ref)

    l = l_ref[0:N_EXPERTS, :] + l_ref[N_EXPERTS:2 * N_EXPERTS, :] + b_ref[...]
    ne, bt = l.shape
    iota_e = lax.broadcasted_iota(I32, (ne, bt), 0)
    picked = jnp.zeros((ne, bt), jnp.bool_)
    vals, idxs = [], []
    for _ in range(TOP_K):
        m = jnp.max(l, axis=0, keepdims=True)
        idx = jnp.min(jnp.where(l == m, iota_e, ne), axis=0, keepdims=True)
        sel = iota_e == idx
        picked = picked | sel
        l = jnp.where(sel, -jnp.inf, l)
        vals.append(m)
        idxs.append(idx)
    ex = [jnp.exp(v - vals[0]) for v in vals]
    tot = ex[0] + ex[1] + ex[2] + ex[3]
    onehot = jnp.where(picked, 1.0, 0.0)
    before = jnp.dot(onehot.astype(BF16), tri_ref[...], preferred_element_type=F32) + cnt_ref[:, 0:1]
    for k in range(TOP_K):
        eid_ref[k:k + 1, :] = idxs[k]
        gate_ref[k:k + 1, :] = ex[k] / tot
        rank_ref[k:k + 1, :] = jnp.sum(jnp.where(iota_e == idxs[k], before, 0.0), axis=0,
                                       keepdims=True).astype(I32)
    cnt_ref[...] = cnt_ref[...] + jnp.sum(onehot, axis=1, keepdims=True)


def _route(logits_t, bias_col, tri):
    bt = tri.shape[0]
    ne, t = N_EXPERTS, logits_t.shape[1]
    blk = lambda i: (0, i)
    return pl.pallas_call(
        _route_kernel,
        out_shape=(
            jax.ShapeDtypeStruct((TOP_K, t), I32),
            jax.ShapeDtypeStruct((TOP_K, t), I32),
            jax.ShapeDtypeStruct((TOP_K, t), F32),
            jax.ShapeDtypeStruct((ne, 128), F32),
        ),
        grid=(t // bt,),
        in_specs=[pl.BlockSpec((ROUTER_LANES, bt), blk), pl.BlockSpec((ne, 1), lambda i: (0, 0)),
                  pl.BlockSpec((bt, bt), lambda i: (0, 0))],
        out_specs=(
            pl.BlockSpec((TOP_K, bt), blk),
            pl.BlockSpec((TOP_K, bt), blk),
            pl.BlockSpec((TOP_K, bt), blk),
            pl.BlockSpec((ne, 128), lambda i: (0, 0)),
        ),
        compiler_params=_cp(("arbitrary",)),
        name="route_topk",
    )(logits_t, bias_col, tri)


def _slot_kernel(start_ref, eid_ref, rank_ref, slot_ref):
    eid = eid_ref[...]
    base = jnp.zeros(eid.shape, I32)
    for e in range(N_EXPERTS):
        base = jnp.where(eid == e, start_ref[e], base)
    slot_ref[...] = base + rank_ref[...]


def _slots(pad_starts, eid, rank, bt=2048):
    t = eid.shape[1]
    blk = lambda i, s: (0, i)
    return pl.pallas_call(
        _slot_kernel,
        out_shape=jax.ShapeDtypeStruct((TOP_K, t), I32),
        grid_spec=pltpu.PrefetchScalarGridSpec(
            num_scalar_prefetch=1,
            grid=(t // bt,),
            in_specs=[pl.BlockSpec((TOP_K, bt), blk), pl.BlockSpec((TOP_K, bt), blk)],
            out_specs=pl.BlockSpec((TOP_K, bt), blk),
        ),
        compiler_params=_cp(("parallel",)),
        name="route_slots",
    )(pad_starts, eid, rank)


def _sc_workers():
    info = plsc.get_sparse_core_info()
    return info.num_cores, info.num_subcores


def _dispatch_rows(hp, slot, n_rows, win=64):
    t, w = hp.shape
    nc, ns = _sc_workers()
    per_worker = t // (nc * ns)
    mesh = plsc.VectorSubcoreMesh(core_axis_name="c", subcore_axis_name="s")

    @functools.partial(
        pl.kernel,
        out_type=jax.ShapeDtypeStruct((n_rows, w), hp.dtype),
        mesh=mesh,
        scratch_types=[pltpu.VMEM((win,), I32)] * TOP_K + [pltpu.VMEM((win, w), hp.dtype)],
        name="moe_dispatch",
    )
    def k(hp_hbm, slot_hbm, buf_hbm, *scratch):
        idx_v, rows_v = scratch[:TOP_K], scratch[TOP_K]
        wid = lax.axis_index("s") * nc + lax.axis_index("c")
        base = wid * per_worker

        @pl.loop(0, per_worker // win)
        def _(i):
            off = pl.multiple_of(base + i * win, win)
            pltpu.sync_copy(hp_hbm.at[pl.ds(off, win)], rows_v)
            for kk in range(TOP_K):
                pltpu.sync_copy(slot_hbm.at[pl.ds(kk * t + off, win)], idx_v[kk])
                pltpu.sync_copy(rows_v, buf_hbm.at[idx_v[kk]])

    return k(hp, slot.reshape(-1))


def _gather_rows(out_rows, slot, win=64):
    t = slot.shape[1]
    w = out_rows.shape[1]
    nc, ns = _sc_workers()
    per_worker = t // (nc * ns)
    mesh = plsc.VectorSubcoreMesh(core_axis_name="c", subcore_axis_name="s")

    @functools.partial(
        pl.kernel,
        out_type=jax.ShapeDtypeStruct((TOP_K * t, w), out_rows.dtype),
        mesh=mesh,
        scratch_types=[pltpu.VMEM((win,), I32), pltpu.VMEM((win, w), out_rows.dtype)],
        name="moe_gather",
    )
    def k(rows_hbm, slot_hbm, og_hbm, idx_v, rows_v):
        wid = lax.axis_index("s") * nc + lax.axis_index("c")
        base = wid * per_worker

        @pl.loop(0, per_worker // win)
        def _(i):
            off = pl.multiple_of(base + i * win, win)
            for kk in range(TOP_K):
                pltpu.sync_copy(slot_hbm.at[pl.ds(kk * t + off, win)], idx_v)
                pltpu.sync_copy(rows_hbm.at[idx_v], rows_v)
                pltpu.sync_copy(rows_v, og_hbm.at[pl.ds(kk * t + off, win)])

    return k(out_rows, slot.reshape(-1)).reshape(TOP_K, t, w)


def _expert_kernel(be_ref, nu_ref, x_ref, wi_ref, bi_ref, wo_ref, bo_ref, o_ref, wi_bf, wo_bf, act_scr):
    i = pl.program_id(0)
    n_used = nu_ref[0]
    last = be_ref.shape[0] - 1
    e_up = be_ref[jnp.minimum(i, last)]
    e_up_prev = be_ref[jnp.clip(i - 1, 0, last)]
    e_down_prev = be_ref[jnp.clip(i - 2, 0, last)]
    up = i < n_used
    down = jnp.logical_and(i >= 1, i <= n_used)
    rows = 128

    @pl.when(jnp.logical_and(up, jnp.logical_or(i == 0, e_up != e_up_prev)))
    def _():
        def cast(c, carry):
            sl = pl.ds(pl.multiple_of(c * rows, rows), rows)
            wi_bf[sl, :] = wi_ref[sl, :].astype(BF16)
            return carry

        lax.fori_loop(0, D_MODEL // rows, cast, 0)

    @pl.when(jnp.logical_and(down, jnp.logical_or(i == 1, e_up_prev != e_down_prev)))
    def _():
        def cast(c, carry):
            sl = pl.ds(pl.multiple_of(c * rows, rows), rows)
            wo_bf[sl, :] = wo_ref[sl, :].astype(BF16)
            return carry

        lax.fori_loop(0, D_FF // rows, cast, 0)

    def up_half():
        hi, lo = _unpack_bf16_pairs(x_ref[...])
        x = jnp.concatenate([hi.astype(BF16), lo.astype(BF16)], axis=1)
        h = jnp.dot(x, wi_bf[...], preferred_element_type=F32) + bi_ref[...]
        gate = jnp.minimum(h[:, :D_FF], SWIGLU_LIMIT)
        lin = jnp.clip(h[:, D_FF:], -SWIGLU_LIMIT, SWIGLU_LIMIT)
        act = gate * (0.5 * jnp.tanh((0.5 * SWIGLU_ALPHA) * gate) + 0.5) * (lin + 1.0)
        act_scr[i % 2] = act.astype(BF16)

    def down_half():
        o = jnp.dot(act_scr[(i + 1) % 2], wo_bf[...], preferred_element_type=F32) + bo_ref[...]
        o_ref[...] = _pack_bf16_pairs(o)

    @pl.when(jnp.logical_and(up, down))
    def _():
        down_half()
        up_half()

    @pl.when(jnp.logical_and(up, jnp.logical_not(down)))
    def _():
        up_half()

    @pl.when(jnp.logical_and(down, jnp.logical_not(up)))
    def _():
        down_half()


def _experts(block_expert, n_used, buf, w_in, b_in, w_out, b_out, blk):
    n_rows, w = buf.shape
    n_blocks = n_rows // blk
    last = n_blocks - 1
    up_blk = lambda i, be, nu: (jnp.minimum(i, last), 0)
    up_exp = lambda i, be, nu: (be[jnp.minimum(i, last)], 0, 0)
    down_blk = lambda i, be, nu: (jnp.maximum(i - 1, 0), 0)
    down_exp = lambda i, be, nu: (be[jnp.maximum(i - 1, 0)], 0, 0)
    return pl.pallas_call(
        _expert_kernel,
        out_shape=jax.ShapeDtypeStruct((n_rows, w), buf.dtype),
        grid_spec=pltpu.PrefetchScalarGridSpec(
            num_scalar_prefetch=2,
            grid=(n_blocks + 1,),
            in_specs=[
                pl.BlockSpec((blk, w), up_blk),
                pl.BlockSpec((None, D_MODEL, 2 * D_FF), up_exp),
                pl.BlockSpec((None, 1, 2 * D_FF), up_exp),
                pl.BlockSpec((None, D_FF, D_MODEL), down_exp),
                pl.BlockSpec((None, 1, D_MODEL), down_exp),
            ],
            out_specs=pl.BlockSpec((blk, w), down_blk),
            scratch_shapes=[pltpu.VMEM((D_MODEL, 2 * D_FF), BF16), pltpu.VMEM((D_FF, D_MODEL), BF16),
                            pltpu.VMEM((2, blk, D_FF), BF16)],
        ),
        compiler_params=_cp(("arbitrary",)),
        name="expert_ffn",
    )(block_expert, n_used, buf, w_in, b_in, w_out, b_out)


def _combine_kernel(x_ref, og_ref, g_ref, y_ref):
    g = g_ref[...]
    acc_hi = x_ref[:, :HALF]
    acc_lo = x_ref[:, HALF:]
    for k in range(TOP_K):
        hi, lo = _unpack_bf16_pairs(og_ref[k])
        gk = g[:, k:k + 1]
        acc_hi = acc_hi + gk * hi
        acc_lo = acc_lo + gk * lo
    y_ref[:, :HALF] = acc_hi
    y_ref[:, HALF:] = acc_lo


def _combine(x_mid, og, gates_tk, bm=1024):
    t = x_mid.shape[0]
    return pl.pallas_call(
        _combine_kernel,
        out_shape=jax.ShapeDtypeStruct((t, D_MODEL), F32),
        grid=(t // bm,),
        in_specs=[
            pl.BlockSpec((bm, D_MODEL), lambda i: (i, 0)),
            pl.BlockSpec((TOP_K, bm, HALF), lambda i: (0, i, 0)),
            pl.BlockSpec((bm, TOP_K), lambda i: (i, 0)),
        ],
        out_specs=pl.BlockSpec((bm, D_MODEL), lambda i: (i, 0)),
        compiler_params=_cp(("parallel",)),
        name="moe_combine",
    )(x_mid, og, gates_tk)


def _block_tables(counts, n_blocks, blk):
    padded = (counts + blk - 1) // blk * blk
    pad_ends = jnp.cumsum(padded)
    pad_starts = pad_ends - padded
    block_start = jnp.arange(n_blocks, dtype=I32) * blk
    block_expert = jnp.minimum(jnp.sum(pad_ends[None, :] <= block_start[:, None], axis=1), N_EXPERTS - 1)
    n_used = (pad_ends[-1] // blk).reshape(1)
    return pad_starts.astype(I32), block_expert.astype(I32), n_used.astype(I32)


def _layer(x, mem, consts, params, after=None):
    b, seq, d = x.shape
    m_len = mem.shape[1]
    t = b * seq
    x2 = x.reshape(t, d)
    gain = params["norm_mix"]
    if after is not None:
        gain = gain + 0.0 * after.astype(F32)
    proj = _in_proj(x2, gain, params["w_in"])
    proj3 = proj.reshape(b, seq, IN_WIDTH)
    ret = _retention(proj3, params["ret_out_norm"], consts["cos"], consts["sin"], consts["tabs"],
                     consts["decays"])
    att = _window_attention(proj3, params["att_q_norm"], params["att_k_norm"], params["att_sink"],
                            consts["bias"])
    mkv = _mem_kv(mem.reshape(b * m_len, d), params["mem_norm"], params["w_mem_kv"],
                  params["mem_k_norm"])
    mo = _mem_attention(proj3, mkv.reshape(b, m_len, 2 * MEM_HEADS * MEM_D), params["mem_q_norm"])
    x_mid, hp, logits_t = _merge(x2, proj, ret.reshape(t, d), att.reshape(t, d), mo.reshape(t, d),
                               params["w_branch"], params["w_out"], params["norm_ffn"],
                               params["w_router"])
    eid, rank, gates, counts = _route(logits_t, params["b_router"], consts["tri"])
    blk = EXPERT_BLOCK_LARGE if t * TOP_K // N_EXPERTS >= LARGE_BLOCK_MIN_ROWS else EXPERT_BLOCK
    n_blocks = t * TOP_K // blk + N_EXPERTS
    pad_starts, block_expert, n_used = _block_tables(counts[:, 0].astype(I32), n_blocks, blk)
    slot = _slots(pad_starts, eid, rank)
    buf = _dispatch_rows(hp, slot, n_blocks * blk)
    out_rows = _experts(block_expert, n_used, buf, params["w_e_in"], params["b_e_in"],
                        params["w_e_out"], params["b_e_out"], blk)
    og = _gather_rows(out_rows, slot)
    y = _combine(x_mid, og, gates.T)
    return y.reshape(b, seq, d), slot[0, 0]


def _router_weights(w):
    w_hi = w.astype(BF16)
    w_lo = (w - w_hi.astype(F32)).astype(BF16)
    pad = jnp.zeros((D_MODEL, ROUTER_LANES - 2 * N_EXPERTS), BF16)
    top = jnp.concatenate([w_hi, w_lo, pad], axis=1)
    bottom = jnp.concatenate([w_hi, jnp.zeros_like(w_lo), pad], axis=1)
    return jnp.concatenate([top, bottom], axis=0)


def _prepare(seq, rel_table, norm_mix, w_in, ret_out_norm, att_q_norm, att_k_norm, att_sink, mem_norm,
             w_mem_kv, mem_q_norm, mem_k_norm, w_branch, w_out, norm_ffn, w_router, b_router,
             w_e_in, b_e_in, w_e_out, b_e_out):
    gates_at = IN_WIDTH - 3 * D_MODEL
    w_in0 = w_in[0]
    params = {
        "norm_mix": norm_mix[0].reshape(1, -1),
        "w_in": jnp.concatenate([w_in0[:, gates_at:], w_in0[:, :gates_at]], axis=1).astype(BF16),
        "ret_out_norm": ret_out_norm[0].reshape(1, -1),
        "att_q_norm": att_q_norm[0].reshape(1, -1),
        "att_k_norm": att_k_norm[0].reshape(1, -1),
        "att_sink": att_sink[0].astype(F32),
        "mem_norm": mem_norm[0].reshape(1, -1),
        "w_mem_kv": w_mem_kv[0].astype(BF16),
        "mem_q_norm": mem_q_norm[0].reshape(1, -1),
        "mem_k_norm": mem_k_norm[0].reshape(1, -1),
        "w_branch": w_branch[0].astype(BF16),
        "w_out": w_out[0].astype(BF16),
        "norm_ffn": norm_ffn[0].reshape(1, -1),
        "w_router": _router_weights(w_router[0]),
        "b_router": b_router[0].reshape(-1, 1),
        "w_e_in": w_e_in[0],
        "b_e_in": b_e_in[0].reshape(N_EXPERTS, 1, -1),
        "w_e_out": w_e_out[0],
        "b_e_out": b_e_out[0].reshape(N_EXPERTS, 1, -1),
    }
    cos, sin, tabs, decays = _retention_tables(seq)
    route_bt = 1024
    tri = (jnp.arange(route_bt)[:, None] < jnp.arange(route_bt)[None, :]).astype(BF16)
    consts = {"cos": cos, "sin": sin, "tabs": tabs, "decays": decays,
              "bias": _window_bias(rel_table), "tri": tri}
    return consts, params


def kernel(x_prompt, x_sample, mem_prompt, mem_sample, rel_table, norm_mix, w_in, ret_out_norm, att_q_norm, att_k_norm, att_sink, mem_norm, w_mem_kv, mem_q_norm, mem_k_norm, w_branch, w_out, norm_ffn, w_router, b_router, w_e_in, b_e_in, w_e_out, b_e_out):
    consts, params = _prepare(x_prompt.shape[1], rel_table, norm_mix, w_in, ret_out_norm, att_q_norm,
                              att_k_norm, att_sink, mem_norm, w_mem_kv, mem_q_norm, mem_k_norm,
                              w_branch, w_out, norm_ffn, w_router, b_router, w_e_in, b_e_in,
                              w_e_out, b_e_out)
    y_sample, first_slot = _layer(x_sample, mem_sample, consts, params)
    y_prompt, _ = _layer(x_prompt, mem_prompt, consts, params, after=first_slot)
    return (y_prompt, y_sample)
```

```python
import functools
import math

import jax
import jax.numpy as jnp
import numpy as np
from jax import lax
from jax.experimental import pallas as pl
from jax.experimental.pallas import tpu as pltpu
from jax.experimental.pallas import tpu_sc as plsc

F32 = jnp.float32
BF16 = jnp.bfloat16
I32 = jnp.int32

D_MODEL = 1024
EPS = 1e-6
RET_HEADS = 4
RET_QK = 128
RET_V = 256
CHUNK = 128
ROPE_BASE = 10000.0
DECAY_FWD = 5.0
DECAY_BWD = 5.5
ATT_HEADS = 8
ATT_KV = 2
ATT_GROUP = ATT_HEADS // ATT_KV
ATT_D = 128
WINDOW = 128
REL_BUCKETS = 32
REL_MAX_DIST = 128
MEM_HEADS = 4
MEM_D = 256
N_EXPERTS = 32
TOP_K = 4
D_FF = 1024
SWIGLU_LIMIT = 7.0
SWIGLU_ALPHA = 1.702
EXPERT_BLOCK = 512
HALF = D_MODEL // 2
ROUTER_LANES = 128

IN_WIDTH = 8704
OFF_GATES = 0
OFF_RQ = 3072
OFF_RK = 3584
OFF_RV = 4096
OFF_RG = 5120
OFF_AQ = 6144
OFF_AK = 7168
OFF_AV = 7424
OFF_MQ = 7680

VMEM_LIMIT = 56 * 1024 * 1024
LOG2E = 1.4426950408889634


def _cp(sem, vmem=VMEM_LIMIT):
    return pltpu.CompilerParams(dimension_semantics=sem, vmem_limit_bytes=vmem)


def _rms(x, g):
    return x * lax.rsqrt(jnp.mean(x * x, axis=-1, keepdims=True) + EPS) * g


def _pack_bf16_pairs(x):
    w = x.shape[1] // 2
    bits = pltpu.bitcast(x.astype(BF16).astype(F32), I32)
    hi = bits[:, :w] & jnp.int32(-65536)
    lo = lax.shift_right_logical(bits[:, w:], jnp.int32(16))
    return hi | lo


def _unpack_bf16_pairs(u):
    hi = pltpu.bitcast(u & jnp.int32(-65536), F32)
    lo = pltpu.bitcast(lax.shift_left(u, jnp.int32(16)), F32)
    return hi, lo


def _in_proj_kernel(x_ref, g_ref, w_ref, o_ref, h_scr):
    @pl.when(pl.program_id(1) == 0)
    def _():
        h_scr[...] = _rms(x_ref[...], g_ref[...]).astype(BF16)

    sub = 256

    def cols(c, carry):
        sl = pl.ds(pl.multiple_of(c * sub, sub), sub)
        o_ref[:, sl] = jnp.dot(h_scr[...], w_ref[:, sl], preferred_element_type=F32).astype(BF16)
        return carry

    lax.fori_loop(0, o_ref.shape[1] // sub, cols, 0, unroll=True)


def _in_proj(x2, gain, w_bf16, bm=1024, bn=4352):
    t = x2.shape[0]
    n = w_bf16.shape[1]
    return pl.pallas_call(
        _in_proj_kernel,
        out_shape=jax.ShapeDtypeStruct((t, n), BF16),
        grid=(t // bm, n // bn),
        in_specs=[
            pl.BlockSpec((bm, D_MODEL), lambda i, j: (i, 0)),
            pl.BlockSpec((1, D_MODEL), lambda i, j: (0, 0)),
            pl.BlockSpec((D_MODEL, bn), lambda i, j: (0, j)),
        ],
        out_specs=pl.BlockSpec((bm, bn), lambda i, j: (i, j)),
        scratch_shapes=[pltpu.VMEM((bm, D_MODEL), BF16)],
        compiler_params=_cp(("parallel", "arbitrary")),
        name="in_proj",
    )(x2, gain, w_bf16)


def _retention_kernel(dec_ref, q_ref, k_ref, v_ref, g_ref, cos_ref, sin_ref, tab_ref, gain_ref,
                      o_ref, lhs, rhs, acc, kvf, kvb):
    h = pl.program_id(1)
    seq = q_ref.shape[0]
    n_chunks = seq // CHUNK
    dec_f = dec_ref[2 * h]
    dec_b = dec_ref[2 * h + 1]
    nt = (((1,), (1,)), ((), ()))

    def rope(x, c, s):
        return x * c + pltpu.roll(x, RET_QK // 2, 1) * s

    def intra(n, carry):
        rows = pl.ds(pl.multiple_of(n * CHUNK, CHUNK), CHUNK)
        c = cos_ref[rows, :]
        s = sin_ref[rows, :]
        q = rope(q_ref[rows, :].astype(F32), c, s)
        k = rope(k_ref[rows, :].astype(F32), c, s) * (RET_QK ** -0.5)
        lhs[rows, 0:CHUNK] = (q * tab_ref[1]).astype(BF16)
        lhs[rows, CHUNK:2 * CHUNK] = (q * tab_ref[3]).astype(BF16)
        sc = lax.dot_general(q.astype(BF16), k.astype(BF16), nt, preferred_element_type=F32)
        a = jnp.concatenate([(sc * tab_ref[0]).astype(BF16),
                             (k * tab_ref[2]).T.astype(BF16),
                             (k * tab_ref[4]).T.astype(BF16)], axis=0)
        r = jnp.dot(a, v_ref[rows, :], preferred_element_type=F32)
        acc[rows, :] = r[0:CHUNK]
        kvf[n] = r[CHUNK:2 * CHUNK]
        kvb[n] = r[2 * CHUNK:3 * CHUNK]
        return carry

    lax.fori_loop(0, n_chunks, intra, 0, unroll=8)

    def scan_f(n, state):
        rhs[n, 0:CHUNK, :] = state.astype(BF16)
        return state * dec_f + kvf[n]

    lax.fori_loop(0, n_chunks, scan_f, jnp.zeros((RET_QK, RET_V), F32))

    def scan_b(i, state):
        n = n_chunks - 1 - i
        rhs[n, CHUNK:2 * CHUNK, :] = state.astype(BF16)
        return state * dec_b + kvb[n]

    lax.fori_loop(0, n_chunks, scan_b, jnp.zeros((RET_QK, RET_V), F32))

    def cross(n, carry):
        rows = pl.ds(pl.multiple_of(n * CHUNK, CHUNK), CHUNK)
        o = acc[rows, :] + jnp.dot(lhs[rows, :], rhs[n], preferred_element_type=F32)
        y = _rms(o, gain_ref[...])
        g = g_ref[rows, :].astype(F32)
        o_ref[rows, :] = (y * (g * (0.5 * jnp.tanh(0.5 * g) + 0.5))).astype(BF16)
        return carry

    lax.fori_loop(0, n_chunks, cross, 0, unroll=True)


def _retention(proj3, gain, cos, sin, tabs, decays):
    b, seq, _ = proj3.shape
    return pl.pallas_call(
        _retention_kernel,
        out_shape=jax.ShapeDtypeStruct((b, seq, RET_HEADS * RET_V), BF16),
        grid=(b, RET_HEADS),
        in_specs=[
            pl.BlockSpec(memory_space=pltpu.SMEM),
            pl.BlockSpec((None, seq, RET_QK), lambda i, h: (i, 0, OFF_RQ // RET_QK + h)),
            pl.BlockSpec((None, seq, RET_QK), lambda i, h: (i, 0, OFF_RK // RET_QK + h)),
            pl.BlockSpec((None, seq, RET_V), lambda i, h: (i, 0, OFF_RV // RET_V + h)),
            pl.BlockSpec((None, seq, RET_V), lambda i, h: (i, 0, OFF_RG // RET_V + h)),
            pl.BlockSpec((seq, RET_QK), lambda i, h: (0, 0)),
            pl.BlockSpec((seq, RET_QK), lambda i, h: (0, 0)),
            pl.BlockSpec((None, 5, CHUNK, CHUNK), lambda i, h: (h, 0, 0, 0)),
            pl.BlockSpec((1, RET_V), lambda i, h: (0, h)),
        ],
        out_specs=pl.BlockSpec((None, seq, RET_V), lambda i, h: (i, 0, h)),
        scratch_shapes=[
            pltpu.VMEM((seq, 2 * CHUNK), BF16),
            pltpu.VMEM((seq // CHUNK, 2 * CHUNK, RET_V), BF16),
            pltpu.VMEM((seq, RET_V), F32),
            pltpu.VMEM((seq // CHUNK, RET_QK, RET_V), F32),
            pltpu.VMEM((seq // CHUNK, RET_QK, RET_V), F32),
        ],
        compiler_params=_cp(("parallel", "parallel")),
        name="retention",
    )(decays, proj3, proj3, proj3, proj3, cos, sin, tabs, gain)


def _retention_tables(seq):
    half = RET_QK // 2
    inv = ROPE_BASE ** (-jnp.arange(half, dtype=F32) / half)
    ang = jnp.arange(seq, dtype=F32)[:, None] * inv[None, :]
    cos = jnp.concatenate([jnp.cos(ang), jnp.cos(ang)], axis=1)
    sin = jnp.concatenate([-jnp.sin(ang), jnp.sin(ang)], axis=1)
    heads = jnp.arange(RET_HEADS, dtype=F32)
    lg_f = jnp.log1p(-jnp.exp2(-DECAY_FWD - heads))[:, None, None]
    lg_b = jnp.log1p(-jnp.exp2(-DECAY_BWD - heads))[:, None, None]
    idx = jnp.arange(CHUNK, dtype=F32)
    diff = (idx[:, None] - idx[None, :])[None]
    dmat = jnp.where(diff >= 0, jnp.exp(jnp.where(diff >= 0, diff, 0.0) * lg_f),
                     jnp.exp(jnp.where(diff < 0, -diff, 0.0) * lg_b))
    col = jnp.broadcast_to(idx[None, :, None], (RET_HEADS, CHUNK, CHUNK))
    xi_f = jnp.exp((col + 1.0) * lg_f)
    zeta_f = jnp.exp((CHUNK - 1.0 - col) * lg_f)
    xi_b = jnp.exp((CHUNK - col) * lg_b)
    zeta_b = jnp.exp(col * lg_b)
    tabs = jnp.stack([dmat, xi_f, zeta_f, xi_b, zeta_b], axis=1)
    decays = jnp.stack([jnp.exp(CHUNK * lg_f[:, 0, 0]), jnp.exp(CHUNK * lg_b[:, 0, 0])], axis=1).reshape(-1)
    return cos, sin, tabs, decays


def _window_kernel(sink_ref, q_ref, k_ref, v_ref, qg_ref, kg_ref, bias_ref, o_ref,
                   qn, kp, vp, s_scr, p_scr, e_scr):
    g = pl.program_id(1)
    seq = q_ref.shape[0]
    n_blocks = seq // WINDOW
    stack = ATT_GROUP * WINDOW
    unit = 64
    nt = (((1,), (1,)), ((), ()))
    heads = [slice(r * ATT_D, (r + 1) * ATT_D) for r in range(ATT_GROUP)]
    kp[0:WINDOW, :] = jnp.zeros((WINDOW, ATT_D), BF16)
    kp[seq + WINDOW:seq + 2 * WINDOW, :] = jnp.zeros((WINDOW, ATT_D), BF16)
    vp[0:WINDOW, :] = jnp.zeros((WINDOW, 2 * ATT_D), BF16)
    vp[seq + WINDOW:seq + 2 * WINDOW, :] = jnp.zeros((WINDOW, 2 * ATT_D), BF16)

    def start(n, size):
        return n * size if isinstance(n, int) else pl.multiple_of(n * size, size)

    def prep(n):
        r0 = start(n, WINDOW)
        src = pl.ds(r0, WINDOW)
        dst = pl.ds(r0 + WINDOW, WINDOW)
        kp[dst, :] = _rms(k_ref[src, :].astype(F32), kg_ref[...]).astype(BF16)
        vp[dst, 0:ATT_D] = v_ref[src, :]
        vp[dst, ATT_D:2 * ATT_D] = jnp.ones((WINDOW, ATT_D), BF16)
        for hs in heads:
            q = _rms(q_ref[src, hs].astype(F32), qg_ref[...]) * (ATT_D ** -0.5 * LOG2E)
            qn[src, hs] = q.astype(BF16)

    def scores(n):
        r0 = start(n, WINDOW)
        kb = kp[pl.ds(r0, 3 * WINDOW), :]
        q4 = jnp.concatenate([qn[pl.ds(r0, WINDOW), hs] for hs in heads], axis=0)
        if isinstance(n, int):
            edge = 0 if n == 0 else (2 if n == n_blocks - 1 else 1)
        else:
            edge = jnp.where(n == 0, 0, jnp.where(n == n_blocks - 1, 2, 1))
        s_scr[pl.ds(start(n, stack), stack), :] = (
            lax.dot_general(q4, kb, nt, preferred_element_type=F32) + bias_ref[edge])

    def softmax(n):
        base = start(n, stack)
        for u in range(stack // unit):
            rows = pl.ds(base + u * unit, unit)
            sink = sink_ref[g * ATT_GROUP + (u * unit) // WINDOW] * LOG2E
            s = s_scr[rows, :]
            m = jnp.maximum(jnp.max(s, axis=-1, keepdims=True), sink)
            mb = jnp.broadcast_to(m, (unit, WINDOW))
            p_scr[rows, :] = jnp.exp2(s - jnp.concatenate([mb, mb, mb], axis=1)).astype(BF16)
            e_scr[rows, :] = jnp.exp2(sink - mb)

    def outputs(n):
        r0 = start(n, WINDOW)
        rows = pl.ds(start(n, stack), stack)
        oe = jnp.dot(p_scr[rows, :], vp[pl.ds(r0, 3 * WINDOW), :], preferred_element_type=F32)
        o = oe[:, 0:ATT_D] / (oe[:, ATT_D:2 * ATT_D] + e_scr[rows, :])
        for r, hs in enumerate(heads):
            o_ref[pl.ds(r0, WINDOW), hs] = o[r * WINDOW:(r + 1) * WINDOW, :].astype(BF16)

    prep(0)
    prep(1)
    scores(0)
    prep(2)
    softmax(0)
    scores(1)
    prep(3)

    def steady(n, carry):
        outputs(n - 4)
        softmax(n - 3)
        scores(n - 2)
        prep(n)
        return carry

    lax.fori_loop(4, n_blocks, steady, 0, unroll=2)
    last = n_blocks - 1
    outputs(last - 3)
    softmax(last - 2)
    scores(last - 1)
    outputs(last - 2)
    softmax(last - 1)
    scores(last)
    outputs(last - 1)
    softmax(last)
    outputs(last)


def _window_attention(proj3, q_gain, k_gain, sink, bias):
    b, seq, _ = proj3.shape
    gw = ATT_GROUP * ATT_D
    return pl.pallas_call(
        _window_kernel,
        out_shape=jax.ShapeDtypeStruct((b, seq, ATT_HEADS * ATT_D), BF16),
        grid=(b, ATT_KV),
        in_specs=[
            pl.BlockSpec(memory_space=pltpu.SMEM),
            pl.BlockSpec((None, seq, gw), lambda i, g: (i, 0, OFF_AQ // gw + g)),
            pl.BlockSpec((None, seq, ATT_D), lambda i, g: (i, 0, OFF_AK // ATT_D + g)),
            pl.BlockSpec((None, seq, ATT_D), lambda i, g: (i, 0, OFF_AV // ATT_D + g)),
            pl.BlockSpec((1, ATT_D), lambda i, g: (0, 0)),
            pl.BlockSpec((1, ATT_D), lambda i, g: (0, 0)),
            pl.BlockSpec((None, 3, ATT_GROUP * WINDOW, 3 * WINDOW), lambda i, g: (g, 0, 0, 0)),
        ],
        out_specs=pl.BlockSpec((None, seq, gw), lambda i, g: (i, 0, g)),
        scratch_shapes=[
            pltpu.VMEM((seq, gw), BF16),
            pltpu.VMEM((seq + 2 * WINDOW, ATT_D), BF16),
            pltpu.VMEM((seq + 2 * WINDOW, 2 * ATT_D), BF16),
            pltpu.VMEM((seq * ATT_GROUP, 3 * WINDOW), F32),
            pltpu.VMEM((seq * ATT_GROUP, 3 * WINDOW), BF16),
            pltpu.VMEM((seq * ATT_GROUP, WINDOW), F32),
        ],
        compiler_params=_cp(("parallel", "parallel")),
        name="window_attention",
    )(sink, proj3, proj3, proj3, q_gain, k_gain, bias)


def _window_bias(rel_table):
    nb = REL_BUCKETS // 2
    max_exact = nb // 2
    qi = jnp.arange(WINDOW)[:, None]
    ki = jnp.arange(3 * WINDOW)[None, :]
    rel = ki - WINDOW - qi
    ret = jnp.where(rel > 0, nb, 0)
    n = jnp.abs(rel)
    large = max_exact + (jnp.log(jnp.maximum(n, 1).astype(F32) / max_exact)
                         / math.log(REL_MAX_DIST / max_exact) * (nb - max_exact)).astype(I32)
    large = jnp.minimum(large, nb - 1)
    bucket = ret + jnp.where(n < max_exact, n, large)
    onehot = (bucket[None] == jnp.arange(REL_BUCKETS)[:, None, None]).astype(F32)
    bias = jnp.einsum("bh,bqk->hqk", rel_table.astype(F32), onehot, precision=lax.Precision.HIGHEST)
    bias = jnp.where((jnp.abs(rel) <= WINDOW)[None], bias * LOG2E, -jnp.inf)
    bias = bias.reshape(ATT_KV, ATT_GROUP * WINDOW, 3 * WINDOW)
    first = jnp.where(ki < WINDOW, -jnp.inf, bias)
    last = jnp.where(ki >= 2 * WINDOW, -jnp.inf, bias)
    return jnp.stack([first, bias, last], axis=1)


def _mem_kv_kernel(m_ref, g_ref, w_ref, kg_ref, o_ref):
    h = _rms(m_ref[...], g_ref[...]).astype(BF16)
    kw = MEM_HEADS * MEM_D
    for j in range(MEM_HEADS):
        cols = slice(j * MEM_D, (j + 1) * MEM_D)
        k = jnp.dot(h, w_ref[:, cols], preferred_element_type=F32)
        o_ref[:, cols] = _rms(k, kg_ref[...]).astype(BF16)
    o_ref[:, kw:] = jnp.dot(h, w_ref[:, kw:], preferred_element_type=F32).astype(BF16)


def _mem_kv(mem2, gain, w_bf16, k_gain, bm=512):
    rows = mem2.shape[0]
    bm = min(bm, rows)
    n = w_bf16.shape[1]
    return pl.pallas_call(
        _mem_kv_kernel,
        out_shape=jax.ShapeDtypeStruct((rows, n), BF16),
        grid=(rows // bm,),
        in_specs=[
            pl.BlockSpec((bm, D_MODEL), lambda i: (i, 0)),
            pl.BlockSpec((1, D_MODEL), lambda i: (0, 0)),
            pl.BlockSpec((D_MODEL, n), lambda i: (0, 0)),
            pl.BlockSpec((1, MEM_D), lambda i: (0, 0)),
        ],
        out_specs=pl.BlockSpec((bm, n), lambda i: (i, 0)),
        compiler_params=_cp(("parallel",)),
        name="mem_kv",
    )(mem2, gain, w_bf16, k_gain)


def _mem_attn_kernel(q_ref, k_ref, v_ref, qg_ref, o_ref, *, bq):
    seq = q_ref.shape[0]
    nt = (((1,), (1,)), ((), ()))

    def block(n, carry):
        rows = pl.ds(pl.multiple_of(n * bq, bq), bq)
        q = _rms(q_ref[rows, :].astype(F32), qg_ref[...]) * (MEM_D ** -0.5)
        s = lax.dot_general(q.astype(BF16), k_ref[...], nt, preferred_element_type=F32)
        p = jnp.exp(s - jnp.max(s, axis=-1, keepdims=True))
        denom = jnp.sum(p, axis=-1, keepdims=True)
        o = jnp.dot(p.astype(BF16), v_ref[...], preferred_element_type=F32) / denom
        o_ref[rows, :] = o.astype(BF16)
        return carry

    lax.fori_loop(0, seq // bq, block, 0, unroll=8)


def _mem_attention(proj3, mkv3, q_gain, bq=256):
    b, seq, _ = proj3.shape
    m_len = mkv3.shape[1]
    return pl.pallas_call(
        functools.partial(_mem_attn_kernel, bq=bq),
        out_shape=jax.ShapeDtypeStruct((b, seq, MEM_HEADS * MEM_D), BF16),
        grid=(b, MEM_HEADS),
        in_specs=[
            pl.BlockSpec((None, seq, MEM_D), lambda i, h: (i, 0, OFF_MQ // MEM_D + h)),
            pl.BlockSpec((None, m_len, MEM_D), lambda i, h: (i, 0, h)),
            pl.BlockSpec((None, m_len, MEM_D), lambda i, h: (i, 0, MEM_HEADS + h)),
            pl.BlockSpec((1, MEM_D), lambda i, h: (0, 0)),
        ],
        out_specs=pl.BlockSpec((None, seq, MEM_D), lambda i, h: (i, 0, h)),
        compiler_params=_cp(("parallel", "parallel")),
        name="mem_attention",
    )(proj3, mkv3, mkv3, q_gain)


def _merge_kernel(x_ref, g0_ref, g1_ref, g2_ref, b0_ref, b1_ref, b2_ref, wb_ref, wo_ref, gn_ref,
                  wr_ref, xo_ref, hp_ref, lg_ref, lg_scr, x_scr):
    i = pl.program_id(0)
    n = pl.num_programs(0) - 1

    def mix():
        merged = None
        for gate_ref, br, j in ((g0_ref, b0_ref, 0), (g1_ref, b1_ref, 1), (g2_ref, b2_ref, 2)):
            gate = 0.5 * jnp.tanh(0.5 * gate_ref[...].astype(F32)) + 0.5
            t = gate * jnp.dot(br[...], wb_ref[j], preferred_element_type=F32)
            merged = t if merged is None else merged + t
        x = x_ref[...] + jnp.dot(merged.astype(BF16), wo_ref[...], preferred_element_type=F32)
        xo_ref[...] = x
        x_scr[i % 2] = x

    def route_inputs():
        h = _rms(x_scr[(i + 1) % 2], gn_ref[...])
        h_hi = h.astype(BF16)
        h_lo = (h - h_hi.astype(F32)).astype(BF16)
        lg_scr[...] = jnp.dot(jnp.concatenate([h_hi, h_lo], axis=1), wr_ref[...],
                              preferred_element_type=F32)
        lg_ref[...] = lg_scr[...].T
        hp_ref[...] = _pack_bf16_pairs(h)

    @pl.when(jnp.logical_and(i >= 1, i < n))
    def _():
        route_inputs()
        mix()

    @pl.when(i == 0)
    def _():
        mix()

    @pl.when(i == n)
    def _():
        route_inputs()


def _merge(x2, proj, ret, att, mo, wb, wo, gain, w_router_cat, bm=512):
    t = x2.shape[0]
    last = t // bm - 1
    const2 = lambda i: (0, 0)
    row = lambda i: (jnp.minimum(i, last), 0)
    lag = lambda i: (jnp.maximum(i - 1, 0), 0)
    return pl.pallas_call(
        _merge_kernel,
        out_shape=(
            jax.ShapeDtypeStruct((t, D_MODEL), F32),
            jax.ShapeDtypeStruct((t, HALF), I32),
            jax.ShapeDtypeStruct((ROUTER_LANES, t), F32),
        ),
        grid=(t // bm + 1,),
        in_specs=[
            pl.BlockSpec((bm, D_MODEL), row),
            pl.BlockSpec((bm, D_MODEL), lambda i: (jnp.minimum(i, last), 0)),
            pl.BlockSpec((bm, D_MODEL), lambda i: (jnp.minimum(i, last), 1)),
            pl.BlockSpec((bm, D_MODEL), lambda i: (jnp.minimum(i, last), 2)),
            pl.BlockSpec((bm, D_MODEL), row),
            pl.BlockSpec((bm, D_MODEL), row),
            pl.BlockSpec((bm, D_MODEL), row),
            pl.BlockSpec((3, D_MODEL, D_MODEL), lambda i: (0, 0, 0)),
            pl.BlockSpec((D_MODEL, D_MODEL), const2),
            pl.BlockSpec((1, D_MODEL), const2),
            pl.BlockSpec((2 * D_MODEL, ROUTER_LANES), const2),
        ],
        out_specs=(
            pl.BlockSpec((bm, D_MODEL), row),
            pl.BlockSpec((bm, HALF), lag),
            pl.BlockSpec((ROUTER_LANES, bm), lambda i: (0, jnp.maximum(i - 1, 0))),
        ),
        scratch_shapes=[pltpu.VMEM((bm, ROUTER_LANES), F32), pltpu.VMEM((2, bm, D_MODEL), F32)],
        compiler_params=_cp(("arbitrary",)),
        name="merge_router",
    )(x2, proj, proj, proj, ret, att, mo, wb, wo, gain, w_router_cat)


def _route_kernel(l_ref, b_ref, tri_ref, eid_ref, rank_ref, gate_ref, cnt_ref):
    @pl.when(pl.program_id(0) == 0)
    def _():
        cnt_ref[...] = jnp.zeros_like(cnt_ref)

    l = l_ref[0:N_EXPERTS, :] + l_ref[N_EXPERTS:2 * N_EXPERTS, :] + b_ref[...]
    ne, bt = l.shape
    iota_e = lax.broadcasted_iota(I32, (ne, bt), 0)
    picked = jnp.zeros((ne, bt), jnp.bool_)
    vals, idxs = [], []
    for _ in range(TOP_K):
        m = jnp.max(l, axis=0, keepdims=True)
        idx = jnp.min(jnp.where(l == m, iota_e, ne), axis=0, keepdims=True)
        sel = iota_e == idx
        picked = picked | sel
        l = jnp.where(sel, -jnp.inf, l)
        vals.append(m)
        idxs.append(idx)
    ex = [jnp.exp(v - vals[0]) for v in vals]
    tot = ex[0] + ex[1] + ex[2] + ex[3]
    onehot = jnp.where(picked, 1.0, 0.0)
    before = jnp.dot(onehot.astype(BF16), tri_ref[...], preferred_element_type=F32) + cnt_ref[:, 0:1]
    for k in range(TOP_K):
        eid_ref[k:k + 1, :] = idxs[k]
        gate_ref[k:k + 1, :] = ex[k] / tot
        rank_ref[k:k + 1, :] = jnp.sum(jnp.where(iota_e == idxs[k], before, 0.0), axis=0,
                                       keepdims=True).astype(I32)
    cnt_ref[...] = cnt_ref[...] + jnp.sum(onehot, axis=1, keepdims=True)


def _route(logits_t, bias_col, tri):
    bt = tri.shape[0]
    ne, t = N_EXPERTS, logits_t.shape[1]
    blk = lambda i: (0, i)
    return pl.pallas_call(
        _route_kernel,
        out_shape=(
            jax.ShapeDtypeStruct((TOP_K, t), I32),
            jax.ShapeDtypeStruct((TOP_K, t), I32),
            jax.ShapeDtypeStruct((TOP_K, t), F32),
            jax.ShapeDtypeStruct((ne, 128), F32),
        ),
        grid=(t // bt,),
        in_specs=[pl.BlockSpec((ROUTER_LANES, bt), blk), pl.BlockSpec((ne, 1), lambda i: (0, 0)),
                  pl.BlockSpec((bt, bt), lambda i: (0, 0))],
        out_specs=(
            pl.BlockSpec((TOP_K, bt), blk),
            pl.BlockSpec((TOP_K, bt), blk),
            pl.BlockSpec((TOP_K, bt), blk),
            pl.BlockSpec((ne, 128), lambda i: (0, 0)),
        ),
        compiler_params=_cp(("arbitrary",)),
        name="route_topk",
    )(logits_t, bias_col, tri)


def _slot_kernel(start_ref, eid_ref, rank_ref, slot_ref):
    eid = eid_ref[...]
    base = jnp.zeros(eid.shape, I32)
    for e in range(N_EXPERTS):
        base = jnp.where(eid == e, start_ref[e], base)
    slot_ref[...] = base + rank_ref[...]


def _slots(pad_starts, eid, rank, bt=2048):
    t = eid.shape[1]
    blk = lambda i, s: (0, i)
    return pl.pallas_call(
        _slot_kernel,
        out_shape=jax.ShapeDtypeStruct((TOP_K, t), I32),
        grid_spec=pltpu.PrefetchScalarGridSpec(
            num_scalar_prefetch=1,
            grid=(t // bt,),
            in_specs=[pl.BlockSpec((TOP_K, bt), blk), pl.BlockSpec((TOP_K, bt), blk)],
            out_specs=pl.BlockSpec((TOP_K, bt), blk),
        ),
        compiler_params=_cp(("parallel",)),
        name="route_slots",
    )(pad_starts, eid, rank)


def _sc_workers():
    info = plsc.get_sparse_core_info()
    return info.num_cores, info.num_subcores


def _dispatch_rows(hp, slot, n_rows, win=64):
    t, w = hp.shape
    nc, ns = _sc_workers()
    per_worker = t // (nc * ns)
    mesh = plsc.VectorSubcoreMesh(core_axis_name="c", subcore_axis_name="s")

    @functools.partial(
        pl.kernel,
        out_type=jax.ShapeDtypeStruct((n_rows, w), hp.dtype),
        mesh=mesh,
        scratch_types=[pltpu.VMEM((win,), I32)] * TOP_K + [pltpu.VMEM((win, w), hp.dtype)],
        name="moe_dispatch",
    )
    def k(hp_hbm, slot_hbm, buf_hbm, *scratch):
        idx_v, rows_v = scratch[:TOP_K], scratch[TOP_K]
        wid = lax.axis_index("s") * nc + lax.axis_index("c")
        base = wid * per_worker

        @pl.loop(0, per_worker // win)
        def _(i):
            off = pl.multiple_of(base + i * win, win)
            pltpu.sync_copy(hp_hbm.at[pl.ds(off, win)], rows_v)
            for kk in range(TOP_K):
                pltpu.sync_copy(slot_hbm.at[pl.ds(kk * t + off, win)], idx_v[kk])
                pltpu.sync_copy(rows_v, buf_hbm.at[idx_v[kk]])

    return k(hp, slot.reshape(-1))


def _gather_rows(out_rows, slot, win=64):
    t = slot.shape[1]
    w = out_rows.shape[1]
    nc, ns = _sc_workers()
    per_worker = t // (nc * ns)
    mesh = plsc.VectorSubcoreMesh(core_axis_name="c", subcore_axis_name="s")

    @functools.partial(
        pl.kernel,
        out_type=jax.ShapeDtypeStruct((TOP_K * t, w), out_rows.dtype),
        mesh=mesh,
        scratch_types=[pltpu.VMEM((win,), I32), pltpu.VMEM((win, w), out_rows.dtype)],
        name="moe_gather",
    )
    def k(rows_hbm, slot_hbm, og_hbm, idx_v, rows_v):
        wid = lax.axis_index("s") * nc + lax.axis_index("c")
        base = wid * per_worker

        @pl.loop(0, per_worker // win)
        def _(i):
            off = pl.multiple_of(base + i * win, win)
            for kk in range(TOP_K):
                pltpu.sync_copy(slot_hbm.at[pl.ds(kk * t + off, win)], idx_v)
                pltpu.sync_copy(rows_hbm.at[idx_v], rows_v)
                pltpu.sync_copy(rows_v, og_hbm.at[pl.ds(kk * t + off, win)])

    return k(out_rows, slot.reshape(-1)).reshape(TOP_K, t, w)


def _expert_kernel(be_ref, nu_ref, x_ref, wi_ref, bi_ref, wo_ref, bo_ref, o_ref, wi_bf, wo_bf, act_scr):
    i = pl.program_id(0)
    n_used = nu_ref[0]
    last = be_ref.shape[0] - 1
    e_up = be_ref[jnp.minimum(i, last)]
    e_up_prev = be_ref[jnp.clip(i - 1, 0, last)]
    e_down_prev = be_ref[jnp.clip(i - 2, 0, last)]
    up = i < n_used
    down = jnp.logical_and(i >= 1, i <= n_used)
    rows = 128

    @pl.when(jnp.logical_and(up, jnp.logical_or(i == 0, e_up != e_up_prev)))
    def _():
        def cast(c, carry):
            sl = pl.ds(pl.multiple_of(c * rows, rows), rows)
            wi_bf[sl, :] = wi_ref[sl, :].astype(BF16)
            return carry

        lax.fori_loop(0, D_MODEL // rows, cast, 0)

    @pl.when(jnp.logical_and(down, jnp.logical_or(i == 1, e_up_prev != e_down_prev)))
    def _():
        def cast(c, carry):
            sl = pl.ds(pl.multiple_of(c * rows, rows), rows)
            wo_bf[sl, :] = wo_ref[sl, :].astype(BF16)
            return carry

        lax.fori_loop(0, D_FF // rows, cast, 0)

    def up_half():
        hi, lo = _unpack_bf16_pairs(x_ref[...])
        x = jnp.concatenate([hi.astype(BF16), lo.astype(BF16)], axis=1)
        h = jnp.dot(x, wi_bf[...], preferred_element_type=F32) + bi_ref[...]
        gate = jnp.minimum(h[:, :D_FF], SWIGLU_LIMIT)
        lin = jnp.clip(h[:, D_FF:], -SWIGLU_LIMIT, SWIGLU_LIMIT)
        act = gate * (0.5 * jnp.tanh((0.5 * SWIGLU_ALPHA) * gate) + 0.5) * (lin + 1.0)
        act_scr[i % 2] = act.astype(BF16)

    def down_half():
        o = jnp.dot(act_scr[(i + 1) % 2], wo_bf[...], preferred_element_type=F32) + bo_ref[...]
        o_ref[...] = _pack_bf16_pairs(o)

    @pl.when(jnp.logical_and(up, down))
    def _():
        down_half()
        up_half()

    @pl.when(jnp.logical_and(up, jnp.logical_not(down)))
    def _():
        up_half()

    @pl.when(jnp.logical_and(down, jnp.logical_not(up)))
    def _():
        down_half()


def _experts(block_expert, n_used, buf, w_in, b_in, w_out, b_out, blk):
    n_rows, w = buf.shape
    n_blocks = n_rows // blk
    last = n_blocks - 1
    up_blk = lambda i, be, nu: (jnp.minimum(i, last), 0)
    up_exp = lambda i, be, nu: (be[jnp.minimum(i, last)], 0, 0)
    down_blk = lambda i, be, nu: (jnp.maximum(i - 1, 0), 0)
    down_exp = lambda i, be, nu: (be[jnp.maximum(i - 1, 0)], 0, 0)
    return pl.pallas_call(
        _expert_kernel,
        out_shape=jax.ShapeDtypeStruct((n_rows, w), buf.dtype),
        grid_spec=pltpu.PrefetchScalarGridSpec(
            num_scalar_prefetch=2,
            grid=(n_blocks + 1,),
            in_specs=[
                pl.BlockSpec((blk, w), up_blk),
                pl.BlockSpec((None, D_MODEL, 2 * D_FF), up_exp),
                pl.BlockSpec((None, 1, 2 * D_FF), up_exp),
                pl.BlockSpec((None, D_FF, D_MODEL), down_exp),
                pl.BlockSpec((None, 1, D_MODEL), down_exp),
            ],
            out_specs=pl.BlockSpec((blk, w), down_blk),
            scratch_shapes=[pltpu.VMEM((D_MODEL, 2 * D_FF), BF16), pltpu.VMEM((D_FF, D_MODEL), BF16),
                            pltpu.VMEM((2, blk, D_FF), BF16)],
        ),
        compiler_params=_cp(("arbitrary",)),
        name="expert_ffn",
    )(block_expert, n_used, buf, w_in, b_in, w_out, b_out)


def _combine_kernel(x_ref, og_ref, g_ref, y_ref):
    g = g_ref[...]
    acc_hi = x_ref[:, :HALF]
    acc_lo = x_ref[:, HALF:]
    for k in range(TOP_K):
        hi, lo = _unpack_bf16_pairs(og_ref[k])
        gk = g[:, k:k + 1]
        acc_hi = acc_hi + gk * hi
        acc_lo = acc_lo + gk * lo
    y_ref[:, :HALF] = acc_hi
    y_ref[:, HALF:] = acc_lo


def _combine(x_mid, og, gates_tk, bm=1024):
    t = x_mid.shape[0]
    return pl.pallas_call(
        _combine_kernel,
        out_shape=jax.ShapeDtypeStruct((t, D_MODEL), F32),
        grid=(t // bm,),
        in_specs=[
            pl.BlockSpec((bm, D_MODEL), lambda i: (i, 0)),
            pl.BlockSpec((TOP_K, bm, HALF), lambda i: (0, i, 0)),
            pl.BlockSpec((bm, TOP_K), lambda i: (i, 0)),
        ],
        out_specs=pl.BlockSpec((bm, D_MODEL), lambda i: (i, 0)),
        compiler_params=_cp(("parallel",)),
        name="moe_combine",
    )(x_mid, og, gates_tk)


def _block_tables(counts, n_blocks, blk):
    padded = (counts + blk - 1) // blk * blk
    pad_ends = jnp.cumsum(padded)
    pad_starts = pad_ends - padded
    block_start = jnp.arange(n_blocks, dtype=I32) * blk
    block_expert = jnp.minimum(jnp.sum(pad_ends[None, :] <= block_start[:, None], axis=1), N_EXPERTS - 1)
    n_used = (pad_ends[-1] // blk).reshape(1)
    return pad_starts.astype(I32), block_expert.astype(I32), n_used.astype(I32)


def _layer(x, mem, consts, params, after=None):
    b, seq, d = x.shape
    m_len = mem.shape[1]
    t = b * seq
    x2 = x.reshape(t, d)
    gain = params["norm_mix"]
    if after is not None:
        gain = gain + 0.0 * after.astype(F32)
    proj = _in_proj(x2, gain, params["w_in"])
    proj3 = proj.reshape(b, seq, IN_WIDTH)
    ret = _retention(proj3, params["ret_out_norm"], consts["cos"], consts["sin"], consts["tabs"],
                     consts["decays"])
    att = _window_attention(proj3, params["att_q_norm"], params["att_k_norm"], params["att_sink"],
                            consts["bias"])
    mkv = _mem_kv(mem.reshape(b * m_len, d), params["mem_norm"], params["w_mem_kv"],
                  params["mem_k_norm"])
    mo = _mem_attention(proj3, mkv.reshape(b, m_len, 2 * MEM_HEADS * MEM_D), params["mem_q_norm"])
    x_mid, hp, logits_t = _merge(x2, proj, ret.reshape(t, d), att.reshape(t, d), mo.reshape(t, d),
                               params["w_branch"], params["w_out"], params["norm_ffn"],
                               params["w_router"])
    eid, rank, gates, counts = _route(logits_t, params["b_router"], consts["tri"])
    blk = EXPERT_BLOCK
    n_blocks = t * TOP_K // blk + N_EXPERTS
    pad_starts, block_expert, n_used = _block_tables(counts[:, 0].astype(I32), n_blocks, blk)
    slot = _slots(pad_starts, eid, rank)
    buf = _dispatch_rows(hp, slot, n_blocks * blk)
    out_rows = _experts(block_expert, n_used, buf, params["w_e_in"], params["b_e_in"],
                        params["w_e_out"], params["b_e_out"], blk)
    og = _gather_rows(out_rows, slot)
    y = _combine(x_mid, og, gates.T)
    return y.reshape(b, seq, d), slot[0, 0]


def _router_weights(w):
    w_hi = w.astype(BF16)
    w_lo = (w - w_hi.astype(F32)).astype(BF16)
    pad = jnp.zeros((D_MODEL, ROUTER_LANES - 2 * N_EXPERTS), BF16)
    top = jnp.concatenate([w_hi, w_lo, pad], axis=1)
    bottom = jnp.concatenate([w_hi, jnp.zeros_like(w_lo), pad], axis=1)
    return jnp.concatenate([top, bottom], axis=0)


def _prepare(seq, rel_table, norm_mix, w_in, ret_out_norm, att_q_norm, att_k_norm, att_sink, mem_norm,
             w_mem_kv, mem_q_norm, mem_k_norm, w_branch, w_out, norm_ffn, w_router, b_router,
             w_e_in, b_e_in, w_e_out, b_e_out):
    gates_at = IN_WIDTH - 3 * D_MODEL
    w_in0 = w_in[0]
    params = {
        "norm_mix": norm_mix[0].reshape(1, -1),
        "w_in": jnp.concatenate([w_in0[:, gates_at:], w_in0[:, :gates_at]], axis=1).astype(BF16),
        "ret_out_norm": ret_out_norm[0].reshape(1, -1),
        "att_q_norm": att_q_norm[0].reshape(1, -1),
        "att_k_norm": att_k_norm[0].reshape(1, -1),
        "att_sink": att_sink[0].astype(F32),
        "mem_norm": mem_norm[0].reshape(1, -1),
        "w_mem_kv": w_mem_kv[0].astype(BF16),
        "mem_q_norm": mem_q_norm[0].reshape(1, -1),
        "mem_k_norm": mem_k_norm[0].reshape(1, -1),
        "w_branch": w_branch[0].astype(BF16),
        "w_out": w_out[0].astype(BF16),
        "norm_ffn": norm_ffn[0].reshape(1, -1),
        "w_router": _router_weights(w_router[0]),
        "b_router": b_router[0].reshape(-1, 1),
        "w_e_in": w_e_in[0],
        "b_e_in": b_e_in[0].reshape(N_EXPERTS, 1, -1),
        "w_e_out": w_e_out[0],
        "b_e_out": b_e_out[0].reshape(N_EXPERTS, 1, -1),
    }
    cos, sin, tabs, decays = _retention_tables(seq)
    route_bt = 1024
    tri = (jnp.arange(route_bt)[:, None] < jnp.arange(route_bt)[None, :]).astype(BF16)
    consts = {"cos": cos, "sin": sin, "tabs": tabs, "decays": decays,
              "bias": _window_bias(rel_table), "tri": tri}
    return consts, params


def kernel(x_prompt, x_sample, mem_prompt, mem_sample, rel_table, norm_mix, w_in, ret_out_norm, att_q_norm, att_k_norm, att_sink, mem_norm, w_mem_kv, mem_q_norm, mem_k_norm, w_branch, w_out, norm_ffn, w_router, b_router, w_e_in, b_e_in, w_e_out, b_e_out):
    consts, params = _prepare(x_prompt.shape[1], rel_table, norm_mix, w_in, ret_out_norm, att_q_norm,
                              att_k_norm, att_sink, mem_norm, w_mem_kv, mem_q_norm, mem_k_norm,
                              w_branch, w_out, norm_ffn, w_router, b_router, w_e_in, b_e_in,
                              w_e_out, b_e_out)
    y_sample, first_slot = _layer(x_sample, mem_sample, consts, params)
    y_prompt, _ = _layer(x_prompt, mem_prompt, consts, params, after=first_slot)
    return (y_prompt, y_sample)
```

```python
import functools
import math

import jax
import jax.numpy as jnp
import numpy as np
from jax import lax
from jax.experimental import pallas as pl
from jax.experimental.pallas import tpu as pltpu
from jax.experimental.pallas import tpu_sc as plsc

F32 = jnp.float32
BF16 = jnp.bfloat16
I32 = jnp.int32

D_MODEL = 1024
EPS = 1e-6
RET_HEADS = 4
RET_QK = 128
RET_V = 256
CHUNK = 128
ROPE_BASE = 10000.0
DECAY_FWD = 5.0
DECAY_BWD = 5.5
ATT_HEADS = 8
ATT_KV = 2
ATT_GROUP = ATT_HEADS // ATT_KV
ATT_D = 128
WINDOW = 128
REL_BUCKETS = 32
REL_MAX_DIST = 128
MEM_HEADS = 4
MEM_D = 256
N_EXPERTS = 32
TOP_K = 4
D_FF = 1024
SWIGLU_LIMIT = 7.0
SWIGLU_ALPHA = 1.702
EXPERT_BLOCK = 512
HALF = D_MODEL // 2
ROUTER_LANES = 128

IN_WIDTH = 8704
OFF_GATES = 0
OFF_RQ = 3072
OFF_RK = 3584
OFF_RV = 4096
OFF_RG = 5120
OFF_AQ = 6144
OFF_AK = 7168
OFF_AV = 7424
OFF_MQ = 7680

VMEM_LIMIT = 56 * 1024 * 1024
LOG2E = 1.4426950408889634


def _cp(sem, vmem=VMEM_LIMIT):
    return pltpu.CompilerParams(dimension_semantics=sem, vmem_limit_bytes=vmem)


def _rms(x, g):
    return x * lax.rsqrt(jnp.mean(x * x, axis=-1, keepdims=True) + EPS) * g


def _pack_bf16_pairs(x):
    w = x.shape[1] // 2
    bits = pltpu.bitcast(x.astype(BF16).astype(F32), I32)
    hi = bits[:, :w] & jnp.int32(-65536)
    lo = lax.shift_right_logical(bits[:, w:], jnp.int32(16))
    return hi | lo


def _unpack_bf16_pairs(u):
    hi = pltpu.bitcast(u & jnp.int32(-65536), F32)
    lo = pltpu.bitcast(lax.shift_left(u, jnp.int32(16)), F32)
    return hi, lo


def _in_proj_kernel(x_ref, g_ref, w_ref, o_ref, h_scr):
    @pl.when(pl.program_id(1) == 0)
    def _():
        h_scr[...] = _rms(x_ref[...], g_ref[...]).astype(BF16)

    sub = 256

    def cols(c, carry):
        sl = pl.ds(pl.multiple_of(c * sub, sub), sub)
        o_ref[:, sl] = jnp.dot(h_scr[...], w_ref[:, sl], preferred_element_type=F32).astype(BF16)
        return carry

    lax.fori_loop(0, o_ref.shape[1] // sub, cols, 0, unroll=True)


def _in_proj(x2, gain, w_bf16, bm=1024, bn=4352):
    t = x2.shape[0]
    n = w_bf16.shape[1]
    return pl.pallas_call(
        _in_proj_kernel,
        out_shape=jax.ShapeDtypeStruct((t, n), BF16),
        grid=(t // bm, n // bn),
        in_specs=[
            pl.BlockSpec((bm, D_MODEL), lambda i, j: (i, 0)),
            pl.BlockSpec((1, D_MODEL), lambda i, j: (0, 0)),
            pl.BlockSpec((D_MODEL, bn), lambda i, j: (0, j)),
        ],
        out_specs=pl.BlockSpec((bm, bn), lambda i, j: (i, j)),
        scratch_shapes=[pltpu.VMEM((bm, D_MODEL), BF16)],
        compiler_params=_cp(("parallel", "arbitrary")),
        name="in_proj",
    )(x2, gain, w_bf16)


def _retention_kernel(dec_ref, q_ref, k_ref, v_ref, g_ref, cos_ref, sin_ref, tab_ref, gain_ref,
                      o_ref, lhs, rhs, acc, kvf, kvb):
    h = pl.program_id(1)
    seq = q_ref.shape[0]
    n_chunks = seq // CHUNK
    dec_f = dec_ref[2 * h]
    dec_b = dec_ref[2 * h + 1]
    nt = (((1,), (1,)), ((), ()))

    def rope(x, c, s):
        return x * c + pltpu.roll(x, RET_QK // 2, 1) * s

    def intra(n, carry):
        rows = pl.ds(pl.multiple_of(n * CHUNK, CHUNK), CHUNK)
        c = cos_ref[rows, :]
        s = sin_ref[rows, :]
        q = rope(q_ref[rows, :].astype(F32), c, s)
        k = rope(k_ref[rows, :].astype(F32), c, s) * (RET_QK ** -0.5)
        lhs[rows, 0:CHUNK] = (q * tab_ref[1]).astype(BF16)
        lhs[rows, CHUNK:2 * CHUNK] = (q * tab_ref[3]).astype(BF16)
        sc = jnp.dot(q.astype(BF16), k.T.astype(BF16), preferred_element_type=F32)
        a = jnp.concatenate([(sc * tab_ref[0]).astype(BF16),
                             (k * tab_ref[2]).T.astype(BF16),
                             (k * tab_ref[4]).T.astype(BF16)], axis=0)
        r = jnp.dot(a, v_ref[rows, :], preferred_element_type=F32)
        acc[rows, :] = r[0:CHUNK]
        kvf[n] = r[CHUNK:2 * CHUNK]
        kvb[n] = r[2 * CHUNK:3 * CHUNK]
        return carry

    lax.fori_loop(0, n_chunks, intra, 0, unroll=8)

    def scan_f(n, state):
        rhs[n, 0:CHUNK, :] = state.astype(BF16)
        return state * dec_f + kvf[n]

    lax.fori_loop(0, n_chunks, scan_f, jnp.zeros((RET_QK, RET_V), F32))

    def scan_b(i, state):
        n = n_chunks - 1 - i
        rhs[n, CHUNK:2 * CHUNK, :] = state.astype(BF16)
        return state * dec_b + kvb[n]

    lax.fori_loop(0, n_chunks, scan_b, jnp.zeros((RET_QK, RET_V), F32))

    def cross(n, carry):
        rows = pl.ds(pl.multiple_of(n * CHUNK, CHUNK), CHUNK)
        o = acc[rows, :] + jnp.dot(lhs[rows, :], rhs[n], preferred_element_type=F32)
        y = _rms(o, gain_ref[...])
        g = g_ref[rows, :].astype(F32)
        o_ref[rows, :] = (y * (g * (0.5 * jnp.tanh(0.5 * g) + 0.5))).astype(BF16)
        return carry

    lax.fori_loop(0, n_chunks, cross, 0, unroll=True)


def _retention(proj3, gain, cos, sin, tabs, decays):
    b, seq, _ = proj3.shape
    return pl.pallas_call(
        _retention_kernel,
        out_shape=jax.ShapeDtypeStruct((b, seq, RET_HEADS * RET_V), BF16),
        grid=(b, RET_HEADS),
        in_specs=[
            pl.BlockSpec(memory_space=pltpu.SMEM),
            pl.BlockSpec((None, seq, RET_QK), lambda i, h: (i, 0, OFF_RQ // RET_QK + h)),
            pl.BlockSpec((None, seq, RET_QK), lambda i, h: (i, 0, OFF_RK // RET_QK + h)),
            pl.BlockSpec((None, seq, RET_V), lambda i, h: (i, 0, OFF_RV // RET_V + h)),
            pl.BlockSpec((None, seq, RET_V), lambda i, h: (i, 0, OFF_RG // RET_V + h)),
            pl.BlockSpec((seq, RET_QK), lambda i, h: (0, 0)),
            pl.BlockSpec((seq, RET_QK), lambda i, h: (0, 0)),
            pl.BlockSpec((None, 5, CHUNK, CHUNK), lambda i, h: (h, 0, 0, 0)),
            pl.BlockSpec((1, RET_V), lambda i, h: (0, h)),
        ],
        out_specs=pl.BlockSpec((None, seq, RET_V), lambda i, h: (i, 0, h)),
        scratch_shapes=[
            pltpu.VMEM((seq, 2 * CHUNK), BF16),
            pltpu.VMEM((seq // CHUNK, 2 * CHUNK, RET_V), BF16),
            pltpu.VMEM((seq, RET_V), F32),
            pltpu.VMEM((seq // CHUNK, RET_QK, RET_V), F32),
            pltpu.VMEM((seq // CHUNK, RET_QK, RET_V), F32),
        ],
        compiler_params=_cp(("parallel", "parallel")),
        name="retention",
    )(decays, proj3, proj3, proj3, proj3, cos, sin, tabs, gain)


def _retention_tables(seq):
    half = RET_QK // 2
    inv = ROPE_BASE ** (-jnp.arange(half, dtype=F32) / half)
    ang = jnp.arange(seq, dtype=F32)[:, None] * inv[None, :]
    cos = jnp.concatenate([jnp.cos(ang), jnp.cos(ang)], axis=1)
    sin = jnp.concatenate([-jnp.sin(ang), jnp.sin(ang)], axis=1)
    heads = jnp.arange(RET_HEADS, dtype=F32)
    lg_f = jnp.log1p(-jnp.exp2(-DECAY_FWD - heads))[:, None, None]
    lg_b = jnp.log1p(-jnp.exp2(-DECAY_BWD - heads))[:, None, None]
    idx = jnp.arange(CHUNK, dtype=F32)
    diff = (idx[:, None] - idx[None, :])[None]
    dmat = jnp.where(diff >= 0, jnp.exp(jnp.where(diff >= 0, diff, 0.0) * lg_f),
                     jnp.exp(jnp.where(diff < 0, -diff, 0.0) * lg_b))
    col = jnp.broadcast_to(idx[None, :, None], (RET_HEADS, CHUNK, CHUNK))
    xi_f = jnp.exp((col + 1.0) * lg_f)
    zeta_f = jnp.exp((CHUNK - 1.0 - col) * lg_f)
    xi_b = jnp.exp((CHUNK - col) * lg_b)
    zeta_b = jnp.exp(col * lg_b)
    tabs = jnp.stack([dmat, xi_f, zeta_f, xi_b, zeta_b], axis=1)
    decays = jnp.stack([jnp.exp(CHUNK * lg_f[:, 0, 0]), jnp.exp(CHUNK * lg_b[:, 0, 0])], axis=1).reshape(-1)
    return cos, sin, tabs, decays


def _window_kernel(sink_ref, q_ref, k_ref, v_ref, qg_ref, kg_ref, bias_ref, o_ref,
                   qn, kt, vp, s_scr, p_scr, e_scr):
    g = pl.program_id(1)
    seq = q_ref.shape[0]
    n_blocks = seq // WINDOW
    stack = ATT_GROUP * WINDOW
    unit = 64
    heads = [slice(r * ATT_D, (r + 1) * ATT_D) for r in range(ATT_GROUP)]
    kt[:, 0:WINDOW] = jnp.zeros((ATT_D, WINDOW), BF16)
    kt[:, seq + WINDOW:seq + 2 * WINDOW] = jnp.zeros((ATT_D, WINDOW), BF16)
    vp[0:WINDOW, :] = jnp.zeros((WINDOW, 2 * ATT_D), BF16)
    vp[seq + WINDOW:seq + 2 * WINDOW, :] = jnp.zeros((WINDOW, 2 * ATT_D), BF16)

    def start(n, size):
        return n * size if isinstance(n, int) else pl.multiple_of(n * size, size)

    def prep(n):
        r0 = start(n, WINDOW)
        src = pl.ds(r0, WINDOW)
        dst = pl.ds(r0 + WINDOW, WINDOW)
        kt[:, dst] = _rms(k_ref[src, :].astype(F32), kg_ref[...]).T.astype(BF16)
        vp[dst, 0:ATT_D] = v_ref[src, :]
        vp[dst, ATT_D:2 * ATT_D] = jnp.ones((WINDOW, ATT_D), BF16)
        for hs in heads:
            q = _rms(q_ref[src, hs].astype(F32), qg_ref[...]) * (ATT_D ** -0.5 * LOG2E)
            qn[src, hs] = q.astype(BF16)

    def scores(n):
        r0 = start(n, WINDOW)
        kb = kt[:, pl.ds(r0, 3 * WINDOW)]
        q4 = jnp.concatenate([qn[pl.ds(r0, WINDOW), hs] for hs in heads], axis=0)
        if isinstance(n, int):
            edge = 0 if n == 0 else (2 if n == n_blocks - 1 else 1)
        else:
            edge = jnp.where(n == 0, 0, jnp.where(n == n_blocks - 1, 2, 1))
        s_scr[pl.ds(start(n, stack), stack), :] = (
            jnp.dot(q4, kb, preferred_element_type=F32) + bias_ref[edge])

    def softmax(n):
        base = start(n, stack)
        for u in range(stack // unit):
            rows = pl.ds(base + u * unit, unit)
            sink = sink_ref[g * ATT_GROUP + (u * unit) // WINDOW] * LOG2E
            s = s_scr[rows, :]
            m = jnp.maximum(jnp.max(s, axis=-1, keepdims=True), sink)
            mb = jnp.broadcast_to(m, (unit, WINDOW))
            p_scr[rows, :] = jnp.exp2(s - jnp.concatenate([mb, mb, mb], axis=1)).astype(BF16)
            e_scr[rows, :] = jnp.exp2(sink - mb)

    def outputs(n):
        r0 = start(n, WINDOW)
        rows = pl.ds(start(n, stack), stack)
        oe = jnp.dot(p_scr[rows, :], vp[pl.ds(r0, 3 * WINDOW), :], preferred_element_type=F32)
        o = oe[:, 0:ATT_D] / (oe[:, ATT_D:2 * ATT_D] + e_scr[rows, :])
        for r, hs in enumerate(heads):
            o_ref[pl.ds(r0, WINDOW), hs] = o[r * WINDOW:(r + 1) * WINDOW, :].astype(BF16)

    prep(0)
    prep(1)
    scores(0)
    prep(2)
    softmax(0)
    scores(1)
    prep(3)

    def steady(n, carry):
        outputs(n - 4)
        softmax(n - 3)
        scores(n - 2)
        prep(n)
        return carry

    lax.fori_loop(4, n_blocks, steady, 0, unroll=2)
    last = n_blocks - 1
    outputs(last - 3)
    softmax(last - 2)
    scores(last - 1)
    outputs(last - 2)
    softmax(last - 1)
    scores(last)
    outputs(last - 1)
    softmax(last)
    outputs(last)


def _window_attention(proj3, q_gain, k_gain, sink, bias):
    b, seq, _ = proj3.shape
    gw = ATT_GROUP * ATT_D
    return pl.pallas_call(
        _window_kernel,
        out_shape=jax.ShapeDtypeStruct((b, seq, ATT_HEADS * ATT_D), BF16),
        grid=(b, ATT_KV),
        in_specs=[
            pl.BlockSpec(memory_space=pltpu.SMEM),
            pl.BlockSpec((None, seq, gw), lambda i, g: (i, 0, OFF_AQ // gw + g)),
            pl.BlockSpec((None, seq, ATT_D), lambda i, g: (i, 0, OFF_AK // ATT_D + g)),
            pl.BlockSpec((None, seq, ATT_D), lambda i, g: (i, 0, OFF_AV // ATT_D + g)),
            pl.BlockSpec((1, ATT_D), lambda i, g: (0, 0)),
            pl.BlockSpec((1, ATT_D), lambda i, g: (0, 0)),
            pl.BlockSpec((None, 3, ATT_GROUP * WINDOW, 3 * WINDOW), lambda i, g: (g, 0, 0, 0)),
        ],
        out_specs=pl.BlockSpec((None, seq, gw), lambda i, g: (i, 0, g)),
        scratch_shapes=[
            pltpu.VMEM((seq, gw), BF16),
            pltpu.VMEM((ATT_D, seq + 2 * WINDOW), BF16),
            pltpu.VMEM((seq + 2 * WINDOW, 2 * ATT_D), BF16),
            pltpu.VMEM((seq * ATT_GROUP, 3 * WINDOW), F32),
            pltpu.VMEM((seq * ATT_GROUP, 3 * WINDOW), BF16),
            pltpu.VMEM((seq * ATT_GROUP, WINDOW), F32),
        ],
        compiler_params=_cp(("parallel", "parallel")),
        name="window_attention",
    )(sink, proj3, proj3, proj3, q_gain, k_gain, bias)


def _window_bias(rel_table):
    nb = REL_BUCKETS // 2
    max_exact = nb // 2
    qi = jnp.arange(WINDOW)[:, None]
    ki = jnp.arange(3 * WINDOW)[None, :]
    rel = ki - WINDOW - qi
    ret = jnp.where(rel > 0, nb, 0)
    n = jnp.abs(rel)
    large = max_exact + (jnp.log(jnp.maximum(n, 1).astype(F32) / max_exact)
                         / math.log(REL_MAX_DIST / max_exact) * (nb - max_exact)).astype(I32)
    large = jnp.minimum(large, nb - 1)
    bucket = ret + jnp.where(n < max_exact, n, large)
    onehot = (bucket[None] == jnp.arange(REL_BUCKETS)[:, None, None]).astype(F32)
    bias = jnp.einsum("bh,bqk->hqk", rel_table.astype(F32), onehot, precision=lax.Precision.HIGHEST)
    bias = jnp.where((jnp.abs(rel) <= WINDOW)[None], bias * LOG2E, -jnp.inf)
    bias = bias.reshape(ATT_KV, ATT_GROUP * WINDOW, 3 * WINDOW)
    first = jnp.where(ki < WINDOW, -jnp.inf, bias)
    last = jnp.where(ki >= 2 * WINDOW, -jnp.inf, bias)
    return jnp.stack([first, bias, last], axis=1)


def _mem_kv_kernel(m_ref, g_ref, w_ref, kg_ref, o_ref):
    h = _rms(m_ref[...], g_ref[...]).astype(BF16)
    kw = MEM_HEADS * MEM_D
    for j in range(MEM_HEADS):
        cols = slice(j * MEM_D, (j + 1) * MEM_D)
        k = jnp.dot(h, w_ref[:, cols], preferred_element_type=F32)
        o_ref[:, cols] = _rms(k, kg_ref[...]).astype(BF16)
    o_ref[:, kw:] = jnp.dot(h, w_ref[:, kw:], preferred_element_type=F32).astype(BF16)


def _mem_kv(mem2, gain, w_bf16, k_gain, bm=512):
    rows = mem2.shape[0]
    bm = min(bm, rows)
    n = w_bf16.shape[1]
    return pl.pallas_call(
        _mem_kv_kernel,
        out_shape=jax.ShapeDtypeStruct((rows, n), BF16),
        grid=(rows // bm,),
        in_specs=[
            pl.BlockSpec((bm, D_MODEL), lambda i: (i, 0)),
            pl.BlockSpec((1, D_MODEL), lambda i: (0, 0)),
            pl.BlockSpec((D_MODEL, n), lambda i: (0, 0)),
            pl.BlockSpec((1, MEM_D), lambda i: (0, 0)),
        ],
        out_specs=pl.BlockSpec((bm, n), lambda i: (i, 0)),
        compiler_params=_cp(("parallel",)),
        name="mem_kv",
    )(mem2, gain, w_bf16, k_gain)


def _mem_attn_kernel(q_ref, k_ref, v_ref, qg_ref, o_ref, kt, *, bq):
    seq = q_ref.shape[0]
    kt[...] = k_ref[...].astype(F32).T.astype(BF16)

    def block(n, carry):
        rows = pl.ds(pl.multiple_of(n * bq, bq), bq)
        q = _rms(q_ref[rows, :].astype(F32), qg_ref[...]) * (MEM_D ** -0.5)
        s = jnp.dot(q.astype(BF16), kt[...], preferred_element_type=F32)
        p = jnp.exp(s - jnp.max(s, axis=-1, keepdims=True))
        denom = jnp.sum(p, axis=-1, keepdims=True)
        o = jnp.dot(p.astype(BF16), v_ref[...], preferred_element_type=F32) / denom
        o_ref[rows, :] = o.astype(BF16)
        return carry

    lax.fori_loop(0, seq // bq, block, 0, unroll=8)


def _mem_attention(proj3, mkv3, q_gain, bq=256):
    b, seq, _ = proj3.shape
    m_len = mkv3.shape[1]
    return pl.pallas_call(
        functools.partial(_mem_attn_kernel, bq=bq),
        out_shape=jax.ShapeDtypeStruct((b, seq, MEM_HEADS * MEM_D), BF16),
        grid=(b, MEM_HEADS),
        in_specs=[
            pl.BlockSpec((None, seq, MEM_D), lambda i, h: (i, 0, OFF_MQ // MEM_D + h)),
            pl.BlockSpec((None, m_len, MEM_D), lambda i, h: (i, 0, h)),
            pl.BlockSpec((None, m_len, MEM_D), lambda i, h: (i, 0, MEM_HEADS + h)),
            pl.BlockSpec((1, MEM_D), lambda i, h: (0, 0)),
        ],
        out_specs=pl.BlockSpec((None, seq, MEM_D), lambda i, h: (i, 0, h)),
        scratch_shapes=[pltpu.VMEM((MEM_D, m_len), BF16)],
        compiler_params=_cp(("parallel", "parallel")),
        name="mem_attention",
    )(proj3, mkv3, mkv3, q_gain)


def _merge_kernel(x_ref, g0_ref, g1_ref, g2_ref, b0_ref, b1_ref, b2_ref, wb_ref, wo_ref, gn_ref,
                  wr_ref, xo_ref, hp_ref, lg_ref, lg_scr, x_scr):
    i = pl.program_id(0)
    n = pl.num_programs(0) - 1

    def mix():
        merged = None
        for gate_ref, br, j in ((g0_ref, b0_ref, 0), (g1_ref, b1_ref, 1), (g2_ref, b2_ref, 2)):
            gate = 0.5 * jnp.tanh(0.5 * gate_ref[...].astype(F32)) + 0.5
            t = gate * jnp.dot(br[...], wb_ref[j], preferred_element_type=F32)
            merged = t if merged is None else merged + t
        x = x_ref[...] + jnp.dot(merged.astype(BF16), wo_ref[...], preferred_element_type=F32)
        xo_ref[...] = x
        x_scr[i % 2] = x

    def route_inputs():
        h = _rms(x_scr[(i + 1) % 2], gn_ref[...])
        h_hi = h.astype(BF16)
        h_lo = (h - h_hi.astype(F32)).astype(BF16)
        lg_scr[...] = jnp.dot(jnp.concatenate([h_hi, h_lo], axis=1), wr_ref[...],
                              preferred_element_type=F32)
        lg_ref[...] = lg_scr[...].T
        hp_ref[...] = _pack_bf16_pairs(h)

    @pl.when(jnp.logical_and(i >= 1, i < n))
    def _():
        route_inputs()
        mix()

    @pl.when(i == 0)
    def _():
        mix()

    @pl.when(i == n)
    def _():
        route_inputs()


def _merge(x2, proj, ret, att, mo, wb, wo, gain, w_router_cat, bm=512):
    t = x2.shape[0]
    last = t // bm - 1
    const2 = lambda i: (0, 0)
    row = lambda i: (jnp.minimum(i, last), 0)
    lag = lambda i: (jnp.maximum(i - 1, 0), 0)
    return pl.pallas_call(
        _merge_kernel,
        out_shape=(
            jax.ShapeDtypeStruct((t, D_MODEL), F32),
            jax.ShapeDtypeStruct((t, HALF), I32),
            jax.ShapeDtypeStruct((ROUTER_LANES, t), F32),
        ),
        grid=(t // bm + 1,),
        in_specs=[
            pl.BlockSpec((bm, D_MODEL), row),
            pl.BlockSpec((bm, D_MODEL), lambda i: (jnp.minimum(i, last), 0)),
            pl.BlockSpec((bm, D_MODEL), lambda i: (jnp.minimum(i, last), 1)),
            pl.BlockSpec((bm, D_MODEL), lambda i: (jnp.minimum(i, last), 2)),
            pl.BlockSpec((bm, D_MODEL), row),
            pl.BlockSpec((bm, D_MODEL), row),
            pl.BlockSpec((bm, D_MODEL), row),
            pl.BlockSpec((3, D_MODEL, D_MODEL), lambda i: (0, 0, 0)),
            pl.BlockSpec((D_MODEL, D_MODEL), const2),
            pl.BlockSpec((1, D_MODEL), const2),
            pl.BlockSpec((2 * D_MODEL, ROUTER_LANES), const2),
        ],
        out_specs=(
            pl.BlockSpec((bm, D_MODEL), row),
            pl.BlockSpec((bm, HALF), lag),
            pl.BlockSpec((ROUTER_LANES, bm), lambda i: (0, jnp.maximum(i - 1, 0))),
        ),
        scratch_shapes=[pltpu.VMEM((bm, ROUTER_LANES), F32), pltpu.VMEM((2, bm, D_MODEL), F32)],
        compiler_params=_cp(("arbitrary",)),
        name="merge_router",
    )(x2, proj, proj, proj, ret, att, mo, wb, wo, gain, w_router_cat)


def _route_kernel(l_ref, b_ref, tri_ref, eid_ref, rank_ref, gate_ref, cnt_ref):
    @pl.when(pl.program_id(0) == 0)
    def _():
        cnt_ref[...] = jnp.zeros_like(cnt_ref)

    l = l_ref[0:N_EXPERTS, :] + l_ref[N_EXPERTS:2 * N_EXPERTS, :] + b_ref[...]
    ne, bt = l.shape
    iota_e = lax.broadcasted_iota(I32, (ne, bt), 0)
    picked = jnp.zeros((ne, bt), jnp.bool_)
    vals, idxs = [], []
    for _ in range(TOP_K):
        m = jnp.max(l, axis=0, keepdims=True)
        idx = jnp.min(jnp.where(l == m, iota_e, ne), axis=0, keepdims=True)
        sel = iota_e == idx
        picked = picked | sel
        l = jnp.where(sel, -jnp.inf, l)
        vals.append(m)
        idxs.append(idx)
    ex = [jnp.exp(v - vals[0]) for v in vals]
    tot = ex[0] + ex[1] + ex[2] + ex[3]
    onehot = jnp.where(picked, 1.0, 0.0)
    before = jnp.dot(onehot.astype(BF16), tri_ref[...], preferred_element_type=F32) + cnt_ref[:, 0:1]
    for k in range(TOP_K):
        eid_ref[k:k + 1, :] = idxs[k]
        gate_ref[k:k + 1, :] = ex[k] / tot
        rank_ref[k:k + 1, :] = jnp.sum(jnp.where(iota_e == idxs[k], before, 0.0), axis=0,
                                       keepdims=True).astype(I32)
    cnt_ref[...] = cnt_ref[...] + jnp.sum(onehot, axis=1, keepdims=True)


def _route(logits_t, bias_col, tri):
    bt = tri.shape[0]
    ne, t = N_EXPERTS, logits_t.shape[1]
    blk = lambda i: (0, i)
    return pl.pallas_call(
        _route_kernel,
        out_shape=(
            jax.ShapeDtypeStruct((TOP_K, t), I32),
            jax.ShapeDtypeStruct((TOP_K, t), I32),
            jax.ShapeDtypeStruct((TOP_K, t), F32),
            jax.ShapeDtypeStruct((ne, 128), F32),
        ),
        grid=(t // bt,),
        in_specs=[pl.BlockSpec((ROUTER_LANES, bt), blk), pl.BlockSpec((ne, 1), lambda i: (0, 0)),
                  pl.BlockSpec((bt, bt), lambda i: (0, 0))],
        out_specs=(
            pl.BlockSpec((TOP_K, bt), blk),
            pl.BlockSpec((TOP_K, bt), blk),
            pl.BlockSpec((TOP_K, bt), blk),
            pl.BlockSpec((ne, 128), lambda i: (0, 0)),
        ),
        compiler_params=_cp(("arbitrary",)),
        name="route_topk",
    )(logits_t, bias_col, tri)


def _slot_kernel(start_ref, eid_ref, rank_ref, slot_ref):
    eid = eid_ref[...]
    base = jnp.zeros(eid.shape, I32)
    for e in range(N_EXPERTS):
        base = jnp.where(eid == e, start_ref[e], base)
    slot_ref[...] = base + rank_ref[...]


def _slots(pad_starts, eid, rank, bt=2048):
    t = eid.shape[1]
    blk = lambda i, s: (0, i)
    return pl.pallas_call(
        _slot_kernel,
        out_shape=jax.ShapeDtypeStruct((TOP_K, t), I32),
        grid_spec=pltpu.PrefetchScalarGridSpec(
            num_scalar_prefetch=1,
            grid=(t // bt,),
            in_specs=[pl.BlockSpec((TOP_K, bt), blk), pl.BlockSpec((TOP_K, bt), blk)],
            out_specs=pl.BlockSpec((TOP_K, bt), blk),
        ),
        compiler_params=_cp(("parallel",)),
        name="route_slots",
    )(pad_starts, eid, rank)


def _sc_workers():
    info = plsc.get_sparse_core_info()
    return info.num_cores, info.num_subcores


def _dispatch_rows(hp, slot, n_rows, win=64):
    t, w = hp.shape
    nc, ns = _sc_workers()
    per_worker = t // (nc * ns)
    mesh = plsc.VectorSubcoreMesh(core_axis_name="c", subcore_axis_name="s")

    @functools.partial(
        pl.kernel,
        out_type=jax.ShapeDtypeStruct((n_rows, w), hp.dtype),
        mesh=mesh,
        scratch_types=[pltpu.VMEM((win,), I32)] * TOP_K + [pltpu.VMEM((win, w), hp.dtype)],
        name="moe_dispatch",
    )
    def k(hp_hbm, slot_hbm, buf_hbm, *scratch):
        idx_v, rows_v = scratch[:TOP_K], scratch[TOP_K]
        wid = lax.axis_index("s") * nc + lax.axis_index("c")
        base = wid * per_worker

        @pl.loop(0, per_worker // win)
        def _(i):
            off = pl.multiple_of(base + i * win, win)
            pltpu.sync_copy(hp_hbm.at[pl.ds(off, win)], rows_v)
            for kk in range(TOP_K):
                pltpu.sync_copy(slot_hbm.at[pl.ds(kk * t + off, win)], idx_v[kk])
                pltpu.sync_copy(rows_v, buf_hbm.at[idx_v[kk]])

    return k(hp, slot.reshape(-1))


def _gather_rows(out_rows, slot, win=64):
    t = slot.shape[1]
    w = out_rows.shape[1]
    nc, ns = _sc_workers()
    per_worker = t // (nc * ns)
    mesh = plsc.VectorSubcoreMesh(core_axis_name="c", subcore_axis_name="s")

    @functools.partial(
        pl.kernel,
        out_type=jax.ShapeDtypeStruct((TOP_K * t, w), out_rows.dtype),
        mesh=mesh,
        scratch_types=[pltpu.VMEM((win,), I32), pltpu.VMEM((win, w), out_rows.dtype)],
        name="moe_gather",
    )
    def k(rows_hbm, slot_hbm, og_hbm, idx_v, rows_v):
        wid = lax.axis_index("s") * nc + lax.axis_index("c")
        base = wid * per_worker

        @pl.loop(0, per_worker // win)
        def _(i):
            off = pl.multiple_of(base + i * win, win)
            for kk in range(TOP_K):
                pltpu.sync_copy(slot_hbm.at[pl.ds(kk * t + off, win)], idx_v)
                pltpu.sync_copy(rows_hbm.at[idx_v], rows_v)
                pltpu.sync_copy(rows_v, og_hbm.at[pl.ds(kk * t + off, win)])

    return k(out_rows, slot.reshape(-1)).reshape(TOP_K, t, w)


def _expert_kernel(be_ref, nu_ref, x_ref, wi_ref, bi_ref, wo_ref, bo_ref, o_ref, wi_bf, wo_bf, act_scr):
    i = pl.program_id(0)
    n_used = nu_ref[0]
    last = be_ref.shape[0] - 1
    e_up = be_ref[jnp.minimum(i, last)]
    e_up_prev = be_ref[jnp.clip(i - 1, 0, last)]
    e_down_prev = be_ref[jnp.clip(i - 2, 0, last)]
    up = i < n_used
    down = jnp.logical_and(i >= 1, i <= n_used)
    rows = 128

    @pl.when(jnp.logical_and(up, jnp.logical_or(i == 0, e_up != e_up_prev)))
    def _():
        def cast(c, carry):
            sl = pl.ds(pl.multiple_of(c * rows, rows), rows)
            wi_bf[sl, :] = wi_ref[sl, :].astype(BF16)
            return carry

        lax.fori_loop(0, D_MODEL // rows, cast, 0)

    @pl.when(jnp.logical_and(down, jnp.logical_or(i == 1, e_up_prev != e_down_prev)))
    def _():
        def cast(c, carry):
            sl = pl.ds(pl.multiple_of(c * rows, rows), rows)
            wo_bf[sl, :] = wo_ref[sl, :].astype(BF16)
            return carry

        lax.fori_loop(0, D_FF // rows, cast, 0)

    def up_half():
        hi, lo = _unpack_bf16_pairs(x_ref[...])
        x = jnp.concatenate([hi.astype(BF16), lo.astype(BF16)], axis=1)
        h = jnp.dot(x, wi_bf[...], preferred_element_type=F32) + bi_ref[...]
        gate = jnp.minimum(h[:, :D_FF], SWIGLU_LIMIT)
        lin = jnp.clip(h[:, D_FF:], -SWIGLU_LIMIT, SWIGLU_LIMIT)
        act = gate * (0.5 * jnp.tanh((0.5 * SWIGLU_ALPHA) * gate) + 0.5) * (lin + 1.0)
        act_scr[i % 2] = act.astype(BF16)

    def down_half():
        o = jnp.dot(act_scr[(i + 1) % 2], wo_bf[...], preferred_element_type=F32) + bo_ref[...]
        o_ref[...] = _pack_bf16_pairs(o)

    @pl.when(jnp.logical_and(up, down))
    def _():
        down_half()
        up_half()

    @pl.when(jnp.logical_and(up, jnp.logical_not(down)))
    def _():
        up_half()

    @pl.when(jnp.logical_and(down, jnp.logical_not(up)))
    def _():
        down_half()


def _experts(block_expert, n_used, buf, w_in, b_in, w_out, b_out, blk):
    n_rows, w = buf.shape
    n_blocks = n_rows // blk
    last = n_blocks - 1
    up_blk = lambda i, be, nu: (jnp.minimum(i, last), 0)
    up_exp = lambda i, be, nu: (be[jnp.minimum(i, last)], 0, 0)
    down_blk = lambda i, be, nu: (jnp.maximum(i - 1, 0), 0)
    down_exp = lambda i, be, nu: (be[jnp.maximum(i - 1, 0)], 0, 0)
    return pl.pallas_call(
        _expert_kernel,
        out_shape=jax.ShapeDtypeStruct((n_rows, w), buf.dtype),
        grid_spec=pltpu.PrefetchScalarGridSpec(
            num_scalar_prefetch=2,
            grid=(n_blocks + 1,),
            in_specs=[
                pl.BlockSpec((blk, w), up_blk),
                pl.BlockSpec((None, D_MODEL, 2 * D_FF), up_exp),
                pl.BlockSpec((None, 1, 2 * D_FF), up_exp),
                pl.BlockSpec((None, D_FF, D_MODEL), down_exp),
                pl.BlockSpec((None, 1, D_MODEL), down_exp),
            ],
            out_specs=pl.BlockSpec((blk, w), down_blk),
            scratch_shapes=[pltpu.VMEM((D_MODEL, 2 * D_FF), BF16), pltpu.VMEM((D_FF, D_MODEL), BF16),
                            pltpu.VMEM((2, blk, D_FF), BF16)],
        ),
        compiler_params=_cp(("arbitrary",)),
        name="expert_ffn",
    )(block_expert, n_used, buf, w_in, b_in, w_out, b_out)


def _combine_kernel(x_ref, og_ref, g_ref, y_ref):
    g = g_ref[...]
    acc_hi = x_ref[:, :HALF]
    acc_lo = x_ref[:, HALF:]
    for k in range(TOP_K):
        hi, lo = _unpack_bf16_pairs(og_ref[k])
        gk = g[:, k:k + 1]
        acc_hi = acc_hi + gk * hi
        acc_lo = acc_lo + gk * lo
    y_ref[:, :HALF] = acc_hi
    y_ref[:, HALF:] = acc_lo


def _combine(x_mid, og, gates_tk, bm=1024):
    t = x_mid.shape[0]
    return pl.pallas_call(
        _combine_kernel,
        out_shape=jax.ShapeDtypeStruct((t, D_MODEL), F32),
        grid=(t // bm,),
        in_specs=[
            pl.BlockSpec((bm, D_MODEL), lambda i: (i, 0)),
            pl.BlockSpec((TOP_K, bm, HALF), lambda i: (0, i, 0)),
            pl.BlockSpec((bm, TOP_K), lambda i: (i, 0)),
        ],
        out_specs=pl.BlockSpec((bm, D_MODEL), lambda i: (i, 0)),
        compiler_params=_cp(("parallel",)),
        name="moe_combine",
    )(x_mid, og, gates_tk)


def _block_tables(counts, n_blocks, blk):
    padded = (counts + blk - 1) // blk * blk
    pad_ends = jnp.cumsum(padded)
    pad_starts = pad_ends - padded
    block_start = jnp.arange(n_blocks, dtype=I32) * blk
    block_expert = jnp.minimum(jnp.sum(pad_ends[None, :] <= block_start[:, None], axis=1), N_EXPERTS - 1)
    n_used = (pad_ends[-1] // blk).reshape(1)
    return pad_starts.astype(I32), block_expert.astype(I32), n_used.astype(I32)


def _layer(x, mem, consts, params, after=None):
    b, seq, d = x.shape
    m_len = mem.shape[1]
    t = b * seq
    x2 = x.reshape(t, d)
    gain = params["norm_mix"]
    if after is not None:
        gain = gain + 0.0 * after.astype(F32)
    proj = _in_proj(x2, gain, params["w_in"])
    proj3 = proj.reshape(b, seq, IN_WIDTH)
    ret = _retention(proj3, params["ret_out_norm"], consts["cos"], consts["sin"], consts["tabs"],
                     consts["decays"])
    att = _window_attention(proj3, params["att_q_norm"], params["att_k_norm"], params["att_sink"],
                            consts["bias"])
    mkv = _mem_kv(mem.reshape(b * m_len, d), params["mem_norm"], params["w_mem_kv"],
                  params["mem_k_norm"])
    mo = _mem_attention(proj3, mkv.reshape(b, m_len, 2 * MEM_HEADS * MEM_D), params["mem_q_norm"])
    x_mid, hp, logits_t = _merge(x2, proj, ret.reshape(t, d), att.reshape(t, d), mo.reshape(t, d),
                               params["w_branch"], params["w_out"], params["norm_ffn"],
                               params["w_router"])
    eid, rank, gates, counts = _route(logits_t, params["b_router"], consts["tri"])
    blk = EXPERT_BLOCK
    n_blocks = t * TOP_K // blk + N_EXPERTS
    pad_starts, block_expert, n_used = _block_tables(counts[:, 0].astype(I32), n_blocks, blk)
    slot = _slots(pad_starts, eid, rank)
    buf = _dispatch_rows(hp, slot, n_blocks * blk)
    out_rows = _experts(block_expert, n_used, buf, params["w_e_in"], params["b_e_in"],
                        params["w_e_out"], params["b_e_out"], blk)
    og = _gather_rows(out_rows, slot)
    y = _combine(x_mid, og, gates.T)
    return y.reshape(b, seq, d), slot[0, 0]


def _router_weights(w):
    w_hi = w.astype(BF16)
    w_lo = (w - w_hi.astype(F32)).astype(BF16)
    pad = jnp.zeros((D_MODEL, ROUTER_LANES - 2 * N_EXPERTS), BF16)
    top = jnp.concatenate([w_hi, w_lo, pad], axis=1)
    bottom = jnp.concatenate([w_hi, jnp.zeros_like(w_lo), pad], axis=1)
    return jnp.concatenate([top, bottom], axis=0)


def _prepare(seq, rel_table, norm_mix, w_in, ret_out_norm, att_q_norm, att_k_norm, att_sink, mem_norm,
             w_mem_kv, mem_q_norm, mem_k_norm, w_branch, w_out, norm_ffn, w_router, b_router,
             w_e_in, b_e_in, w_e_out, b_e_out):
    gates_at = IN_WIDTH - 3 * D_MODEL
    w_in0 = w_in[0]
    params = {
        "norm_mix": norm_mix[0].reshape(1, -1),
        "w_in": jnp.concatenate([w_in0[:, gates_at:], w_in0[:, :gates_at]], axis=1).astype(BF16),
        "ret_out_norm": ret_out_norm[0].reshape(1, -1),
        "att_q_norm": att_q_norm[0].reshape(1, -1),
        "att_k_norm": att_k_norm[0].reshape(1, -1),
        "att_sink": att_sink[0].astype(F32),
        "mem_norm": mem_norm[0].reshape(1, -1),
        "w_mem_kv": w_mem_kv[0].astype(BF16),
        "mem_q_norm": mem_q_norm[0].reshape(1, -1),
        "mem_k_norm": mem_k_norm[0].reshape(1, -1),
        "w_branch": w_branch[0].astype(BF16),
        "w_out": w_out[0].astype(BF16),
        "norm_ffn": norm_ffn[0].reshape(1, -1),
        "w_router": _router_weights(w_router[0]),
        "b_router": b_router[0].reshape(-1, 1),
        "w_e_in": w_e_in[0],
        "b_e_in": b_e_in[0].reshape(N_EXPERTS, 1, -1),
        "w_e_out": w_e_out[0],
        "b_e_out": b_e_out[0].reshape(N_EXPERTS, 1, -1),
    }
    cos, sin, tabs, decays = _retention_tables(seq)
    route_bt = 1024
    tri = (jnp.arange(route_bt)[:, None] < jnp.arange(route_bt)[None, :]).astype(BF16)
    consts = {"cos": cos, "sin": sin, "tabs": tabs, "decays": decays,
              "bias": _window_bias(rel_table), "tri": tri}
    return consts, params


def kernel(x_prompt, x_sample, mem_prompt, mem_sample, rel_table, norm_mix, w_in, ret_out_norm, att_q_norm, att_k_norm, att_sink, mem_norm, w_mem_kv, mem_q_norm, mem_k_norm, w_branch, w_out, norm_ffn, w_router, b_router, w_e_in, b_e_in, w_e_out, b_e_out):
    consts, params = _prepare(x_prompt.shape[1], rel_table, norm_mix, w_in, ret_out_norm, att_q_norm,
                              att_k_norm, att_sink, mem_norm, w_mem_kv, mem_q_norm, mem_k_norm,
                              w_branch, w_out, norm_ffn, w_router, b_router, w_e_in, b_e_in,
                              w_e_out, b_e_out)
    y_sample, first_slot = _layer(x_sample, mem_sample, consts, params)
    y_prompt, _ = _layer(x_prompt, mem_prompt, consts, params, after=first_slot)
    return (y_prompt, y_sample)
```

```python
import functools
import math

import jax
import jax.numpy as jnp
import numpy as np
from jax import lax
from jax.experimental import pallas as pl
from jax.experimental.pallas import tpu as pltpu
from jax.experimental.pallas import tpu_sc as plsc

F32 = jnp.float32
BF16 = jnp.bfloat16
I32 = jnp.int32

D_MODEL = 1024
EPS = 1e-6
RET_HEADS = 4
RET_QK = 128
RET_V = 256
CHUNK = 128
ROPE_BASE = 10000.0
DECAY_FWD = 5.0
DECAY_BWD = 5.5
ATT_HEADS = 8
ATT_KV = 2
ATT_GROUP = ATT_HEADS // ATT_KV
ATT_D = 128
WINDOW = 128
REL_BUCKETS = 32
REL_MAX_DIST = 128
MEM_HEADS = 4
MEM_D = 256
N_EXPERTS = 32
TOP_K = 4
D_FF = 1024
SWIGLU_LIMIT = 7.0
SWIGLU_ALPHA = 1.702
EXPERT_BLOCK = 512
HALF = D_MODEL // 2
ROUTER_LANES = 128

IN_WIDTH = 8704
OFF_GATES = 0
OFF_RQ = 3072
OFF_RK = 3584
OFF_RV = 4096
OFF_RG = 5120
OFF_AQ = 6144
OFF_AK = 7168
OFF_AV = 7424
OFF_MQ = 7680

VMEM_LIMIT = 56 * 1024 * 1024
LOG2E = 1.4426950408889634


def _cp(sem, vmem=VMEM_LIMIT):
    return pltpu.CompilerParams(dimension_semantics=sem, vmem_limit_bytes=vmem)


def _rms(x, g):
    return x * lax.rsqrt(jnp.mean(x * x, axis=-1, keepdims=True) + EPS) * g


def _pack_bf16_pairs(x):
    w = x.shape[1] // 2
    bits = pltpu.bitcast(x.astype(BF16).astype(F32), I32)
    hi = bits[:, :w] & jnp.int32(-65536)
    lo = lax.shift_right_logical(bits[:, w:], jnp.int32(16))
    return hi | lo


def _unpack_bf16_pairs(u):
    hi = pltpu.bitcast(u & jnp.int32(-65536), F32)
    lo = pltpu.bitcast(lax.shift_left(u, jnp.int32(16)), F32)
    return hi, lo


def _in_proj_kernel(x_ref, g_ref, w_ref, o_ref, h_scr):
    @pl.when(pl.program_id(1) == 0)
    def _():
        h_scr[...] = _rms(x_ref[...], g_ref[...]).astype(BF16)

    sub = 256

    def cols(c, carry):
        sl = pl.ds(pl.multiple_of(c * sub, sub), sub)
        o_ref[:, sl] = jnp.dot(h_scr[...], w_ref[:, sl], preferred_element_type=F32).astype(BF16)
        return carry

    lax.fori_loop(0, o_ref.shape[1] // sub, cols, 0, unroll=True)


def _in_proj(x2, gain, w_bf16, bm=1024, bn=4352):
    t = x2.shape[0]
    n = w_bf16.shape[1]
    return pl.pallas_call(
        _in_proj_kernel,
        out_shape=jax.ShapeDtypeStruct((t, n), BF16),
        grid=(t // bm, n // bn),
        in_specs=[
            pl.BlockSpec((bm, D_MODEL), lambda i, j: (i, 0)),
            pl.BlockSpec((1, D_MODEL), lambda i, j: (0, 0)),
            pl.BlockSpec((D_MODEL, bn), lambda i, j: (0, j)),
        ],
        out_specs=pl.BlockSpec((bm, bn), lambda i, j: (i, j)),
        scratch_shapes=[pltpu.VMEM((bm, D_MODEL), BF16)],
        compiler_params=_cp(("parallel", "arbitrary")),
        name="in_proj",
    )(x2, gain, w_bf16)


def _retention_kernel(dec_ref, q_ref, k_ref, v_ref, g_ref, cos_ref, sin_ref, tab_ref, gain_ref,
                      o_ref, lhs, rhs, acc, kvf, kvb):
    h = pl.program_id(0)
    seq = q_ref.shape[0]
    n_chunks = seq // CHUNK
    dec_f = dec_ref[2 * h]
    dec_b = dec_ref[2 * h + 1]
    nt = (((1,), (1,)), ((), ()))

    def rope(x, c, s):
        return x * c + pltpu.roll(x, RET_QK // 2, 1) * s

    def intra(n, carry):
        rows = pl.ds(pl.multiple_of(n * CHUNK, CHUNK), CHUNK)
        c = cos_ref[rows, :]
        s = sin_ref[rows, :]
        q = rope(q_ref[rows, :].astype(F32), c, s)
        k = rope(k_ref[rows, :].astype(F32), c, s) * (RET_QK ** -0.5)
        lhs[rows, 0:CHUNK] = (q * tab_ref[1]).astype(BF16)
        lhs[rows, CHUNK:2 * CHUNK] = (q * tab_ref[3]).astype(BF16)
        sc = lax.dot_general(q.astype(BF16), k.astype(BF16), nt, preferred_element_type=F32)
        a = jnp.concatenate([(sc * tab_ref[0]).astype(BF16),
                             (k * tab_ref[2]).T.astype(BF16),
                             (k * tab_ref[4]).T.astype(BF16)], axis=0)
        r = jnp.dot(a, v_ref[rows, :], preferred_element_type=F32)
        acc[rows, :] = r[0:CHUNK]
        kvf[n] = r[CHUNK:2 * CHUNK]
        kvb[n] = r[2 * CHUNK:3 * CHUNK]
        return carry

    lax.fori_loop(0, n_chunks, intra, 0, unroll=8)

    def scan_f(n, state):
        rhs[n, 0:CHUNK, :] = state.astype(BF16)
        return state * dec_f + kvf[n]

    lax.fori_loop(0, n_chunks, scan_f, jnp.zeros((RET_QK, RET_V), F32))

    def scan_b(i, state):
        n = n_chunks - 1 - i
        rhs[n, CHUNK:2 * CHUNK, :] = state.astype(BF16)
        return state * dec_b + kvb[n]

    lax.fori_loop(0, n_chunks, scan_b, jnp.zeros((RET_QK, RET_V), F32))

    def cross(n, carry):
        rows = pl.ds(pl.multiple_of(n * CHUNK, CHUNK), CHUNK)
        o = acc[rows, :] + jnp.dot(lhs[rows, :], rhs[n], preferred_element_type=F32)
        y = _rms(o, gain_ref[...])
        g = g_ref[rows, :].astype(F32)
        o_ref[rows, :] = (y * (g * (0.5 * jnp.tanh(0.5 * g) + 0.5))).astype(BF16)
        return carry

    lax.fori_loop(0, n_chunks, cross, 0, unroll=True)


def _retention(proj3, gain, cos, sin, tabs, decays):
    b, seq, _ = proj3.shape
    return pl.pallas_call(
        _retention_kernel,
        out_shape=jax.ShapeDtypeStruct((b, seq, RET_HEADS * RET_V), BF16),
        grid=(RET_HEADS, b),
        in_specs=[
            pl.BlockSpec(memory_space=pltpu.SMEM),
            pl.BlockSpec((None, seq, RET_QK), lambda h, i: (i, 0, OFF_RQ // RET_QK + h)),
            pl.BlockSpec((None, seq, RET_QK), lambda h, i: (i, 0, OFF_RK // RET_QK + h)),
            pl.BlockSpec((None, seq, RET_V), lambda h, i: (i, 0, OFF_RV // RET_V + h)),
            pl.BlockSpec((None, seq, RET_V), lambda h, i: (i, 0, OFF_RG // RET_V + h)),
            pl.BlockSpec((seq, RET_QK), lambda h, i: (0, 0)),
            pl.BlockSpec((seq, RET_QK), lambda h, i: (0, 0)),
            pl.BlockSpec((None, 5, CHUNK, CHUNK), lambda h, i: (h, 0, 0, 0)),
            pl.BlockSpec((1, RET_V), lambda h, i: (0, h)),
        ],
        out_specs=pl.BlockSpec((None, seq, RET_V), lambda h, i: (i, 0, h)),
        scratch_shapes=[
            pltpu.VMEM((seq, 2 * CHUNK), BF16),
            pltpu.VMEM((seq // CHUNK, 2 * CHUNK, RET_V), BF16),
            pltpu.VMEM((seq, RET_V), F32),
            pltpu.VMEM((seq // CHUNK, RET_QK, RET_V), F32),
            pltpu.VMEM((seq // CHUNK, RET_QK, RET_V), F32),
        ],
        compiler_params=_cp(("parallel", "parallel")),
        name="retention",
    )(decays, proj3, proj3, proj3, proj3, cos, sin, tabs, gain)


def _retention_tables(seq):
    half = RET_QK // 2
    inv = ROPE_BASE ** (-jnp.arange(half, dtype=F32) / half)
    ang = jnp.arange(seq, dtype=F32)[:, None] * inv[None, :]
    cos = jnp.concatenate([jnp.cos(ang), jnp.cos(ang)], axis=1)
    sin = jnp.concatenate([-jnp.sin(ang), jnp.sin(ang)], axis=1)
    heads = jnp.arange(RET_HEADS, dtype=F32)
    lg_f = jnp.log1p(-jnp.exp2(-DECAY_FWD - heads))[:, None, None]
    lg_b = jnp.log1p(-jnp.exp2(-DECAY_BWD - heads))[:, None, None]
    idx = jnp.arange(CHUNK, dtype=F32)
    diff = (idx[:, None] - idx[None, :])[None]
    dmat = jnp.where(diff >= 0, jnp.exp(jnp.where(diff >= 0, diff, 0.0) * lg_f),
                     jnp.exp(jnp.where(diff < 0, -diff, 0.0) * lg_b))
    col = jnp.broadcast_to(idx[None, :, None], (RET_HEADS, CHUNK, CHUNK))
    xi_f = jnp.exp((col + 1.0) * lg_f)
    zeta_f = jnp.exp((CHUNK - 1.0 - col) * lg_f)
    xi_b = jnp.exp((CHUNK - col) * lg_b)
    zeta_b = jnp.exp(col * lg_b)
    tabs = jnp.stack([dmat, xi_f, zeta_f, xi_b, zeta_b], axis=1)
    decays = jnp.stack([jnp.exp(CHUNK * lg_f[:, 0, 0]), jnp.exp(CHUNK * lg_b[:, 0, 0])], axis=1).reshape(-1)
    return cos, sin, tabs, decays


def _window_kernel(sink_ref, q_ref, k_ref, v_ref, qg_ref, kg_ref, bias_ref, o_ref,
                   qn, kt, vp, s_scr, p_scr, e_scr):
    g = pl.program_id(0)
    seq = q_ref.shape[0]
    n_blocks = seq // WINDOW
    stack = ATT_GROUP * WINDOW
    unit = 64
    heads = [slice(r * ATT_D, (r + 1) * ATT_D) for r in range(ATT_GROUP)]
    kt[:, 0:WINDOW] = jnp.zeros((ATT_D, WINDOW), BF16)
    kt[:, seq + WINDOW:seq + 2 * WINDOW] = jnp.zeros((ATT_D, WINDOW), BF16)
    vp[0:WINDOW, :] = jnp.zeros((WINDOW, 2 * ATT_D), BF16)
    vp[seq + WINDOW:seq + 2 * WINDOW, :] = jnp.zeros((WINDOW, 2 * ATT_D), BF16)

    def start(n, size):
        return n * size if isinstance(n, int) else pl.multiple_of(n * size, size)

    def prep(n):
        r0 = start(n, WINDOW)
        src = pl.ds(r0, WINDOW)
        dst = pl.ds(r0 + WINDOW, WINDOW)
        kt[:, dst] = _rms(k_ref[src, :].astype(F32), kg_ref[...]).T.astype(BF16)
        vp[dst, 0:ATT_D] = v_ref[src, :]
        vp[dst, ATT_D:2 * ATT_D] = jnp.ones((WINDOW, ATT_D), BF16)
        for hs in heads:
            q = _rms(q_ref[src, hs].astype(F32), qg_ref[...]) * (ATT_D ** -0.5 * LOG2E)
            qn[src, hs] = q.astype(BF16)

    def scores(n):
        r0 = start(n, WINDOW)
        kb = kt[:, pl.ds(r0, 3 * WINDOW)]
        q4 = jnp.concatenate([qn[pl.ds(r0, WINDOW), hs] for hs in heads], axis=0)
        if isinstance(n, int):
            edge = 0 if n == 0 else (2 if n == n_blocks - 1 else 1)
        else:
            edge = jnp.where(n == 0, 0, jnp.where(n == n_blocks - 1, 2, 1))
        s_scr[pl.ds(start(n, stack), stack), :] = (
            jnp.dot(q4, kb, preferred_element_type=F32) + bias_ref[edge])

    def softmax(n):
        base = start(n, stack)
        for u in range(stack // unit):
            rows = pl.ds(base + u * unit, unit)
            sink = sink_ref[g * ATT_GROUP + (u * unit) // WINDOW] * LOG2E
            s = s_scr[rows, :]
            m = jnp.maximum(jnp.max(s, axis=-1, keepdims=True), sink)
            mb = jnp.broadcast_to(m, (unit, WINDOW))
            p_scr[rows, :] = jnp.exp2(s - jnp.concatenate([mb, mb, mb], axis=1)).astype(BF16)
            e_scr[rows, :] = jnp.exp2(sink - mb)

    def outputs(n):
        r0 = start(n, WINDOW)
        rows = pl.ds(start(n, stack), stack)
        oe = jnp.dot(p_scr[rows, :], vp[pl.ds(r0, 3 * WINDOW), :], preferred_element_type=F32)
        o = oe[:, 0:ATT_D] / (oe[:, ATT_D:2 * ATT_D] + e_scr[rows, :])
        for r, hs in enumerate(heads):
            o_ref[pl.ds(r0, WINDOW), hs] = o[r * WINDOW:(r + 1) * WINDOW, :].astype(BF16)

    prep(0)
    prep(1)
    scores(0)
    prep(2)
    softmax(0)
    scores(1)
    prep(3)

    def steady(n, carry):
        outputs(n - 4)
        softmax(n - 3)
        scores(n - 2)
        prep(n)
        return carry

    lax.fori_loop(4, n_blocks, steady, 0, unroll=2)
    last = n_blocks - 1
    outputs(last - 3)
    softmax(last - 2)
    scores(last - 1)
    outputs(last - 2)
    softmax(last - 1)
    scores(last)
    outputs(last - 1)
    softmax(last)
    outputs(last)


def _window_attention(proj3, q_gain, k_gain, sink, bias):
    b, seq, _ = proj3.shape
    gw = ATT_GROUP * ATT_D
    return pl.pallas_call(
        _window_kernel,
        out_shape=jax.ShapeDtypeStruct((b, seq, ATT_HEADS * ATT_D), BF16),
        grid=(ATT_KV, b),
        in_specs=[
            pl.BlockSpec(memory_space=pltpu.SMEM),
            pl.BlockSpec((None, seq, gw), lambda g, i: (i, 0, OFF_AQ // gw + g)),
            pl.BlockSpec((None, seq, ATT_D), lambda g, i: (i, 0, OFF_AK // ATT_D + g)),
            pl.BlockSpec((None, seq, ATT_D), lambda g, i: (i, 0, OFF_AV // ATT_D + g)),
            pl.BlockSpec((1, ATT_D), lambda g, i: (0, 0)),
            pl.BlockSpec((1, ATT_D), lambda g, i: (0, 0)),
            pl.BlockSpec((None, 3, ATT_GROUP * WINDOW, 3 * WINDOW), lambda g, i: (g, 0, 0, 0)),
        ],
        out_specs=pl.BlockSpec((None, seq, gw), lambda g, i: (i, 0, g)),
        scratch_shapes=[
            pltpu.VMEM((seq, gw), BF16),
            pltpu.VMEM((ATT_D, seq + 2 * WINDOW), BF16),
            pltpu.VMEM((seq + 2 * WINDOW, 2 * ATT_D), BF16),
            pltpu.VMEM((seq * ATT_GROUP, 3 * WINDOW), F32),
            pltpu.VMEM((seq * ATT_GROUP, 3 * WINDOW), BF16),
            pltpu.VMEM((seq * ATT_GROUP, WINDOW), F32),
        ],
        compiler_params=_cp(("parallel", "parallel")),
        name="window_attention",
    )(sink, proj3, proj3, proj3, q_gain, k_gain, bias)


def _window_bias(rel_table):
    nb = REL_BUCKETS // 2
    max_exact = nb // 2
    qi = jnp.arange(WINDOW)[:, None]
    ki = jnp.arange(3 * WINDOW)[None, :]
    rel = ki - WINDOW - qi
    ret = jnp.where(rel > 0, nb, 0)
    n = jnp.abs(rel)
    large = max_exact + (jnp.log(jnp.maximum(n, 1).astype(F32) / max_exact)
                         / math.log(REL_MAX_DIST / max_exact) * (nb - max_exact)).astype(I32)
    large = jnp.minimum(large, nb - 1)
    bucket = ret + jnp.where(n < max_exact, n, large)
    onehot = (bucket[None] == jnp.arange(REL_BUCKETS)[:, None, None]).astype(F32)
    bias = jnp.einsum("bh,bqk->hqk", rel_table.astype(F32), onehot, precision=lax.Precision.HIGHEST)
    bias = jnp.where((jnp.abs(rel) <= WINDOW)[None], bias * LOG2E, -jnp.inf)
    bias = bias.reshape(ATT_KV, ATT_GROUP * WINDOW, 3 * WINDOW)
    first = jnp.where(ki < WINDOW, -jnp.inf, bias)
    last = jnp.where(ki >= 2 * WINDOW, -jnp.inf, bias)
    return jnp.stack([first, bias, last], axis=1)


def _mem_kv_kernel(m_ref, g_ref, w_ref, kg_ref, o_ref):
    h = _rms(m_ref[...], g_ref[...]).astype(BF16)
    kw = MEM_HEADS * MEM_D
    for j in range(MEM_HEADS):
        cols = slice(j * MEM_D, (j + 1) * MEM_D)
        k = jnp.dot(h, w_ref[:, cols], preferred_element_type=F32)
        o_ref[:, cols] = _rms(k, kg_ref[...]).astype(BF16)
    o_ref[:, kw:] = jnp.dot(h, w_ref[:, kw:], preferred_element_type=F32).astype(BF16)


def _mem_kv(mem2, gain, w_bf16, k_gain, bm=512):
    rows = mem2.shape[0]
    bm = min(bm, rows)
    n = w_bf16.shape[1]
    return pl.pallas_call(
        _mem_kv_kernel,
        out_shape=jax.ShapeDtypeStruct((rows, n), BF16),
        grid=(rows // bm,),
        in_specs=[
            pl.BlockSpec((bm, D_MODEL), lambda i: (i, 0)),
            pl.BlockSpec((1, D_MODEL), lambda i: (0, 0)),
            pl.BlockSpec((D_MODEL, n), lambda i: (0, 0)),
            pl.BlockSpec((1, MEM_D), lambda i: (0, 0)),
        ],
        out_specs=pl.BlockSpec((bm, n), lambda i: (i, 0)),
        compiler_params=_cp(("parallel",)),
        name="mem_kv",
    )(mem2, gain, w_bf16, k_gain)


def _mem_attn_kernel(q_ref, k_ref, v_ref, qg_ref, o_ref, kt, *, bq):
    seq = q_ref.shape[0]
    kt[...] = k_ref[...].astype(F32).T.astype(BF16)

    def block(n, carry):
        rows = pl.ds(pl.multiple_of(n * bq, bq), bq)
        q = _rms(q_ref[rows, :].astype(F32), qg_ref[...]) * (MEM_D ** -0.5)
        s = jnp.dot(q.astype(BF16), kt[...], preferred_element_type=F32)
        p = jnp.exp(s - jnp.max(s, axis=-1, keepdims=True))
        denom = jnp.sum(p, axis=-1, keepdims=True)
        o = jnp.dot(p.astype(BF16), v_ref[...], preferred_element_type=F32) / denom
        o_ref[rows, :] = o.astype(BF16)
        return carry

    lax.fori_loop(0, seq // bq, block, 0, unroll=8)


def _mem_attention(proj3, mkv3, q_gain, bq=256):
    b, seq, _ = proj3.shape
    m_len = mkv3.shape[1]
    return pl.pallas_call(
        functools.partial(_mem_attn_kernel, bq=bq),
        out_shape=jax.ShapeDtypeStruct((b, seq, MEM_HEADS * MEM_D), BF16),
        grid=(b, MEM_HEADS),
        in_specs=[
            pl.BlockSpec((None, seq, MEM_D), lambda i, h: (i, 0, OFF_MQ // MEM_D + h)),
            pl.BlockSpec((None, m_len, MEM_D), lambda i, h: (i, 0, h)),
            pl.BlockSpec((None, m_len, MEM_D), lambda i, h: (i, 0, MEM_HEADS + h)),
            pl.BlockSpec((1, MEM_D), lambda i, h: (0, 0)),
        ],
        out_specs=pl.BlockSpec((None, seq, MEM_D), lambda i, h: (i, 0, h)),
        scratch_shapes=[pltpu.VMEM((MEM_D, m_len), BF16)],
        compiler_params=_cp(("parallel", "parallel")),
        name="mem_attention",
    )(proj3, mkv3, mkv3, q_gain)


def _merge_kernel(x_ref, g0_ref, g1_ref, g2_ref, b0_ref, b1_ref, b2_ref, wb_ref, wo_ref, gn_ref,
                  wr_ref, xo_ref, hp_ref, lg_ref, lg_scr, x_scr):
    i = pl.program_id(0)
    n = pl.num_programs(0) - 1

    def mix():
        merged = None
        for gate_ref, br, j in ((g0_ref, b0_ref, 0), (g1_ref, b1_ref, 1), (g2_ref, b2_ref, 2)):
            gate = 0.5 * jnp.tanh(0.5 * gate_ref[...].astype(F32)) + 0.5
            t = gate * jnp.dot(br[...], wb_ref[j], preferred_element_type=F32)
            merged = t if merged is None else merged + t
        x = x_ref[...] + jnp.dot(merged.astype(BF16), wo_ref[...], preferred_element_type=F32)
        xo_ref[...] = x
        x_scr[i % 2] = x

    def route_inputs():
        h = _rms(x_scr[(i + 1) % 2], gn_ref[...])
        h_hi = h.astype(BF16)
        h_lo = (h - h_hi.astype(F32)).astype(BF16)
        lg_scr[...] = jnp.dot(jnp.concatenate([h_hi, h_lo], axis=1), wr_ref[...],
                              preferred_element_type=F32)
        lg_ref[...] = lg_scr[...].T
        hp_ref[...] = _pack_bf16_pairs(h)

    @pl.when(jnp.logical_and(i >= 1, i < n))
    def _():
        route_inputs()
        mix()

    @pl.when(i == 0)
    def _():
        mix()

    @pl.when(i == n)
    def _():
        route_inputs()


def _merge(x2, proj, ret, att, mo, wb, wo, gain, w_router_cat, bm=512):
    t = x2.shape[0]
    last = t // bm - 1
    const2 = lambda i: (0, 0)
    row = lambda i: (jnp.minimum(i, last), 0)
    lag = lambda i: (jnp.maximum(i - 1, 0), 0)
    return pl.pallas_call(
        _merge_kernel,
        out_shape=(
            jax.ShapeDtypeStruct((t, D_MODEL), F32),
            jax.ShapeDtypeStruct((t, HALF), I32),
            jax.ShapeDtypeStruct((ROUTER_LANES, t), F32),
        ),
        grid=(t // bm + 1,),
        in_specs=[
            pl.BlockSpec((bm, D_MODEL), row),
            pl.BlockSpec((bm, D_MODEL), lambda i: (jnp.minimum(i, last), 0)),
            pl.BlockSpec((bm, D_MODEL), lambda i: (jnp.minimum(i, last), 1)),
            pl.BlockSpec((bm, D_MODEL), lambda i: (jnp.minimum(i, last), 2)),
            pl.BlockSpec((bm, D_MODEL), row),
            pl.BlockSpec((bm, D_MODEL), row),
            pl.BlockSpec((bm, D_MODEL), row),
            pl.BlockSpec((3, D_MODEL, D_MODEL), lambda i: (0, 0, 0)),
            pl.BlockSpec((D_MODEL, D_MODEL), const2),
            pl.BlockSpec((1, D_MODEL), const2),
            pl.BlockSpec((2 * D_MODEL, ROUTER_LANES), const2),
        ],
        out_specs=(
            pl.BlockSpec((bm, D_MODEL), row),
            pl.BlockSpec((bm, HALF), lag),
            pl.BlockSpec((ROUTER_LANES, bm), lambda i: (0, jnp.maximum(i - 1, 0))),
        ),
        scratch_shapes=[pltpu.VMEM((bm, ROUTER_LANES), F32), pltpu.VMEM((2, bm, D_MODEL), F32)],
        compiler_params=_cp(("arbitrary",)),
        name="merge_router",
    )(x2, proj, proj, proj, ret, att, mo, wb, wo, gain, w_router_cat)


def _route_kernel(l_ref, b_ref, tri_ref, eid_ref, rank_ref, gate_ref, cnt_ref):
    @pl.when(pl.program_id(0) == 0)
    def _():
        cnt_ref[...] = jnp.zeros_like(cnt_ref)

    l = l_ref[0:N_EXPERTS, :] + l_ref[N_EXPERTS:2 * N_EXPERTS, :] + b_ref[...]
    ne, bt = l.shape
    iota_e = lax.broadcasted_iota(I32, (ne, bt), 0)
    picked = jnp.zeros((ne, bt), jnp.bool_)
    vals, idxs = [], []
    for _ in range(TOP_K):
        m = jnp.max(l, axis=0, keepdims=True)
        idx = jnp.min(jnp.where(l == m, iota_e, ne), axis=0, keepdims=True)
        sel = iota_e == idx
        picked = picked | sel
        l = jnp.where(sel, -jnp.inf, l)
        vals.append(m)
        idxs.append(idx)
    ex = [jnp.exp(v - vals[0]) for v in vals]
    tot = ex[0] + ex[1] + ex[2] + ex[3]
    onehot = jnp.where(picked, 1.0, 0.0)
    before = jnp.dot(onehot.astype(BF16), tri_ref[...], preferred_element_type=F32) + cnt_ref[:, 0:1]
    for k in range(TOP_K):
        eid_ref[k:k + 1, :] = idxs[k]
        gate_ref[k:k + 1, :] = ex[k] / tot
        rank_ref[k:k + 1, :] = jnp.sum(jnp.where(iota_e == idxs[k], before, 0.0), axis=0,
                                       keepdims=True).astype(I32)
    cnt_ref[...] = cnt_ref[...] + jnp.sum(onehot, axis=1, keepdims=True)


def _route(logits_t, bias_col, tri):
    bt = tri.shape[0]
    ne, t = N_EXPERTS, logits_t.shape[1]
    blk = lambda i: (0, i)
    return pl.pallas_call(
        _route_kernel,
        out_shape=(
            jax.ShapeDtypeStruct((TOP_K, t), I32),
            jax.ShapeDtypeStruct((TOP_K, t), I32),
            jax.ShapeDtypeStruct((TOP_K, t), F32),
            jax.ShapeDtypeStruct((ne, 128), F32),
        ),
        grid=(t // bt,),
        in_specs=[pl.BlockSpec((ROUTER_LANES, bt), blk), pl.BlockSpec((ne, 1), lambda i: (0, 0)),
                  pl.BlockSpec((bt, bt), lambda i: (0, 0))],
        out_specs=(
            pl.BlockSpec((TOP_K, bt), blk),
            pl.BlockSpec((TOP_K, bt), blk),
            pl.BlockSpec((TOP_K, bt), blk),
            pl.BlockSpec((ne, 128), lambda i: (0, 0)),
        ),
        compiler_params=_cp(("arbitrary",)),
        name="route_topk",
    )(logits_t, bias_col, tri)


def _slot_kernel(start_ref, eid_ref, rank_ref, slot_ref):
    eid = eid_ref[...]
    base = jnp.zeros(eid.shape, I32)
    for e in range(N_EXPERTS):
        base = jnp.where(eid == e, start_ref[e], base)
    slot_ref[...] = base + rank_ref[...]


def _slots(pad_starts, eid, rank, bt=2048):
    t = eid.shape[1]
    blk = lambda i, s: (0, i)
    return pl.pallas_call(
        _slot_kernel,
        out_shape=jax.ShapeDtypeStruct((TOP_K, t), I32),
        grid_spec=pltpu.PrefetchScalarGridSpec(
            num_scalar_prefetch=1,
            grid=(t // bt,),
            in_specs=[pl.BlockSpec((TOP_K, bt), blk), pl.BlockSpec((TOP_K, bt), blk)],
            out_specs=pl.BlockSpec((TOP_K, bt), blk),
        ),
        compiler_params=_cp(("parallel",)),
        name="route_slots",
    )(pad_starts, eid, rank)


def _sc_workers():
    info = plsc.get_sparse_core_info()
    return info.num_cores, info.num_subcores


def _dispatch_rows(hp, slot, n_rows, win=64):
    t, w = hp.shape
    nc, ns = _sc_workers()
    per_worker = t // (nc * ns)
    mesh = plsc.VectorSubcoreMesh(core_axis_name="c", subcore_axis_name="s")

    @functools.partial(
        pl.kernel,
        out_type=jax.ShapeDtypeStruct((n_rows, w), hp.dtype),
        mesh=mesh,
        scratch_types=[pltpu.VMEM((win,), I32)] * TOP_K + [pltpu.VMEM((win, w), hp.dtype)],
        name="moe_dispatch",
    )
    def k(hp_hbm, slot_hbm, buf_hbm, *scratch):
        idx_v, rows_v = scratch[:TOP_K], scratch[TOP_K]
        wid = lax.axis_index("s") * nc + lax.axis_index("c")
        base = wid * per_worker

        @pl.loop(0, per_worker // win)
        def _(i):
            off = pl.multiple_of(base + i * win, win)
            pltpu.sync_copy(hp_hbm.at[pl.ds(off, win)], rows_v)
            for kk in range(TOP_K):
                pltpu.sync_copy(slot_hbm.at[pl.ds(kk * t + off, win)], idx_v[kk])
                pltpu.sync_copy(rows_v, buf_hbm.at[idx_v[kk]])

    return k(hp, slot.reshape(-1))


def _gather_rows(out_rows, slot, win=64):
    t = slot.shape[1]
    w = out_rows.shape[1]
    nc, ns = _sc_workers()
    per_worker = t // (nc * ns)
    mesh = plsc.VectorSubcoreMesh(core_axis_name="c", subcore_axis_name="s")

    @functools.partial(
        pl.kernel,
        out_type=jax.ShapeDtypeStruct((TOP_K * t, w), out_rows.dtype),
        mesh=mesh,
        scratch_types=[pltpu.VMEM((win,), I32), pltpu.VMEM((win, w), out_rows.dtype)],
        name="moe_gather",
    )
    def k(rows_hbm, slot_hbm, og_hbm, idx_v, rows_v):
        wid = lax.axis_index("s") * nc + lax.axis_index("c")
        base = wid * per_worker

        @pl.loop(0, per_worker // win)
        def _(i):
            off = pl.multiple_of(base + i * win, win)
            for kk in range(TOP_K):
                pltpu.sync_copy(slot_hbm.at[pl.ds(kk * t + off, win)], idx_v)
                pltpu.sync_copy(rows_hbm.at[idx_v], rows_v)
                pltpu.sync_copy(rows_v, og_hbm.at[pl.ds(kk * t + off, win)])

    return k(out_rows, slot.reshape(-1)).reshape(TOP_K, t, w)


def _expert_kernel(be_ref, nu_ref, x_ref, wi_ref, bi_ref, wo_ref, bo_ref, o_ref, wi_bf, wo_bf, act_scr):
    i = pl.program_id(0)
    n_used = nu_ref[0]
    last = be_ref.shape[0] - 1
    e_up = be_ref[jnp.minimum(i, last)]
    e_up_prev = be_ref[jnp.clip(i - 1, 0, last)]
    e_down_prev = be_ref[jnp.clip(i - 2, 0, last)]
    up = i < n_used
    down = jnp.logical_and(i >= 1, i <= n_used)
    rows = 128

    @pl.when(jnp.logical_and(up, jnp.logical_or(i == 0, e_up != e_up_prev)))
    def _():
        def cast(c, carry):
            sl = pl.ds(pl.multiple_of(c * rows, rows), rows)
            wi_bf[sl, :] = wi_ref[sl, :].astype(BF16)
            return carry

        lax.fori_loop(0, D_MODEL // rows, cast, 0)

    @pl.when(jnp.logical_and(down, jnp.logical_or(i == 1, e_up_prev != e_down_prev)))
    def _():
        def cast(c, carry):
            sl = pl.ds(pl.multiple_of(c * rows, rows), rows)
            wo_bf[sl, :] = wo_ref[sl, :].astype(BF16)
            return carry

        lax.fori_loop(0, D_FF // rows, cast, 0)

    def up_half():
        hi, lo = _unpack_bf16_pairs(x_ref[...])
        x = jnp.concatenate([hi.astype(BF16), lo.astype(BF16)], axis=1)
        h = jnp.dot(x, wi_bf[...], preferred_element_type=F32) + bi_ref[...]
        gate = jnp.minimum(h[:, :D_FF], SWIGLU_LIMIT)
        lin = jnp.clip(h[:, D_FF:], -SWIGLU_LIMIT, SWIGLU_LIMIT)
        act = gate * (0.5 * jnp.tanh((0.5 * SWIGLU_ALPHA) * gate) + 0.5) * (lin + 1.0)
        act_scr[i % 2] = act.astype(BF16)

    def down_half():
        o = jnp.dot(act_scr[(i + 1) % 2], wo_bf[...], preferred_element_type=F32) + bo_ref[...]
        o_ref[...] = _pack_bf16_pairs(o)

    @pl.when(jnp.logical_and(up, down))
    def _():
        down_half()
        up_half()

    @pl.when(jnp.logical_and(up, jnp.logical_not(down)))
    def _():
        up_half()

    @pl.when(jnp.logical_and(down, jnp.logical_not(up)))
    def _():
        down_half()


def _experts(block_expert, n_used, buf, w_in, b_in, w_out, b_out, blk):
    n_rows, w = buf.shape
    n_blocks = n_rows // blk
    last = n_blocks - 1
    up_blk = lambda i, be, nu: (jnp.minimum(i, last), 0)
    up_exp = lambda i, be, nu: (be[jnp.minimum(i, last)], 0, 0)
    down_blk = lambda i, be, nu: (jnp.maximum(i - 1, 0), 0)
    down_exp = lambda i, be, nu: (be[jnp.maximum(i - 1, 0)], 0, 0)
    return pl.pallas_call(
        _expert_kernel,
        out_shape=jax.ShapeDtypeStruct((n_rows, w), buf.dtype),
        grid_spec=pltpu.PrefetchScalarGridSpec(
            num_scalar_prefetch=2,
            grid=(n_blocks + 1,),
            in_specs=[
                pl.BlockSpec((blk, w), up_blk),
                pl.BlockSpec((None, D_MODEL, 2 * D_FF), up_exp),
                pl.BlockSpec((None, 1, 2 * D_FF), up_exp),
                pl.BlockSpec((None, D_FF, D_MODEL), down_exp),
                pl.BlockSpec((None, 1, D_MODEL), down_exp),
            ],
            out_specs=pl.BlockSpec((blk, w), down_blk),
            scratch_shapes=[pltpu.VMEM((D_MODEL, 2 * D_FF), BF16), pltpu.VMEM((D_FF, D_MODEL), BF16),
                            pltpu.VMEM((2, blk, D_FF), BF16)],
        ),
        compiler_params=_cp(("arbitrary",)),
        name="expert_ffn",
    )(block_expert, n_used, buf, w_in, b_in, w_out, b_out)


def _combine_kernel(x_ref, og_ref, g_ref, y_ref):
    g = g_ref[...]
    acc_hi = x_ref[:, :HALF]
    acc_lo = x_ref[:, HALF:]
    for k in range(TOP_K):
        hi, lo = _unpack_bf16_pairs(og_ref[k])
        gk = g[:, k:k + 1]
        acc_hi = acc_hi + gk * hi
        acc_lo = acc_lo + gk * lo
    y_ref[:, :HALF] = acc_hi
    y_ref[:, HALF:] = acc_lo


def _combine(x_mid, og, gates_tk, bm=1024):
    t = x_mid.shape[0]
    return pl.pallas_call(
        _combine_kernel,
        out_shape=jax.ShapeDtypeStruct((t, D_MODEL), F32),
        grid=(t // bm,),
        in_specs=[
            pl.BlockSpec((bm, D_MODEL), lambda i: (i, 0)),
            pl.BlockSpec((TOP_K, bm, HALF), lambda i: (0, i, 0)),
            pl.BlockSpec((bm, TOP_K), lambda i: (i, 0)),
        ],
        out_specs=pl.BlockSpec((bm, D_MODEL), lambda i: (i, 0)),
        compiler_params=_cp(("parallel",)),
        name="moe_combine",
    )(x_mid, og, gates_tk)


def _block_tables(counts, n_blocks, blk):
    padded = (counts + blk - 1) // blk * blk
    pad_ends = jnp.cumsum(padded)
    pad_starts = pad_ends - padded
    block_start = jnp.arange(n_blocks, dtype=I32) * blk
    block_expert = jnp.minimum(jnp.sum(pad_ends[None, :] <= block_start[:, None], axis=1), N_EXPERTS - 1)
    n_used = (pad_ends[-1] // blk).reshape(1)
    return pad_starts.astype(I32), block_expert.astype(I32), n_used.astype(I32)


def _layer(x, mem, consts, params, after=None):
    b, seq, d = x.shape
    m_len = mem.shape[1]
    t = b * seq
    x2 = x.reshape(t, d)
    gain = params["norm_mix"]
    if after is not None:
        gain = gain + 0.0 * after.astype(F32)
    proj = _in_proj(x2, gain, params["w_in"])
    proj3 = proj.reshape(b, seq, IN_WIDTH)
    ret = _retention(proj3, params["ret_out_norm"], consts["cos"], consts["sin"], consts["tabs"],
                     consts["decays"])
    att = _window_attention(proj3, params["att_q_norm"], params["att_k_norm"], params["att_sink"],
                            consts["bias"])
    mkv = _mem_kv(mem.reshape(b * m_len, d), params["mem_norm"], params["w_mem_kv"],
                  params["mem_k_norm"])
    mo = _mem_attention(proj3, mkv.reshape(b, m_len, 2 * MEM_HEADS * MEM_D), params["mem_q_norm"])
    x_mid, hp, logits_t = _merge(x2, proj, ret.reshape(t, d), att.reshape(t, d), mo.reshape(t, d),
                               params["w_branch"], params["w_out"], params["norm_ffn"],
                               params["w_router"])
    eid, rank, gates, counts = _route(logits_t, params["b_router"], consts["tri"])
    blk = EXPERT_BLOCK
    n_blocks = t * TOP_K // blk + N_EXPERTS
    pad_starts, block_expert, n_used = _block_tables(counts[:, 0].astype(I32), n_blocks, blk)
    slot = _slots(pad_starts, eid, rank)
    buf = _dispatch_rows(hp, slot, n_blocks * blk)
    out_rows = _experts(block_expert, n_used, buf, params["w_e_in"], params["b_e_in"],
                        params["w_e_out"], params["b_e_out"], blk)
    og = _gather_rows(out_rows, slot)
    y = _combine(x_mid, og, gates.T)
    return y.reshape(b, seq, d), slot[0, 0]


def _router_weights(w):
    w_hi = w.astype(BF16)
    w_lo = (w - w_hi.astype(F32)).astype(BF16)
    pad = jnp.zeros((D_MODEL, ROUTER_LANES - 2 * N_EXPERTS), BF16)
    top = jnp.concatenate([w_hi, w_lo, pad], axis=1)
    bottom = jnp.concatenate([w_hi, jnp.zeros_like(w_lo), pad], axis=1)
    return jnp.concatenate([top, bottom], axis=0)


def _prepare(seq, rel_table, norm_mix, w_in, ret_out_norm, att_q_norm, att_k_norm, att_sink, mem_norm,
             w_mem_kv, mem_q_norm, mem_k_norm, w_branch, w_out, norm_ffn, w_router, b_router,
             w_e_in, b_e_in, w_e_out, b_e_out):
    gates_at = IN_WIDTH - 3 * D_MODEL
    w_in0 = w_in[0]
    params = {
        "norm_mix": norm_mix[0].reshape(1, -1),
        "w_in": jnp.concatenate([w_in0[:, gates_at:], w_in0[:, :gates_at]], axis=1).astype(BF16),
        "ret_out_norm": ret_out_norm[0].reshape(1, -1),
        "att_q_norm": att_q_norm[0].reshape(1, -1),
        "att_k_norm": att_k_norm[0].reshape(1, -1),
        "att_sink": att_sink[0].astype(F32),
        "mem_norm": mem_norm[0].reshape(1, -1),
        "w_mem_kv": w_mem_kv[0].astype(BF16),
        "mem_q_norm": mem_q_norm[0].reshape(1, -1),
        "mem_k_norm": mem_k_norm[0].reshape(1, -1),
        "w_branch": w_branch[0].astype(BF16),
        "w_out": w_out[0].astype(BF16),
        "norm_ffn": norm_ffn[0].reshape(1, -1),
        "w_router": _router_weights(w_router[0]),
        "b_router": b_router[0].reshape(-1, 1),
        "w_e_in": w_e_in[0],
        "b_e_in": b_e_in[0].reshape(N_EXPERTS, 1, -1),
        "w_e_out": w_e_out[0],
        "b_e_out": b_e_out[0].reshape(N_EXPERTS, 1, -1),
    }
    cos, sin, tabs, decays = _retention_tables(seq)
    route_bt = 1024
    tri = (jnp.arange(route_bt)[:, None] < jnp.arange(route_bt)[None, :]).astype(BF16)
    consts = {"cos": cos, "sin": sin, "tabs": tabs, "decays": decays,
              "bias": _window_bias(rel_table), "tri": tri}
    return consts, params


def kernel(x_prompt, x_sample, mem_prompt, mem_sample, rel_table, norm_mix, w_in, ret_out_norm, att_q_norm, att_k_norm, att_sink, mem_norm, w_mem_kv, mem_q_norm, mem_k_norm, w_branch, w_out, norm_ffn, w_router, b_router, w_e_in, b_e_in, w_e_out, b_e_out):
    consts, params = _prepare(x_prompt.shape[1], rel_table, norm_mix, w_in, ret_out_norm, att_q_norm,
                              att_k_norm, att_sink, mem_norm, w_mem_kv, mem_q_norm, mem_k_norm,
                              w_branch, w_out, norm_ffn, w_router, b_router, w_e_in, b_e_in,
                              w_e_out, b_e_out)
    y_sample, first_slot = _layer(x_sample, mem_sample, consts, params)
    y_prompt, _ = _layer(x_prompt, mem_prompt, consts, params, after=first_slot)
    return (y_prompt, y_sample)
```

```python
import functools
import math

import jax
import jax.numpy as jnp
import numpy as np
from jax import lax
from jax.experimental import pallas as pl
from jax.experimental.pallas import tpu as pltpu
from jax.experimental.pallas import tpu_sc as plsc

F32 = jnp.float32
BF16 = jnp.bfloat16
I32 = jnp.int32

D_MODEL = 1024
EPS = 1e-6
RET_HEADS = 4
RET_QK = 128
RET_V = 256
CHUNK = 128
ROPE_BASE = 10000.0
DECAY_FWD = 5.0
DECAY_BWD = 5.5
ATT_HEADS = 8
ATT_KV = 2
ATT_GROUP = ATT_HEADS // ATT_KV
ATT_D = 128
WINDOW = 128
REL_BUCKETS = 32
REL_MAX_DIST = 128
MEM_HEADS = 4
MEM_D = 256
N_EXPERTS = 32
TOP_K = 4
D_FF = 1024
SWIGLU_LIMIT = 7.0
SWIGLU_ALPHA = 1.702
EXPERT_BLOCK = 512
HALF = D_MODEL // 2
ROUTER_LANES = 128

IN_WIDTH = 8704
OFF_GATES = 0
OFF_RQ = 3072
OFF_RK = 3584
OFF_RV = 4096
OFF_RG = 5120
OFF_AQ = 6144
OFF_AK = 7168
OFF_AV = 7424
OFF_MQ = 7680

VMEM_LIMIT = 56 * 1024 * 1024
LOG2E = 1.4426950408889634


def _cp(sem, vmem=VMEM_LIMIT):
    return pltpu.CompilerParams(dimension_semantics=sem, vmem_limit_bytes=vmem)


def _rms(x, g):
    return x * lax.rsqrt(jnp.mean(x * x, axis=-1, keepdims=True) + EPS) * g


def _pack_bf16_pairs(x):
    w = x.shape[1] // 2
    bits = pltpu.bitcast(x.astype(BF16).astype(F32), I32)
    hi = bits[:, :w] & jnp.int32(-65536)
    lo = lax.shift_right_logical(bits[:, w:], jnp.int32(16))
    return hi | lo


def _unpack_bf16_pairs(u):
    hi = pltpu.bitcast(u & jnp.int32(-65536), F32)
    lo = pltpu.bitcast(lax.shift_left(u, jnp.int32(16)), F32)
    return hi, lo


IN_PROJ_CHUNK = 256


def _in_proj_kernel(x_ref, g_ref, w_ref, hg_ref, cos_ref, sin_ref, o_ref, h_scr, *, plans):
    j = pl.program_id(1)

    @pl.when(j == 0)
    def _():
        h_scr[...] = _rms(x_ref[...], g_ref[...]).astype(BF16)

    def per_head(r, width, fn):
        return jnp.concatenate([fn(r[:, k * width:(k + 1) * width]) for k in range(IN_PROJ_CHUNK // width)],
                               axis=1)

    def column_step(plan):
        for c in range(o_ref.shape[1] // IN_PROJ_CHUNK):
            sl = slice(c * IN_PROJ_CHUNK, (c + 1) * IN_PROJ_CHUNK)
            r = jnp.dot(h_scr[...], w_ref[:, sl], preferred_element_type=F32)
            kind = plan.get(c, ("plain",))
            if kind[0] == "norm":
                gain = hg_ref[kind[1]:kind[1] + 1, 0:kind[2]]
                r = per_head(r, kind[2], lambda x: _rms(x, gain))
            elif kind[0] == "rope":
                r = per_head(r, RET_QK, lambda x: (x * cos_ref[...] + pltpu.roll(x, RET_QK // 2, 1)
                                                   * sin_ref[...]) * kind[1])
            elif kind[0] == "silu":
                r = r * (0.5 * jnp.tanh(0.5 * r) + 0.5)
            o_ref[:, sl] = r.astype(BF16)

    for step, plan in enumerate(plans):
        pl.when(j == step)(functools.partial(column_step, plan))


def _epilogue_plans(bn, n):
    regions = (
        (OFF_AQ, ATT_HEADS * ATT_D, ("norm", 0, ATT_D)),
        (OFF_AK, ATT_KV * ATT_D, ("norm", 1, ATT_D)),
        (OFF_MQ, MEM_HEADS * MEM_D, ("norm", 2, MEM_D)),
        (OFF_RQ, RET_HEADS * RET_QK, ("rope", 1.0)),
        (OFF_RK, RET_HEADS * RET_QK, ("rope", RET_QK ** -0.5)),
        (OFF_RG, RET_HEADS * RET_V, ("silu",)),
    )
    plans = [dict() for _ in range(n // bn)]
    for off, width, kind in regions:
        assert off % IN_PROJ_CHUNK == 0 and width % IN_PROJ_CHUNK == 0 and bn % IN_PROJ_CHUNK == 0
        for col in range(off, off + width, IN_PROJ_CHUNK):
            plans[col // bn][(col % bn) // IN_PROJ_CHUNK] = kind
    return plans


def _in_proj(x2, gain, w_bf16, head_gains, cos, sin, bm=1024, bn=4352):
    t = x2.shape[0]
    n = w_bf16.shape[1]
    seq_tiles = cos.shape[0] // bm
    return pl.pallas_call(
        functools.partial(_in_proj_kernel, plans=_epilogue_plans(bn, n)),
        out_shape=jax.ShapeDtypeStruct((t, n), BF16),
        grid=(t // bm, n // bn),
        in_specs=[
            pl.BlockSpec((bm, D_MODEL), lambda i, j: (i, 0)),
            pl.BlockSpec((1, D_MODEL), lambda i, j: (0, 0)),
            pl.BlockSpec((D_MODEL, bn), lambda i, j: (0, j)),
            pl.BlockSpec(head_gains.shape, lambda i, j: (0, 0)),
            pl.BlockSpec((bm, RET_QK), lambda i, j: (i % seq_tiles, 0)),
            pl.BlockSpec((bm, RET_QK), lambda i, j: (i % seq_tiles, 0)),
        ],
        out_specs=pl.BlockSpec((bm, bn), lambda i, j: (i, j)),
        scratch_shapes=[pltpu.VMEM((bm, D_MODEL), BF16)],
        compiler_params=_cp(("parallel", "arbitrary")),
        name="in_proj",
    )(x2, gain, w_bf16, head_gains, cos, sin)


def _retention_kernel(dec_ref, q_ref, k_ref, v_ref, g_ref, tab_ref, gain_ref, o_ref, lhs, rhs, acc, kvf, kvb):
    h = pl.program_id(0)
    seq = q_ref.shape[0]
    n_chunks = seq // CHUNK
    dec_f = dec_ref[2 * h]
    dec_b = dec_ref[2 * h + 1]
    nt = (((1,), (1,)), ((), ()))

    def intra(n, carry):
        rows = pl.ds(pl.multiple_of(n * CHUNK, CHUNK), CHUNK)
        q = q_ref[rows, :].astype(F32)
        k = k_ref[rows, :].astype(F32)
        lhs[rows, 0:CHUNK] = (q * tab_ref[1]).astype(BF16)
        lhs[rows, CHUNK:2 * CHUNK] = (q * tab_ref[3]).astype(BF16)
        sc = lax.dot_general(q_ref[rows, :], k_ref[rows, :], nt, preferred_element_type=F32)
        a = jnp.concatenate([(sc * tab_ref[0]).astype(BF16),
                             (k * tab_ref[2]).T.astype(BF16),
                             (k * tab_ref[4]).T.astype(BF16)], axis=0)
        r = jnp.dot(a, v_ref[rows, :], preferred_element_type=F32)
        acc[rows, :] = r[0:CHUNK]
        kvf[n] = r[CHUNK:2 * CHUNK]
        kvb[n] = r[2 * CHUNK:3 * CHUNK]
        return carry

    lax.fori_loop(0, n_chunks, intra, 0, unroll=8)

    def scan_f(n, state):
        rhs[n, 0:CHUNK, :] = state.astype(BF16)
        return state * dec_f + kvf[n]

    lax.fori_loop(0, n_chunks, scan_f, jnp.zeros((RET_QK, RET_V), F32))

    def scan_b(i, state):
        n = n_chunks - 1 - i
        rhs[n, CHUNK:2 * CHUNK, :] = state.astype(BF16)
        return state * dec_b + kvb[n]

    lax.fori_loop(0, n_chunks, scan_b, jnp.zeros((RET_QK, RET_V), F32))

    def cross(n, carry):
        rows = pl.ds(pl.multiple_of(n * CHUNK, CHUNK), CHUNK)
        o = acc[rows, :] + jnp.dot(lhs[rows, :], rhs[n], preferred_element_type=F32)
        y = _rms(o, gain_ref[...])
        o_ref[rows, :] = (y * g_ref[rows, :].astype(F32)).astype(BF16)
        return carry

    lax.fori_loop(0, n_chunks, cross, 0, unroll=True)


def _retention(proj3, gain, tabs, decays):
    b, seq, _ = proj3.shape
    return pl.pallas_call(
        _retention_kernel,
        out_shape=jax.ShapeDtypeStruct((b, seq, RET_HEADS * RET_V), BF16),
        grid=(RET_HEADS, b),
        in_specs=[
            pl.BlockSpec(memory_space=pltpu.SMEM),
            pl.BlockSpec((None, seq, RET_QK), lambda h, i: (i, 0, OFF_RQ // RET_QK + h)),
            pl.BlockSpec((None, seq, RET_QK), lambda h, i: (i, 0, OFF_RK // RET_QK + h)),
            pl.BlockSpec((None, seq, RET_V), lambda h, i: (i, 0, OFF_RV // RET_V + h)),
            pl.BlockSpec((None, seq, RET_V), lambda h, i: (i, 0, OFF_RG // RET_V + h)),
            pl.BlockSpec((None, 5, CHUNK, CHUNK), lambda h, i: (h, 0, 0, 0)),
            pl.BlockSpec((1, RET_V), lambda h, i: (0, h)),
        ],
        out_specs=pl.BlockSpec((None, seq, RET_V), lambda h, i: (i, 0, h)),
        scratch_shapes=[
            pltpu.VMEM((seq, 2 * CHUNK), BF16),
            pltpu.VMEM((seq // CHUNK, 2 * CHUNK, RET_V), BF16),
            pltpu.VMEM((seq, RET_V), F32),
            pltpu.VMEM((seq // CHUNK, RET_QK, RET_V), F32),
            pltpu.VMEM((seq // CHUNK, RET_QK, RET_V), F32),
        ],
        compiler_params=_cp(("parallel", "parallel")),
        name="retention",
    )(decays, proj3, proj3, proj3, proj3, tabs, gain)


def _retention_tables(seq):
    half = RET_QK // 2
    inv = ROPE_BASE ** (-jnp.arange(half, dtype=F32) / half)
    ang = jnp.arange(seq, dtype=F32)[:, None] * inv[None, :]
    cos = jnp.concatenate([jnp.cos(ang), jnp.cos(ang)], axis=1)
    sin = jnp.concatenate([-jnp.sin(ang), jnp.sin(ang)], axis=1)
    heads = jnp.arange(RET_HEADS, dtype=F32)
    lg_f = jnp.log1p(-jnp.exp2(-DECAY_FWD - heads))[:, None, None]
    lg_b = jnp.log1p(-jnp.exp2(-DECAY_BWD - heads))[:, None, None]
    idx = jnp.arange(CHUNK, dtype=F32)
    diff = (idx[:, None] - idx[None, :])[None]
    dmat = jnp.where(diff >= 0, jnp.exp(jnp.where(diff >= 0, diff, 0.0) * lg_f),
                     jnp.exp(jnp.where(diff < 0, -diff, 0.0) * lg_b))
    col = jnp.broadcast_to(idx[None, :, None], (RET_HEADS, CHUNK, CHUNK))
    xi_f = jnp.exp((col + 1.0) * lg_f)
    zeta_f = jnp.exp((CHUNK - 1.0 - col) * lg_f)
    xi_b = jnp.exp((CHUNK - col) * lg_b)
    zeta_b = jnp.exp(col * lg_b)
    tabs = jnp.stack([dmat, xi_f, zeta_f, xi_b, zeta_b], axis=1)
    decays = jnp.stack([jnp.exp(CHUNK * lg_f[:, 0, 0]), jnp.exp(CHUNK * lg_b[:, 0, 0])], axis=1).reshape(-1)
    return cos, sin, tabs, decays


def _window_kernel(sink_ref, q_ref, k_ref, v_ref, bias_ref, o_ref, kt, vp, s_scr, p_scr, e_scr):
    g = pl.program_id(0)
    seq = q_ref.shape[0]
    n_blocks = seq // WINDOW
    stack = ATT_GROUP * WINDOW
    unit = 64
    heads = [slice(r * ATT_D, (r + 1) * ATT_D) for r in range(ATT_GROUP)]
    kt[:, 0:WINDOW] = jnp.zeros((ATT_D, WINDOW), BF16)
    kt[:, seq + WINDOW:seq + 2 * WINDOW] = jnp.zeros((ATT_D, WINDOW), BF16)
    vp[0:WINDOW, :] = jnp.zeros((WINDOW, 2 * ATT_D), BF16)
    vp[seq + WINDOW:seq + 2 * WINDOW, :] = jnp.zeros((WINDOW, 2 * ATT_D), BF16)

    ones = jnp.ones((WINDOW, ATT_D), BF16)

    def start(n, size):
        return n * size if isinstance(n, int) else pl.multiple_of(n * size, size)

    def prep(n):
        r0 = start(n, WINDOW)
        src = pl.ds(r0, WINDOW)
        dst = pl.ds(r0 + WINDOW, WINDOW)
        vp[dst, 0:ATT_D] = v_ref[src, :]
        vp[dst, ATT_D:2 * ATT_D] = ones
        kt[:, dst] = k_ref[src, :].astype(F32).T.astype(BF16)

    def scores(n):
        r0 = start(n, WINDOW)
        kb = kt[:, pl.ds(r0, 3 * WINDOW)]
        q4 = jnp.concatenate([q_ref[pl.ds(r0, WINDOW), hs] for hs in heads], axis=0)
        if isinstance(n, int):
            edge = 0 if n == 0 else (2 if n == n_blocks - 1 else 1)
        else:
            edge = jnp.where(n == 0, 0, jnp.where(n == n_blocks - 1, 2, 1))
        s_scr[pl.ds(start(n, stack), stack), :] = (
            jnp.dot(q4, kb, preferred_element_type=F32) + bias_ref[edge])

    def softmax(n):
        base = start(n, stack)
        for u in range(stack // unit):
            rows = pl.ds(base + u * unit, unit)
            sink = sink_ref[g * ATT_GROUP + (u * unit) // WINDOW] * LOG2E
            s = s_scr[rows, :]
            m = jnp.maximum(jnp.max(s, axis=-1, keepdims=True), sink)
            mb = jnp.broadcast_to(m, (unit, WINDOW))
            p_scr[rows, :] = jnp.exp2(s - jnp.concatenate([mb, mb, mb], axis=1)).astype(BF16)
            e_scr[rows, :] = jnp.exp2(sink - mb)

    def outputs(n):
        r0 = start(n, WINDOW)
        rows = pl.ds(start(n, stack), stack)
        oe = jnp.dot(p_scr[rows, :], vp[pl.ds(r0, 3 * WINDOW), :], preferred_element_type=F32)
        o = oe[:, 0:ATT_D] / (oe[:, ATT_D:2 * ATT_D] + e_scr[rows, :])
        for r, hs in enumerate(heads):
            o_ref[pl.ds(r0, WINDOW), hs] = o[r * WINDOW:(r + 1) * WINDOW, :].astype(BF16)

    prep(0)
    prep(1)
    scores(0)
    prep(2)
    softmax(0)
    scores(1)
    prep(3)

    def steady(n, carry):
        outputs(n - 4)
        softmax(n - 3)
        scores(n - 2)
        prep(n)
        return carry

    lax.fori_loop(4, n_blocks, steady, 0, unroll=2)
    last = n_blocks - 1
    outputs(last - 3)
    softmax(last - 2)
    scores(last - 1)
    outputs(last - 2)
    softmax(last - 1)
    scores(last)
    outputs(last - 1)
    softmax(last)
    outputs(last)


def _window_attention(proj3, sink, bias):
    b, seq, _ = proj3.shape
    gw = ATT_GROUP * ATT_D
    return pl.pallas_call(
        _window_kernel,
        out_shape=jax.ShapeDtypeStruct((b, seq, ATT_HEADS * ATT_D), BF16),
        grid=(ATT_KV, b),
        in_specs=[
            pl.BlockSpec(memory_space=pltpu.SMEM),
            pl.BlockSpec((None, seq, gw), lambda g, i: (i, 0, OFF_AQ // gw + g)),
            pl.BlockSpec((None, seq, ATT_D), lambda g, i: (i, 0, OFF_AK // ATT_D + g)),
            pl.BlockSpec((None, seq, ATT_D), lambda g, i: (i, 0, OFF_AV // ATT_D + g)),
            pl.BlockSpec((None, 3, ATT_GROUP * WINDOW, 3 * WINDOW), lambda g, i: (g, 0, 0, 0)),
        ],
        out_specs=pl.BlockSpec((None, seq, gw), lambda g, i: (i, 0, g)),
        scratch_shapes=[
            pltpu.VMEM((ATT_D, seq + 2 * WINDOW), BF16),
            pltpu.VMEM((seq + 2 * WINDOW, 2 * ATT_D), BF16),
            pltpu.VMEM((seq * ATT_GROUP, 3 * WINDOW), F32),
            pltpu.VMEM((seq * ATT_GROUP, 3 * WINDOW), BF16),
            pltpu.VMEM((seq * ATT_GROUP, WINDOW), F32),
        ],
        compiler_params=_cp(("parallel", "parallel")),
        name="window_attention",
    )(sink, proj3, proj3, proj3, bias)


def _window_bias(rel_table):
    nb = REL_BUCKETS // 2
    max_exact = nb // 2
    qi = jnp.arange(WINDOW)[:, None]
    ki = jnp.arange(3 * WINDOW)[None, :]
    rel = ki - WINDOW - qi
    ret = jnp.where(rel > 0, nb, 0)
    n = jnp.abs(rel)
    large = max_exact + (jnp.log(jnp.maximum(n, 1).astype(F32) / max_exact)
                         / math.log(REL_MAX_DIST / max_exact) * (nb - max_exact)).astype(I32)
    large = jnp.minimum(large, nb - 1)
    bucket = ret + jnp.where(n < max_exact, n, large)
    onehot = (bucket[None] == jnp.arange(REL_BUCKETS)[:, None, None]).astype(F32)
    bias = jnp.einsum("bh,bqk->hqk", rel_table.astype(F32), onehot, precision=lax.Precision.HIGHEST)
    bias = jnp.where((jnp.abs(rel) <= WINDOW)[None], bias * LOG2E, -jnp.inf)
    bias = bias.reshape(ATT_KV, ATT_GROUP * WINDOW, 3 * WINDOW)
    first = jnp.where(ki < WINDOW, -jnp.inf, bias)
    last = jnp.where(ki >= 2 * WINDOW, -jnp.inf, bias)
    return jnp.stack([first, bias, last], axis=1)


def _mem_kv_kernel(m_ref, g_ref, w_ref, kg_ref, o_ref):
    h = _rms(m_ref[...], g_ref[...]).astype(BF16)
    kw = MEM_HEADS * MEM_D
    for j in range(MEM_HEADS):
        cols = slice(j * MEM_D, (j + 1) * MEM_D)
        k = jnp.dot(h, w_ref[:, cols], preferred_element_type=F32)
        o_ref[:, cols] = _rms(k, kg_ref[...]).astype(BF16)
    o_ref[:, kw:] = jnp.dot(h, w_ref[:, kw:], preferred_element_type=F32).astype(BF16)


def _mem_kv(mem2, gain, w_bf16, k_gain, bm=512):
    rows = mem2.shape[0]
    bm = min(bm, rows)
    n = w_bf16.shape[1]
    return pl.pallas_call(
        _mem_kv_kernel,
        out_shape=jax.ShapeDtypeStruct((rows, n), BF16),
        grid=(rows // bm,),
        in_specs=[
            pl.BlockSpec((bm, D_MODEL), lambda i: (i, 0)),
            pl.BlockSpec((1, D_MODEL), lambda i: (0, 0)),
            pl.BlockSpec((D_MODEL, n), lambda i: (0, 0)),
            pl.BlockSpec((1, MEM_D), lambda i: (0, 0)),
        ],
        out_specs=pl.BlockSpec((bm, n), lambda i: (i, 0)),
        compiler_params=_cp(("parallel",)),
        name="mem_kv",
    )(mem2, gain, w_bf16, k_gain)


def _mem_attn_kernel(q_ref, k_ref, v_ref, o_ref, kt, *, bq):
    seq = q_ref.shape[0]
    kt[...] = k_ref[...].astype(F32).T.astype(BF16)

    def block(n, carry):
        rows = pl.ds(pl.multiple_of(n * bq, bq), bq)
        s = jnp.dot(q_ref[rows, :], kt[...], preferred_element_type=F32)
        p = jnp.exp(s - jnp.max(s, axis=-1, keepdims=True))
        denom = jnp.sum(p, axis=-1, keepdims=True)
        o = jnp.dot(p.astype(BF16), v_ref[...], preferred_element_type=F32) / denom
        o_ref[rows, :] = o.astype(BF16)
        return carry

    lax.fori_loop(0, seq // bq, block, 0, unroll=8)


def _mem_attention(proj3, mkv3, bq=256):
    b, seq, _ = proj3.shape
    m_len = mkv3.shape[1]
    return pl.pallas_call(
        functools.partial(_mem_attn_kernel, bq=bq),
        out_shape=jax.ShapeDtypeStruct((b, seq, MEM_HEADS * MEM_D), BF16),
        grid=(b, MEM_HEADS),
        in_specs=[
            pl.BlockSpec((None, seq, MEM_D), lambda i, h: (i, 0, OFF_MQ // MEM_D + h)),
            pl.BlockSpec((None, m_len, MEM_D), lambda i, h: (i, 0, h)),
            pl.BlockSpec((None, m_len, MEM_D), lambda i, h: (i, 0, MEM_HEADS + h)),
        ],
        out_specs=pl.BlockSpec((None, seq, MEM_D), lambda i, h: (i, 0, h)),
        scratch_shapes=[pltpu.VMEM((MEM_D, m_len), BF16)],
        compiler_params=_cp(("parallel", "parallel")),
        name="mem_attention",
    )(proj3, mkv3, mkv3)


def _merge_kernel(x_ref, g0_ref, g1_ref, g2_ref, b0_ref, b1_ref, b2_ref, wb_ref, wo_ref, gn_ref,
                  wr_ref, xo_ref, hp_ref, lg_ref, lg_scr, x_scr):
    i = pl.program_id(0)
    n = pl.num_programs(0) - 1

    def mix():
        merged = None
        for gate_ref, br, j in ((g0_ref, b0_ref, 0), (g1_ref, b1_ref, 1), (g2_ref, b2_ref, 2)):
            gate = 0.5 * jnp.tanh(0.5 * gate_ref[...].astype(F32)) + 0.5
            t = gate * jnp.dot(br[...], wb_ref[j], preferred_element_type=F32)
            merged = t if merged is None else merged + t
        x = x_ref[...] + jnp.dot(merged.astype(BF16), wo_ref[...], preferred_element_type=F32)
        xo_ref[...] = x
        x_scr[i % 2] = x

    def route_inputs():
        h = _rms(x_scr[(i + 1) % 2], gn_ref[...])
        h_hi = h.astype(BF16)
        h_lo = (h - h_hi.astype(F32)).astype(BF16)
        lg_scr[...] = jnp.dot(jnp.concatenate([h_hi, h_lo], axis=1), wr_ref[...],
                              preferred_element_type=F32)
        lg_ref[...] = lg_scr[...].T
        hp_ref[...] = _pack_bf16_pairs(h)

    @pl.when(jnp.logical_and(i >= 1, i < n))
    def _():
        route_inputs()
        mix()

    @pl.when(i == 0)
    def _():
        mix()

    @pl.when(i == n)
    def _():
        route_inputs()


def _merge(x2, proj, ret, att, mo, wb, wo, gain, w_router_cat, bm=512):
    t = x2.shape[0]
    last = t // bm - 1
    const2 = lambda i: (0, 0)
    row = lambda i: (jnp.minimum(i, last), 0)
    lag = lambda i: (jnp.maximum(i - 1, 0), 0)
    return pl.pallas_call(
        _merge_kernel,
        out_shape=(
            jax.ShapeDtypeStruct((t, D_MODEL), F32),
            jax.ShapeDtypeStruct((t, HALF), I32),
            jax.ShapeDtypeStruct((ROUTER_LANES, t), F32),
        ),
        grid=(t // bm + 1,),
        in_specs=[
            pl.BlockSpec((bm, D_MODEL), row),
            pl.BlockSpec((bm, D_MODEL), lambda i: (jnp.minimum(i, last), 0)),
            pl.BlockSpec((bm, D_MODEL), lambda i: (jnp.minimum(i, last), 1)),
            pl.BlockSpec((bm, D_MODEL), lambda i: (jnp.minimum(i, last), 2)),
            pl.BlockSpec((bm, D_MODEL), row),
            pl.BlockSpec((bm, D_MODEL), row),
            pl.BlockSpec((bm, D_MODEL), row),
            pl.BlockSpec((3, D_MODEL, D_MODEL), lambda i: (0, 0, 0)),
            pl.BlockSpec((D_MODEL, D_MODEL), const2),
            pl.BlockSpec((1, D_MODEL), const2),
            pl.BlockSpec((2 * D_MODEL, ROUTER_LANES), const2),
        ],
        out_specs=(
            pl.BlockSpec((bm, D_MODEL), row),
            pl.BlockSpec((bm, HALF), lag),
            pl.BlockSpec((ROUTER_LANES, bm), lambda i: (0, jnp.maximum(i - 1, 0))),
        ),
        scratch_shapes=[pltpu.VMEM((bm, ROUTER_LANES), F32), pltpu.VMEM((2, bm, D_MODEL), F32)],
        compiler_params=_cp(("arbitrary",)),
        name="merge_router",
    )(x2, proj, proj, proj, ret, att, mo, wb, wo, gain, w_router_cat)


def _route_kernel(l_ref, b_ref, tri_ref, eid_ref, rank_ref, gate_ref, cnt_ref):
    @pl.when(pl.program_id(0) == 0)
    def _():
        cnt_ref[...] = jnp.zeros_like(cnt_ref)

    l = l_ref[0:N_EXPERTS, :] + l_ref[N_EXPERTS:2 * N_EXPERTS, :] + b_ref[...]
    ne, bt = l.shape
    iota_e = lax.broadcasted_iota(I32, (ne, bt), 0)
    picked = jnp.zeros((ne, bt), jnp.bool_)
    vals, idxs = [], []
    for _ in range(TOP_K):
        m = jnp.max(l, axis=0, keepdims=True)
        idx = jnp.min(jnp.where(l == m, iota_e, ne), axis=0, keepdims=True)
        sel = iota_e == idx
        picked = picked | sel
        l = jnp.where(sel, -jnp.inf, l)
        vals.append(m)
        idxs.append(idx)
    ex = [jnp.exp(v - vals[0]) for v in vals]
    tot = ex[0] + ex[1] + ex[2] + ex[3]
    onehot = jnp.where(picked, 1.0, 0.0)
    before = jnp.dot(onehot.astype(BF16), tri_ref[...], preferred_element_type=F32) + cnt_ref[:, 0:1]
    for k in range(TOP_K):
        eid_ref[k:k + 1, :] = idxs[k]
        gate_ref[k:k + 1, :] = ex[k] / tot
        rank_ref[k:k + 1, :] = jnp.sum(jnp.where(iota_e == idxs[k], before, 0.0), axis=0,
                                       keepdims=True).astype(I32)
    cnt_ref[...] = cnt_ref[...] + jnp.sum(onehot, axis=1, keepdims=True)


def _route(logits_t, bias_col, tri):
    bt = tri.shape[0]
    ne, t = N_EXPERTS, logits_t.shape[1]
    blk = lambda i: (0, i)
    return pl.pallas_call(
        _route_kernel,
        out_shape=(
            jax.ShapeDtypeStruct((TOP_K, t), I32),
            jax.ShapeDtypeStruct((TOP_K, t), I32),
            jax.ShapeDtypeStruct((TOP_K, t), F32),
            jax.ShapeDtypeStruct((ne, 128), F32),
        ),
        grid=(t // bt,),
        in_specs=[pl.BlockSpec((ROUTER_LANES, bt), blk), pl.BlockSpec((ne, 1), lambda i: (0, 0)),
                  pl.BlockSpec((bt, bt), lambda i: (0, 0))],
        out_specs=(
            pl.BlockSpec((TOP_K, bt), blk),
            pl.BlockSpec((TOP_K, bt), blk),
            pl.BlockSpec((TOP_K, bt), blk),
            pl.BlockSpec((ne, 128), lambda i: (0, 0)),
        ),
        compiler_params=_cp(("arbitrary",)),
        name="route_topk",
    )(logits_t, bias_col, tri)


def _slot_kernel(start_ref, eid_ref, rank_ref, slot_ref):
    eid = eid_ref[...]
    base = jnp.zeros(eid.shape, I32)
    for e in range(N_EXPERTS):
        base = jnp.where(eid == e, start_ref[e], base)
    slot_ref[...] = base + rank_ref[...]


def _slots(pad_starts, eid, rank, bt=2048):
    t = eid.shape[1]
    blk = lambda i, s: (0, i)
    return pl.pallas_call(
        _slot_kernel,
        out_shape=jax.ShapeDtypeStruct((TOP_K, t), I32),
        grid_spec=pltpu.PrefetchScalarGridSpec(
            num_scalar_prefetch=1,
            grid=(t // bt,),
            in_specs=[pl.BlockSpec((TOP_K, bt), blk), pl.BlockSpec((TOP_K, bt), blk)],
            out_specs=pl.BlockSpec((TOP_K, bt), blk),
        ),
        compiler_params=_cp(("parallel",)),
        name="route_slots",
    )(pad_starts, eid, rank)


def _sc_workers():
    info = plsc.get_sparse_core_info()
    return info.num_cores, info.num_subcores


def _dispatch_rows(hp, slot, n_rows, win=64):
    t, w = hp.shape
    nc, ns = _sc_workers()
    per_worker = t // (nc * ns)
    mesh = plsc.VectorSubcoreMesh(core_axis_name="c", subcore_axis_name="s")

    @functools.partial(
        pl.kernel,
        out_type=jax.ShapeDtypeStruct((n_rows, w), hp.dtype),
        mesh=mesh,
        scratch_types=[pltpu.VMEM((win,), I32)] * TOP_K + [pltpu.VMEM((win, w), hp.dtype)],
        name="moe_dispatch",
    )
    def k(hp_hbm, slot_hbm, buf_hbm, *scratch):
        idx_v, rows_v = scratch[:TOP_K], scratch[TOP_K]
        wid = lax.axis_index("s") * nc + lax.axis_index("c")
        base = wid * per_worker

        @pl.loop(0, per_worker // win)
        def _(i):
            off = pl.multiple_of(base + i * win, win)
            pltpu.sync_copy(hp_hbm.at[pl.ds(off, win)], rows_v)
            for kk in range(TOP_K):
                pltpu.sync_copy(slot_hbm.at[pl.ds(kk * t + off, win)], idx_v[kk])
                pltpu.sync_copy(rows_v, buf_hbm.at[idx_v[kk]])

    return k(hp, slot.reshape(-1))


def _gather_rows(out_rows, slot, win=64):
    t = slot.shape[1]
    w = out_rows.shape[1]
    nc, ns = _sc_workers()
    per_worker = t // (nc * ns)
    mesh = plsc.VectorSubcoreMesh(core_axis_name="c", subcore_axis_name="s")

    @functools.partial(
        pl.kernel,
        out_type=jax.ShapeDtypeStruct((TOP_K * t, w), out_rows.dtype),
        mesh=mesh,
        scratch_types=[pltpu.VMEM((win,), I32), pltpu.VMEM((win, w), out_rows.dtype)],
        name="moe_gather",
    )
    def k(rows_hbm, slot_hbm, og_hbm, idx_v, rows_v):
        wid = lax.axis_index("s") * nc + lax.axis_index("c")
        base = wid * per_worker

        @pl.loop(0, per_worker // win)
        def _(i):
            off = pl.multiple_of(base + i * win, win)
            for kk in range(TOP_K):
                pltpu.sync_copy(slot_hbm.at[pl.ds(kk * t + off, win)], idx_v)
                pltpu.sync_copy(rows_hbm.at[idx_v], rows_v)
                pltpu.sync_copy(rows_v, og_hbm.at[pl.ds(kk * t + off, win)])

    return k(out_rows, slot.reshape(-1)).reshape(TOP_K, t, w)


def _expert_kernel(be_ref, nu_ref, x_ref, wi_ref, bi_ref, wo_ref, bo_ref, o_ref, wi_bf, wo_bf, act_scr):
    i = pl.program_id(0)
    n_used = nu_ref[0]
    last = be_ref.shape[0] - 1
    e_up = be_ref[jnp.minimum(i, last)]
    e_up_prev = be_ref[jnp.clip(i - 1, 0, last)]
    e_down_prev = be_ref[jnp.clip(i - 2, 0, last)]
    up = i < n_used
    down = jnp.logical_and(i >= 1, i <= n_used)
    rows = 128

    @pl.when(jnp.logical_and(up, jnp.logical_or(i == 0, e_up != e_up_prev)))
    def _():
        def cast(c, carry):
            sl = pl.ds(pl.multiple_of(c * rows, rows), rows)
            wi_bf[sl, :] = wi_ref[sl, :].astype(BF16)
            return carry

        lax.fori_loop(0, D_MODEL // rows, cast, 0)

    @pl.when(jnp.logical_and(down, jnp.logical_or(i == 1, e_up_prev != e_down_prev)))
    def _():
        def cast(c, carry):
            sl = pl.ds(pl.multiple_of(c * rows, rows), rows)
            wo_bf[sl, :] = wo_ref[sl, :].astype(BF16)
            return carry

        lax.fori_loop(0, D_FF // rows, cast, 0)

    def up_half():
        hi, lo = _unpack_bf16_pairs(x_ref[...])
        x = jnp.concatenate([hi.astype(BF16), lo.astype(BF16)], axis=1)
        h = jnp.dot(x, wi_bf[...], preferred_element_type=F32) + bi_ref[...]
        gate = jnp.minimum(h[:, :D_FF], SWIGLU_LIMIT)
        lin = jnp.clip(h[:, D_FF:], -SWIGLU_LIMIT, SWIGLU_LIMIT)
        act = gate * (0.5 * jnp.tanh((0.5 * SWIGLU_ALPHA) * gate) + 0.5) * (lin + 1.0)
        act_scr[i % 2] = act.astype(BF16)

    def down_half():
        o = jnp.dot(act_scr[(i + 1) % 2], wo_bf[...], preferred_element_type=F32) + bo_ref[...]
        o_ref[...] = _pack_bf16_pairs(o)

    @pl.when(jnp.logical_and(up, down))
    def _():
        down_half()
        up_half()

    @pl.when(jnp.logical_and(up, jnp.logical_not(down)))
    def _():
        up_half()

    @pl.when(jnp.logical_and(down, jnp.logical_not(up)))
    def _():
        down_half()


def _experts(block_expert, n_used, buf, w_in, b_in, w_out, b_out, blk):
    n_rows, w = buf.shape
    n_blocks = n_rows // blk
    last = n_blocks - 1
    up_blk = lambda i, be, nu: (jnp.minimum(i, last), 0)
    up_exp = lambda i, be, nu: (be[jnp.minimum(i, last)], 0, 0)
    down_blk = lambda i, be, nu: (jnp.maximum(i - 1, 0), 0)
    down_exp = lambda i, be, nu: (be[jnp.maximum(i - 1, 0)], 0, 0)
    return pl.pallas_call(
        _expert_kernel,
        out_shape=jax.ShapeDtypeStruct((n_rows, w), buf.dtype),
        grid_spec=pltpu.PrefetchScalarGridSpec(
            num_scalar_prefetch=2,
            grid=(n_blocks + 1,),
            in_specs=[
                pl.BlockSpec((blk, w), up_blk),
                pl.BlockSpec((None, D_MODEL, 2 * D_FF), up_exp),
                pl.BlockSpec((None, 1, 2 * D_FF), up_exp),
                pl.BlockSpec((None, D_FF, D_MODEL), down_exp),
                pl.BlockSpec((None, 1, D_MODEL), down_exp),
            ],
            out_specs=pl.BlockSpec((blk, w), down_blk),
            scratch_shapes=[pltpu.VMEM((D_MODEL, 2 * D_FF), BF16), pltpu.VMEM((D_FF, D_MODEL), BF16),
                            pltpu.VMEM((2, blk, D_FF), BF16)],
        ),
        compiler_params=_cp(("arbitrary",)),
        name="expert_ffn",
    )(block_expert, n_used, buf, w_in, b_in, w_out, b_out)


def _combine_kernel(x_ref, og_ref, g_ref, y_ref):
    g = g_ref[...]
    acc_hi = x_ref[:, :HALF]
    acc_lo = x_ref[:, HALF:]
    for k in range(TOP_K):
        hi, lo = _unpack_bf16_pairs(og_ref[k])
        gk = g[:, k:k + 1]
        acc_hi = acc_hi + gk * hi
        acc_lo = acc_lo + gk * lo
    y_ref[:, :HALF] = acc_hi
    y_ref[:, HALF:] = acc_lo


def _combine(x_mid, og, gates_tk, bm=1024):
    t = x_mid.shape[0]
    return pl.pallas_call(
        _combine_kernel,
        out_shape=jax.ShapeDtypeStruct((t, D_MODEL), F32),
        grid=(t // bm,),
        in_specs=[
            pl.BlockSpec((bm, D_MODEL), lambda i: (i, 0)),
            pl.BlockSpec((TOP_K, bm, HALF), lambda i: (0, i, 0)),
            pl.BlockSpec((bm, TOP_K), lambda i: (i, 0)),
        ],
        out_specs=pl.BlockSpec((bm, D_MODEL), lambda i: (i, 0)),
        compiler_params=_cp(("parallel",)),
        name="moe_combine",
    )(x_mid, og, gates_tk)


def _block_tables(counts, n_blocks, blk):
    padded = (counts + blk - 1) // blk * blk
    pad_ends = jnp.cumsum(padded)
    pad_starts = pad_ends - padded
    block_start = jnp.arange(n_blocks, dtype=I32) * blk
    block_expert = jnp.minimum(jnp.sum(pad_ends[None, :] <= block_start[:, None], axis=1), N_EXPERTS - 1)
    n_used = (pad_ends[-1] // blk).reshape(1)
    return pad_starts.astype(I32), block_expert.astype(I32), n_used.astype(I32)


def _layer(x, mem, consts, params, after=None):
    b, seq, d = x.shape
    m_len = mem.shape[1]
    t = b * seq
    x2 = x.reshape(t, d)
    gain = params["norm_mix"]
    if after is not None:
        gain = gain + 0.0 * after.astype(F32)
    proj = _in_proj(x2, gain, params["w_in"], params["head_gains"], consts["cos"], consts["sin"])
    proj3 = proj.reshape(b, seq, IN_WIDTH)
    ret = _retention(proj3, params["ret_out_norm"], consts["tabs"], consts["decays"])
    att = _window_attention(proj3, params["att_sink"], consts["bias"])
    mkv = _mem_kv(mem.reshape(b * m_len, d), params["mem_norm"], params["w_mem_kv"],
                  params["mem_k_norm"])
    mo = _mem_attention(proj3, mkv.reshape(b, m_len, 2 * MEM_HEADS * MEM_D))
    x_mid, hp, logits_t = _merge(x2, proj, ret.reshape(t, d), att.reshape(t, d), mo.reshape(t, d),
                               params["w_branch"], params["w_out"], params["norm_ffn"],
                               params["w_router"])
    eid, rank, gates, counts = _route(logits_t, params["b_router"], consts["tri"])
    blk = EXPERT_BLOCK
    n_blocks = t * TOP_K // blk + N_EXPERTS
    pad_starts, block_expert, n_used = _block_tables(counts[:, 0].astype(I32), n_blocks, blk)
    slot = _slots(pad_starts, eid, rank)
    buf = _dispatch_rows(hp, slot, n_blocks * blk)
    out_rows = _experts(block_expert, n_used, buf, params["w_e_in"], params["b_e_in"],
                        params["w_e_out"], params["b_e_out"], blk)
    og = _gather_rows(out_rows, slot)
    y = _combine(x_mid, og, gates.T)
    return y.reshape(b, seq, d), slot[0, 0]


def _head_gains(att_q, att_k, mem_q):
    rows = jnp.zeros((8, MEM_D), F32)
    rows = rows.at[0, :ATT_D].set(att_q.astype(F32) * (ATT_D ** -0.5 * LOG2E))
    rows = rows.at[1, :ATT_D].set(att_k.astype(F32))
    return rows.at[2, :MEM_D].set(mem_q.astype(F32) * (MEM_D ** -0.5))


def _router_weights(w):
    w_hi = w.astype(BF16)
    w_lo = (w - w_hi.astype(F32)).astype(BF16)
    pad = jnp.zeros((D_MODEL, ROUTER_LANES - 2 * N_EXPERTS), BF16)
    top = jnp.concatenate([w_hi, w_lo, pad], axis=1)
    bottom = jnp.concatenate([w_hi, jnp.zeros_like(w_lo), pad], axis=1)
    return jnp.concatenate([top, bottom], axis=0)


def _prepare(seq, rel_table, norm_mix, w_in, ret_out_norm, att_q_norm, att_k_norm, att_sink, mem_norm,
             w_mem_kv, mem_q_norm, mem_k_norm, w_branch, w_out, norm_ffn, w_router, b_router,
             w_e_in, b_e_in, w_e_out, b_e_out):
    gates_at = IN_WIDTH - 3 * D_MODEL
    w_in0 = w_in[0]
    params = {
        "norm_mix": norm_mix[0].reshape(1, -1),
        "w_in": jnp.concatenate([w_in0[:, gates_at:], w_in0[:, :gates_at]], axis=1).astype(BF16),
        "ret_out_norm": ret_out_norm[0].reshape(1, -1),
        "head_gains": _head_gains(att_q_norm[0], att_k_norm[0], mem_q_norm[0]),
        "att_sink": att_sink[0].astype(F32),
        "mem_norm": mem_norm[0].reshape(1, -1),
        "w_mem_kv": w_mem_kv[0].astype(BF16),
        "mem_k_norm": mem_k_norm[0].reshape(1, -1),
        "w_branch": w_branch[0].astype(BF16),
        "w_out": w_out[0].astype(BF16),
        "norm_ffn": norm_ffn[0].reshape(1, -1),
        "w_router": _router_weights(w_router[0]),
        "b_router": b_router[0].reshape(-1, 1),
        "w_e_in": w_e_in[0],
        "b_e_in": b_e_in[0].reshape(N_EXPERTS, 1, -1),
        "w_e_out": w_e_out[0],
        "b_e_out": b_e_out[0].reshape(N_EXPERTS, 1, -1),
    }
    cos, sin, tabs, decays = _retention_tables(seq)
    route_bt = 1024
    tri = (jnp.arange(route_bt)[:, None] < jnp.arange(route_bt)[None, :]).astype(BF16)
    consts = {"cos": cos, "sin": sin, "tabs": tabs, "decays": decays,
              "bias": _window_bias(rel_table), "tri": tri}
    return consts, params


def kernel(x_prompt, x_sample, mem_prompt, mem_sample, rel_table, norm_mix, w_in, ret_out_norm, att_q_norm, att_k_norm, att_sink, mem_norm, w_mem_kv, mem_q_norm, mem_k_norm, w_branch, w_out, norm_ffn, w_router, b_router, w_e_in, b_e_in, w_e_out, b_e_out):
    consts, params = _prepare(x_prompt.shape[1], rel_table, norm_mix, w_in, ret_out_norm, att_q_norm,
                              att_k_norm, att_sink, mem_norm, w_mem_kv, mem_q_norm, mem_k_norm,
                              w_branch, w_out, norm_ffn, w_router, b_router, w_e_in, b_e_in,
                              w_e_out, b_e_out)
    y_sample, first_slot = _layer(x_sample, mem_sample, consts, params)
    y_prompt, _ = _layer(x_prompt, mem_prompt, consts, params, after=first_slot)
    return (y_prompt, y_sample)
```

```python
import functools
import math

import jax
import jax.numpy as jnp
import numpy as np
from jax import lax
from jax.experimental import pallas as pl
from jax.experimental.pallas import tpu as pltpu
from jax.experimental.pallas import tpu_sc as plsc

F32 = jnp.float32
BF16 = jnp.bfloat16
I32 = jnp.int32

D_MODEL = 1024
EPS = 1e-6
RET_HEADS = 4
RET_QK = 128
RET_V = 256
CHUNK = 128
ROPE_BASE = 10000.0
DECAY_FWD = 5.0
DECAY_BWD = 5.5
ATT_HEADS = 8
ATT_KV = 2
ATT_GROUP = ATT_HEADS // ATT_KV
ATT_D = 128
WINDOW = 128
REL_BUCKETS = 32
REL_MAX_DIST = 128
MEM_HEADS = 4
MEM_D = 256
N_EXPERTS = 32
TOP_K = 4
D_FF = 1024
SWIGLU_LIMIT = 7.0
SWIGLU_ALPHA = 1.702
EXPERT_BLOCK = 512
HALF = D_MODEL // 2
ROUTER_LANES = 128

IN_WIDTH = 8704
OFF_GATES = 0
OFF_RQ = 3072
OFF_RK = 3584
OFF_RV = 4096
OFF_RG = 5120
OFF_AQ = 6144
OFF_AK = 7168
OFF_AV = 7424
OFF_MQ = 7680

VMEM_LIMIT = 56 * 1024 * 1024
LOG2E = 1.4426950408889634


def _cp(sem, vmem=VMEM_LIMIT):
    return pltpu.CompilerParams(dimension_semantics=sem, vmem_limit_bytes=vmem)


def _rms(x, g):
    return x * lax.rsqrt(jnp.mean(x * x, axis=-1, keepdims=True) + EPS) * g


def _pack_bf16_pairs(x):
    w = x.shape[1] // 2
    bits = pltpu.bitcast(x.astype(BF16).astype(F32), I32)
    hi = bits[:, :w] & jnp.int32(-65536)
    lo = lax.shift_right_logical(bits[:, w:], jnp.int32(16))
    return hi | lo


def _unpack_bf16_pairs(u):
    hi = pltpu.bitcast(u & jnp.int32(-65536), F32)
    lo = pltpu.bitcast(lax.shift_left(u, jnp.int32(16)), F32)
    return hi, lo


IN_PROJ_CHUNK = 256


def _in_proj_kernel(x_ref, g_ref, w_ref, hg_ref, cos_ref, sin_ref, o_ref, h_scr, *, plans):
    j = pl.program_id(1)

    @pl.when(j == 0)
    def _():
        h_scr[...] = _rms(x_ref[...], g_ref[...]).astype(BF16)

    def per_head(r, width, fn):
        return jnp.concatenate([fn(r[:, k * width:(k + 1) * width]) for k in range(IN_PROJ_CHUNK // width)],
                               axis=1)

    def column_step(plan):
        for c in range(o_ref.shape[1] // IN_PROJ_CHUNK):
            sl = slice(c * IN_PROJ_CHUNK, (c + 1) * IN_PROJ_CHUNK)
            r = jnp.dot(h_scr[...], w_ref[:, sl], preferred_element_type=F32)
            kind = plan.get(c, ("plain",))
            if kind[0] == "norm":
                gain = hg_ref[kind[1]:kind[1] + 1, 0:kind[2]]
                r = per_head(r, kind[2], lambda x: _rms(x, gain))
            elif kind[0] == "rope":
                r = per_head(r, RET_QK, lambda x: (x * cos_ref[...] + pltpu.roll(x, RET_QK // 2, 1)
                                                   * sin_ref[...]) * kind[1])
            elif kind[0] == "silu":
                r = r * (0.5 * jnp.tanh(0.5 * r) + 0.5)
            o_ref[:, sl] = r.astype(BF16)

    for step, plan in enumerate(plans):
        pl.when(j == step)(functools.partial(column_step, plan))


def _epilogue_plans(bn, n):
    regions = (
        (OFF_AQ, ATT_HEADS * ATT_D, ("norm", 0, ATT_D)),
        (OFF_AK, ATT_KV * ATT_D, ("norm", 1, ATT_D)),
        (OFF_MQ, MEM_HEADS * MEM_D, ("norm", 2, MEM_D)),
        (OFF_RQ, RET_HEADS * RET_QK, ("rope", 1.0)),
        (OFF_RK, RET_HEADS * RET_QK, ("rope", RET_QK ** -0.5)),
        (OFF_RG, RET_HEADS * RET_V, ("silu",)),
    )
    plans = [dict() for _ in range(n // bn)]
    for off, width, kind in regions:
        assert off % IN_PROJ_CHUNK == 0 and width % IN_PROJ_CHUNK == 0 and bn % IN_PROJ_CHUNK == 0
        for col in range(off, off + width, IN_PROJ_CHUNK):
            plans[col // bn][(col % bn) // IN_PROJ_CHUNK] = kind
    return plans


def _in_proj(x2, gain, w_bf16, head_gains, cos, sin, bm=1024, bn=4352):
    t = x2.shape[0]
    n = w_bf16.shape[1]
    seq_tiles = cos.shape[0] // bm
    return pl.pallas_call(
        functools.partial(_in_proj_kernel, plans=_epilogue_plans(bn, n)),
        out_shape=jax.ShapeDtypeStruct((t, n), BF16),
        grid=(t // bm, n // bn),
        in_specs=[
            pl.BlockSpec((bm, D_MODEL), lambda i, j: (i, 0)),
            pl.BlockSpec((1, D_MODEL), lambda i, j: (0, 0)),
            pl.BlockSpec((D_MODEL, bn), lambda i, j: (0, j)),
            pl.BlockSpec(head_gains.shape, lambda i, j: (0, 0)),
            pl.BlockSpec((bm, RET_QK), lambda i, j: (i % seq_tiles, 0)),
            pl.BlockSpec((bm, RET_QK), lambda i, j: (i % seq_tiles, 0)),
        ],
        out_specs=pl.BlockSpec((bm, bn), lambda i, j: (i, j)),
        scratch_shapes=[pltpu.VMEM((bm, D_MODEL), BF16)],
        compiler_params=_cp(("parallel", "arbitrary")),
        name="in_proj",
    )(x2, gain, w_bf16, head_gains, cos, sin)


def _retention_kernel(dec_ref, q_ref, k_ref, v_ref, g_ref, tab_ref, gain_ref, o_ref, lhs, rhs, acc, kvf, kvb):
    h = pl.program_id(0)
    seq = q_ref.shape[0]
    n_chunks = seq // CHUNK
    dec_f = dec_ref[2 * h]
    dec_b = dec_ref[2 * h + 1]
    nt = (((1,), (1,)), ((), ()))

    def intra(n, carry):
        rows = pl.ds(pl.multiple_of(n * CHUNK, CHUNK), CHUNK)
        q = q_ref[rows, :].astype(F32)
        k = k_ref[rows, :].astype(F32)
        lhs[rows, 0:CHUNK] = (q * tab_ref[1]).astype(BF16)
        lhs[rows, CHUNK:2 * CHUNK] = (q * tab_ref[3]).astype(BF16)
        sc = lax.dot_general(q_ref[rows, :], k_ref[rows, :], nt, preferred_element_type=F32)
        a = jnp.concatenate([(sc * tab_ref[0]).astype(BF16),
                             (k * tab_ref[2]).T.astype(BF16),
                             (k * tab_ref[4]).T.astype(BF16)], axis=0)
        r = jnp.dot(a, v_ref[rows, :], preferred_element_type=F32)
        acc[rows, :] = r[0:CHUNK]
        kvf[n] = r[CHUNK:2 * CHUNK]
        kvb[n] = r[2 * CHUNK:3 * CHUNK]
        return carry

    lax.fori_loop(0, n_chunks, intra, 0, unroll=8)

    def scan_f(n, state):
        rhs[n, 0:CHUNK, :] = state.astype(BF16)
        return state * dec_f + kvf[n]

    lax.fori_loop(0, n_chunks, scan_f, jnp.zeros((RET_QK, RET_V), F32))

    def scan_b(i, state):
        n = n_chunks - 1 - i
        rhs[n, CHUNK:2 * CHUNK, :] = state.astype(BF16)
        return state * dec_b + kvb[n]

    lax.fori_loop(0, n_chunks, scan_b, jnp.zeros((RET_QK, RET_V), F32))

    def cross(n, carry):
        rows = pl.ds(pl.multiple_of(n * CHUNK, CHUNK), CHUNK)
        o = acc[rows, :] + jnp.dot(lhs[rows, :], rhs[n], preferred_element_type=F32)
        y = _rms(o, gain_ref[...])
        o_ref[rows, :] = (y * g_ref[rows, :].astype(F32)).astype(BF16)
        return carry

    lax.fori_loop(0, n_chunks, cross, 0, unroll=True)


def _retention(proj3, gain, tabs, decays):
    b, seq, _ = proj3.shape
    return pl.pallas_call(
        _retention_kernel,
        out_shape=jax.ShapeDtypeStruct((b, seq, RET_HEADS * RET_V), BF16),
        grid=(RET_HEADS, b),
        in_specs=[
            pl.BlockSpec(memory_space=pltpu.SMEM),
            pl.BlockSpec((None, seq, RET_QK), lambda h, i: (i, 0, OFF_RQ // RET_QK + h)),
            pl.BlockSpec((None, seq, RET_QK), lambda h, i: (i, 0, OFF_RK // RET_QK + h)),
            pl.BlockSpec((None, seq, RET_V), lambda h, i: (i, 0, OFF_RV // RET_V + h)),
            pl.BlockSpec((None, seq, RET_V), lambda h, i: (i, 0, OFF_RG // RET_V + h)),
            pl.BlockSpec((None, 5, CHUNK, CHUNK), lambda h, i: (h, 0, 0, 0)),
            pl.BlockSpec((1, RET_V), lambda h, i: (0, h)),
        ],
        out_specs=pl.BlockSpec((None, seq, RET_V), lambda h, i: (i, 0, h)),
        scratch_shapes=[
            pltpu.VMEM((seq, 2 * CHUNK), BF16),
            pltpu.VMEM((seq // CHUNK, 2 * CHUNK, RET_V), BF16),
            pltpu.VMEM((seq, RET_V), F32),
            pltpu.VMEM((seq // CHUNK, RET_QK, RET_V), F32),
            pltpu.VMEM((seq // CHUNK, RET_QK, RET_V), F32),
        ],
        compiler_params=_cp(("parallel", "parallel")),
        name="retention",
    )(decays, proj3, proj3, proj3, proj3, tabs, gain)


def _retention_tables(seq):
    half = RET_QK // 2
    inv = ROPE_BASE ** (-jnp.arange(half, dtype=F32) / half)
    ang = jnp.arange(seq, dtype=F32)[:, None] * inv[None, :]
    cos = jnp.concatenate([jnp.cos(ang), jnp.cos(ang)], axis=1)
    sin = jnp.concatenate([-jnp.sin(ang), jnp.sin(ang)], axis=1)
    heads = jnp.arange(RET_HEADS, dtype=F32)
    lg_f = jnp.log1p(-jnp.exp2(-DECAY_FWD - heads))[:, None, None]
    lg_b = jnp.log1p(-jnp.exp2(-DECAY_BWD - heads))[:, None, None]
    idx = jnp.arange(CHUNK, dtype=F32)
    diff = (idx[:, None] - idx[None, :])[None]
    dmat = jnp.where(diff >= 0, jnp.exp(jnp.where(diff >= 0, diff, 0.0) * lg_f),
                     jnp.exp(jnp.where(diff < 0, -diff, 0.0) * lg_b))
    col = jnp.broadcast_to(idx[None, :, None], (RET_HEADS, CHUNK, CHUNK))
    xi_f = jnp.exp((col + 1.0) * lg_f)
    zeta_f = jnp.exp((CHUNK - 1.0 - col) * lg_f)
    xi_b = jnp.exp((CHUNK - col) * lg_b)
    zeta_b = jnp.exp(col * lg_b)
    tabs = jnp.stack([dmat, xi_f, zeta_f, xi_b, zeta_b], axis=1)
    decays = jnp.stack([jnp.exp(CHUNK * lg_f[:, 0, 0]), jnp.exp(CHUNK * lg_b[:, 0, 0])], axis=1).reshape(-1)
    return cos, sin, tabs, decays


def _window_kernel(sink_ref, q_ref, k_ref, v_ref, bias_ref, o_ref, kt, vp, s_scr, p_scr, e_scr):
    g = pl.program_id(0)
    seq = q_ref.shape[0]
    n_blocks = seq // WINDOW
    stack = ATT_GROUP * WINDOW
    unit = 64
    heads = [slice(r * ATT_D, (r + 1) * ATT_D) for r in range(ATT_GROUP)]
    kt[:, 0:WINDOW] = jnp.zeros((ATT_D, WINDOW), BF16)
    kt[:, seq + WINDOW:seq + 2 * WINDOW] = jnp.zeros((ATT_D, WINDOW), BF16)
    vp[0:WINDOW, :] = jnp.zeros((WINDOW, 2 * ATT_D), BF16)
    vp[seq + WINDOW:seq + 2 * WINDOW, :] = jnp.zeros((WINDOW, 2 * ATT_D), BF16)

    ones = jnp.ones((WINDOW, ATT_D), BF16)

    def start(n, size):
        return n * size if isinstance(n, int) else pl.multiple_of(n * size, size)

    def prep(n):
        r0 = start(n, WINDOW)
        src = pl.ds(r0, WINDOW)
        dst = pl.ds(r0 + WINDOW, WINDOW)
        vp[dst, 0:ATT_D] = v_ref[src, :]
        vp[dst, ATT_D:2 * ATT_D] = ones
        kt[:, dst] = k_ref[src, :].astype(F32).T.astype(BF16)

    def scores(n):
        r0 = start(n, WINDOW)
        kb = kt[:, pl.ds(r0, 3 * WINDOW)]
        q4 = jnp.concatenate([q_ref[pl.ds(r0, WINDOW), hs] for hs in heads], axis=0)
        if isinstance(n, int):
            edge = 0 if n == 0 else (2 if n == n_blocks - 1 else 1)
        else:
            edge = jnp.where(n == 0, 0, jnp.where(n == n_blocks - 1, 2, 1))
        s_scr[pl.ds(start(n, stack), stack), :] = (
            jnp.dot(q4, kb, preferred_element_type=F32) + bias_ref[edge])

    def softmax(n):
        base = start(n, stack)
        for u in range(stack // unit):
            rows = pl.ds(base + u * unit, unit)
            sink = sink_ref[g * ATT_GROUP + (u * unit) // WINDOW] * LOG2E
            s = s_scr[rows, :]
            m = jnp.maximum(jnp.max(s, axis=-1, keepdims=True), sink)
            mb = jnp.broadcast_to(m, (unit, WINDOW))
            p_scr[rows, :] = jnp.exp2(s - jnp.concatenate([mb, mb, mb], axis=1)).astype(BF16)
            e_scr[rows, :] = jnp.exp2(sink - mb)

    def outputs(n):
        r0 = start(n, WINDOW)
        rows = pl.ds(start(n, stack), stack)
        oe = jnp.dot(p_scr[rows, :], vp[pl.ds(r0, 3 * WINDOW), :], preferred_element_type=F32)
        o = oe[:, 0:ATT_D] / (oe[:, ATT_D:2 * ATT_D] + e_scr[rows, :])
        for r, hs in enumerate(heads):
            o_ref[pl.ds(r0, WINDOW), hs] = o[r * WINDOW:(r + 1) * WINDOW, :].astype(BF16)

    prep(0)
    prep(1)
    scores(0)
    prep(2)
    softmax(0)
    scores(1)
    prep(3)

    def steady(n, carry):
        outputs(n - 4)
        softmax(n - 3)
        scores(n - 2)
        prep(n)
        return carry

    lax.fori_loop(4, n_blocks, steady, 0, unroll=2)
    last = n_blocks - 1
    outputs(last - 3)
    softmax(last - 2)
    scores(last - 1)
    outputs(last - 2)
    softmax(last - 1)
    scores(last)
    outputs(last - 1)
    softmax(last)
    outputs(last)


def _window_attention(proj3, sink, bias):
    b, seq, _ = proj3.shape
    gw = ATT_GROUP * ATT_D
    return pl.pallas_call(
        _window_kernel,
        out_shape=jax.ShapeDtypeStruct((b, seq, ATT_HEADS * ATT_D), BF16),
        grid=(ATT_KV, b),
        in_specs=[
            pl.BlockSpec(memory_space=pltpu.SMEM),
            pl.BlockSpec((None, seq, gw), lambda g, i: (i, 0, OFF_AQ // gw + g)),
            pl.BlockSpec((None, seq, ATT_D), lambda g, i: (i, 0, OFF_AK // ATT_D + g)),
            pl.BlockSpec((None, seq, ATT_D), lambda g, i: (i, 0, OFF_AV // ATT_D + g)),
            pl.BlockSpec((None, 3, ATT_GROUP * WINDOW, 3 * WINDOW), lambda g, i: (g, 0, 0, 0)),
        ],
        out_specs=pl.BlockSpec((None, seq, gw), lambda g, i: (i, 0, g)),
        scratch_shapes=[
            pltpu.VMEM((ATT_D, seq + 2 * WINDOW), BF16),
            pltpu.VMEM((seq + 2 * WINDOW, 2 * ATT_D), BF16),
            pltpu.VMEM((seq * ATT_GROUP, 3 * WINDOW), F32),
            pltpu.VMEM((seq * ATT_GROUP, 3 * WINDOW), BF16),
            pltpu.VMEM((seq * ATT_GROUP, WINDOW), F32),
        ],
        compiler_params=_cp(("parallel", "parallel")),
        name="window_attention",
    )(sink, proj3, proj3, proj3, bias)


def _window_bias(rel_table):
    nb = REL_BUCKETS // 2
    max_exact = nb // 2
    qi = jnp.arange(WINDOW)[:, None]
    ki = jnp.arange(3 * WINDOW)[None, :]
    rel = ki - WINDOW - qi
    ret = jnp.where(rel > 0, nb, 0)
    n = jnp.abs(rel)
    large = max_exact + (jnp.log(jnp.maximum(n, 1).astype(F32) / max_exact)
                         / math.log(REL_MAX_DIST / max_exact) * (nb - max_exact)).astype(I32)
    large = jnp.minimum(large, nb - 1)
    bucket = ret + jnp.where(n < max_exact, n, large)
    onehot = (bucket[None] == jnp.arange(REL_BUCKETS)[:, None, None]).astype(F32)
    bias = jnp.einsum("bh,bqk->hqk", rel_table.astype(F32), onehot, precision=lax.Precision.HIGHEST)
    bias = jnp.where((jnp.abs(rel) <= WINDOW)[None], bias * LOG2E, -jnp.inf)
    bias = bias.reshape(ATT_KV, ATT_GROUP * WINDOW, 3 * WINDOW)
    first = jnp.where(ki < WINDOW, -jnp.inf, bias)
    last = jnp.where(ki >= 2 * WINDOW, -jnp.inf, bias)
    return jnp.stack([first, bias, last], axis=1)


def _mem_kv_kernel(m_ref, g_ref, w_ref, kg_ref, o_ref):
    h = _rms(m_ref[...], g_ref[...]).astype(BF16)
    kw = MEM_HEADS * MEM_D
    for j in range(MEM_HEADS):
        cols = slice(j * MEM_D, (j + 1) * MEM_D)
        k = jnp.dot(h, w_ref[:, cols], preferred_element_type=F32)
        o_ref[:, cols] = _rms(k, kg_ref[...]).astype(BF16)
    o_ref[:, kw:] = jnp.dot(h, w_ref[:, kw:], preferred_element_type=F32).astype(BF16)


def _mem_kv(mem2, gain, w_bf16, k_gain, bm=512):
    rows = mem2.shape[0]
    bm = min(bm, rows)
    n = w_bf16.shape[1]
    return pl.pallas_call(
        _mem_kv_kernel,
        out_shape=jax.ShapeDtypeStruct((rows, n), BF16),
        grid=(rows // bm,),
        in_specs=[
            pl.BlockSpec((bm, D_MODEL), lambda i: (i, 0)),
            pl.BlockSpec((1, D_MODEL), lambda i: (0, 0)),
            pl.BlockSpec((D_MODEL, n), lambda i: (0, 0)),
            pl.BlockSpec((1, MEM_D), lambda i: (0, 0)),
        ],
        out_specs=pl.BlockSpec((bm, n), lambda i: (i, 0)),
        compiler_params=_cp(("parallel",)),
        name="mem_kv",
    )(mem2, gain, w_bf16, k_gain)


def _mem_attn_kernel(q_ref, k_ref, v_ref, o_ref, kt, *, bq):
    seq = q_ref.shape[0]
    kt[...] = k_ref[...].astype(F32).T.astype(BF16)

    def block(n, carry):
        rows = pl.ds(pl.multiple_of(n * bq, bq), bq)
        s = jnp.dot(q_ref[rows, :], kt[...], preferred_element_type=F32)
        p = jnp.exp(s - jnp.max(s, axis=-1, keepdims=True))
        denom = jnp.sum(p, axis=-1, keepdims=True)
        o = jnp.dot(p.astype(BF16), v_ref[...], preferred_element_type=F32) / denom
        o_ref[rows, :] = o.astype(BF16)
        return carry

    lax.fori_loop(0, seq // bq, block, 0, unroll=8)


def _mem_attention(proj3, mkv3, bq=256):
    b, seq, _ = proj3.shape
    m_len = mkv3.shape[1]
    return pl.pallas_call(
        functools.partial(_mem_attn_kernel, bq=bq),
        out_shape=jax.ShapeDtypeStruct((b, seq, MEM_HEADS * MEM_D), BF16),
        grid=(b, MEM_HEADS),
        in_specs=[
            pl.BlockSpec((None, seq, MEM_D), lambda i, h: (i, 0, OFF_MQ // MEM_D + h)),
            pl.BlockSpec((None, m_len, MEM_D), lambda i, h: (i, 0, h)),
            pl.BlockSpec((None, m_len, MEM_D), lambda i, h: (i, 0, MEM_HEADS + h)),
        ],
        out_specs=pl.BlockSpec((None, seq, MEM_D), lambda i, h: (i, 0, h)),
        scratch_shapes=[pltpu.VMEM((MEM_D, m_len), BF16)],
        compiler_params=_cp(("parallel", "parallel")),
        name="mem_attention",
    )(proj3, mkv3, mkv3)


def _merge_kernel(x_ref, g0_ref, g1_ref, g2_ref, b0_ref, b1_ref, b2_ref, wb_ref, wo_ref, gn_ref,
                  wr_ref, xo_ref, hp_ref, lg_ref, lg_scr, x_scr):
    i = pl.program_id(0)
    n = pl.num_programs(0) - 1

    def mix():
        merged = None
        for gate_ref, br, j in ((g0_ref, b0_ref, 0), (g1_ref, b1_ref, 1), (g2_ref, b2_ref, 2)):
            gate = 0.5 * jnp.tanh(0.5 * gate_ref[...].astype(F32)) + 0.5
            t = gate * jnp.dot(br[...], wb_ref[j], preferred_element_type=F32)
            merged = t if merged is None else merged + t
        x = x_ref[...] + jnp.dot(merged.astype(BF16), wo_ref[...], preferred_element_type=F32)
        xo_ref[...] = x
        x_scr[i % 2] = x

    def route_inputs():
        h = _rms(x_scr[(i + 1) % 2], gn_ref[...])
        h_hi = h.astype(BF16)
        h_lo = (h - h_hi.astype(F32)).astype(BF16)
        lg_scr[...] = jnp.dot(jnp.concatenate([h_hi, h_lo], axis=1), wr_ref[...],
                              preferred_element_type=F32)
        lg_ref[...] = lg_scr[...].T
        hp_ref[...] = _pack_bf16_pairs(h)

    @pl.when(jnp.logical_and(i >= 1, i < n))
    def _():
        route_inputs()
        mix()

    @pl.when(i == 0)
    def _():
        mix()

    @pl.when(i == n)
    def _():
        route_inputs()


def _merge(x2, proj, ret, att, mo, wb, wo, gain, w_router_cat, bm=512):
    t = x2.shape[0]
    last = t // bm - 1
    const2 = lambda i: (0, 0)
    row = lambda i: (jnp.minimum(i, last), 0)
    lag = lambda i: (jnp.maximum(i - 1, 0), 0)
    return pl.pallas_call(
        _merge_kernel,
        out_shape=(
            jax.ShapeDtypeStruct((t, D_MODEL), F32),
            jax.ShapeDtypeStruct((t, HALF), I32),
            jax.ShapeDtypeStruct((ROUTER_LANES, t), F32),
        ),
        grid=(t // bm + 1,),
        in_specs=[
            pl.BlockSpec((bm, D_MODEL), row),
            pl.BlockSpec((bm, D_MODEL), lambda i: (jnp.minimum(i, last), 0)),
            pl.BlockSpec((bm, D_MODEL), lambda i: (jnp.minimum(i, last), 1)),
            pl.BlockSpec((bm, D_MODEL), lambda i: (jnp.minimum(i, last), 2)),
            pl.BlockSpec((bm, D_MODEL), row),
            pl.BlockSpec((bm, D_MODEL), row),
            pl.BlockSpec((bm, D_MODEL), row),
            pl.BlockSpec((3, D_MODEL, D_MODEL), lambda i: (0, 0, 0)),
            pl.BlockSpec((D_MODEL, D_MODEL), const2),
            pl.BlockSpec((1, D_MODEL), const2),
            pl.BlockSpec((2 * D_MODEL, ROUTER_LANES), const2),
        ],
        out_specs=(
            pl.BlockSpec((bm, D_MODEL), row),
            pl.BlockSpec((bm, HALF), lag),
            pl.BlockSpec((ROUTER_LANES, bm), lambda i: (0, jnp.maximum(i - 1, 0))),
        ),
        scratch_shapes=[pltpu.VMEM((bm, ROUTER_LANES), F32), pltpu.VMEM((2, bm, D_MODEL), F32)],
        compiler_params=_cp(("arbitrary",)),
        name="merge_router",
    )(x2, proj, proj, proj, ret, att, mo, wb, wo, gain, w_router_cat)


def _route_kernel(l_ref, b_ref, tri_ref, eid_ref, rank_ref, gate_ref, cnt_ref):
    @pl.when(pl.program_id(0) == 0)
    def _():
        cnt_ref[...] = jnp.zeros_like(cnt_ref)

    l = l_ref[0:N_EXPERTS, :] + l_ref[N_EXPERTS:2 * N_EXPERTS, :] + b_ref[...]
    ne, bt = l.shape
    iota_e = lax.broadcasted_iota(I32, (ne, bt), 0)
    picked = jnp.zeros((ne, bt), jnp.bool_)
    vals, idxs = [], []
    for _ in range(TOP_K):
        m = jnp.max(l, axis=0, keepdims=True)
        idx = jnp.min(jnp.where(l == m, iota_e, ne), axis=0, keepdims=True)
        sel = iota_e == idx
        picked = picked | sel
        l = jnp.where(sel, -jnp.inf, l)
        vals.append(m)
        idxs.append(idx)
    ex = [jnp.exp(v - vals[0]) for v in vals]
    tot = ex[0] + ex[1] + ex[2] + ex[3]
    onehot = jnp.where(picked, 1.0, 0.0)
    before = jnp.dot(onehot.astype(BF16), tri_ref[...], preferred_element_type=F32) + cnt_ref[:, 0:1]
    for k in range(TOP_K):
        eid_ref[k:k + 1, :] = idxs[k]
        gate_ref[k:k + 1, :] = ex[k] / tot
        rank_ref[k:k + 1, :] = jnp.sum(jnp.where(iota_e == idxs[k], before, 0.0), axis=0,
                                       keepdims=True).astype(I32)
    cnt_ref[...] = cnt_ref[...] + jnp.sum(onehot, axis=1, keepdims=True)


def _route(logits_t, bias_col, tri):
    bt = tri.shape[0]
    ne, t = N_EXPERTS, logits_t.shape[1]
    blk = lambda i: (0, i)
    return pl.pallas_call(
        _route_kernel,
        out_shape=(
            jax.ShapeDtypeStruct((TOP_K, t), I32),
            jax.ShapeDtypeStruct((TOP_K, t), I32),
            jax.ShapeDtypeStruct((TOP_K, t), F32),
            jax.ShapeDtypeStruct((ne, 128), F32),
        ),
        grid=(t // bt,),
        in_specs=[pl.BlockSpec((ROUTER_LANES, bt), blk), pl.BlockSpec((ne, 1), lambda i: (0, 0)),
                  pl.BlockSpec((bt, bt), lambda i: (0, 0))],
        out_specs=(
            pl.BlockSpec((TOP_K, bt), blk),
            pl.BlockSpec((TOP_K, bt), blk),
            pl.BlockSpec((TOP_K, bt), blk),
            pl.BlockSpec((ne, 128), lambda i: (0, 0)),
        ),
        compiler_params=_cp(("arbitrary",)),
        name="route_topk",
    )(logits_t, bias_col, tri)


def _slot_kernel(start_ref, eid_ref, rank_ref, slot_ref):
    eid = eid_ref[...]
    base = jnp.zeros(eid.shape, I32)
    for e in range(N_EXPERTS):
        base = jnp.where(eid == e, start_ref[e], base)
    slot_ref[...] = base + rank_ref[...]


def _slots(pad_starts, eid, rank, bt=2048):
    t = eid.shape[1]
    blk = lambda i, s: (0, i)
    return pl.pallas_call(
        _slot_kernel,
        out_shape=jax.ShapeDtypeStruct((TOP_K, t), I32),
        grid_spec=pltpu.PrefetchScalarGridSpec(
            num_scalar_prefetch=1,
            grid=(t // bt,),
            in_specs=[pl.BlockSpec((TOP_K, bt), blk), pl.BlockSpec((TOP_K, bt), blk)],
            out_specs=pl.BlockSpec((TOP_K, bt), blk),
        ),
        compiler_params=_cp(("parallel",)),
        name="route_slots",
    )(pad_starts, eid, rank)


def _sc_workers():
    info = plsc.get_sparse_core_info()
    return info.num_cores, info.num_subcores


def _dispatch_rows(hp, slot, n_rows, win=64):
    t, w = hp.shape
    nc, ns = _sc_workers()
    per_worker = t // (nc * ns)
    mesh = plsc.VectorSubcoreMesh(core_axis_name="c", subcore_axis_name="s")

    @functools.partial(
        pl.kernel,
        out_type=jax.ShapeDtypeStruct((n_rows, w), hp.dtype),
        mesh=mesh,
        scratch_types=[pltpu.VMEM((win,), I32)] * TOP_K + [pltpu.VMEM((win, w), hp.dtype)],
        name="moe_dispatch",
    )
    def k(hp_hbm, slot_hbm, buf_hbm, *scratch):
        idx_v, rows_v = scratch[:TOP_K], scratch[TOP_K]
        wid = lax.axis_index("s") * nc + lax.axis_index("c")
        base = wid * per_worker

        @pl.loop(0, per_worker // win)
        def _(i):
            off = pl.multiple_of(base + i * win, win)
            pltpu.sync_copy(hp_hbm.at[pl.ds(off, win)], rows_v)
            for kk in range(TOP_K):
                pltpu.sync_copy(slot_hbm.at[pl.ds(kk * t + off, win)], idx_v[kk])
                pltpu.sync_copy(rows_v, buf_hbm.at[idx_v[kk]])

    return k(hp, slot.reshape(-1))


def _gather_rows(out_rows, slot, win=64):
    t = slot.shape[1]
    w = out_rows.shape[1]
    nc, ns = _sc_workers()
    per_worker = t // (nc * ns)
    mesh = plsc.VectorSubcoreMesh(core_axis_name="c", subcore_axis_name="s")

    @functools.partial(
        pl.kernel,
        out_type=jax.ShapeDtypeStruct((TOP_K * t, w), out_rows.dtype),
        mesh=mesh,
        scratch_types=[pltpu.VMEM((win,), I32), pltpu.VMEM((win, w), out_rows.dtype)],
        name="moe_gather",
    )
    def k(rows_hbm, slot_hbm, og_hbm, idx_v, rows_v):
        wid = lax.axis_index("s") * nc + lax.axis_index("c")
        base = wid * per_worker

        @pl.loop(0, per_worker // win)
        def _(i):
            off = pl.multiple_of(base + i * win, win)
            for kk in range(TOP_K):
                pltpu.sync_copy(slot_hbm.at[pl.ds(kk * t + off, win)], idx_v)
                pltpu.sync_copy(rows_hbm.at[idx_v], rows_v)
                pltpu.sync_copy(rows_v, og_hbm.at[pl.ds(kk * t + off, win)])

    return k(out_rows, slot.reshape(-1)).reshape(TOP_K, t, w)


def _expert_kernel(be_ref, nu_ref, x_ref, wi_ref, bi_ref, wo_ref, bo_ref, o_ref, wi_bf, wo_bf, act_scr):
    i = pl.program_id(0)
    n_used = nu_ref[0]
    last = be_ref.shape[0] - 1
    e_up = be_ref[jnp.minimum(i, last)]
    e_up_prev = be_ref[jnp.clip(i - 1, 0, last)]
    e_down_prev = be_ref[jnp.clip(i - 2, 0, last)]
    up = i < n_used
    down = jnp.logical_and(i >= 1, i <= n_used)
    rows = 128

    @pl.when(jnp.logical_and(up, jnp.logical_or(i == 0, e_up != e_up_prev)))
    def _():
        def cast(c, carry):
            sl = pl.ds(pl.multiple_of(c * rows, rows), rows)
            wi_bf[sl, :] = wi_ref[sl, :].astype(BF16)
            return carry

        lax.fori_loop(0, D_MODEL // rows, cast, 0)

    @pl.when(jnp.logical_and(down, jnp.logical_or(i == 1, e_up_prev != e_down_prev)))
    def _():
        def cast(c, carry):
            sl = pl.ds(pl.multiple_of(c * rows, rows), rows)
            wo_bf[sl, :] = wo_ref[sl, :].astype(BF16)
            return carry

        lax.fori_loop(0, D_FF // rows, cast, 0)

    def up_half():
        hi, lo = _unpack_bf16_pairs(x_ref[...])
        x = jnp.concatenate([hi.astype(BF16), lo.astype(BF16)], axis=1)
        h = jnp.dot(x, wi_bf[...], preferred_element_type=F32) + bi_ref[...]
        gate = jnp.minimum(h[:, :D_FF], SWIGLU_LIMIT)
        lin = jnp.clip(h[:, D_FF:], -SWIGLU_LIMIT, SWIGLU_LIMIT)
        act = gate * (0.5 * jnp.tanh((0.5 * SWIGLU_ALPHA) * gate) + 0.5) * (lin + 1.0)
        act_scr[i % 2] = act.astype(BF16)

    def down_half():
        o = jnp.dot(act_scr[(i + 1) % 2], wo_bf[...], preferred_element_type=F32) + bo_ref[...]
        o_ref[...] = _pack_bf16_pairs(o)

    @pl.when(jnp.logical_and(up, down))
    def _():
        down_half()
        up_half()

    @pl.when(jnp.logical_and(up, jnp.logical_not(down)))
    def _():
        up_half()

    @pl.when(jnp.logical_and(down, jnp.logical_not(up)))
    def _():
        down_half()


def _experts(block_expert, n_used, buf, w_in, b_in, w_out, b_out, blk):
    n_rows, w = buf.shape
    n_blocks = n_rows // blk
    last = n_blocks - 1
    up_blk = lambda i, be, nu: (jnp.minimum(i, last), 0)
    up_exp = lambda i, be, nu: (be[jnp.minimum(i, last)], 0, 0)
    down_blk = lambda i, be, nu: (jnp.maximum(i - 1, 0), 0)
    down_exp = lambda i, be, nu: (be[jnp.maximum(i - 1, 0)], 0, 0)
    return pl.pallas_call(
        _expert_kernel,
        out_shape=jax.ShapeDtypeStruct((n_rows, w), buf.dtype),
        grid_spec=pltpu.PrefetchScalarGridSpec(
            num_scalar_prefetch=2,
            grid=(n_blocks + 1,),
            in_specs=[
                pl.BlockSpec((blk, w), up_blk),
                pl.BlockSpec((None, D_MODEL, 2 * D_FF), up_exp),
                pl.BlockSpec((None, 1, 2 * D_FF), up_exp),
                pl.BlockSpec((None, D_FF, D_MODEL), down_exp),
                pl.BlockSpec((None, 1, D_MODEL), down_exp),
            ],
            out_specs=pl.BlockSpec((blk, w), down_blk),
            scratch_shapes=[pltpu.VMEM((D_MODEL, 2 * D_FF), BF16), pltpu.VMEM((D_FF, D_MODEL), BF16),
                            pltpu.VMEM((2, blk, D_FF), BF16)],
        ),
        compiler_params=_cp(("arbitrary",)),
        name="expert_ffn",
    )(block_expert, n_used, buf, w_in, b_in, w_out, b_out)


def _combine_kernel(x_ref, og_ref, g_ref, y_ref):
    g = g_ref[...]
    acc_hi = x_ref[:, :HALF]
    acc_lo = x_ref[:, HALF:]
    for k in range(TOP_K):
        hi, lo = _unpack_bf16_pairs(og_ref[k])
        gk = g[:, k:k + 1]
        acc_hi = acc_hi + gk * hi
        acc_lo = acc_lo + gk * lo
    y_ref[:, :HALF] = acc_hi
    y_ref[:, HALF:] = acc_lo


def _combine(x_mid, og, gates_tk, bm=1024):
    t = x_mid.shape[0]
    return pl.pallas_call(
        _combine_kernel,
        out_shape=jax.ShapeDtypeStruct((t, D_MODEL), F32),
        grid=(t // bm,),
        in_specs=[
            pl.BlockSpec((bm, D_MODEL), lambda i: (i, 0)),
            pl.BlockSpec((TOP_K, bm, HALF), lambda i: (0, i, 0)),
            pl.BlockSpec((bm, TOP_K), lambda i: (i, 0)),
        ],
        out_specs=pl.BlockSpec((bm, D_MODEL), lambda i: (i, 0)),
        compiler_params=_cp(("parallel",)),
        name="moe_combine",
    )(x_mid, og, gates_tk)


def _block_tables(counts, n_blocks, blk):
    padded = (counts + blk - 1) // blk * blk
    pad_ends = jnp.cumsum(padded)
    pad_starts = pad_ends - padded
    block_start = jnp.arange(n_blocks, dtype=I32) * blk
    block_expert = jnp.minimum(jnp.sum(pad_ends[None, :] <= block_start[:, None], axis=1), N_EXPERTS - 1)
    n_used = (pad_ends[-1] // blk).reshape(1)
    return pad_starts.astype(I32), block_expert.astype(I32), n_used.astype(I32)


def _layer(x, mem, consts, params, after=None):
    b, seq, d = x.shape
    m_len = mem.shape[1]
    t = b * seq
    x2 = x.reshape(t, d)
    zero = 0.0 if after is None else 0.0 * after.astype(F32)
    proj = _in_proj(x2, params["norm_mix"], params["w_in"], params["head_gains"], consts["cos"],
                    consts["sin"])
    proj3 = proj.reshape(b, seq, IN_WIDTH)
    ret = _retention(proj3, params["ret_out_norm"] + zero, consts["tabs"], consts["decays"])
    att = _window_attention(proj3, params["att_sink"] + zero, consts["bias"])
    mkv = _mem_kv(mem.reshape(b * m_len, d), params["mem_norm"] + zero, params["w_mem_kv"],
                  params["mem_k_norm"])
    mo = _mem_attention(proj3, mkv.reshape(b, m_len, 2 * MEM_HEADS * MEM_D))
    x_mid, hp, logits_t = _merge(x2, proj, ret.reshape(t, d), att.reshape(t, d), mo.reshape(t, d),
                               params["w_branch"], params["w_out"], params["norm_ffn"],
                               params["w_router"])
    eid, rank, gates, counts = _route(logits_t, params["b_router"], consts["tri"])
    blk = EXPERT_BLOCK
    n_blocks = t * TOP_K // blk + N_EXPERTS
    pad_starts, block_expert, n_used = _block_tables(counts[:, 0].astype(I32), n_blocks, blk)
    slot = _slots(pad_starts, eid, rank)
    buf = _dispatch_rows(hp, slot, n_blocks * blk)
    out_rows = _experts(block_expert, n_used, buf, params["w_e_in"], params["b_e_in"],
                        params["w_e_out"], params["b_e_out"], blk)
    og = _gather_rows(out_rows, slot)
    y = _combine(x_mid, og, gates.T)
    return y.reshape(b, seq, d), slot[0, 0]


def _head_gains(att_q, att_k, mem_q):
    rows = jnp.zeros((8, MEM_D), F32)
    rows = rows.at[0, :ATT_D].set(att_q.astype(F32) * (ATT_D ** -0.5 * LOG2E))
    rows = rows.at[1, :ATT_D].set(att_k.astype(F32))
    return rows.at[2, :MEM_D].set(mem_q.astype(F32) * (MEM_D ** -0.5))


def _router_weights(w):
    w_hi = w.astype(BF16)
    w_lo = (w - w_hi.astype(F32)).astype(BF16)
    pad = jnp.zeros((D_MODEL, ROUTER_LANES - 2 * N_EXPERTS), BF16)
    top = jnp.concatenate([w_hi, w_lo, pad], axis=1)
    bottom = jnp.concatenate([w_hi, jnp.zeros_like(w_lo), pad], axis=1)
    return jnp.concatenate([top, bottom], axis=0)


def _prepare(seq, rel_table, norm_mix, w_in, ret_out_norm, att_q_norm, att_k_norm, att_sink, mem_norm,
             w_mem_kv, mem_q_norm, mem_k_norm, w_branch, w_out, norm_ffn, w_router, b_router,
             w_e_in, b_e_in, w_e_out, b_e_out):
    gates_at = IN_WIDTH - 3 * D_MODEL
    w_in0 = w_in[0]
    params = {
        "norm_mix": norm_mix[0].reshape(1, -1),
        "w_in": jnp.concatenate([w_in0[:, gates_at:], w_in0[:, :gates_at]], axis=1).astype(BF16),
        "ret_out_norm": ret_out_norm[0].reshape(1, -1),
        "head_gains": _head_gains(att_q_norm[0], att_k_norm[0], mem_q_norm[0]),
        "att_sink": att_sink[0].astype(F32),
        "mem_norm": mem_norm[0].reshape(1, -1),
        "w_mem_kv": w_mem_kv[0].astype(BF16),
        "mem_k_norm": mem_k_norm[0].reshape(1, -1),
        "w_branch": w_branch[0].astype(BF16),
        "w_out": w_out[0].astype(BF16),
        "norm_ffn": norm_ffn[0].reshape(1, -1),
        "w_router": _router_weights(w_router[0]),
        "b_router": b_router[0].reshape(-1, 1),
        "w_e_in": w_e_in[0],
        "b_e_in": b_e_in[0].reshape(N_EXPERTS, 1, -1),
        "w_e_out": w_e_out[0],
        "b_e_out": b_e_out[0].reshape(N_EXPERTS, 1, -1),
    }
    cos, sin, tabs, decays = _retention_tables(seq)
    route_bt = 1024
    tri = (jnp.arange(route_bt)[:, None] < jnp.arange(route_bt)[None, :]).astype(BF16)
    consts = {"cos": cos, "sin": sin, "tabs": tabs, "decays": decays,
              "bias": _window_bias(rel_table), "tri": tri}
    return consts, params


def kernel(x_prompt, x_sample, mem_prompt, mem_sample, rel_table, norm_mix, w_in, ret_out_norm, att_q_norm, att_k_norm, att_sink, mem_norm, w_mem_kv, mem_q_norm, mem_k_norm, w_branch, w_out, norm_ffn, w_router, b_router, w_e_in, b_e_in, w_e_out, b_e_out):
    consts, params = _prepare(x_prompt.shape[1], rel_table, norm_mix, w_in, ret_out_norm, att_q_norm,
                              att_k_norm, att_sink, mem_norm, w_mem_kv, mem_q_norm, mem_k_norm,
                              w_branch, w_out, norm_ffn, w_router, b_router, w_e_in, b_e_in,
                              w_e_out, b_e_out)
    y_sample, first_slot = _layer(x_sample, mem_sample, consts, params)
    y_prompt, _ = _layer(x_prompt, mem_prompt, consts, params, after=first_slot)
    return (y_prompt, y_sample)
```

```python
import functools
import math

import jax
import jax.numpy as jnp
from jax import lax
from jax.experimental import pallas as pl
from jax.experimental.pallas import tpu as pltpu
from jax.experimental.pallas import tpu_sc as plsc

F32 = jnp.float32
BF16 = jnp.bfloat16
I32 = jnp.int32

D_MODEL = 1024
EPS = 1e-6
RET_HEADS = 4
RET_QK = 128
RET_V = 256
CHUNK = 128
ROPE_BASE = 10000.0
DECAY_FWD = 5.0
DECAY_BWD = 5.5
ATT_HEADS = 8
ATT_KV = 2
ATT_GROUP = ATT_HEADS // ATT_KV
ATT_D = 128
WINDOW = 128
REL_BUCKETS = 32
REL_MAX_DIST = 128
MEM_HEADS = 4
MEM_D = 256
N_EXPERTS = 32
TOP_K = 4
D_FF = 1024
SWIGLU_LIMIT = 7.0
SWIGLU_ALPHA = 1.702
EXPERT_BLOCK = 512
HALF = D_MODEL // 2
ROUTER_LANES = 128

IN_WIDTH = 8704
OFF_GATES = 0
OFF_RQ = 3072
OFF_RK = 3584
OFF_RV = 4096
OFF_RG = 5120
OFF_AQ = 6144
OFF_AK = 7168
OFF_AV = 7424
OFF_MQ = 7680

VMEM_LIMIT = 56 * 1024 * 1024
LOG2E = 1.4426950408889634


def _cp(sem, vmem=VMEM_LIMIT):
    return pltpu.CompilerParams(dimension_semantics=sem, vmem_limit_bytes=vmem)


def _rms(x, g):
    return x * lax.rsqrt(jnp.mean(x * x, axis=-1, keepdims=True) + EPS) * g


def _pack_bf16_pairs(x):
    w = x.shape[1] // 2
    bits = pltpu.bitcast(x.astype(BF16).astype(F32), I32)
    hi = bits[:, :w] & jnp.int32(-65536)
    lo = lax.shift_right_logical(bits[:, w:], jnp.int32(16))
    return hi | lo


def _unpack_bf16_pairs(u):
    hi = pltpu.bitcast(u & jnp.int32(-65536), F32)
    lo = pltpu.bitcast(lax.shift_left(u, jnp.int32(16)), F32)
    return hi, lo


IN_PROJ_CHUNK = 256


def _in_proj_kernel(x_ref, g_ref, w_ref, hg_ref, cos_ref, sin_ref, o_ref, h_scr, *, plans):
    j = pl.program_id(1)

    @pl.when(j == 0)
    def _():
        h_scr[...] = _rms(x_ref[...], g_ref[...]).astype(BF16)

    def per_head(r, width, fn):
        return jnp.concatenate([fn(r[:, k * width:(k + 1) * width]) for k in range(IN_PROJ_CHUNK // width)],
                               axis=1)

    def column_step(plan):
        for c in range(o_ref.shape[1] // IN_PROJ_CHUNK):
            sl = slice(c * IN_PROJ_CHUNK, (c + 1) * IN_PROJ_CHUNK)
            r = jnp.dot(h_scr[...], w_ref[:, sl], preferred_element_type=F32)
            kind = plan.get(c, ("plain",))
            if kind[0] == "norm":
                gain = hg_ref[kind[1]:kind[1] + 1, 0:kind[2]]
                r = per_head(r, kind[2], lambda x: _rms(x, gain))
            elif kind[0] == "rope":
                r = per_head(r, RET_QK, lambda x: (x * cos_ref[...] + pltpu.roll(x, RET_QK // 2, 1)
                                                   * sin_ref[...]) * kind[1])
            elif kind[0] == "silu":
                r = r * (0.5 * jnp.tanh(0.5 * r) + 0.5)
            o_ref[:, sl] = r.astype(BF16)

    for step, plan in enumerate(plans):
        pl.when(j == step)(functools.partial(column_step, plan))


def _epilogue_plans(bn, n):
    regions = (
        (OFF_AQ, ATT_HEADS * ATT_D, ("norm", 0, ATT_D)),
        (OFF_AK, ATT_KV * ATT_D, ("norm", 1, ATT_D)),
        (OFF_MQ, MEM_HEADS * MEM_D, ("norm", 2, MEM_D)),
        (OFF_RQ, RET_HEADS * RET_QK, ("rope", 1.0)),
        (OFF_RK, RET_HEADS * RET_QK, ("rope", RET_QK ** -0.5)),
        (OFF_RG, RET_HEADS * RET_V, ("silu",)),
    )
    plans = [dict() for _ in range(n // bn)]
    for off, width, kind in regions:
        assert off % IN_PROJ_CHUNK == 0 and width % IN_PROJ_CHUNK == 0 and bn % IN_PROJ_CHUNK == 0
        for col in range(off, off + width, IN_PROJ_CHUNK):
            plans[col // bn][(col % bn) // IN_PROJ_CHUNK] = kind
    return plans


def _in_proj(x2, gain, w_bf16, head_gains, cos, sin, bm=1024, bn=4352):
    t = x2.shape[0]
    n = w_bf16.shape[1]
    seq_tiles = cos.shape[0] // bm
    return pl.pallas_call(
        functools.partial(_in_proj_kernel, plans=_epilogue_plans(bn, n)),
        out_shape=jax.ShapeDtypeStruct((t, n), BF16),
        grid=(t // bm, n // bn),
        in_specs=[
            pl.BlockSpec((bm, D_MODEL), lambda i, j: (i, 0)),
            pl.BlockSpec((1, D_MODEL), lambda i, j: (0, 0)),
            pl.BlockSpec((D_MODEL, bn), lambda i, j: (0, j)),
            pl.BlockSpec(head_gains.shape, lambda i, j: (0, 0)),
            pl.BlockSpec((bm, RET_QK), lambda i, j: (i % seq_tiles, 0)),
            pl.BlockSpec((bm, RET_QK), lambda i, j: (i % seq_tiles, 0)),
        ],
        out_specs=pl.BlockSpec((bm, bn), lambda i, j: (i, j)),
        scratch_shapes=[pltpu.VMEM((bm, D_MODEL), BF16)],
        compiler_params=_cp(("parallel", "arbitrary")),
        name="in_proj",
    )(x2, gain, w_bf16, head_gains, cos, sin)


def _retention_kernel(dec_ref, q_ref, k_ref, v_ref, g_ref, tab_ref, gain_ref, o_ref, lhs, rhs, acc, kvf, kvb):
    h = pl.program_id(0)
    seq = q_ref.shape[0]
    n_chunks = seq // CHUNK
    dec_f = dec_ref[2 * h]
    dec_b = dec_ref[2 * h + 1]
    nt = (((1,), (1,)), ((), ()))

    def intra(n, carry):
        rows = pl.ds(pl.multiple_of(n * CHUNK, CHUNK), CHUNK)
        q = q_ref[rows, :].astype(F32)
        k = k_ref[rows, :].astype(F32)
        lhs[rows, 0:CHUNK] = (q * tab_ref[1]).astype(BF16)
        lhs[rows, CHUNK:2 * CHUNK] = (q * tab_ref[3]).astype(BF16)
        sc = lax.dot_general(q_ref[rows, :], k_ref[rows, :], nt, preferred_element_type=F32)
        a = jnp.concatenate([(sc * tab_ref[0]).astype(BF16),
                             (k * tab_ref[2]).T.astype(BF16),
                             (k * tab_ref[4]).T.astype(BF16)], axis=0)
        r = jnp.dot(a, v_ref[rows, :], preferred_element_type=F32)
        acc[rows, :] = r[0:CHUNK]
        kvf[n] = r[CHUNK:2 * CHUNK]
        kvb[n] = r[2 * CHUNK:3 * CHUNK]
        return carry

    lax.fori_loop(0, n_chunks, intra, 0, unroll=8)

    def scan_f(n, state):
        rhs[n, 0:CHUNK, :] = state.astype(BF16)
        return state * dec_f + kvf[n]

    lax.fori_loop(0, n_chunks, scan_f, jnp.zeros((RET_QK, RET_V), F32))

    def scan_b(i, state):
        n = n_chunks - 1 - i
        rhs[n, CHUNK:2 * CHUNK, :] = state.astype(BF16)
        return state * dec_b + kvb[n]

    lax.fori_loop(0, n_chunks, scan_b, jnp.zeros((RET_QK, RET_V), F32))

    def cross(n, carry):
        rows = pl.ds(pl.multiple_of(n * CHUNK, CHUNK), CHUNK)
        o = acc[rows, :] + jnp.dot(lhs[rows, :], rhs[n], preferred_element_type=F32)
        y = _rms(o, gain_ref[...])
        o_ref[rows, :] = (y * g_ref[rows, :].astype(F32)).astype(BF16)
        return carry

    lax.fori_loop(0, n_chunks, cross, 0, unroll=True)


def _retention(proj3, gain, tabs, decays):
    b, seq, _ = proj3.shape
    return pl.pallas_call(
        _retention_kernel,
        out_shape=jax.ShapeDtypeStruct((b, seq, RET_HEADS * RET_V), BF16),
        grid=(RET_HEADS, b),
        in_specs=[
            pl.BlockSpec(memory_space=pltpu.SMEM),
            pl.BlockSpec((None, seq, RET_QK), lambda h, i: (i, 0, OFF_RQ // RET_QK + h)),
            pl.BlockSpec((None, seq, RET_QK), lambda h, i: (i, 0, OFF_RK // RET_QK + h)),
            pl.BlockSpec((None, seq, RET_V), lambda h, i: (i, 0, OFF_RV // RET_V + h)),
            pl.BlockSpec((None, seq, RET_V), lambda h, i: (i, 0, OFF_RG // RET_V + h)),
            pl.BlockSpec((None, 5, CHUNK, CHUNK), lambda h, i: (h, 0, 0, 0)),
            pl.BlockSpec((1, RET_V), lambda h, i: (0, h)),
        ],
        out_specs=pl.BlockSpec((None, seq, RET_V), lambda h, i: (i, 0, h)),
        scratch_shapes=[
            pltpu.VMEM((seq, 2 * CHUNK), BF16),
            pltpu.VMEM((seq // CHUNK, 2 * CHUNK, RET_V), BF16),
            pltpu.VMEM((seq, RET_V), F32),
            pltpu.VMEM((seq // CHUNK, RET_QK, RET_V), F32),
            pltpu.VMEM((seq // CHUNK, RET_QK, RET_V), F32),
        ],
        compiler_params=_cp(("parallel", "parallel")),
        name="retention",
    )(decays, proj3, proj3, proj3, proj3, tabs, gain)


def _retention_tables(seq):
    half = RET_QK // 2
    inv = ROPE_BASE ** (-jnp.arange(half, dtype=F32) / half)
    ang = jnp.arange(seq, dtype=F32)[:, None] * inv[None, :]
    cos = jnp.concatenate([jnp.cos(ang), jnp.cos(ang)], axis=1)
    sin = jnp.concatenate([-jnp.sin(ang), jnp.sin(ang)], axis=1)
    heads = jnp.arange(RET_HEADS, dtype=F32)
    lg_f = jnp.log1p(-jnp.exp2(-DECAY_FWD - heads))[:, None, None]
    lg_b = jnp.log1p(-jnp.exp2(-DECAY_BWD - heads))[:, None, None]
    idx = jnp.arange(CHUNK, dtype=F32)
    diff = (idx[:, None] - idx[None, :])[None]
    dmat = jnp.where(diff >= 0, jnp.exp(jnp.where(diff >= 0, diff, 0.0) * lg_f),
                     jnp.exp(jnp.where(diff < 0, -diff, 0.0) * lg_b))
    col = jnp.broadcast_to(idx[None, :, None], (RET_HEADS, CHUNK, CHUNK))
    xi_f = jnp.exp((col + 1.0) * lg_f)
    zeta_f = jnp.exp((CHUNK - 1.0 - col) * lg_f)
    xi_b = jnp.exp((CHUNK - col) * lg_b)
    zeta_b = jnp.exp(col * lg_b)
    tabs = jnp.stack([dmat, xi_f, zeta_f, xi_b, zeta_b], axis=1)
    decays = jnp.stack([jnp.exp(CHUNK * lg_f[:, 0, 0]), jnp.exp(CHUNK * lg_b[:, 0, 0])], axis=1).reshape(-1)
    return cos, sin, tabs, decays


def _window_kernel(sink_ref, q_ref, k_ref, v_ref, bias_ref, o_ref, kt, vp, s_scr, p_scr, e_scr):
    g = pl.program_id(0)
    seq = q_ref.shape[0]
    n_blocks = seq // WINDOW
    stack = ATT_GROUP * WINDOW
    unit = 64
    heads = [slice(r * ATT_D, (r + 1) * ATT_D) for r in range(ATT_GROUP)]
    kt[:, 0:WINDOW] = jnp.zeros((ATT_D, WINDOW), BF16)
    kt[:, seq + WINDOW:seq + 2 * WINDOW] = jnp.zeros((ATT_D, WINDOW), BF16)
    vp[0:WINDOW, :] = jnp.zeros((WINDOW, 2 * ATT_D), BF16)
    vp[seq + WINDOW:seq + 2 * WINDOW, :] = jnp.zeros((WINDOW, 2 * ATT_D), BF16)

    ones = jnp.ones((WINDOW, ATT_D), BF16)

    def start(n, size):
        return n * size if isinstance(n, int) else pl.multiple_of(n * size, size)

    def prep(n):
        r0 = start(n, WINDOW)
        src = pl.ds(r0, WINDOW)
        dst = pl.ds(r0 + WINDOW, WINDOW)
        vp[dst, 0:ATT_D] = v_ref[src, :]
        vp[dst, ATT_D:2 * ATT_D] = ones
        kt[:, dst] = k_ref[src, :].astype(F32).T.astype(BF16)

    def scores(n):
        r0 = start(n, WINDOW)
        kb = kt[:, pl.ds(r0, 3 * WINDOW)]
        q4 = jnp.concatenate([q_ref[pl.ds(r0, WINDOW), hs] for hs in heads], axis=0)
        if isinstance(n, int):
            edge = 0 if n == 0 else (2 if n == n_blocks - 1 else 1)
        else:
            edge = jnp.where(n == 0, 0, jnp.where(n == n_blocks - 1, 2, 1))
        s_scr[pl.ds(start(n, stack), stack), :] = (
            jnp.dot(q4, kb, preferred_element_type=F32) + bias_ref[edge])

    def softmax(n):
        base = start(n, stack)
        for u in range(stack // unit):
            rows = pl.ds(base + u * unit, unit)
            sink = sink_ref[g * ATT_GROUP + (u * unit) // WINDOW] * LOG2E
            s = s_scr[rows, :]
            m = jnp.maximum(jnp.max(s, axis=-1, keepdims=True), sink)
            mb = jnp.broadcast_to(m, (unit, WINDOW))
            p_scr[rows, :] = jnp.exp2(s - jnp.concatenate([mb, mb, mb], axis=1)).astype(BF16)
            e_scr[rows, :] = jnp.exp2(sink - mb)

    def outputs(n):
        r0 = start(n, WINDOW)
        rows = pl.ds(start(n, stack), stack)
        oe = jnp.dot(p_scr[rows, :], vp[pl.ds(r0, 3 * WINDOW), :], preferred_element_type=F32)
        o = oe[:, 0:ATT_D] / (oe[:, ATT_D:2 * ATT_D] + e_scr[rows, :])
        for r, hs in enumerate(heads):
            o_ref[pl.ds(r0, WINDOW), hs] = o[r * WINDOW:(r + 1) * WINDOW, :].astype(BF16)

    prep(0)
    prep(1)
    scores(0)
    prep(2)
    softmax(0)
    scores(1)
    prep(3)

    def steady(n, carry):
        outputs(n - 4)
        softmax(n - 3)
        scores(n - 2)
        prep(n)
        return carry

    lax.fori_loop(4, n_blocks, steady, 0, unroll=2)
    last = n_blocks - 1
    outputs(last - 3)
    softmax(last - 2)
    scores(last - 1)
    outputs(last - 2)
    softmax(last - 1)
    scores(last)
    outputs(last - 1)
    softmax(last)
    outputs(last)


def _window_attention(proj3, sink, bias):
    b, seq, _ = proj3.shape
    gw = ATT_GROUP * ATT_D
    return pl.pallas_call(
        _window_kernel,
        out_shape=jax.ShapeDtypeStruct((b, seq, ATT_HEADS * ATT_D), BF16),
        grid=(ATT_KV, b),
        in_specs=[
            pl.BlockSpec(memory_space=pltpu.SMEM),
            pl.BlockSpec((None, seq, gw), lambda g, i: (i, 0, OFF_AQ // gw + g)),
            pl.BlockSpec((None, seq, ATT_D), lambda g, i: (i, 0, OFF_AK // ATT_D + g)),
            pl.BlockSpec((None, seq, ATT_D), lambda g, i: (i, 0, OFF_AV // ATT_D + g)),
            pl.BlockSpec((None, 3, ATT_GROUP * WINDOW, 3 * WINDOW), lambda g, i: (g, 0, 0, 0)),
        ],
        out_specs=pl.BlockSpec((None, seq, gw), lambda g, i: (i, 0, g)),
        scratch_shapes=[
            pltpu.VMEM((ATT_D, seq + 2 * WINDOW), BF16),
            pltpu.VMEM((seq + 2 * WINDOW, 2 * ATT_D), BF16),
            pltpu.VMEM((seq * ATT_GROUP, 3 * WINDOW), F32),
            pltpu.VMEM((seq * ATT_GROUP, 3 * WINDOW), BF16),
            pltpu.VMEM((seq * ATT_GROUP, WINDOW), F32),
        ],
        compiler_params=_cp(("parallel", "parallel")),
        name="window_attention",
    )(sink, proj3, proj3, proj3, bias)


def _window_bias(rel_table):
    nb = REL_BUCKETS // 2
    max_exact = nb // 2
    qi = jnp.arange(WINDOW)[:, None]
    ki = jnp.arange(3 * WINDOW)[None, :]
    rel = ki - WINDOW - qi
    ret = jnp.where(rel > 0, nb, 0)
    n = jnp.abs(rel)
    large = max_exact + (jnp.log(jnp.maximum(n, 1).astype(F32) / max_exact)
                         / math.log(REL_MAX_DIST / max_exact) * (nb - max_exact)).astype(I32)
    large = jnp.minimum(large, nb - 1)
    bucket = ret + jnp.where(n < max_exact, n, large)
    onehot = (bucket[None] == jnp.arange(REL_BUCKETS)[:, None, None]).astype(F32)
    bias = jnp.einsum("bh,bqk->hqk", rel_table.astype(F32), onehot, precision=lax.Precision.HIGHEST)
    bias = jnp.where((jnp.abs(rel) <= WINDOW)[None], bias * LOG2E, -jnp.inf)
    bias = bias.reshape(ATT_KV, ATT_GROUP * WINDOW, 3 * WINDOW)
    first = jnp.where(ki < WINDOW, -jnp.inf, bias)
    last = jnp.where(ki >= 2 * WINDOW, -jnp.inf, bias)
    return jnp.stack([first, bias, last], axis=1)


def _mem_kv_kernel(m_ref, g_ref, w_ref, kg_ref, o_ref):
    h = _rms(m_ref[...], g_ref[...]).astype(BF16)
    kw = MEM_HEADS * MEM_D
    for j in range(MEM_HEADS):
        cols = slice(j * MEM_D, (j + 1) * MEM_D)
        k = jnp.dot(h, w_ref[:, cols], preferred_element_type=F32)
        o_ref[:, cols] = _rms(k, kg_ref[...]).astype(BF16)
    o_ref[:, kw:] = jnp.dot(h, w_ref[:, kw:], preferred_element_type=F32).astype(BF16)


def _mem_kv(mem2, gain, w_bf16, k_gain, bm=512):
    rows = mem2.shape[0]
    bm = min(bm, rows)
    n = w_bf16.shape[1]
    return pl.pallas_call(
        _mem_kv_kernel,
        out_shape=jax.ShapeDtypeStruct((rows, n), BF16),
        grid=(rows // bm,),
        in_specs=[
            pl.BlockSpec((bm, D_MODEL), lambda i: (i, 0)),
            pl.BlockSpec((1, D_MODEL), lambda i: (0, 0)),
            pl.BlockSpec((D_MODEL, n), lambda i: (0, 0)),
            pl.BlockSpec((1, MEM_D), lambda i: (0, 0)),
        ],
        out_specs=pl.BlockSpec((bm, n), lambda i: (i, 0)),
        compiler_params=_cp(("parallel",)),
        name="mem_kv",
    )(mem2, gain, w_bf16, k_gain)


def _mem_attn_kernel(q_ref, k_ref, v_ref, o_ref, kt, *, bq):
    seq = q_ref.shape[0]
    kt[...] = k_ref[...].astype(F32).T.astype(BF16)

    def block(n, carry):
        rows = pl.ds(pl.multiple_of(n * bq, bq), bq)
        s = jnp.dot(q_ref[rows, :], kt[...], preferred_element_type=F32)
        p = jnp.exp(s - jnp.max(s, axis=-1, keepdims=True))
        denom = jnp.sum(p, axis=-1, keepdims=True)
        o = jnp.dot(p.astype(BF16), v_ref[...], preferred_element_type=F32) / denom
        o_ref[rows, :] = o.astype(BF16)
        return carry

    lax.fori_loop(0, seq // bq, block, 0, unroll=8)


def _mem_attention(proj3, mkv3, bq=256):
    b, seq, _ = proj3.shape
    m_len = mkv3.shape[1]
    return pl.pallas_call(
        functools.partial(_mem_attn_kernel, bq=bq),
        out_shape=jax.ShapeDtypeStruct((b, seq, MEM_HEADS * MEM_D), BF16),
        grid=(b, MEM_HEADS),
        in_specs=[
            pl.BlockSpec((None, seq, MEM_D), lambda i, h: (i, 0, OFF_MQ // MEM_D + h)),
            pl.BlockSpec((None, m_len, MEM_D), lambda i, h: (i, 0, h)),
            pl.BlockSpec((None, m_len, MEM_D), lambda i, h: (i, 0, MEM_HEADS + h)),
        ],
        out_specs=pl.BlockSpec((None, seq, MEM_D), lambda i, h: (i, 0, h)),
        scratch_shapes=[pltpu.VMEM((MEM_D, m_len), BF16)],
        compiler_params=_cp(("parallel", "parallel")),
        name="mem_attention",
    )(proj3, mkv3, mkv3)


def _merge_kernel(x_ref, g0_ref, g1_ref, g2_ref, b0_ref, b1_ref, b2_ref, wb_ref, wo_ref, gn_ref,
                  wr_ref, xo_ref, hp_ref, lg_ref, lg_scr, x_scr):
    i = pl.program_id(0)
    n = pl.num_programs(0) - 1

    def mix():
        merged = None
        for gate_ref, br, j in ((g0_ref, b0_ref, 0), (g1_ref, b1_ref, 1), (g2_ref, b2_ref, 2)):
            gate = 0.5 * jnp.tanh(0.5 * gate_ref[...].astype(F32)) + 0.5
            t = gate * jnp.dot(br[...], wb_ref[j], preferred_element_type=F32)
            merged = t if merged is None else merged + t
        x = x_ref[...] + jnp.dot(merged.astype(BF16), wo_ref[...], preferred_element_type=F32)
        xo_ref[...] = x
        x_scr[i % 2] = x

    def route_inputs():
        h = _rms(x_scr[(i + 1) % 2], gn_ref[...])
        h_hi = h.astype(BF16)
        h_lo = (h - h_hi.astype(F32)).astype(BF16)
        lg_scr[...] = jnp.dot(jnp.concatenate([h_hi, h_lo], axis=1), wr_ref[...],
                              preferred_element_type=F32)
        lg_ref[...] = lg_scr[...].T
        hp_ref[...] = _pack_bf16_pairs(h)

    @pl.when(jnp.logical_and(i >= 1, i < n))
    def _():
        route_inputs()
        mix()

    @pl.when(i == 0)
    def _():
        mix()

    @pl.when(i == n)
    def _():
        route_inputs()


def _merge(x2, proj, ret, att, mo, wb, wo, gain, w_router_cat, bm=512):
    t = x2.shape[0]
    last = t // bm - 1
    const2 = lambda i: (0, 0)
    row = lambda i: (jnp.minimum(i, last), 0)
    lag = lambda i: (jnp.maximum(i - 1, 0), 0)
    return pl.pallas_call(
        _merge_kernel,
        out_shape=(
            jax.ShapeDtypeStruct((t, D_MODEL), F32),
            jax.ShapeDtypeStruct((t, HALF), I32),
            jax.ShapeDtypeStruct((ROUTER_LANES, t), F32),
        ),
        grid=(t // bm + 1,),
        in_specs=[
            pl.BlockSpec((bm, D_MODEL), row),
            pl.BlockSpec((bm, D_MODEL), lambda i: (jnp.minimum(i, last), OFF_GATES // D_MODEL)),
            pl.BlockSpec((bm, D_MODEL), lambda i: (jnp.minimum(i, last), OFF_GATES // D_MODEL + 1)),
            pl.BlockSpec((bm, D_MODEL), lambda i: (jnp.minimum(i, last), OFF_GATES // D_MODEL + 2)),
            pl.BlockSpec((bm, D_MODEL), row),
            pl.BlockSpec((bm, D_MODEL), row),
            pl.BlockSpec((bm, D_MODEL), row),
            pl.BlockSpec((3, D_MODEL, D_MODEL), lambda i: (0, 0, 0)),
            pl.BlockSpec((D_MODEL, D_MODEL), const2),
            pl.BlockSpec((1, D_MODEL), const2),
            pl.BlockSpec((2 * D_MODEL, ROUTER_LANES), const2),
        ],
        out_specs=(
            pl.BlockSpec((bm, D_MODEL), row),
            pl.BlockSpec((bm, HALF), lag),
            pl.BlockSpec((ROUTER_LANES, bm), lambda i: (0, jnp.maximum(i - 1, 0))),
        ),
        scratch_shapes=[pltpu.VMEM((bm, ROUTER_LANES), F32), pltpu.VMEM((2, bm, D_MODEL), F32)],
        compiler_params=_cp(("arbitrary",)),
        name="merge_router",
    )(x2, proj, proj, proj, ret, att, mo, wb, wo, gain, w_router_cat)


def _route_kernel(l_ref, b_ref, tri_ref, eid_ref, rank_ref, gate_ref, cnt_ref):
    @pl.when(pl.program_id(0) == 0)
    def _():
        cnt_ref[...] = jnp.zeros_like(cnt_ref)

    l = l_ref[0:N_EXPERTS, :] + l_ref[N_EXPERTS:2 * N_EXPERTS, :] + b_ref[...]
    ne, bt = l.shape
    iota_e = lax.broadcasted_iota(I32, (ne, bt), 0)
    picked = jnp.zeros((ne, bt), jnp.bool_)
    vals, idxs = [], []
    for _ in range(TOP_K):
        m = jnp.max(l, axis=0, keepdims=True)
        idx = jnp.min(jnp.where(l == m, iota_e, ne), axis=0, keepdims=True)
        sel = iota_e == idx
        picked = picked | sel
        l = jnp.where(sel, -jnp.inf, l)
        vals.append(m)
        idxs.append(idx)
    ex = [jnp.exp(v - vals[0]) for v in vals]
    tot = ex[0] + ex[1] + ex[2] + ex[3]
    onehot = jnp.where(picked, 1.0, 0.0)
    before = jnp.dot(onehot.astype(BF16), tri_ref[...], preferred_element_type=F32) + cnt_ref[:, 0:1]
    for k in range(TOP_K):
        eid_ref[k:k + 1, :] = idxs[k]
        gate_ref[k:k + 1, :] = ex[k] / tot
        rank_ref[k:k + 1, :] = jnp.sum(jnp.where(iota_e == idxs[k], before, 0.0), axis=0,
                                       keepdims=True).astype(I32)
    cnt_ref[...] = cnt_ref[...] + jnp.sum(onehot, axis=1, keepdims=True)


def _route(logits_t, bias_col, tri):
    bt = tri.shape[0]
    ne, t = N_EXPERTS, logits_t.shape[1]
    blk = lambda i: (0, i)
    return pl.pallas_call(
        _route_kernel,
        out_shape=(
            jax.ShapeDtypeStruct((TOP_K, t), I32),
            jax.ShapeDtypeStruct((TOP_K, t), I32),
            jax.ShapeDtypeStruct((TOP_K, t), F32),
            jax.ShapeDtypeStruct((ne, 128), F32),
        ),
        grid=(t // bt,),
        in_specs=[pl.BlockSpec((ROUTER_LANES, bt), blk), pl.BlockSpec((ne, 1), lambda i: (0, 0)),
                  pl.BlockSpec((bt, bt), lambda i: (0, 0))],
        out_specs=(
            pl.BlockSpec((TOP_K, bt), blk),
            pl.BlockSpec((TOP_K, bt), blk),
            pl.BlockSpec((TOP_K, bt), blk),
            pl.BlockSpec((ne, 128), lambda i: (0, 0)),
        ),
        compiler_params=_cp(("arbitrary",)),
        name="route_topk",
    )(logits_t, bias_col, tri)


def _slot_kernel(start_ref, eid_ref, rank_ref, slot_ref):
    eid = eid_ref[...]
    base = jnp.zeros(eid.shape, I32)
    for e in range(N_EXPERTS):
        base = jnp.where(eid == e, start_ref[e], base)
    slot_ref[...] = base + rank_ref[...]


def _slots(pad_starts, eid, rank, bt=2048):
    t = eid.shape[1]
    blk = lambda i, s: (0, i)
    return pl.pallas_call(
        _slot_kernel,
        out_shape=jax.ShapeDtypeStruct((TOP_K, t), I32),
        grid_spec=pltpu.PrefetchScalarGridSpec(
            num_scalar_prefetch=1,
            grid=(t // bt,),
            in_specs=[pl.BlockSpec((TOP_K, bt), blk), pl.BlockSpec((TOP_K, bt), blk)],
            out_specs=pl.BlockSpec((TOP_K, bt), blk),
        ),
        compiler_params=_cp(("parallel",)),
        name="route_slots",
    )(pad_starts, eid, rank)


def _sc_workers():
    info = plsc.get_sparse_core_info()
    return info.num_cores, info.num_subcores


def _dispatch_rows(hp, slot, n_rows, win=64):
    t, w = hp.shape
    nc, ns = _sc_workers()
    per_worker = t // (nc * ns)
    mesh = plsc.VectorSubcoreMesh(core_axis_name="c", subcore_axis_name="s")

    @functools.partial(
        pl.kernel,
        out_type=jax.ShapeDtypeStruct((n_rows, w), hp.dtype),
        mesh=mesh,
        scratch_types=[pltpu.VMEM((win,), I32)] * TOP_K + [pltpu.VMEM((win, w), hp.dtype)],
        name="moe_dispatch",
    )
    def k(hp_hbm, slot_hbm, buf_hbm, *scratch):
        idx_v, rows_v = scratch[:TOP_K], scratch[TOP_K]
        wid = lax.axis_index("s") * nc + lax.axis_index("c")
        base = wid * per_worker

        @pl.loop(0, per_worker // win)
        def _(i):
            off = pl.multiple_of(base + i * win, win)
            pltpu.sync_copy(hp_hbm.at[pl.ds(off, win)], rows_v)
            for kk in range(TOP_K):
                pltpu.sync_copy(slot_hbm.at[pl.ds(kk * t + off, win)], idx_v[kk])
                pltpu.sync_copy(rows_v, buf_hbm.at[idx_v[kk]])

    return k(hp, slot.reshape(-1))


def _gather_rows(out_rows, slot, win=64):
    t = slot.shape[1]
    w = out_rows.shape[1]
    nc, ns = _sc_workers()
    per_worker = t // (nc * ns)
    mesh = plsc.VectorSubcoreMesh(core_axis_name="c", subcore_axis_name="s")

    @functools.partial(
        pl.kernel,
        out_type=jax.ShapeDtypeStruct((TOP_K * t, w), out_rows.dtype),
        mesh=mesh,
        scratch_types=[pltpu.VMEM((win,), I32), pltpu.VMEM((win, w), out_rows.dtype)],
        name="moe_gather",
    )
    def k(rows_hbm, slot_hbm, og_hbm, idx_v, rows_v):
        wid = lax.axis_index("s") * nc + lax.axis_index("c")
        base = wid * per_worker

        @pl.loop(0, per_worker // win)
        def _(i):
            off = pl.multiple_of(base + i * win, win)
            for kk in range(TOP_K):
                pltpu.sync_copy(slot_hbm.at[pl.ds(kk * t + off, win)], idx_v)
                pltpu.sync_copy(rows_hbm.at[idx_v], rows_v)
                pltpu.sync_copy(rows_v, og_hbm.at[pl.ds(kk * t + off, win)])

    return k(out_rows, slot.reshape(-1)).reshape(TOP_K, t, w)


def _expert_kernel(be_ref, nu_ref, x_ref, wi_ref, bi_ref, wo_ref, bo_ref, o_ref, wi_bf, wo_bf, act_scr):
    i = pl.program_id(0)
    n_used = nu_ref[0]
    last = be_ref.shape[0] - 1
    e_up = be_ref[jnp.minimum(i, last)]
    e_up_prev = be_ref[jnp.clip(i - 1, 0, last)]
    e_down_prev = be_ref[jnp.clip(i - 2, 0, last)]
    up = i < n_used
    down = jnp.logical_and(i >= 1, i <= n_used)
    rows = 128

    @pl.when(jnp.logical_and(up, jnp.logical_or(i == 0, e_up != e_up_prev)))
    def _():
        def cast(c, carry):
            sl = pl.ds(pl.multiple_of(c * rows, rows), rows)
            wi_bf[sl, :] = wi_ref[sl, :].astype(BF16)
            return carry

        lax.fori_loop(0, D_MODEL // rows, cast, 0)

    @pl.when(jnp.logical_and(down, jnp.logical_or(i == 1, e_up_prev != e_down_prev)))
    def _():
        def cast(c, carry):
            sl = pl.ds(pl.multiple_of(c * rows, rows), rows)
            wo_bf[sl, :] = wo_ref[sl, :].astype(BF16)
            return carry

        lax.fori_loop(0, D_FF // rows, cast, 0)

    def up_half():
        hi, lo = _unpack_bf16_pairs(x_ref[...])
        x = jnp.concatenate([hi.astype(BF16), lo.astype(BF16)], axis=1)
        h = jnp.dot(x, wi_bf[...], preferred_element_type=F32) + bi_ref[...]
        gate = jnp.minimum(h[:, :D_FF], SWIGLU_LIMIT)
        lin = jnp.clip(h[:, D_FF:], -SWIGLU_LIMIT, SWIGLU_LIMIT)
        act = gate * (0.5 * jnp.tanh((0.5 * SWIGLU_ALPHA) * gate) + 0.5) * (lin + 1.0)
        act_scr[i % 2] = act.astype(BF16)

    def down_half():
        o = jnp.dot(act_scr[(i + 1) % 2], wo_bf[...], preferred_element_type=F32) + bo_ref[...]
        o_ref[...] = _pack_bf16_pairs(o)

    @pl.when(jnp.logical_and(up, down))
    def _():
        down_half()
        up_half()

    @pl.when(jnp.logical_and(up, jnp.logical_not(down)))
    def _():
        up_half()

    @pl.when(jnp.logical_and(down, jnp.logical_not(up)))
    def _():
        down_half()


def _experts(block_expert, n_used, buf, w_in, b_in, w_out, b_out, blk):
    n_rows, w = buf.shape
    n_blocks = n_rows // blk
    last = n_blocks - 1
    up_blk = lambda i, be, nu: (jnp.minimum(i, last), 0)
    up_exp = lambda i, be, nu: (be[jnp.minimum(i, last)], 0, 0)
    down_blk = lambda i, be, nu: (jnp.maximum(i - 1, 0), 0)
    down_exp = lambda i, be, nu: (be[jnp.maximum(i - 1, 0)], 0, 0)
    return pl.pallas_call(
        _expert_kernel,
        out_shape=jax.ShapeDtypeStruct((n_rows, w), buf.dtype),
        grid_spec=pltpu.PrefetchScalarGridSpec(
            num_scalar_prefetch=2,
            grid=(n_blocks + 1,),
            in_specs=[
                pl.BlockSpec((blk, w), up_blk),
                pl.BlockSpec((None, D_MODEL, 2 * D_FF), up_exp),
                pl.BlockSpec((None, 1, 2 * D_FF), up_exp),
                pl.BlockSpec((None, D_FF, D_MODEL), down_exp),
                pl.BlockSpec((None, 1, D_MODEL), down_exp),
            ],
            out_specs=pl.BlockSpec((blk, w), down_blk),
            scratch_shapes=[pltpu.VMEM((D_MODEL, 2 * D_FF), BF16), pltpu.VMEM((D_FF, D_MODEL), BF16),
                            pltpu.VMEM((2, blk, D_FF), BF16)],
        ),
        compiler_params=_cp(("arbitrary",)),
        name="expert_ffn",
    )(block_expert, n_used, buf, w_in, b_in, w_out, b_out)


def _combine_kernel(x_ref, og_ref, g_ref, y_ref):
    g = g_ref[...]
    acc_hi = x_ref[:, :HALF]
    acc_lo = x_ref[:, HALF:]
    for k in range(TOP_K):
        hi, lo = _unpack_bf16_pairs(og_ref[k])
        gk = g[:, k:k + 1]
        acc_hi = acc_hi + gk * hi
        acc_lo = acc_lo + gk * lo
    y_ref[:, :HALF] = acc_hi
    y_ref[:, HALF:] = acc_lo


def _combine(x_mid, og, gates_tk, bm=1024):
    t = x_mid.shape[0]
    return pl.pallas_call(
        _combine_kernel,
        out_shape=jax.ShapeDtypeStruct((t, D_MODEL), F32),
        grid=(t // bm,),
        in_specs=[
            pl.BlockSpec((bm, D_MODEL), lambda i: (i, 0)),
            pl.BlockSpec((TOP_K, bm, HALF), lambda i: (0, i, 0)),
            pl.BlockSpec((bm, TOP_K), lambda i: (i, 0)),
        ],
        out_specs=pl.BlockSpec((bm, D_MODEL), lambda i: (i, 0)),
        compiler_params=_cp(("parallel",)),
        name="moe_combine",
    )(x_mid, og, gates_tk)


def _block_tables(counts, n_blocks, blk):
    padded = (counts + blk - 1) // blk * blk
    pad_ends = jnp.cumsum(padded)
    pad_starts = pad_ends - padded
    block_start = jnp.arange(n_blocks, dtype=I32) * blk
    block_expert = jnp.minimum(jnp.sum(pad_ends[None, :] <= block_start[:, None], axis=1), N_EXPERTS - 1)
    n_used = (pad_ends[-1] // blk).reshape(1)
    return pad_starts.astype(I32), block_expert.astype(I32), n_used.astype(I32)


def _layer(x, mem, consts, params, after=None):
    b, seq, d = x.shape
    m_len = mem.shape[1]
    t = b * seq
    x2 = x.reshape(t, d)
    gain = params["norm_mix"]
    if after is not None:
        gain = gain + 0.0 * after.astype(F32)
    proj = _in_proj(x2, gain, params["w_in"], params["head_gains"], consts["cos"], consts["sin"])
    proj3 = proj.reshape(b, seq, IN_WIDTH)
    ret = _retention(proj3, params["ret_out_norm"], consts["tabs"], consts["decays"])
    att = _window_attention(proj3, params["att_sink"], consts["bias"])
    mkv = _mem_kv(mem.reshape(b * m_len, d), params["mem_norm"], params["w_mem_kv"],
                  params["mem_k_norm"])
    mo = _mem_attention(proj3, mkv.reshape(b, m_len, 2 * MEM_HEADS * MEM_D))
    x_mid, hp, logits_t = _merge(x2, proj, ret.reshape(t, d), att.reshape(t, d), mo.reshape(t, d),
                               params["w_branch"], params["w_out"], params["norm_ffn"],
                               params["w_router"])
    eid, rank, gates, counts = _route(logits_t, params["b_router"], consts["tri"])
    blk = EXPERT_BLOCK
    n_blocks = t * TOP_K // blk + N_EXPERTS
    pad_starts, block_expert, n_used = _block_tables(counts[:, 0].astype(I32), n_blocks, blk)
    slot = _slots(pad_starts, eid, rank)
    buf = _dispatch_rows(hp, slot, n_blocks * blk)
    out_rows = _experts(block_expert, n_used, buf, params["w_e_in"], params["b_e_in"],
                        params["w_e_out"], params["b_e_out"], blk)
    og = _gather_rows(out_rows, slot)
    y = _combine(x_mid, og, gates.T)
    return y.reshape(b, seq, d), slot[0, 0]


def _head_gains(att_q, att_k, mem_q):
    rows = jnp.zeros((8, MEM_D), F32)
    rows = rows.at[0, :ATT_D].set(att_q.astype(F32) * (ATT_D ** -0.5 * LOG2E))
    rows = rows.at[1, :ATT_D].set(att_k.astype(F32))
    return rows.at[2, :MEM_D].set(mem_q.astype(F32) * (MEM_D ** -0.5))


def _router_weights(w):
    w_hi = w.astype(BF16)
    w_lo = (w - w_hi.astype(F32)).astype(BF16)
    pad = jnp.zeros((D_MODEL, ROUTER_LANES - 2 * N_EXPERTS), BF16)
    top = jnp.concatenate([w_hi, w_lo, pad], axis=1)
    bottom = jnp.concatenate([w_hi, jnp.zeros_like(w_lo), pad], axis=1)
    return jnp.concatenate([top, bottom], axis=0)


def _prepare(seq, rel_table, norm_mix, w_in, ret_out_norm, att_q_norm, att_k_norm, att_sink, mem_norm,
             w_mem_kv, mem_q_norm, mem_k_norm, w_branch, w_out, norm_ffn, w_router, b_router,
             w_e_in, b_e_in, w_e_out, b_e_out):
    gates_at = IN_WIDTH - 3 * D_MODEL
    w_in0 = w_in[0]
    params = {
        "norm_mix": norm_mix[0].reshape(1, -1),
        "w_in": jnp.concatenate([w_in0[:, gates_at:], w_in0[:, :gates_at]], axis=1).astype(BF16),
        "ret_out_norm": ret_out_norm[0].reshape(1, -1),
        "head_gains": _head_gains(att_q_norm[0], att_k_norm[0], mem_q_norm[0]),
        "att_sink": att_sink[0].astype(F32),
        "mem_norm": mem_norm[0].reshape(1, -1),
        "w_mem_kv": w_mem_kv[0].astype(BF16),
        "mem_k_norm": mem_k_norm[0].reshape(1, -1),
        "w_branch": w_branch[0].astype(BF16),
        "w_out": w_out[0].astype(BF16),
        "norm_ffn": norm_ffn[0].reshape(1, -1),
        "w_router": _router_weights(w_router[0]),
        "b_router": b_router[0].reshape(-1, 1),
        "w_e_in": w_e_in[0],
        "b_e_in": b_e_in[0].reshape(N_EXPERTS, 1, -1),
        "w_e_out": w_e_out[0],
        "b_e_out": b_e_out[0].reshape(N_EXPERTS, 1, -1),
    }
    cos, sin, tabs, decays = _retention_tables(seq)
    route_bt = 1024
    tri = (jnp.arange(route_bt)[:, None] < jnp.arange(route_bt)[None, :]).astype(BF16)
    consts = {"cos": cos, "sin": sin, "tabs": tabs, "decays": decays,
              "bias": _window_bias(rel_table), "tri": tri}
    return consts, params


def kernel(x_prompt, x_sample, mem_prompt, mem_sample, rel_table, norm_mix, w_in, ret_out_norm, att_q_norm, att_k_norm, att_sink, mem_norm, w_mem_kv, mem_q_norm, mem_k_norm, w_branch, w_out, norm_ffn, w_router, b_router, w_e_in, b_e_in, w_e_out, b_e_out):
    consts, params = _prepare(x_prompt.shape[1], rel_table, norm_mix, w_in, ret_out_norm, att_q_norm,
                              att_k_norm, att_sink, mem_norm, w_mem_kv, mem_q_norm, mem_k_norm,
                              w_branch, w_out, norm_ffn, w_router, b_router, w_e_in, b_e_in,
                              w_e_out, b_e_out)
    y_sample, first_slot = _layer(x_sample, mem_sample, consts, params)
    y_prompt, _ = _layer(x_prompt, mem_prompt, consts, params, after=first_slot)
    return (y_prompt, y_sample)
```
